```python
import jax, jax.numpy as jnp
from jax import lax
import numpy as np

D_MODEL = 2048
BATCH = 4
SEQ = 2048
DEPTH = 1
DEC_BATCH = 128
DEC_SEQ = 1
PAST_LEN = 16384
PAGE_SIZE = 128

RET_HEADS = 4
RET_DK = 256
RET_DV = 256
RET_W = RET_HEADS * RET_DV
RET_CHUNK = 64
ROPE_THETA = 10000.0
GLA_HEADS = 4
GLA_DK = 128
GLA_DV = 256
GLA_W = GLA_HEADS * GLA_DV
GLA_RANK = 16
GLA_TAU = 16.0
GLA_CHUNK = 16
MIX_W = RET_W + GLA_W
D_FF = 5632
CONV_W = 3
EPS = 1e-6
PROJ_SIZES = (RET_HEADS * RET_DK, RET_HEADS * RET_DK, RET_W, RET_W,
              GLA_HEADS * GLA_DK, GLA_HEADS * GLA_DK, GLA_W, GLA_W, GLA_RANK)
IN_W = 4 * RET_W + 2 * GLA_HEADS * GLA_DK + 2 * GLA_W + GLA_RANK

kernel_name = "hybrid_retention_gla_convffn_step"


def _rmsnorm(x, g):
    xf = x.astype(jnp.float32)
    y = xf * lax.rsqrt(jnp.mean(xf * xf, axis=-1, keepdims=True) + EPS)
    return (y * g.astype(jnp.float32)).astype(x.dtype)


def _head_norm(o):
    return o * lax.rsqrt(jnp.mean(o * o, axis=-1, keepdims=True) + EPS)


def _heads(t, h):
    b, t_len, _ = t.shape
    return t.reshape(b, t_len, h, -1).transpose(0, 2, 1, 3)


def _merge(o):
    b, h, t_len, d = o.shape
    return o.transpose(0, 2, 1, 3).reshape(b, t_len, h * d)


def _rope(x, pos):
    half = x.shape[-1] // 2
    inv = ROPE_THETA ** (-jnp.arange(half, dtype=jnp.float32) / half)
    ang = pos.astype(jnp.float32)[:, None] * inv[None, :]
    cos, sin = jnp.cos(ang), jnp.sin(ang)
    x1, x2 = x[..., :half], x[..., half:]
    return jnp.concatenate([x1 * cos - x2 * sin, x1 * sin + x2 * cos], axis=-1)


def _chunk_len(t_len, c):
    return c if t_len % c == 0 else t_len


def _to_chunks(a, c):
    b, h, t_len, d = a.shape
    return a.reshape(b, h, t_len // c, c, d).transpose(2, 0, 1, 3, 4)


def _from_chunks(o):
    n, b, h, c, d = o.shape
    return o.transpose(1, 2, 0, 3, 4).reshape(b, h, n * c, d)


def _retention(q, k, v, s0):
    h, t_len = q.shape[1], q.shape[2]
    c = _chunk_len(t_len, RET_CHUNK)
    lg = jnp.log1p(-jnp.exp2(-5.0 - jnp.arange(h, dtype=jnp.float32)))
    idx = jnp.arange(c, dtype=jnp.float32)
    rel = idx[:, None] - idx[None, :]
    dmat = jnp.where(rel[None] >= 0, jnp.exp(jnp.maximum(rel, 0.0)[None] * lg[:, None, None]), 0.0)
    q_dec = jnp.exp((idx + 1.0)[None, :] * lg[:, None])[..., None]
    k_dec = jnp.exp((c - 1.0 - idx)[None, :] * lg[:, None])[..., None]
    c_dec = jnp.exp(c * lg)[:, None, None]

    def step(s, xs):
        qc, kc, vc = xs
        scores = jnp.einsum('bhcd,bhsd->bhcs', qc, kc) * dmat
        o = jnp.einsum('bhcs,bhse->bhce', scores, vc) + jnp.einsum('bhcd,bhde->bhce', qc, s) * q_dec
        s = s * c_dec + jnp.einsum('bhcd,bhce->bhde', kc * k_dec, vc)
        return s, o

    s, o = lax.scan(step, s0, (_to_chunks(q, c), _to_chunks(k, c), _to_chunks(v, c)))
    return _from_chunks(o), s


def _gla(q, k, v, log_a, s0):
    t_len = q.shape[2]
    c = _chunk_len(t_len, GLA_CHUNK)
    mask = jnp.tril(jnp.ones((c, c), dtype=bool))[None, None, :, :, None]

    def step(s, xs):
        qc, kc, vc, gc = xs
        b = jnp.cumsum(gc, axis=2)
        diff = jnp.where(mask, b[:, :, :, None, :] - b[:, :, None, :, :], -jnp.inf)
        scores = jnp.einsum('bhtd,bhsd,bhtsd->bhts', qc, kc, jnp.exp(diff))
        o = jnp.einsum('bhts,bhse->bhte', scores, vc) + jnp.einsum('bhtd,bhde->bhte', qc * jnp.exp(b), s)
        b_last = b[:, :, -1:, :]
        s = s * jnp.exp(b_last[:, :, 0, :])[..., None] + jnp.einsum('bhsd,bhse->bhde', kc * jnp.exp(b_last - b), vc)
        return s, o

    s, o = lax.scan(step, s0, (_to_chunks(q, c), _to_chunks(k, c), _to_chunks(v, c), _to_chunks(log_a, c)))
    return _from_chunks(o), s


def _layer(x, c, pos, s_ret, s_gla, s_conv, w_ada, b_ada, g_attn, w_in, w_a2, b_a2,
           g_ret_out, g_gla_out, w_out, g_ffn, w_up, conv_w, conv_b, w_down):
    f32 = jnp.float32
    t_len = x.shape[1]
    mod = jax.nn.silu(c) @ w_ada + b_ada
    sh1, sc1, gt1, sh2, sc2, gt2 = jnp.split(mod[:, None, :], 6, axis=-1)

    h = _rmsnorm(x, g_attn) * (1 + sc1) + sh1
    p = h @ w_in
    split_idx = [int(i) for i in np.cumsum(PROJ_SIZES)[:-1]]
    rq, rk, rv, rg, gq, gk, gv, gr, ga = jnp.split(p, split_idx, axis=-1)

    q = _rope(_heads(rq, RET_HEADS).astype(f32), pos)
    k = _rope(_heads(rk, RET_HEADS).astype(f32), pos) * RET_DK ** -0.5
    v = _heads(rv, RET_HEADS).astype(f32)
    o_r, s_ret_new = _retention(q, k, v, s_ret.astype(f32))
    o_r = _merge(_head_norm(o_r)) * g_ret_out.astype(f32) * jax.nn.silu(rg.astype(f32))

    q2 = _heads(gq, GLA_HEADS).astype(f32) * GLA_DK ** -0.5
    k2 = _heads(gk, GLA_HEADS).astype(f32)
    v2 = _heads(gv, GLA_HEADS).astype(f32)
    la = jax.nn.log_sigmoid((ga @ w_a2 + b_a2).astype(f32)) / GLA_TAU
    o_g, s_gla_new = _gla(q2, k2, v2, _heads(la, GLA_HEADS), s_gla.astype(f32))
    o_g = _merge(_head_norm(o_g)) * g_gla_out.astype(f32) * jax.nn.silu(gr.astype(f32))

    mix = jnp.concatenate([o_r, o_g], axis=-1).astype(x.dtype) @ w_out
    x = x + gt1 * mix

    h = _rmsnorm(x, g_ffn) * (1 + sc2) + sh2
    u = h @ w_up
    up = jnp.concatenate([s_conv.astype(u.dtype), u], axis=1)
    uc = conv_b + sum(conv_w[j] * up[:, j:j + t_len] for j in range(CONV_W))
    a, bb = jnp.split(uc, 2, axis=-1)
    x = x + gt2 * ((jax.nn.silu(a) * bb) @ w_down)
    return (x, s_ret_new.astype(s_ret.dtype), s_gla_new.astype(s_gla.dtype),
            up[:, -(CONV_W - 1):].astype(s_conv.dtype))


def setup_inputs(seed: int = 0) -> dict:
    key = jax.random.key(seed)
    ks = jax.random.split(key, 24)
    nrm = jax.random.normal
    f32 = jnp.float32
    L = DEPTH
    return {
        "x_prompt": nrm(ks[0], (BATCH, SEQ, D_MODEL), f32),
        "x_sample": nrm(ks[1], (DEC_BATCH, DEC_SEQ, D_MODEL), f32),
        "c_prompt": nrm(ks[2], (BATCH, D_MODEL), f32),
        "c_sample": nrm(ks[3], (DEC_BATCH, D_MODEL), f32),
        "state_ret": nrm(ks[4], (L, DEC_BATCH, RET_HEADS, RET_DK, RET_DV), f32),
        "state_gla": nrm(ks[5], (L, DEC_BATCH, GLA_HEADS, GLA_DK, GLA_DV), f32),
        "state_conv": nrm(ks[6], (L, DEC_BATCH, CONV_W - 1, 2 * D_FF), f32),
        "w_ada": nrm(ks[7], (L, D_MODEL, 6 * D_MODEL), f32) * (0.5 * D_MODEL ** -0.5),
        "b_ada": nrm(ks[8], (L, 6 * D_MODEL), f32) * 0.01,
        "g_attn": 1.0 + 0.02 * nrm(ks[9], (L, D_MODEL), f32),
        "w_in": nrm(ks[10], (L, D_MODEL, IN_W), f32) * D_MODEL ** -0.5,
        "w_a2": nrm(ks[11], (L, GLA_RANK, GLA_HEADS * GLA_DK), f32) * GLA_RANK ** -0.5,
        "b_a2": nrm(ks[12], (L, GLA_HEADS * GLA_DK), f32) * 0.1,
        "g_ret_out": 1.0 + 0.02 * nrm(ks[13], (L, RET_W), f32),
        "g_gla_out": 1.0 + 0.02 * nrm(ks[14], (L, GLA_W), f32),
        "w_out": nrm(ks[15], (L, MIX_W, D_MODEL), f32) * MIX_W ** -0.5,
        "g_ffn": 1.0 + 0.02 * nrm(ks[16], (L, D_MODEL), f32),
        "w_up": nrm(ks[17], (L, D_MODEL, 2 * D_FF), f32) * D_MODEL ** -0.5,
        "conv_w": nrm(ks[18], (L, CONV_W, 2 * D_FF), f32) * CONV_W ** -0.5,
        "conv_b": nrm(ks[19], (L, 2 * D_FF), f32) * 0.01,
        "w_down": nrm(ks[20], (L, D_FF, D_MODEL), f32) * D_FF ** -0.5,
        "g_final": 1.0 + 0.02 * nrm(ks[21], (D_MODEL,), f32),
    }


def reference(x_prompt, x_sample, c_prompt, c_sample, state_ret, state_gla, state_conv,
              w_ada, b_ada, g_attn, w_in, w_a2, b_a2, g_ret_out, g_gla_out, w_out,
              g_ffn, w_up, conv_w, conv_b, w_down, g_final):
    bp, t_p = x_prompt.shape[0], x_prompt.shape[1]
    t_s = x_sample.shape[1]
    pos_p = jnp.arange(t_p, dtype=jnp.int32)
    pos_s = PAST_LEN + jnp.arange(t_s, dtype=jnp.int32)
    dt = x_prompt.dtype
    zr = jnp.zeros((bp, RET_HEADS, RET_DK, RET_DV), dt)
    zg = jnp.zeros((bp, GLA_HEADS, GLA_DK, GLA_DV), dt)
    zc = jnp.zeros((bp, CONV_W - 1, 2 * D_FF), dt)
    hp, hs = x_prompt, x_sample
    rp, rs, gp, gs, cp, cs = [], [], [], [], [], []
    for l in range(DEPTH):
        params = (w_ada[l], b_ada[l], g_attn[l], w_in[l], w_a2[l], b_a2[l], g_ret_out[l],
                  g_gla_out[l], w_out[l], g_ffn[l], w_up[l], conv_w[l], conv_b[l], w_down[l])
        hp, r1, g1, c1 = _layer(hp, c_prompt, pos_p, zr, zg, zc, *params)
        hs, r2, g2, c2 = _layer(hs, c_sample, pos_s, state_ret[l], state_gla[l], state_conv[l], *params)
        rp.append(r1); rs.append(r2); gp.append(g1); gs.append(g2); cp.append(c1); cs.append(c2)
    y_prompt = _rmsnorm(hp, g_final)
    y_sample = _rmsnorm(hs, g_final)
    return (y_prompt, y_sample, jnp.stack(rp), jnp.stack(rs), jnp.stack(gp), jnp.stack(gs),
            jnp.stack(cp), jnp.stack(cs))
```

```python
import functools

import numpy as np
import jax
import jax.numpy as jnp
from jax import lax
from jax.experimental import pallas as pl
from jax.experimental.pallas import tpu as pltpu

F32 = jnp.float32
BF16 = jnp.bfloat16

RET_HEADS = 4
RET_DK = 256
RET_DV = 256
GLA_HEADS = 4
GLA_DK = 128
GLA_DV = 256
GLA_RANK = 16
GLA_TAU = 16.0
ROPE_THETA = 10000.0
PAST_LEN = 16384
CONV_W = 3
EPS = 1e-6

RET_W = RET_HEADS * RET_DV
GLA_W = GLA_HEADS * GLA_DV
COL_RQ = 0
COL_RK = COL_RQ + RET_HEADS * RET_DK
COL_RV = COL_RK + RET_HEADS * RET_DK
COL_RG = COL_RV + RET_W
COL_GQ = COL_RG + RET_W
COL_GK = COL_GQ + GLA_HEADS * GLA_DK
COL_GV = COL_GK + GLA_HEADS * GLA_DK
COL_GR = COL_GV + GLA_W
COL_GA = COL_GR + GLA_W
IN_MAIN = COL_GA

VMEM_LIMIT_BYTES = 56 * 1024 * 1024


def _params(semantics):
    return pltpu.CompilerParams(dimension_semantics=semantics, vmem_limit_bytes=VMEM_LIMIT_BYTES)


def _dot(a, b):
    return jnp.dot(a, b, preferred_element_type=F32)


def _dot_nt(a, b):
    return lax.dot_general(a, b, (((1,), (1,)), ((), ())), preferred_element_type=F32)


def _dot_tn(a, b):
    return lax.dot_general(a, b, (((0,), (0,)), ((), ())), preferred_element_type=F32)


def _silu(x):
    return x * jax.nn.sigmoid(x)


def _rms(x):
    return x * lax.rsqrt(jnp.mean(x * x, axis=-1, keepdims=True) + EPS)


def _split3(x):
    hi = x.astype(BF16)
    r = x - hi.astype(F32)
    mid = r.astype(BF16)
    lo = (r - mid.astype(F32)).astype(BF16)
    return hi, mid, lo


def _ada_body(c_ref, w_ref, b_ref, o_ref):
    s = _silu(c_ref[...]).astype(BF16)
    o_ref[...] = _dot(s, w_ref[...].astype(BF16)) + b_ref[...]


def _ada(c_all, w_ada, b_ada, tn=512):
    r, d = c_all.shape
    n = w_ada.shape[1]
    return pl.pallas_call(
        _ada_body,
        grid=(n // tn,),
        in_specs=[pl.BlockSpec((r, d), lambda j: (0, 0)),
                  pl.BlockSpec((d, tn), lambda j: (0, j)),
                  pl.BlockSpec((1, tn), lambda j: (0, j))],
        out_specs=pl.BlockSpec((r, tn), lambda j: (0, j)),
        out_shape=jax.ShapeDtypeStruct((r, n), F32),
        compiler_params=_params(("arbitrary",)),
        name="ada_mod",
    )(c_all, w_ada, b_ada.reshape(1, n))


def _inproj_body(x_ref, sc_ref, sh_ref, g_ref, w_ref, wga_ref, wa2_ref, ba2_ref, p_ref, la_ref, h_ref):
    @pl.when(pl.program_id(1) == 0)
    def _():
        h = _rms(x_ref[...]) * g_ref[...]
        h = h * (1.0 + sc_ref[0]) + sh_ref[0]
        hb = h.astype(BF16)
        h_ref[...] = hb
        ga = _dot(hb, wga_ref[...].astype(BF16))
        z = _dot(ga.astype(BF16), wa2_ref[...].astype(BF16)) + ba2_ref[...]
        la_ref[...] = (jnp.minimum(z, 0.0) - jnp.log1p(jnp.exp(-jnp.abs(z)))) * (1.0 / GLA_TAU)

    p_ref[...] = _dot(h_ref[...], w_ref[...].astype(BF16))


def _inproj(x, sc, sh, g_attn, w_in, w_a2, b_a2, *, tm, rows_per_mod, tn=512):
    m, d = x.shape
    nq = w_a2.shape[1]
    mod_spec = pl.BlockSpec((1, sc.shape[1], d), lambda i, j: ((i * tm) // rows_per_mod, 0, 0))
    return pl.pallas_call(
        _inproj_body,
        grid=(m // tm, IN_MAIN // tn),
        in_specs=[pl.BlockSpec((tm, d), lambda i, j: (i, 0)),
                  mod_spec, mod_spec,
                  pl.BlockSpec((1, d), lambda i, j: (0, 0)),
                  pl.BlockSpec((d, tn), lambda i, j: (0, j)),
                  pl.BlockSpec((d, GLA_RANK), lambda i, j: (0, 0)),
                  pl.BlockSpec((GLA_RANK, nq), lambda i, j: (0, 0)),
                  pl.BlockSpec((1, nq), lambda i, j: (0, 0))],
        out_specs=[pl.BlockSpec((tm, tn), lambda i, j: (i, j)),
                   pl.BlockSpec((tm, nq), lambda i, j: (i, 0))],
        out_shape=[jax.ShapeDtypeStruct((m, IN_MAIN), F32),
                   jax.ShapeDtypeStruct((m, nq), F32)],
        scratch_shapes=[pltpu.VMEM((tm, d), BF16)],
        compiler_params=_params(("arbitrary", "arbitrary")),
        name="in_proj",
    )(x, sc, sh, g_attn.reshape(1, d), w_in, w_in[:, COL_GA:], w_a2, b_a2.reshape(1, nq))


def _rope(x, cos, sin):
    half = x.shape[-1] // 2
    x1, x2 = x[:, :half], x[:, half:]
    return jnp.concatenate([x1 * cos - x2 * sin, x1 * sin + x2 * cos], axis=-1)


def _ret_body(lg_ref, q_ref, k_ref, v_ref, g_ref, cos_ref, sin_ref, gout_ref, o_ref, sout_ref, s_ref):
    c = pl.program_id(2)
    cl = q_ref.shape[0]

    @pl.when(c == 0)
    def _():
        s_ref[...] = jnp.zeros_like(s_ref)

    lg = lg_ref[0][:1, :1]
    cos, sin = cos_ref[...], sin_ref[...]
    q = _rope(q_ref[...], cos, sin)
    k = _rope(k_ref[...], cos, sin) * (RET_DK ** -0.5)
    rel = (lax.broadcasted_iota(jnp.int32, (cl, cl), 0) - lax.broadcasted_iota(jnp.int32, (cl, cl), 1)).astype(F32)
    dmat = jnp.where(rel >= 0.0, jnp.exp(jnp.maximum(rel, 0.0) * lg), 0.0)
    idx = lax.broadcasted_iota(jnp.int32, (cl, 1), 0).astype(F32)
    q_dec = jnp.exp((idx + 1.0) * lg)
    k_dec = jnp.exp((cl - 1.0 - idx) * lg)
    c_dec = jnp.exp(cl * lg)

    qb, kb, vb = q.astype(BF16), k.astype(BF16), v_ref[...].astype(BF16)
    s_old = s_ref[...]
    scores = _dot_nt(qb, kb) * dmat
    o = _dot(scores.astype(BF16), vb) + _dot(qb, s_old.astype(BF16)) * q_dec
    s_new = s_old * c_dec + _dot_tn((k * k_dec).astype(BF16), vb)
    s_ref[...] = s_new

    gate = g_ref[...]
    o_ref[...] = (_rms(o) * gout_ref[...] * _silu(gate)).astype(o_ref.dtype)

    @pl.when(c == pl.num_programs(2) - 1)
    def _():
        sout_ref[0, 0] = s_new


def _retention_prompt(p, cos, sin, lg, g_out, *, batch, seq, chunk):
    nc = seq // chunk
    blk = lambda off: pl.BlockSpec((chunk, RET_DK), lambda b, h, c: (b * nc + c, off + h))
    tab = pl.BlockSpec((chunk, RET_DK // 2), lambda b, h, c: (c, 0))
    return pl.pallas_call(
        _ret_body,
        grid=(batch, RET_HEADS, nc),
        in_specs=[pl.BlockSpec((1, 8, 128), lambda b, h, c: (h, 0, 0)),
                  blk(COL_RQ // RET_DK), blk(COL_RK // RET_DK), blk(COL_RV // RET_DV), blk(COL_RG // RET_DV),
                  tab, tab,
                  pl.BlockSpec((1, RET_DV), lambda b, h, c: (0, h))],
        out_specs=[pl.BlockSpec((chunk, RET_DV), lambda b, h, c: (b * nc + c, h)),
                   pl.BlockSpec((1, 1, RET_DK, RET_DV), lambda b, h, c: (b, h, 0, 0))],
        out_shape=[jax.ShapeDtypeStruct((batch * seq, RET_W), BF16),
                   jax.ShapeDtypeStruct((batch, RET_HEADS, RET_DK, RET_DV), F32)],
        scratch_shapes=[pltpu.VMEM((RET_DK, RET_DV), F32)],
        compiler_params=_params(("arbitrary", "arbitrary", "arbitrary")),
        name="retention_prompt",
    )(lg, p, p, p, p, cos, sin, g_out.reshape(1, RET_W))


def _gla_level_map(c):
    t = np.arange(c)[:, None]
    s = np.arange(c)[None, :]
    x = np.bitwise_xor(t, s)
    lev = np.floor(np.log2(np.maximum(x, 1))).astype(np.int32)
    lev = np.where(t == s, -1, lev)
    lev = np.where(t < s, -2, lev)
    return lev.astype(np.int32)


def _gla_body(q_ref, k_ref, v_ref, g_ref, la_ref, lev_ref, gout_ref, o_ref, sout_ref, st_ref):
    c = pl.program_id(2)
    cl, dk = q_ref.shape

    @pl.when(c == 0)
    def _():
        st_ref[...] = jnp.zeros_like(st_ref)

    q = q_ref[...] * (GLA_DK ** -0.5)
    k = k_ref[...]
    la = la_ref[...]
    lev = lev_ref[...]
    row = lax.broadcasted_iota(jnp.int32, (cl, dk), 0)

    p = la
    tot = la
    a = jnp.zeros((cl, cl), F32)
    level = 0
    half = 1
    while half < cl:
        upper = (row & half) != 0
        e = jnp.where(upper, p, p - tot)
        z = (jnp.where(upper, q, k) * jnp.exp(-jnp.abs(e))).astype(BF16)
        a = jnp.where(lev == level, _dot_nt(z, z), a)
        partner = jnp.where(upper, pltpu.roll(tot, half, 0), pltpu.roll(tot, cl - half, 0))
        p = p + jnp.where(upper, partner, 0.0)
        tot = tot + partner
        half *= 2
        level += 1
    a = jnp.where(lev == -1, jnp.sum(q * k, axis=-1, keepdims=True), a)

    vb = v_ref[...].astype(BF16)
    st_old = st_ref[...]
    qt = (q * jnp.exp(p)).astype(BF16)
    o = _dot(a.astype(BF16), vb) + _dot_nt(qt, st_old.astype(BF16))
    kt = (k * jnp.exp(tot - p)).astype(BF16)
    st_new = st_old * jnp.exp(tot[0:1, :]) + _dot_tn(vb, kt)
    st_ref[...] = st_new

    gate = g_ref[...]
    o_ref[...] = (_rms(o) * gout_ref[...] * _silu(gate)).astype(o_ref.dtype)

    @pl.when(c == pl.num_programs(2) - 1)
    def _():
        sout_ref[0, 0] = st_new.T


def _gla_prompt(p, la, g_out, *, batch, seq, chunk):
    nc = seq // chunk
    lev = jnp.asarray(_gla_level_map(chunk))
    qk = lambda off: pl.BlockSpec((chunk, GLA_DK), lambda b, h, c: (b * nc + c, off + h))
    vg = lambda off: pl.BlockSpec((chunk, GLA_DV), lambda b, h, c: (b * nc + c, off + h))
    return pl.pallas_call(
        _gla_body,
        grid=(batch, GLA_HEADS, nc),
        in_specs=[qk(COL_GQ // GLA_DK), qk(COL_GK // GLA_DK), vg(COL_GV // GLA_DV), vg(COL_GR // GLA_DV),
                  pl.BlockSpec((chunk, GLA_DK), lambda b, h, c: (b * nc + c, h)),
                  pl.BlockSpec((chunk, chunk), lambda b, h, c: (0, 0)),
                  pl.BlockSpec((1, GLA_DV), lambda b, h, c: (0, h))],
        out_specs=[pl.BlockSpec((chunk, GLA_DV), lambda b, h, c: (b * nc + c, h)),
                   pl.BlockSpec((1, 1, GLA_DK, GLA_DV), lambda b, h, c: (b, h, 0, 0))],
        out_shape=[jax.ShapeDtypeStruct((batch * seq, GLA_W), BF16),
                   jax.ShapeDtypeStruct((batch, GLA_HEADS, GLA_DK, GLA_DV), F32)],
        scratch_shapes=[pltpu.VMEM((GLA_DV, GLA_DK), F32)],
        compiler_params=_params(("arbitrary", "arbitrary", "arbitrary")),
        name="gla_prompt",
    )(p, p, p, p, la, lev, g_out.reshape(1, GLA_W))


COL_ROWS = 16


def _col_selector(nvec, width):
    sel = np.zeros((COL_ROWS, nvec * width), np.float32)
    for i in range(nvec):
        sel[3 * i:3 * i + 3, i * width:(i + 1) * width] = 1.0
    return sel


def _col_bcast(rows, sel):
    n = rows[0].shape[1]
    ridx = lax.broadcasted_iota(jnp.int32, (COL_ROWS, n), 0)
    lhs = jnp.zeros((COL_ROWS, n), F32)
    for i, r in enumerate(rows):
        for j, piece in enumerate(_split3(r)):
            lhs = jnp.where(ridx == 3 * i + j, piece.astype(F32), lhs)
    return _dot_tn(lhs.astype(BF16), sel)


def _state_body(lg_ref, p_ref, la_ref, cos_ref, sin_ref, sr_ref, sg_ref, gro_ref, ggo_ref, sel2_ref, sel3_ref,
                o_ref, sr_out, sg_out):
    nb = p_ref.shape[0]
    cos, sin = cos_ref[...], sin_ref[...]
    sel2, sel3 = sel2_ref[...], sel3_ref[...]
    for i in range(nb):
        prow = p_ref[i]
        outs = []
        for h in range(RET_HEADS):
            sl = lambda off, w: prow[:, off + h * w: off + (h + 1) * w]
            q = _rope(sl(COL_RQ, RET_DK), cos, sin)
            k = _rope(sl(COL_RK, RET_DK), cos, sin) * (RET_DK ** -0.5)
            v = sl(COL_RV, RET_DV)
            gate = sl(COL_RG, RET_DV)
            gamma = jnp.exp(lg_ref[h][:1, :1])
            cols = _col_bcast([k, q], sel2)
            s_new = sr_ref[0, i, h] * gamma + cols[:, :RET_DV] * v
            sr_out[0, i, h] = s_new
            o = jnp.sum(cols[:, RET_DV:] * s_new, axis=0, keepdims=True)
            outs.append(_rms(o) * gro_ref[:, h * RET_DV:(h + 1) * RET_DV] * _silu(gate))
        larow = la_ref[i]
        for h in range(GLA_HEADS):
            sl = lambda off, w: prow[:, off + h * w: off + (h + 1) * w]
            q = sl(COL_GQ, GLA_DK) * (GLA_DK ** -0.5)
            k = sl(COL_GK, GLA_DK)
            v = sl(COL_GV, GLA_DV)
            gate = sl(COL_GR, GLA_DV)
            alpha = jnp.exp(larow[:, h * GLA_DK:(h + 1) * GLA_DK])
            cols = _col_bcast([k, q, alpha], sel3)
            s_new = sg_ref[0, i, h] * cols[:, 2 * GLA_DV:] + cols[:, :GLA_DV] * v
            sg_out[0, i, h] = s_new
            o = jnp.sum(cols[:, GLA_DV:2 * GLA_DV] * s_new, axis=0, keepdims=True)
            outs.append(_rms(o) * ggo_ref[:, h * GLA_DV:(h + 1) * GLA_DV] * _silu(gate))
        o_ref[i] = jnp.concatenate(outs, axis=-1)


def _state_step(p, la, cos, sin, lg, state_ret, state_gla, g_ret_out, g_gla_out, *, nb=2):
    m = p.shape[0]
    mix = RET_W + GLA_W
    return pl.pallas_call(
        _state_body,
        grid=(m // nb,),
        in_specs=[pl.BlockSpec((RET_HEADS, 8, 128), lambda b: (0, 0, 0)),
                  pl.BlockSpec((nb, 1, IN_MAIN), lambda b: (b, 0, 0)),
                  pl.BlockSpec((nb, 1, la.shape[1]), lambda b: (b, 0, 0)),
                  pl.BlockSpec((1, RET_DK // 2), lambda b: (0, 0)),
                  pl.BlockSpec((1, RET_DK // 2), lambda b: (0, 0)),
                  pl.BlockSpec((1, nb, RET_HEADS, RET_DK, RET_DV), lambda b: (0, b, 0, 0, 0)),
                  pl.BlockSpec((1, nb, GLA_HEADS, GLA_DK, GLA_DV), lambda b: (0, b, 0, 0, 0)),
                  pl.BlockSpec((1, RET_W), lambda b: (0, 0)),
                  pl.BlockSpec((1, GLA_W), lambda b: (0, 0)),
                  pl.BlockSpec((COL_ROWS, 2 * RET_DV), lambda b: (0, 0)),
                  pl.BlockSpec((COL_ROWS, 3 * GLA_DV), lambda b: (0, 0))],
        out_specs=[pl.BlockSpec((nb, 1, mix), lambda b: (b, 0, 0)),
                   pl.BlockSpec((1, nb, RET_HEADS, RET_DK, RET_DV), lambda b: (0, b, 0, 0, 0)),
                   pl.BlockSpec((1, nb, GLA_HEADS, GLA_DK, GLA_DV), lambda b: (0, b, 0, 0, 0))],
        out_shape=[jax.ShapeDtypeStruct((m, 1, mix), F32),
                   jax.ShapeDtypeStruct(state_ret.shape, F32),
                   jax.ShapeDtypeStruct(state_gla.shape, F32)],
        compiler_params=_params(("arbitrary",)),
        name="state_step",
    )(lg, p.reshape(m, 1, IN_MAIN), la.reshape(m, 1, la.shape[1]), cos, sin, state_ret, state_gla,
      g_ret_out.reshape(1, RET_W), g_gla_out.reshape(1, GLA_W),
      jnp.asarray(_col_selector(2, RET_DV), BF16), jnp.asarray(_col_selector(3, GLA_DV), BF16))


def _outproj_body(oa_ref, ob_ref, x_ref, gt_ref, sc_ref, sh_ref, g_ref, w_ref, x1_ref, h_ref):
    ka = oa_ref.shape[1]
    mix = _dot(oa_ref[...].astype(BF16), w_ref[:ka, :]) + _dot(ob_ref[...].astype(BF16), w_ref[ka:, :])
    x1 = x_ref[...] + gt_ref[0] * mix
    x1_ref[...] = x1
    h = _rms(x1) * g_ref[...]
    h_ref[...] = (h * (1.0 + sc_ref[0]) + sh_ref[0]).astype(BF16)


def _outproj(o_a, o_b, x, gt, sc, sh, g_ffn, w_out_bf16, *, tm, rows_per_mod):
    m, d = x.shape
    ka, kb = o_a.shape[1], o_b.shape[1]
    mod_spec = pl.BlockSpec((1, gt.shape[1], d), lambda i: ((i * tm) // rows_per_mod, 0, 0))
    return pl.pallas_call(
        _outproj_body,
        grid=(m // tm,),
        in_specs=[pl.BlockSpec((tm, ka), lambda i: (i, 0)),
                  pl.BlockSpec((tm, kb), lambda i: (i, 0)),
                  pl.BlockSpec((tm, d), lambda i: (i, 0)),
                  mod_spec, mod_spec, mod_spec,
                  pl.BlockSpec((1, d), lambda i: (0, 0)),
                  pl.BlockSpec((ka + kb, d), lambda i: (0, 0), pipeline_mode=pl.Buffered(1))],
        out_specs=[pl.BlockSpec((tm, d), lambda i: (i, 0)),
                   pl.BlockSpec((tm, d), lambda i: (i, 0))],
        out_shape=[jax.ShapeDtypeStruct((m, d), F32),
                   jax.ShapeDtypeStruct((m, d), BF16)],
        compiler_params=_params(("arbitrary",)),
        name="out_proj",
    )(o_a, o_b, x, gt, sc, sh, g_ffn.reshape(1, d), w_out_bf16)


def _conv_taps(u, prev):
    row8 = lax.broadcasted_iota(jnp.int32, (8, u.shape[1]), 0)
    r1 = pltpu.roll(u, 1, 0)
    r2 = pltpu.roll(u, 2, 0)
    h1 = jnp.where(row8 >= 1, r1[:8], prev[7:8])
    h2 = jnp.where(row8 >= 2, r2[:8], jnp.where(row8 == 1, prev[7:8], prev[6:7]))
    return jnp.concatenate([h1, r1[8:]], axis=0), jnp.concatenate([h2, r2[8:]], axis=0)


def _ffn_up_body(h_ref, wa_ref, wb_ref, cwa_ref, cwb_ref, cba_ref, cbb_ref, g_ref, cs_ref, tail_ref, *, tiles_per_seq):
    i, f = pl.program_id(0), pl.program_id(1)
    hb = h_ref[...]
    tm = hb.shape[0]
    first = (i % tiles_per_seq) == 0
    ucs = []
    for part, (w_ref, cw_ref, cb_ref) in enumerate(((wa_ref, cwa_ref, cba_ref), (wb_ref, cwb_ref, cbb_ref))):
        u = _dot(hb, w_ref[...].astype(BF16))
        prev = jnp.where(first, 0.0, tail_ref[part, f])
        u1, u2 = _conv_taps(u, prev)
        cw = cw_ref[...]
        ucs.append(cb_ref[...] + cw[0:1] * u2 + cw[1:2] * u1 + cw[2:3] * u)
        tail_ref[part, f] = u[tm - 8:]
        for r in range(CONV_W - 1):
            cs_ref[0, 0, r, part:part + 1, :] = u[tm - (CONV_W - 1) + r: tm - (CONV_W - 1) + r + 1]
    g_ref[...] = (_silu(ucs[0]) * ucs[1]).astype(g_ref.dtype)


def _ffn_up_prompt(h, w_up, conv_w, conv_b, *, batch, seq, tm, tf):
    m, d = h.shape
    ff = w_up.shape[1] // 2
    nf = ff // tf
    cb = conv_b.reshape(1, 2 * ff)
    tps = seq // tm
    body = functools.partial(_ffn_up_body, tiles_per_seq=tps)
    gate, tails = pl.pallas_call(
        body,
        grid=(m // tm, nf),
        in_specs=[pl.BlockSpec((tm, d), lambda i, f: (i, 0)),
                  pl.BlockSpec((d, tf), lambda i, f: (0, f)),
                  pl.BlockSpec((d, tf), lambda i, f: (0, nf + f)),
                  pl.BlockSpec((CONV_W, tf), lambda i, f: (0, f)),
                  pl.BlockSpec((CONV_W, tf), lambda i, f: (0, nf + f)),
                  pl.BlockSpec((1, tf), lambda i, f: (0, f)),
                  pl.BlockSpec((1, tf), lambda i, f: (0, nf + f))],
        out_specs=[pl.BlockSpec((tm, tf), lambda i, f: (i, f)),
                   pl.BlockSpec((1, 1, CONV_W - 1, 2, tf), lambda i, f: (i // tps, i % tps, 0, 0, f))],
        out_shape=[jax.ShapeDtypeStruct((m, ff), BF16),
                   jax.ShapeDtypeStruct((batch, tps, CONV_W - 1, 2, ff), F32)],
        scratch_shapes=[pltpu.VMEM((2, nf, 8, tf), F32)],
        compiler_params=_params(("arbitrary", "arbitrary")),
        name="ffn_up_prompt",
    )(h, w_up, w_up, conv_w, conv_w, cb, cb)
    return gate, tails[:, tps - 1]


def _ffn_up_step_body(h_ref, wa_ref, wb_ref, cwa_ref, cwb_ref, cba_ref, cbb_ref, st_ref, g_ref, cs_ref):
    hb = h_ref[...]
    ucs = []
    for part, (w_ref, cw_ref, cb_ref) in enumerate(((wa_ref, cwa_ref, cba_ref), (wb_ref, cwb_ref, cbb_ref))):
        u = _dot(hb, w_ref[...].astype(BF16))
        s0 = st_ref[:, 0, part, :]
        s1 = st_ref[:, 1, part, :]
        cw = cw_ref[...]
        ucs.append(cb_ref[...] + cw[0:1] * s0 + cw[1:2] * s1 + cw[2:3] * u)
        cs_ref[:, 0, part, :] = s1
        cs_ref[:, 1, part, :] = u
    g_ref[...] = (_silu(ucs[0]) * ucs[1]).astype(g_ref.dtype)


def _ffn_up_step(h, w_up, conv_w, conv_b, state_conv, *, tf):
    m, d = h.shape
    ff = w_up.shape[1] // 2
    nf = ff // tf
    cb = conv_b.reshape(1, 2 * ff)
    st = state_conv.reshape(m, CONV_W - 1, 2, ff)
    st_spec = pl.BlockSpec((m, CONV_W - 1, 2, tf), lambda f: (0, 0, 0, f))
    return pl.pallas_call(
        _ffn_up_step_body,
        grid=(nf,),
        in_specs=[pl.BlockSpec((m, d), lambda f: (0, 0)),
                  pl.BlockSpec((d, tf), lambda f: (0, f)),
                  pl.BlockSpec((d, tf), lambda f: (0, nf + f)),
                  pl.BlockSpec((CONV_W, tf), lambda f: (0, f)),
                  pl.BlockSpec((CONV_W, tf), lambda f: (0, nf + f)),
                  pl.BlockSpec((1, tf), lambda f: (0, f)),
                  pl.BlockSpec((1, tf), lambda f: (0, nf + f)),
                  st_spec],
        out_specs=[pl.BlockSpec((m, tf), lambda f: (0, f)), st_spec],
        out_shape=[jax.ShapeDtypeStruct((m, ff), BF16),
                   jax.ShapeDtypeStruct((m, CONV_W - 1, 2, ff), F32)],
        compiler_params=_params(("arbitrary",)),
        name="ffn_up_step",
    )(h, w_up, w_up, conv_w, conv_w, cb, cb, st)


def _ffn_down_body(g_ref, w_ref, x1_ref, gt_ref, gf_ref, y_ref):
    k = pl.program_id(1)
    part = _dot(g_ref[...], w_ref[...].astype(BF16))

    @pl.when(k == 0)
    def _():
        y_ref[...] = part

    @pl.when(k > 0)
    def _():
        y_ref[...] += part

    @pl.when(k == pl.num_programs(1) - 1)
    def _():
        x2 = x1_ref[...] + gt_ref[0] * y_ref[...]
        y_ref[...] = _rms(x2) * gf_ref[...]


def _ffn_down(g, w_down, x1, gt, g_final, *, tm, tk, rows_per_mod):
    m, d = x1.shape
    ff = g.shape[1]
    return pl.pallas_call(
        _ffn_down_body,
        grid=(m // tm, ff // tk),
        in_specs=[pl.BlockSpec((tm, tk), lambda i, k: (i, k)),
                  pl.BlockSpec((tk, d), lambda i, k: (k, 0)),
                  pl.BlockSpec((tm, d), lambda i, k: (i, 0), pipeline_mode=pl.Buffered(1)),
                  pl.BlockSpec((1, gt.shape[1], d), lambda i, k: ((i * tm) // rows_per_mod, 0, 0)),
                  pl.BlockSpec((1, d), lambda i, k: (0, 0))],
        out_specs=pl.BlockSpec((tm, d), lambda i, k: (i, 0)),
        out_shape=jax.ShapeDtypeStruct((m, d), F32),
        compiler_params=_params(("arbitrary", "arbitrary")),
        name="ffn_down",
    )(g, w_down, x1, gt, g_final.reshape(1, d))


def _rope_tables(pos):
    half = RET_DK // 2
    inv = ROPE_THETA ** (-jnp.arange(half, dtype=F32) / half)
    ang = pos.astype(F32)[:, None] * inv[None, :]
    return jnp.cos(ang), jnp.sin(ang)


def kernel(x_prompt, x_sample, c_prompt, c_sample, state_ret, state_gla, state_conv, w_ada, b_ada, g_attn, w_in, w_a2, b_a2, g_ret_out, g_gla_out, w_out, g_ffn, w_up, conv_w, conv_b, w_down, g_final):
    bp, t_p, d = x_prompt.shape
    bs, t_s, _ = x_sample.shape
    assert t_s == 1 and w_ada.shape[0] == 1
    mp = bp * t_p
    w_ada, b_ada, g_attn, w_in, w_a2, b_a2, g_ret_out, g_gla_out, w_out, g_ffn, w_up, conv_w, conv_b, w_down = (
        a[0] for a in (w_ada, b_ada, g_attn, w_in, w_a2, b_a2, g_ret_out, g_gla_out, w_out, g_ffn, w_up, conv_w, conv_b, w_down))

    mod = _ada(jnp.concatenate([c_prompt, c_sample], axis=0), w_ada, b_ada)
    sh1p, sc1p, gt1p, sh2p, sc2p, gt2p = (mod[:bp, i * d:(i + 1) * d].reshape(bp, 1, d) for i in range(6))
    sh1s, sc1s, gt1s, sh2s, sc2s, gt2s = (mod[bp:, i * d:(i + 1) * d].reshape(1, bs, d) for i in range(6))

    lg = jnp.log1p(-jnp.exp2(-5.0 - jnp.arange(RET_HEADS, dtype=F32)))
    lg = jnp.broadcast_to(lg[:, None, None], (RET_HEADS, 8, 128))
    cos_p, sin_p = _rope_tables(jnp.arange(t_p, dtype=jnp.int32))
    cos_s, sin_s = _rope_tables(PAST_LEN + jnp.arange(t_s, dtype=jnp.int32))

    xp = x_prompt.reshape(mp, d)
    p_p, la_p = _inproj(xp, sc1p, sh1p, g_attn, w_in, w_a2, b_a2, tm=1024, rows_per_mod=t_p)
    o_ret, s_ret_p = _retention_prompt(p_p, cos_p, sin_p, lg, g_ret_out, batch=bp, seq=t_p, chunk=256)
    o_gla, s_gla_p = _gla_prompt(p_p, la_p, g_gla_out, batch=bp, seq=t_p, chunk=128)
    w_out_b = w_out.astype(BF16)
    x1_p, h2_p = _outproj(o_ret, o_gla, xp, gt1p, sc2p, sh2p, g_ffn, w_out_b, tm=512, rows_per_mod=t_p)
    g_p, cs_p = _ffn_up_prompt(h2_p, w_up, conv_w, conv_b, batch=bp, seq=t_p, tm=1024, tf=512)
    y_p = _ffn_down(g_p, w_down, x1_p, gt2p, g_final, tm=1024, tk=512, rows_per_mod=t_p)

    xs = x_sample.reshape(bs, d)
    p_s, la_s = _inproj(xs, sc1s, sh1s, g_attn, w_in, w_a2, b_a2, tm=bs, rows_per_mod=bs)
    o_s, s_ret_s, s_gla_s = _state_step(p_s, la_s, cos_s, sin_s, lg, state_ret, state_gla, g_ret_out, g_gla_out)
    o_s = o_s.reshape(bs, RET_W + GLA_W)
    x1_s, h2_s = _outproj(o_s[:, :RET_W], o_s[:, RET_W:], xs, gt1s, sc2s, sh2s, g_ffn, w_out_b, tm=bs, rows_per_mod=bs)
    g_s, cs_s = _ffn_up_step(h2_s, w_up, conv_w, conv_b, state_conv[0], tf=512)
    y_s = _ffn_down(g_s, w_down, x1_s, gt2s, g_final, tm=bs, tk=512, rows_per_mod=bs)

    return (y_p.reshape(bp, t_p, d), y_s.reshape(bs, t_s, d),
            s_ret_p[None], s_ret_s, s_gla_p[None], s_gla_s,
            cs_p.reshape(1, bp, CONV_W - 1, -1), cs_s.reshape(1, bs, CONV_W - 1, -1))
```

```python
import functools

import numpy as np
import jax
import jax.numpy as jnp
from jax import lax
from jax.experimental import pallas as pl
from jax.experimental.pallas import tpu as pltpu

F32 = jnp.float32
BF16 = jnp.bfloat16

RET_HEADS = 4
RET_DK = 256
RET_DV = 256
GLA_HEADS = 4
GLA_DK = 128
GLA_DV = 256
GLA_RANK = 16
GLA_TAU = 16.0
ROPE_THETA = 10000.0
PAST_LEN = 16384
CONV_W = 3
EPS = 1e-6

RET_W = RET_HEADS * RET_DV
GLA_W = GLA_HEADS * GLA_DV
QK_W = 2 * RET_HEADS * RET_DK
R_RV = 0
R_RG = R_RV + RET_W
R_GQ = R_RG + RET_W
R_GK = R_GQ + GLA_HEADS * GLA_DK
R_GV = R_GK + GLA_HEADS * GLA_DK
R_GR = R_GV + GLA_W
REST_W = R_GR + GLA_W
IN_MAIN = QK_W + REST_W

VMEM_LIMIT_BYTES = 56 * 1024 * 1024


def _params(semantics):
    return pltpu.CompilerParams(dimension_semantics=semantics, vmem_limit_bytes=VMEM_LIMIT_BYTES)


def _dot(a, b):
    return jnp.dot(a, b, preferred_element_type=F32)


def _dot_nt(a, b):
    return lax.dot_general(a, b, (((1,), (1,)), ((), ())), preferred_element_type=F32)


def _dot_tn(a, b):
    return lax.dot_general(a, b, (((0,), (0,)), ((), ())), preferred_element_type=F32)


def _silu(x):
    return x * jax.nn.sigmoid(x)


def _rms(x):
    return x * lax.rsqrt(jnp.mean(x * x, axis=-1, keepdims=True) + EPS)


def _split3(x):
    hi = x.astype(BF16)
    r = x - hi.astype(F32)
    mid = r.astype(BF16)
    lo = (r - mid.astype(F32)).astype(BF16)
    return hi, mid, lo


def _ada_body(c_ref, w_ref, b_ref, o_ref):
    s = _silu(c_ref[...]).astype(BF16)
    o_ref[...] = _dot(s, w_ref[...].astype(BF16)) + b_ref[...]


def _ada(c_all, w_ada, b_ada, tn=512):
    r, d = c_all.shape
    n = w_ada.shape[1]
    return pl.pallas_call(
        _ada_body,
        grid=(n // tn,),
        in_specs=[pl.BlockSpec((r, d), lambda j: (0, 0)),
                  pl.BlockSpec((d, tn), lambda j: (0, j)),
                  pl.BlockSpec((1, tn), lambda j: (0, j))],
        out_specs=pl.BlockSpec((r, tn), lambda j: (0, j)),
        out_shape=jax.ShapeDtypeStruct((r, n), F32),
        compiler_params=_params(("arbitrary",)),
        name="ada_mod",
    )(c_all, w_ada, b_ada.reshape(1, n))


def _inproj_body(x_ref, sc_ref, sh_ref, g_ref, w_ref, wga_ref, wa2_ref, ba2_ref, cos_ref, sin_ref,
                 qk_ref, rest_ref, la_ref, h_ref, *, n_qk):
    j = pl.program_id(1)

    @pl.when(j == 0)
    def _():
        h = _rms(x_ref[...]) * g_ref[...]
        h = h * (1.0 + sc_ref[0]) + sh_ref[0]
        hb = h.astype(BF16)
        h_ref[...] = hb
        ga = _dot(hb, wga_ref[...])
        z = _dot(ga.astype(BF16), wa2_ref[...].astype(BF16)) + ba2_ref[...]
        la_ref[...] = (jnp.minimum(z, 0.0) - jnp.log1p(jnp.exp(-jnp.abs(z)))) * (1.0 / GLA_TAU)

    @pl.when(j < n_qk)
    def _():
        acc = _dot(h_ref[...], w_ref[...])
        cos, sin = cos_ref[...], sin_ref[...]
        half = RET_DK // 2
        scale = jnp.where(j >= n_qk // 2, RET_DK ** -0.5, 1.0)
        outs = []
        for hh in range(acc.shape[1] // RET_DK):
            x1 = acc[:, hh * RET_DK: hh * RET_DK + half]
            x2 = acc[:, hh * RET_DK + half: (hh + 1) * RET_DK]
            outs += [x1 * cos - x2 * sin, x1 * sin + x2 * cos]
        qk_ref[...] = (jnp.concatenate(outs, axis=-1) * scale).astype(qk_ref.dtype)

    @pl.when(j >= n_qk)
    def _():
        rest_ref[...] = _dot(h_ref[...], w_ref[...])


def _inproj(x, sc, sh, g_attn, w_in_b, w_a2, b_a2, cos, sin, *, tm, rows_per_mod, tn=512):
    m, d = x.shape
    nq = w_a2.shape[1]
    n_qk = QK_W // tn
    tab_tiles = cos.shape[0] // tm
    mod_spec = pl.BlockSpec((1, sc.shape[1], d), lambda i, j: ((i * tm) // rows_per_mod, 0, 0))
    tab_spec = pl.BlockSpec((tm, RET_DK // 2), lambda i, j: (i % tab_tiles, 0))
    return pl.pallas_call(
        functools.partial(_inproj_body, n_qk=n_qk),
        grid=(m // tm, IN_MAIN // tn),
        in_specs=[pl.BlockSpec((tm, d), lambda i, j: (i, 0)),
                  mod_spec, mod_spec,
                  pl.BlockSpec((1, d), lambda i, j: (0, 0)),
                  pl.BlockSpec((d, tn), lambda i, j: (0, j)),
                  pl.BlockSpec((d, GLA_RANK), lambda i, j: (0, 0)),
                  pl.BlockSpec((GLA_RANK, nq), lambda i, j: (0, 0)),
                  pl.BlockSpec((1, nq), lambda i, j: (0, 0)),
                  tab_spec, tab_spec],
        out_specs=[pl.BlockSpec((tm, tn), lambda i, j: (i, jnp.minimum(j, n_qk - 1))),
                   pl.BlockSpec((tm, tn), lambda i, j: (i, jnp.maximum(j - n_qk, 0))),
                   pl.BlockSpec((tm, nq), lambda i, j: (i, 0))],
        out_shape=[jax.ShapeDtypeStruct((m, QK_W), BF16),
                   jax.ShapeDtypeStruct((m, REST_W), F32),
                   jax.ShapeDtypeStruct((m, nq), F32)],
        scratch_shapes=[pltpu.VMEM((tm, d), BF16)],
        compiler_params=_params(("arbitrary", "arbitrary")),
        name="in_proj",
    )(x, sc, sh, g_attn.reshape(1, d), w_in_b, w_in_b[:, IN_MAIN:], w_a2, b_a2.reshape(1, nq), cos, sin)


def _ret_log_gamma():
    return jnp.log1p(-jnp.exp2(-5.0 - jnp.arange(RET_HEADS, dtype=F32)))


def _ret_decay_matrix(chunk):
    idx = jnp.arange(chunk, dtype=F32)
    rel = idx[:, None] - idx[None, :]
    lg = _ret_log_gamma()
    return jnp.where(rel[None] >= 0, jnp.exp(jnp.maximum(rel, 0.0)[None] * lg[:, None, None]), 0.0)


def _ret_body(dmat_ref, lg_ref, q_ref, k_ref, v_ref, g_ref, gout_ref, o_ref, sout_ref, s_ref):
    c = pl.program_id(1)
    cl = q_ref.shape[0]

    @pl.when(c == 0)
    def _():
        s_ref[...] = jnp.zeros_like(s_ref)

    idx = lax.broadcasted_iota(jnp.int32, (cl, 1), 0).astype(F32)
    for h in range(RET_HEADS):
        sl = slice(h * RET_DV, (h + 1) * RET_DV)
        lg = lg_ref[h][:1, :1]
        q_dec = jnp.exp((idx + 1.0) * lg)
        k_dec = jnp.exp((cl - 1.0 - idx) * lg)
        c_dec = jnp.exp(cl * lg)
        qb, kb = q_ref[:, sl], k_ref[:, sl]
        v = v_ref[:, sl]
        vb = v.astype(BF16)
        s_old = s_ref[h]
        scores = _dot_nt(qb, kb) * dmat_ref[h]
        o = _dot(scores.astype(BF16), vb) + _dot(qb, s_old.astype(BF16)) * q_dec
        s_new = s_old * c_dec + _dot_tn(kb, (v * k_dec).astype(BF16))
        s_ref[h] = s_new
        o_ref[:, sl] = (_rms(o) * gout_ref[:, sl] * _silu(g_ref[:, sl])).astype(o_ref.dtype)

    @pl.when(c == pl.num_programs(1) - 1)
    def _():
        sout_ref[0] = s_ref[...]


def _retention_prompt(qk, rest, g_out, *, batch, seq, chunk):
    nc = seq // chunk
    lg = jnp.broadcast_to(_ret_log_gamma()[:, None, None], (RET_HEADS, 8, 128))
    row = lambda blk: (lambda b, c: (b * nc + c, blk))
    return pl.pallas_call(
        _ret_body,
        grid=(batch, nc),
        in_specs=[pl.BlockSpec((RET_HEADS, chunk, chunk), lambda b, c: (0, 0, 0)),
                  pl.BlockSpec((RET_HEADS, 8, 128), lambda b, c: (0, 0, 0)),
                  pl.BlockSpec((chunk, RET_W), row(0)),
                  pl.BlockSpec((chunk, RET_W), row(1)),
                  pl.BlockSpec((chunk, RET_W), row(R_RV // RET_W)),
                  pl.BlockSpec((chunk, RET_W), row(R_RG // RET_W)),
                  pl.BlockSpec((1, RET_W), lambda b, c: (0, 0))],
        out_specs=[pl.BlockSpec((chunk, RET_W), row(0)),
                   pl.BlockSpec((1, RET_HEADS, RET_DK, RET_DV), lambda b, c: (b, 0, 0, 0))],
        out_shape=[jax.ShapeDtypeStruct((batch * seq, RET_W), BF16),
                   jax.ShapeDtypeStruct((batch, RET_HEADS, RET_DK, RET_DV), F32)],
        scratch_shapes=[pltpu.VMEM((RET_HEADS, RET_DK, RET_DV), F32)],
        compiler_params=_params(("arbitrary", "arbitrary")),
        name="retention_prompt",
    )(_ret_decay_matrix(chunk), lg, qk, qk, rest, rest, g_out.reshape(1, RET_W))


def _gla_level_map(c):
    t = np.arange(c)[:, None]
    s = np.arange(c)[None, :]
    x = np.bitwise_xor(t, s)
    lev = np.floor(np.log2(np.maximum(x, 1))).astype(np.int32)
    lev = np.where(t == s, -1, lev)
    lev = np.where(t < s, -2, lev)
    return lev.astype(np.int32)


def _gla_chunk(q, k, la, lev, row):
    cl = q.shape[0]
    p = la
    tot = la
    a = jnp.zeros((cl, cl), F32)
    level = 0
    half = 1
    while half < cl:
        upper = (row & half) != 0
        e = jnp.where(upper, p, p - tot)
        z = (jnp.where(upper, q, k) * jnp.exp(-jnp.abs(e))).astype(BF16)
        a = jnp.where(lev == level, _dot_nt(z, z), a)
        partner = jnp.where(upper, pltpu.roll(tot, half, 0), pltpu.roll(tot, cl - half, 0))
        p = p + jnp.where(upper, partner, 0.0)
        tot = tot + partner
        half *= 2
        level += 1
    a = jnp.where(lev == -1, jnp.sum(q * k, axis=-1, keepdims=True), a)
    return a, p, tot


def _gla_body(q_ref, k_ref, v_ref, g_ref, la_ref, lev_ref, gout_ref, o_ref, sout_ref, st_ref):
    c = pl.program_id(1)
    cl = q_ref.shape[0]

    @pl.when(c == 0)
    def _():
        st_ref[...] = jnp.zeros_like(st_ref)

    lev = lev_ref[...]
    row = lax.broadcasted_iota(jnp.int32, (cl, GLA_DK), 0)
    for h in range(GLA_HEADS):
        ks = slice(h * GLA_DK, (h + 1) * GLA_DK)
        vs = slice(h * GLA_DV, (h + 1) * GLA_DV)
        q = q_ref[:, ks] * (GLA_DK ** -0.5)
        k = k_ref[:, ks]
        a, p, tot = _gla_chunk(q, k, la_ref[:, ks], lev, row)
        vb = v_ref[:, vs].astype(BF16)
        st_old = st_ref[h]
        qt = (q * jnp.exp(p)).astype(BF16)
        o = _dot(a.astype(BF16), vb) + _dot_nt(qt, st_old.astype(BF16))
        kt = (k * jnp.exp(tot - p)).astype(BF16)
        st_ref[h] = st_old * jnp.exp(tot[0:1, :]) + _dot_tn(vb, kt)
        o_ref[:, vs] = (_rms(o) * gout_ref[:, vs] * _silu(g_ref[:, vs])).astype(o_ref.dtype)

    @pl.when(c == pl.num_programs(1) - 1)
    def _():
        for h in range(GLA_HEADS):
            sout_ref[0, h] = st_ref[h].T


def _gla_prompt(rest, la, g_out, *, batch, seq, chunk):
    nc = seq // chunk
    lev = jnp.asarray(_gla_level_map(chunk))
    qk_w = GLA_HEADS * GLA_DK
    row = lambda blk: (lambda b, c: (b * nc + c, blk))
    return pl.pallas_call(
        _gla_body,
        grid=(batch, nc),
        in_specs=[pl.BlockSpec((chunk, qk_w), row(R_GQ // qk_w)),
                  pl.BlockSpec((chunk, qk_w), row(R_GK // qk_w)),
                  pl.BlockSpec((chunk, GLA_W), row(R_GV // GLA_W)),
                  pl.BlockSpec((chunk, GLA_W), row(R_GR // GLA_W)),
                  pl.BlockSpec((chunk, qk_w), row(0)),
                  pl.BlockSpec((chunk, chunk), lambda b, c: (0, 0)),
                  pl.BlockSpec((1, GLA_W), lambda b, c: (0, 0))],
        out_specs=[pl.BlockSpec((chunk, GLA_W), row(0)),
                   pl.BlockSpec((1, GLA_HEADS, GLA_DK, GLA_DV), lambda b, c: (b, 0, 0, 0))],
        out_shape=[jax.ShapeDtypeStruct((batch * seq, GLA_W), BF16),
                   jax.ShapeDtypeStruct((batch, GLA_HEADS, GLA_DK, GLA_DV), F32)],
        scratch_shapes=[pltpu.VMEM((GLA_HEADS, GLA_DV, GLA_DK), F32)],
        compiler_params=_params(("arbitrary", "arbitrary")),
        name="gla_prompt",
    )(rest, rest, rest, rest, la, lev, g_out.reshape(1, GLA_W))


COL_ROWS = 16


def _col_selector(nvec, width):
    sel = np.zeros((COL_ROWS, nvec * width), np.float32)
    for i in range(nvec):
        sel[3 * i:3 * i + 3, i * width:(i + 1) * width] = 1.0
    return sel


def _col_bcast(rows, sel):
    n = rows[0].shape[1]
    ridx = lax.broadcasted_iota(jnp.int32, (COL_ROWS, n), 0)
    lhs = jnp.zeros((COL_ROWS, n), F32)
    for i, r in enumerate(rows):
        for j, piece in enumerate(_split3(r)):
            lhs = jnp.where(ridx == 3 * i + j, piece.astype(F32), lhs)
    return _dot_tn(lhs.astype(BF16), sel)


def _state_body(lg_ref, qk_ref, rest_ref, la_ref, sr_ref, sg_ref, gro_ref, ggo_ref, sel2_ref, sel3_ref,
                o_ref, sr_out, sg_out):
    nb = qk_ref.shape[0]
    sel2, sel3 = sel2_ref[...], sel3_ref[...]
    for i in range(nb):
        qkrow = qk_ref[i]
        rrow = rest_ref[i]
        outs = []
        for h in range(RET_HEADS):
            q = qkrow[:, h * RET_DK:(h + 1) * RET_DK]
            k = qkrow[:, QK_W // 2 + h * RET_DK: QK_W // 2 + (h + 1) * RET_DK]
            v = rrow[:, R_RV + h * RET_DV: R_RV + (h + 1) * RET_DV]
            gate = rrow[:, R_RG + h * RET_DV: R_RG + (h + 1) * RET_DV]
            gamma = jnp.exp(lg_ref[h][:1, :1])
            cols = _col_bcast([k, q], sel2)
            s_new = sr_ref[0, i, h] * gamma + cols[:, :RET_DV] * v
            sr_out[0, i, h] = s_new
            o = jnp.sum(cols[:, RET_DV:] * s_new, axis=0, keepdims=True)
            outs.append(_rms(o) * gro_ref[:, h * RET_DV:(h + 1) * RET_DV] * _silu(gate))
        larow = la_ref[i]
        for h in range(GLA_HEADS):
            q = rrow[:, R_GQ + h * GLA_DK: R_GQ + (h + 1) * GLA_DK] * (GLA_DK ** -0.5)
            k = rrow[:, R_GK + h * GLA_DK: R_GK + (h + 1) * GLA_DK]
            v = rrow[:, R_GV + h * GLA_DV: R_GV + (h + 1) * GLA_DV]
            gate = rrow[:, R_GR + h * GLA_DV: R_GR + (h + 1) * GLA_DV]
            alpha = jnp.exp(larow[:, h * GLA_DK:(h + 1) * GLA_DK])
            cols = _col_bcast([k, q, alpha], sel3)
            s_new = sg_ref[0, i, h] * cols[:, 2 * GLA_DV:] + cols[:, :GLA_DV] * v
            sg_out[0, i, h] = s_new
            o = jnp.sum(cols[:, GLA_DV:2 * GLA_DV] * s_new, axis=0, keepdims=True)
            outs.append(_rms(o) * ggo_ref[:, h * GLA_DV:(h + 1) * GLA_DV] * _silu(gate))
        o_ref[i] = jnp.concatenate(outs, axis=-1)


def _state_step(qk, rest, la, state_ret, state_gla, g_ret_out, g_gla_out, *, nb=2):
    m = qk.shape[0]
    mix = RET_W + GLA_W
    lg = jnp.broadcast_to(_ret_log_gamma()[:, None, None], (RET_HEADS, 8, 128))
    return pl.pallas_call(
        _state_body,
        grid=(m // nb,),
        in_specs=[pl.BlockSpec((RET_HEADS, 8, 128), lambda b: (0, 0, 0)),
                  pl.BlockSpec((nb, 1, QK_W), lambda b: (b, 0, 0)),
                  pl.BlockSpec((nb, 1, REST_W), lambda b: (b, 0, 0)),
                  pl.BlockSpec((nb, 1, la.shape[1]), lambda b: (b, 0, 0)),
                  pl.BlockSpec((1, nb, RET_HEADS, RET_DK, RET_DV), lambda b: (0, b, 0, 0, 0)),
                  pl.BlockSpec((1, nb, GLA_HEADS, GLA_DK, GLA_DV), lambda b: (0, b, 0, 0, 0)),
                  pl.BlockSpec((1, RET_W), lambda b: (0, 0)),
                  pl.BlockSpec((1, GLA_W), lambda b: (0, 0)),
                  pl.BlockSpec((COL_ROWS, 2 * RET_DV), lambda b: (0, 0)),
                  pl.BlockSpec((COL_ROWS, 3 * GLA_DV), lambda b: (0, 0))],
        out_specs=[pl.BlockSpec((nb, 1, mix), lambda b: (b, 0, 0)),
                   pl.BlockSpec((1, nb, RET_HEADS, RET_DK, RET_DV), lambda b: (0, b, 0, 0, 0)),
                   pl.BlockSpec((1, nb, GLA_HEADS, GLA_DK, GLA_DV), lambda b: (0, b, 0, 0, 0))],
        out_shape=[jax.ShapeDtypeStruct((m, 1, mix), F32),
                   jax.ShapeDtypeStruct(state_ret.shape, F32),
                   jax.ShapeDtypeStruct(state_gla.shape, F32)],
        compiler_params=_params(("arbitrary",)),
        name="state_step",
    )(lg, qk.astype(F32).reshape(m, 1, QK_W), rest.reshape(m, 1, REST_W), la.reshape(m, 1, la.shape[1]),
      state_ret, state_gla, g_ret_out.reshape(1, RET_W), g_gla_out.reshape(1, GLA_W),
      jnp.asarray(_col_selector(2, RET_DV), BF16), jnp.asarray(_col_selector(3, GLA_DV), BF16))


def _outproj_body(oa_ref, ob_ref, x_ref, gt_ref, sc_ref, sh_ref, g_ref, w_ref, x1_ref, h_ref):
    ka = oa_ref.shape[1]
    mix = _dot(oa_ref[...].astype(BF16), w_ref[:ka, :]) + _dot(ob_ref[...].astype(BF16), w_ref[ka:, :])
    x1 = x_ref[...] + gt_ref[0] * mix
    x1_ref[...] = x1
    h = _rms(x1) * g_ref[...]
    h_ref[...] = (h * (1.0 + sc_ref[0]) + sh_ref[0]).astype(BF16)


def _outproj(o_a, o_b, x, gt, sc, sh, g_ffn, w_out_bf16, *, tm, rows_per_mod):
    m, d = x.shape
    ka, kb = o_a.shape[1], o_b.shape[1]
    mod_spec = pl.BlockSpec((1, gt.shape[1], d), lambda i: ((i * tm) // rows_per_mod, 0, 0))
    return pl.pallas_call(
        _outproj_body,
        grid=(m // tm,),
        in_specs=[pl.BlockSpec((tm, ka), lambda i: (i, 0)),
                  pl.BlockSpec((tm, kb), lambda i: (i, 0)),
                  pl.BlockSpec((tm, d), lambda i: (i, 0)),
                  mod_spec, mod_spec, mod_spec,
                  pl.BlockSpec((1, d), lambda i: (0, 0)),
                  pl.BlockSpec((ka + kb, d), lambda i: (0, 0), pipeline_mode=pl.Buffered(1))],
        out_specs=[pl.BlockSpec((tm, d), lambda i: (i, 0)),
                   pl.BlockSpec((tm, d), lambda i: (i, 0))],
        out_shape=[jax.ShapeDtypeStruct((m, d), F32),
                   jax.ShapeDtypeStruct((m, d), BF16)],
        compiler_params=_params(("arbitrary",)),
        name="out_proj",
    )(o_a, o_b, x, gt, sc, sh, g_ffn.reshape(1, d), w_out_bf16)


def _conv_taps(u, prev):
    row8 = lax.broadcasted_iota(jnp.int32, (8, u.shape[1]), 0)
    r1 = pltpu.roll(u, 1, 0)
    r2 = pltpu.roll(u, 2, 0)
    h1 = jnp.where(row8 >= 1, r1[:8], prev[7:8])
    h2 = jnp.where(row8 >= 2, r2[:8], jnp.where(row8 == 1, prev[7:8], prev[6:7]))
    return jnp.concatenate([h1, r1[8:]], axis=0), jnp.concatenate([h2, r2[8:]], axis=0)


def _ffn_up_body(h_ref, wa_ref, wb_ref, cwa_ref, cwb_ref, cba_ref, cbb_ref, g_ref, cs_ref, tail_ref, *, tiles_per_seq):
    i, f = pl.program_id(0), pl.program_id(1)
    hb = h_ref[...]
    tm = hb.shape[0]
    first = (i % tiles_per_seq) == 0
    ucs = []
    for part, (w_ref, cw_ref, cb_ref) in enumerate(((wa_ref, cwa_ref, cba_ref), (wb_ref, cwb_ref, cbb_ref))):
        u = _dot(hb, w_ref[...].astype(BF16))
        prev = jnp.where(first, 0.0, tail_ref[part, f])
        u1, u2 = _conv_taps(u, prev)
        cw = cw_ref[...]
        ucs.append(cb_ref[...] + cw[0:1] * u2 + cw[1:2] * u1 + cw[2:3] * u)
        tail_ref[part, f] = u[tm - 8:]
        for r in range(CONV_W - 1):
            cs_ref[0, 0, r, part:part + 1, :] = u[tm - (CONV_W - 1) + r: tm - (CONV_W - 1) + r + 1]
    g_ref[...] = (_silu(ucs[0]) * ucs[1]).astype(g_ref.dtype)


def _ffn_up_prompt(h, w_up, conv_w, conv_b, *, batch, seq, tm, tf):
    m, d = h.shape
    ff = w_up.shape[1] // 2
    nf = ff // tf
    cb = conv_b.reshape(1, 2 * ff)
    tps = seq // tm
    body = functools.partial(_ffn_up_body, tiles_per_seq=tps)
    gate, tails = pl.pallas_call(
        body,
        grid=(m // tm, nf),
        in_specs=[pl.BlockSpec((tm, d), lambda i, f: (i, 0)),
                  pl.BlockSpec((d, tf), lambda i, f: (0, f)),
                  pl.BlockSpec((d, tf), lambda i, f: (0, nf + f)),
                  pl.BlockSpec((CONV_W, tf), lambda i, f: (0, f)),
                  pl.BlockSpec((CONV_W, tf), lambda i, f: (0, nf + f)),
                  pl.BlockSpec((1, tf), lambda i, f: (0, f)),
                  pl.BlockSpec((1, tf), lambda i, f: (0, nf + f))],
        out_specs=[pl.BlockSpec((tm, tf), lambda i, f: (i, f)),
                   pl.BlockSpec((1, 1, CONV_W - 1, 2, tf), lambda i, f: (i // tps, i % tps, 0, 0, f))],
        out_shape=[jax.ShapeDtypeStruct((m, ff), BF16),
                   jax.ShapeDtypeStruct((batch, tps, CONV_W - 1, 2, ff), F32)],
        scratch_shapes=[pltpu.VMEM((2, nf, 8, tf), F32)],
        compiler_params=_params(("arbitrary", "arbitrary")),
        name="ffn_up_prompt",
    )(h, w_up, w_up, conv_w, conv_w, cb, cb)
    return gate, tails[:, tps - 1]


def _ffn_up_step_body(h_ref, w_ref, cw_ref, cb_ref, st_ref, g_ref, cs_ref, uca_ref, *, nf):
    j = pl.program_id(0)
    u = _dot(h_ref[...], w_ref[...].astype(BF16))
    s1 = st_ref[0, :, 1, :]
    cw = cw_ref[...]
    uc = cb_ref[...] + cw[0:1] * st_ref[0, :, 0, :] + cw[1:2] * s1 + cw[2:3] * u
    cs_ref[0, :, 0, :] = s1
    cs_ref[0, :, 1, :] = u

    @pl.when(j < nf)
    def _():
        uca_ref[j] = uc

    @pl.when(j >= nf)
    def _():
        g_ref[...] = (_silu(uca_ref[j - nf]) * uc).astype(g_ref.dtype)


def _ffn_up_step(h, w_up, conv_w, conv_b, state_conv, *, tf):
    m, d = h.shape
    ff = w_up.shape[1] // 2
    nf = ff // tf
    st_spec = pl.BlockSpec((1, m, CONV_W - 1, tf), lambda j: (0, 0, 0, j))
    return pl.pallas_call(
        functools.partial(_ffn_up_step_body, nf=nf),
        grid=(2 * nf,),
        in_specs=[pl.BlockSpec((m, d), lambda j: (0, 0)),
                  pl.BlockSpec((d, tf), lambda j: (0, j)),
                  pl.BlockSpec((CONV_W, tf), lambda j: (0, j)),
                  pl.BlockSpec((1, tf), lambda j: (0, j)),
                  st_spec],
        out_specs=[pl.BlockSpec((m, tf), lambda j: (0, jnp.maximum(j - nf, 0))), st_spec],
        out_shape=[jax.ShapeDtypeStruct((m, ff), BF16),
                   jax.ShapeDtypeStruct(state_conv.shape, F32)],
        scratch_shapes=[pltpu.VMEM((nf, m, tf), F32)],
        compiler_params=_params(("arbitrary",)),
        name="ffn_up_step",
    )(h, w_up, conv_w, conv_b.reshape(1, 2 * ff), state_conv)


def _ffn_down_body(g_ref, w_ref, x1_ref, gt_ref, gf_ref, y_ref):
    k = pl.program_id(1)

    @pl.when(k == 0)
    def _():
        y_ref[...] = jnp.zeros_like(y_ref)

    y_ref[...] += _dot(g_ref[...], w_ref[...].astype(BF16))

    @pl.when(k == pl.num_programs(1) - 1)
    def _():
        x2 = x1_ref[...] + gt_ref[0] * y_ref[...]
        y_ref[...] = _rms(x2) * gf_ref[...]


def _ffn_down(g, w_down, x1, gt, g_final, *, tm, tk, rows_per_mod):
    m, d = x1.shape
    ff = g.shape[1]
    return pl.pallas_call(
        _ffn_down_body,
        grid=(m // tm, ff // tk),
        in_specs=[pl.BlockSpec((tm, tk), lambda i, k: (i, k)),
                  pl.BlockSpec((tk, d), lambda i, k: (k, 0)),
                  pl.BlockSpec((tm, d), lambda i, k: (i, 0)),
                  pl.BlockSpec((1, gt.shape[1], d), lambda i, k: ((i * tm) // rows_per_mod, 0, 0)),
                  pl.BlockSpec((1, d), lambda i, k: (0, 0))],
        out_specs=pl.BlockSpec((tm, d), lambda i, k: (i, 0)),
        out_shape=jax.ShapeDtypeStruct((m, d), F32),
        compiler_params=_params(("arbitrary", "arbitrary")),
        name="ffn_down",
    )(g, w_down, x1, gt, g_final.reshape(1, d))


def _rope_tables(pos):
    half = RET_DK // 2
    inv = ROPE_THETA ** (-jnp.arange(half, dtype=F32) / half)
    ang = pos.astype(F32)[:, None] * inv[None, :]
    return jnp.cos(ang), jnp.sin(ang)


def kernel(x_prompt, x_sample, c_prompt, c_sample, state_ret, state_gla, state_conv, w_ada, b_ada, g_attn, w_in, w_a2, b_a2, g_ret_out, g_gla_out, w_out, g_ffn, w_up, conv_w, conv_b, w_down, g_final):
    bp, t_p, d = x_prompt.shape
    bs, t_s, _ = x_sample.shape
    assert t_s == 1 and w_ada.shape[0] == 1
    mp = bp * t_p
    w_ada, b_ada, g_attn, w_in, w_a2, b_a2, g_ret_out, g_gla_out, w_out, g_ffn, w_up, conv_w, conv_b, w_down = (
        a[0] for a in (w_ada, b_ada, g_attn, w_in, w_a2, b_a2, g_ret_out, g_gla_out, w_out, g_ffn, w_up, conv_w, conv_b, w_down))

    mod = _ada(jnp.concatenate([c_prompt, c_sample], axis=0), w_ada, b_ada)
    sh1p, sc1p, gt1p, sh2p, sc2p, gt2p = (mod[:bp, i * d:(i + 1) * d].reshape(bp, 1, d) for i in range(6))
    sh1s, sc1s, gt1s, sh2s, sc2s, gt2s = (mod[bp:, i * d:(i + 1) * d].reshape(1, bs, d) for i in range(6))

    cos_p, sin_p = _rope_tables(jnp.arange(t_p, dtype=jnp.int32))
    cos_s, sin_s = (jnp.broadcast_to(t, (bs, RET_DK // 2)) for t in _rope_tables(PAST_LEN + jnp.arange(t_s, dtype=jnp.int32)))
    w_in_b = w_in.astype(BF16)
    w_out_b = w_out.astype(BF16)

    xp = x_prompt.reshape(mp, d)
    qk_p, rest_p, la_p = _inproj(xp, sc1p, sh1p, g_attn, w_in_b, w_a2, b_a2, cos_p, sin_p, tm=1024, rows_per_mod=t_p)
    o_ret, s_ret_p = _retention_prompt(qk_p, rest_p, g_ret_out, batch=bp, seq=t_p, chunk=256)
    o_gla, s_gla_p = _gla_prompt(rest_p, la_p, g_gla_out, batch=bp, seq=t_p, chunk=256)
    x1_p, h2_p = _outproj(o_ret, o_gla, xp, gt1p, sc2p, sh2p, g_ffn, w_out_b, tm=512, rows_per_mod=t_p)
    g_p, cs_p = _ffn_up_prompt(h2_p, w_up, conv_w, conv_b, batch=bp, seq=t_p, tm=1024, tf=512)
    y_p = _ffn_down(g_p, w_down, x1_p, gt2p, g_final, tm=1024, tk=512, rows_per_mod=t_p)

    xs = x_sample.reshape(bs, d)
    qk_s, rest_s, la_s = _inproj(xs, sc1s, sh1s, g_attn, w_in_b, w_a2, b_a2, cos_s, sin_s, tm=bs, rows_per_mod=bs)
    o_s, s_ret_s, s_gla_s = _state_step(qk_s, rest_s, la_s, state_ret, state_gla, g_ret_out, g_gla_out)
    o_s = o_s.reshape(bs, RET_W + GLA_W)
    x1_s, h2_s = _outproj(o_s[:, :RET_W], o_s[:, RET_W:], xs, gt1s, sc2s, sh2s, g_ffn, w_out_b, tm=bs, rows_per_mod=bs)
    g_s, cs_s = _ffn_up_step(h2_s, w_up, conv_w, conv_b, state_conv, tf=512)
    y_s = _ffn_down(g_s, w_down, x1_s, gt2s, g_final, tm=bs, tk=512, rows_per_mod=bs)

    return (y_p.reshape(bp, t_p, d), y_s.reshape(bs, t_s, d),
            s_ret_p[None], s_ret_s, s_gla_p[None], s_gla_s,
            cs_p.reshape(1, bp, CONV_W - 1, -1), cs_s)
```

```python
import functools

import numpy as np
import jax
import jax.numpy as jnp
from jax import lax
from jax.experimental import pallas as pl
from jax.experimental.pallas import tpu as pltpu

F32 = jnp.float32
BF16 = jnp.bfloat16

RET_HEADS = 4
RET_DK = 256
RET_DV = 256
GLA_HEADS = 4
GLA_DK = 128
GLA_DV = 256
GLA_RANK = 16
GLA_TAU = 16.0
ROPE_THETA = 10000.0
PAST_LEN = 16384
CONV_W = 3
EPS = 1e-6

RET_W = RET_HEADS * RET_DV
GLA_W = GLA_HEADS * GLA_DV
QK_W = 2 * RET_HEADS * RET_DK
R_RV = 0
R_RG = R_RV + RET_W
R_GQ = R_RG + RET_W
R_GK = R_GQ + GLA_HEADS * GLA_DK
R_GV = R_GK + GLA_HEADS * GLA_DK
R_GR = R_GV + GLA_W
REST_W = R_GR + GLA_W
IN_MAIN = QK_W + REST_W

VMEM_LIMIT_BYTES = 56 * 1024 * 1024


def _params(semantics):
    return pltpu.CompilerParams(dimension_semantics=semantics, vmem_limit_bytes=VMEM_LIMIT_BYTES)


def _dot(a, b):
    return jnp.dot(a, b, preferred_element_type=F32)


def _dot_nt(a, b):
    return lax.dot_general(a, b, (((1,), (1,)), ((), ())), preferred_element_type=F32)


def _dot_tn(a, b):
    return lax.dot_general(a, b, (((0,), (0,)), ((), ())), preferred_element_type=F32)


def _silu(x):
    return x * jax.nn.sigmoid(x)


def _rms(x):
    return x * lax.rsqrt(jnp.mean(x * x, axis=-1, keepdims=True) + EPS)


def _split3(x):
    hi = x.astype(BF16)
    r = x - hi.astype(F32)
    mid = r.astype(BF16)
    lo = (r - mid.astype(F32)).astype(BF16)
    return hi, mid, lo


def _ada_body(c_ref, w_ref, b_ref, o_ref):
    s = _silu(c_ref[...]).astype(BF16)
    o_ref[...] = _dot(s, w_ref[...].astype(BF16)) + b_ref[...]


def _ada(c_all, w_ada, b_ada, tn=512):
    r, d = c_all.shape
    n = w_ada.shape[1]
    return pl.pallas_call(
        _ada_body,
        grid=(n // tn,),
        in_specs=[pl.BlockSpec((r, d), lambda j: (0, 0)),
                  pl.BlockSpec((d, tn), lambda j: (0, j)),
                  pl.BlockSpec((1, tn), lambda j: (0, j))],
        out_specs=pl.BlockSpec((r, tn), lambda j: (0, j)),
        out_shape=jax.ShapeDtypeStruct((r, n), F32),
        compiler_params=_params(("arbitrary",)),
        name="ada_mod",
    )(c_all, w_ada, b_ada.reshape(1, n))


def _inproj_body(x_ref, sc_ref, sh_ref, g_ref, w_ref, wga_ref, wa2_ref, ba2_ref, cos_ref, sin_ref,
                 qk_ref, rest_ref, la_ref, h_ref, *, n_qk):
    j = pl.program_id(1)

    @pl.when(j == 0)
    def _():
        gain = g_ref[...] * (1.0 + sc_ref[0])
        hb = (_rms(x_ref[...]) * gain + sh_ref[0]).astype(BF16)
        h_ref[...] = hb
        ga = _dot(hb, wga_ref[...])
        z = _dot(ga.astype(BF16), wa2_ref[...].astype(BF16)) + ba2_ref[...]
        la_ref[...] = (jnp.minimum(z, 0.0) - jnp.log1p(jnp.exp(-jnp.abs(z)))) * (1.0 / GLA_TAU)

    @pl.when(j < n_qk)
    def _():
        acc = _dot(h_ref[...], w_ref[...])
        cos, sin = cos_ref[...], sin_ref[...]
        half = RET_DK // 2
        scale = jnp.where(j >= n_qk // 2, RET_DK ** -0.5, 1.0)
        outs = []
        for hh in range(acc.shape[1] // RET_DK):
            x1 = acc[:, hh * RET_DK: hh * RET_DK + half]
            x2 = acc[:, hh * RET_DK + half: (hh + 1) * RET_DK]
            outs += [x1 * cos - x2 * sin, x1 * sin + x2 * cos]
        qk_ref[...] = (jnp.concatenate(outs, axis=-1) * scale).astype(qk_ref.dtype)

    @pl.when(j >= n_qk)
    def _():
        rest_ref[...] = _dot(h_ref[...], w_ref[...])


def _inproj(x, sc, sh, g_attn, w_in_b, w_a2, b_a2, cos, sin, *, tm, rows_per_mod, tn=1024, x_buffers=2):
    m, d = x.shape
    nq = w_a2.shape[1]
    n_qk = QK_W // tn
    tab_tiles = cos.shape[0] // tm
    mod_spec = pl.BlockSpec((1, sc.shape[1], d), lambda i, j: ((i * tm) // rows_per_mod, 0, 0))
    tab_spec = pl.BlockSpec((tm, RET_DK // 2), lambda i, j: (i % tab_tiles, 0))
    return pl.pallas_call(
        functools.partial(_inproj_body, n_qk=n_qk),
        grid=(m // tm, IN_MAIN // tn),
        in_specs=[pl.BlockSpec((tm, d), lambda i, j: (i, 0), pipeline_mode=pl.Buffered(x_buffers)),
                  mod_spec, mod_spec,
                  pl.BlockSpec((1, d), lambda i, j: (0, 0)),
                  pl.BlockSpec((d, tn), lambda i, j: (0, j)),
                  pl.BlockSpec((d, GLA_RANK), lambda i, j: (0, 0)),
                  pl.BlockSpec((GLA_RANK, nq), lambda i, j: (0, 0)),
                  pl.BlockSpec((1, nq), lambda i, j: (0, 0)),
                  tab_spec, tab_spec],
        out_specs=[pl.BlockSpec((tm, tn), lambda i, j: (i, jnp.minimum(j, n_qk - 1))),
                   pl.BlockSpec((tm, tn), lambda i, j: (i, jnp.maximum(j - n_qk, 0))),
                   pl.BlockSpec((tm, nq), lambda i, j: (i, 0))],
        out_shape=[jax.ShapeDtypeStruct((m, QK_W), BF16),
                   jax.ShapeDtypeStruct((m, REST_W), F32),
                   jax.ShapeDtypeStruct((m, nq), F32)],
        scratch_shapes=[pltpu.VMEM((tm, d), BF16)],
        compiler_params=_params(("arbitrary", "arbitrary")),
        name="in_proj",
    )(x, sc, sh, g_attn.reshape(1, d), w_in_b, w_in_b[:, IN_MAIN:], w_a2, b_a2.reshape(1, nq), cos, sin)


def _ret_log_gamma():
    return jnp.log1p(-jnp.exp2(-5.0 - jnp.arange(RET_HEADS, dtype=F32)))


def _ret_decay_matrix(chunk):
    idx = jnp.arange(chunk, dtype=F32)
    rel = idx[:, None] - idx[None, :]
    lg = _ret_log_gamma()
    return jnp.where(rel[None] >= 0, jnp.exp(jnp.maximum(rel, 0.0)[None] * lg[:, None, None]), 0.0)


def _ret_body(dmat_ref, lg_ref, q_ref, k_ref, v_ref, g_ref, gout_ref, o_ref, sout_ref, s_ref):
    c = pl.program_id(1)
    cl = q_ref.shape[0]

    @pl.when(c == 0)
    def _():
        s_ref[...] = jnp.zeros_like(s_ref)

    idx = lax.broadcasted_iota(jnp.int32, (cl, 1), 0).astype(F32)
    for h in range(RET_HEADS):
        sl = slice(h * RET_DV, (h + 1) * RET_DV)
        lg = lg_ref[h][:1, :1]
        q_dec = jnp.exp((idx + 1.0) * lg)
        k_dec = jnp.exp((cl - 1.0 - idx) * lg)
        c_dec = jnp.exp(cl * lg)
        qb, kb = q_ref[:, sl], k_ref[:, sl]
        v = v_ref[:, sl]
        vb = v.astype(BF16)
        s_old = s_ref[h]
        scores = _dot_nt(qb, kb) * dmat_ref[h]
        o = _dot(scores.astype(BF16), vb) + _dot(qb, s_old.astype(BF16)) * q_dec
        s_new = s_old * c_dec + _dot_tn(kb, (v * k_dec).astype(BF16))
        s_ref[h] = s_new
        o_ref[:, sl] = (_rms(o) * gout_ref[:, sl] * _silu(g_ref[:, sl])).astype(o_ref.dtype)

    @pl.when(c == pl.num_programs(1) - 1)
    def _():
        sout_ref[0] = s_ref[...]


def _retention_prompt(qk, rest, g_out, *, batch, seq, chunk):
    nc = seq // chunk
    lg = jnp.broadcast_to(_ret_log_gamma()[:, None, None], (RET_HEADS, 8, 128))
    row = lambda blk: (lambda b, c: (b * nc + c, blk))
    return pl.pallas_call(
        _ret_body,
        grid=(batch, nc),
        in_specs=[pl.BlockSpec((RET_HEADS, chunk, chunk), lambda b, c: (0, 0, 0)),
                  pl.BlockSpec((RET_HEADS, 8, 128), lambda b, c: (0, 0, 0)),
                  pl.BlockSpec((chunk, RET_W), row(0)),
                  pl.BlockSpec((chunk, RET_W), row(1)),
                  pl.BlockSpec((chunk, RET_W), row(R_RV // RET_W)),
                  pl.BlockSpec((chunk, RET_W), row(R_RG // RET_W)),
                  pl.BlockSpec((1, RET_W), lambda b, c: (0, 0))],
        out_specs=[pl.BlockSpec((chunk, RET_W), row(0)),
                   pl.BlockSpec((1, RET_HEADS, RET_DK, RET_DV), lambda b, c: (b, 0, 0, 0))],
        out_shape=[jax.ShapeDtypeStruct((batch * seq, RET_W), BF16),
                   jax.ShapeDtypeStruct((batch, RET_HEADS, RET_DK, RET_DV), F32)],
        scratch_shapes=[pltpu.VMEM((RET_HEADS, RET_DK, RET_DV), F32)],
        compiler_params=_params(("arbitrary", "arbitrary")),
        name="retention_prompt",
    )(_ret_decay_matrix(chunk), lg, qk, qk, rest, rest, g_out.reshape(1, RET_W))


def _gla_level_map(c):
    t = np.arange(c)[:, None]
    s = np.arange(c)[None, :]
    x = np.bitwise_xor(t, s)
    lev = np.floor(np.log2(np.maximum(x, 1))).astype(np.int32)
    lev = np.where(t == s, -1, lev)
    lev = np.where(t < s, -2, lev)
    return lev.astype(np.int32)


SCORE_BLK = 128
LOG2E = 1.4426950408889634


def _gla_chunk(q, k, la2, lev, uppers):
    cl = q.shape[0]
    nblk = cl // SCORE_BLK
    blk = lambda x, i: x[i * SCORE_BLK:(i + 1) * SCORE_BLK]
    p = la2
    tot = la2
    diag = [jnp.zeros((SCORE_BLK, SCORE_BLK), F32) for _ in range(nblk)]
    off = {}
    level = 0
    half = 1
    while half < cl:
        upper = uppers[level]
        z = (jnp.where(upper, q, k) * jnp.exp2(jnp.where(upper, p, tot - p))).astype(BF16)
        if half < SCORE_BLK:
            for i in range(nblk):
                diag[i] = jnp.where(lev == level, _dot_nt(blk(z, i), blk(z, i)), diag[i])
        else:
            hb = half // SCORE_BLK
            for i in range(nblk):
                if (i // hb) % 2 == 1:
                    base = (i // (2 * hb)) * 2 * hb
                    for j in range(base, base + hb):
                        off[(i, j)] = _dot_nt(blk(z, i), blk(z, j))
        partner = jnp.where(upper, pltpu.roll(tot, half, 0), pltpu.roll(tot, cl - half, 0))
        p = p + jnp.where(upper, partner, 0.0)
        tot = tot + partner
        half *= 2
        level += 1
    dg = jnp.sum(q * k, axis=-1, keepdims=True)
    rows = []
    for i in range(nblk):
        d_i = jnp.where(lev == -1, blk(dg, i), diag[i])
        rows.append(jnp.concatenate([off[(i, j)] for j in range(i)] + [d_i], axis=-1))
    return rows, p, tot


def _gla_body(q_ref, k_ref, v_ref, g_ref, la_ref, lev_ref, gout_ref, o_ref, sout_ref, st_ref):
    c = pl.program_id(1)
    cl = q_ref.shape[0]

    @pl.when(c == 0)
    def _():
        st_ref[...] = jnp.zeros_like(st_ref)

    lev = lev_ref[...]
    row = lax.broadcasted_iota(jnp.int32, (cl, GLA_DK), 0)
    uppers = []
    half = 1
    while half < cl:
        uppers.append((row & half) != 0)
        half *= 2
    for h in range(GLA_HEADS):
        ks = slice(h * GLA_DK, (h + 1) * GLA_DK)
        vs = slice(h * GLA_DV, (h + 1) * GLA_DV)
        q = q_ref[:, ks] * (GLA_DK ** -0.5)
        k = k_ref[:, ks]
        rows, p, tot = _gla_chunk(q, k, la_ref[:, ks] * LOG2E, lev, uppers)
        vb = v_ref[:, vs].astype(BF16)
        st_old = st_ref[h]
        qt = (q * jnp.exp2(p)).astype(BF16)
        intra = jnp.concatenate([_dot(r.astype(BF16), vb[:r.shape[1]]) for r in rows], axis=0)
        o = intra + _dot_nt(qt, st_old.astype(BF16))
        kt = (k * jnp.exp2(tot - p)).astype(BF16)
        st_ref[h] = st_old * jnp.exp2(tot[0:1, :]) + _dot_tn(vb, kt)
        o_ref[:, vs] = (_rms(o) * gout_ref[:, vs] * _silu(g_ref[:, vs])).astype(o_ref.dtype)

    @pl.when(c == pl.num_programs(1) - 1)
    def _():
        for h in range(GLA_HEADS):
            sout_ref[0, h] = st_ref[h].T


def _gla_prompt(rest, la, g_out, *, batch, seq, chunk):
    nc = seq // chunk
    lev = jnp.asarray(_gla_level_map(SCORE_BLK))
    qk_w = GLA_HEADS * GLA_DK
    row = lambda blk: (lambda b, c: (b * nc + c, blk))
    return pl.pallas_call(
        _gla_body,
        grid=(batch, nc),
        in_specs=[pl.BlockSpec((chunk, qk_w), row(R_GQ // qk_w)),
                  pl.BlockSpec((chunk, qk_w), row(R_GK // qk_w)),
                  pl.BlockSpec((chunk, GLA_W), row(R_GV // GLA_W)),
                  pl.BlockSpec((chunk, GLA_W), row(R_GR // GLA_W)),
                  pl.BlockSpec((chunk, qk_w), row(0)),
                  pl.BlockSpec((SCORE_BLK, SCORE_BLK), lambda b, c: (0, 0)),
                  pl.BlockSpec((1, GLA_W), lambda b, c: (0, 0))],
        out_specs=[pl.BlockSpec((chunk, GLA_W), row(0)),
                   pl.BlockSpec((1, GLA_HEADS, GLA_DK, GLA_DV), lambda b, c: (b, 0, 0, 0))],
        out_shape=[jax.ShapeDtypeStruct((batch * seq, GLA_W), BF16),
                   jax.ShapeDtypeStruct((batch, GLA_HEADS, GLA_DK, GLA_DV), F32)],
        scratch_shapes=[pltpu.VMEM((GLA_HEADS, GLA_DV, GLA_DK), F32)],
        compiler_params=_params(("arbitrary", "arbitrary")),
        name="gla_prompt",
    )(rest, rest, rest, rest, la, lev, g_out.reshape(1, GLA_W))


COL_ROWS = 16


def _col_selector(nvec, width):
    sel = np.zeros((COL_ROWS, nvec * width), np.float32)
    for i in range(nvec):
        sel[3 * i:3 * i + 3, i * width:(i + 1) * width] = 1.0
    return sel


def _col_bcast(rows, sel):
    n = rows[0].shape[1]
    ridx = lax.broadcasted_iota(jnp.int32, (COL_ROWS, n), 0)
    lhs = jnp.zeros((COL_ROWS, n), F32)
    for i, r in enumerate(rows):
        for j, piece in enumerate(_split3(r)):
            lhs = jnp.where(ridx == 3 * i + j, piece.astype(F32), lhs)
    return _dot_tn(lhs.astype(BF16), sel)


def _state_body(lg_ref, qk_ref, rest_ref, la_ref, sr_ref, sg_ref, gro_ref, ggo_ref, sel2_ref, sel3_ref,
                o_ref, sr_out, sg_out):
    nb = qk_ref.shape[0]
    sel2, sel3 = sel2_ref[...], sel3_ref[...]
    for i in range(nb):
        qkrow = qk_ref[i]
        rrow = rest_ref[i]
        outs = []
        for h in range(RET_HEADS):
            q = qkrow[:, h * RET_DK:(h + 1) * RET_DK]
            k = qkrow[:, QK_W // 2 + h * RET_DK: QK_W // 2 + (h + 1) * RET_DK]
            v = rrow[:, R_RV + h * RET_DV: R_RV + (h + 1) * RET_DV]
            gate = rrow[:, R_RG + h * RET_DV: R_RG + (h + 1) * RET_DV]
            gamma = jnp.exp(lg_ref[h][:1, :1])
            cols = _col_bcast([k, q], sel2)
            s_new = sr_ref[0, i, h] * gamma + cols[:, :RET_DV] * v
            sr_out[0, i, h] = s_new
            o = jnp.sum(cols[:, RET_DV:] * s_new, axis=0, keepdims=True)
            outs.append(_rms(o) * gro_ref[:, h * RET_DV:(h + 1) * RET_DV] * _silu(gate))
        larow = la_ref[i]
        for h in range(GLA_HEADS):
            q = rrow[:, R_GQ + h * GLA_DK: R_GQ + (h + 1) * GLA_DK] * (GLA_DK ** -0.5)
            k = rrow[:, R_GK + h * GLA_DK: R_GK + (h + 1) * GLA_DK]
            v = rrow[:, R_GV + h * GLA_DV: R_GV + (h + 1) * GLA_DV]
            gate = rrow[:, R_GR + h * GLA_DV: R_GR + (h + 1) * GLA_DV]
            alpha = jnp.exp(larow[:, h * GLA_DK:(h + 1) * GLA_DK])
            cols = _col_bcast([k, q, alpha], sel3)
            s_new = sg_ref[0, i, h] * cols[:, 2 * GLA_DV:] + cols[:, :GLA_DV] * v
            sg_out[0, i, h] = s_new
            o = jnp.sum(cols[:, GLA_DV:2 * GLA_DV] * s_new, axis=0, keepdims=True)
            outs.append(_rms(o) * ggo_ref[:, h * GLA_DV:(h + 1) * GLA_DV] * _silu(gate))
        o_ref[i] = jnp.concatenate(outs, axis=-1)


def _state_step(qk, rest, la, state_ret, state_gla, g_ret_out, g_gla_out, *, nb=4):
    m = qk.shape[0]
    mix = RET_W + GLA_W
    lg = jnp.broadcast_to(_ret_log_gamma()[:, None, None], (RET_HEADS, 8, 128))
    return pl.pallas_call(
        _state_body,
        grid=(m // nb,),
        in_specs=[pl.BlockSpec((RET_HEADS, 8, 128), lambda b: (0, 0, 0)),
                  pl.BlockSpec((nb, 1, QK_W), lambda b: (b, 0, 0)),
                  pl.BlockSpec((nb, 1, REST_W), lambda b: (b, 0, 0)),
                  pl.BlockSpec((nb, 1, la.shape[1]), lambda b: (b, 0, 0)),
                  pl.BlockSpec((1, nb, RET_HEADS, RET_DK, RET_DV), lambda b: (0, b, 0, 0, 0)),
                  pl.BlockSpec((1, nb, GLA_HEADS, GLA_DK, GLA_DV), lambda b: (0, b, 0, 0, 0)),
                  pl.BlockSpec((1, RET_W), lambda b: (0, 0)),
                  pl.BlockSpec((1, GLA_W), lambda b: (0, 0)),
                  pl.BlockSpec((COL_ROWS, 2 * RET_DV), lambda b: (0, 0)),
                  pl.BlockSpec((COL_ROWS, 3 * GLA_DV), lambda b: (0, 0))],
        out_specs=[pl.BlockSpec((nb, 1, mix), lambda b: (b, 0, 0)),
                   pl.BlockSpec((1, nb, RET_HEADS, RET_DK, RET_DV), lambda b: (0, b, 0, 0, 0)),
                   pl.BlockSpec((1, nb, GLA_HEADS, GLA_DK, GLA_DV), lambda b: (0, b, 0, 0, 0))],
        out_shape=[jax.ShapeDtypeStruct((m, 1, mix), F32),
                   jax.ShapeDtypeStruct(state_ret.shape, F32),
                   jax.ShapeDtypeStruct(state_gla.shape, F32)],
        compiler_params=_params(("arbitrary",)),
        name="state_step",
    )(lg, qk.astype(F32).reshape(m, 1, QK_W), rest.reshape(m, 1, REST_W), la.reshape(m, 1, la.shape[1]),
      state_ret, state_gla, g_ret_out.reshape(1, RET_W), g_gla_out.reshape(1, GLA_W),
      jnp.asarray(_col_selector(2, RET_DV), BF16), jnp.asarray(_col_selector(3, GLA_DV), BF16))


def _outproj_body(oa_ref, ob_ref, x_ref, gt_ref, sc_ref, sh_ref, g_ref, w_ref, x1_ref, h_ref):
    ka = oa_ref.shape[1]
    mix = _dot(oa_ref[...].astype(BF16), w_ref[:ka, :]) + _dot(ob_ref[...].astype(BF16), w_ref[ka:, :])
    x1 = x_ref[...] + gt_ref[0] * mix
    x1_ref[...] = x1
    h = _rms(x1) * g_ref[...]
    h_ref[...] = (h * (1.0 + sc_ref[0]) + sh_ref[0]).astype(BF16)


def _outproj(o_a, o_b, x, gt, sc, sh, g_ffn, w_out_bf16, *, tm, rows_per_mod):
    m, d = x.shape
    ka, kb = o_a.shape[1], o_b.shape[1]
    mod_spec = pl.BlockSpec((1, gt.shape[1], d), lambda i: ((i * tm) // rows_per_mod, 0, 0))
    return pl.pallas_call(
        _outproj_body,
        grid=(m // tm,),
        in_specs=[pl.BlockSpec((tm, ka), lambda i: (i, 0)),
                  pl.BlockSpec((tm, kb), lambda i: (i, 0)),
                  pl.BlockSpec((tm, d), lambda i: (i, 0)),
                  mod_spec, mod_spec, mod_spec,
                  pl.BlockSpec((1, d), lambda i: (0, 0)),
                  pl.BlockSpec((ka + kb, d), lambda i: (0, 0), pipeline_mode=pl.Buffered(1))],
        out_specs=[pl.BlockSpec((tm, d), lambda i: (i, 0)),
                   pl.BlockSpec((tm, d), lambda i: (i, 0))],
        out_shape=[jax.ShapeDtypeStruct((m, d), F32),
                   jax.ShapeDtypeStruct((m, d), BF16)],
        compiler_params=_params(("arbitrary",)),
        name="out_proj",
    )(o_a, o_b, x, gt, sc, sh, g_ffn.reshape(1, d), w_out_bf16)


LANES = 128
HALO = 8


def _ffn_up_body(h_ref, wa_ref, wb_ref, cwa_ref, cwb_ref, cba_ref, cbb_ref, g_ref, cs_ref, win_ref, tail_ref, *, tiles_per_seq):
    i, f = pl.program_id(0), pl.program_id(1)
    hb = h_ref[...]
    tm = hb.shape[0]
    first = (i % tiles_per_seq) == 0
    ucs = []
    for part, (w_ref, cw_ref, cb_ref) in enumerate(((wa_ref, cwa_ref, cba_ref), (wb_ref, cwb_ref, cbb_ref))):
        u = _dot(hb, w_ref[...].astype(BF16))
        prev = jnp.where(first, 0.0, tail_ref[part, f])
        pieces = []
        for c in range(u.shape[1] // LANES):
            lc = slice(c * LANES, (c + 1) * LANES)
            win_ref[part, c, 0:HALO, :] = prev[:, lc]
            win_ref[part, c, HALO:HALO + tm, :] = u[:, lc]
            cw, cb = cw_ref[:, lc], cb_ref[:, lc]
            pieces.append(cb + cw[0:1] * win_ref[part, c, HALO - 2:HALO - 2 + tm, :]
                          + cw[1:2] * win_ref[part, c, HALO - 1:HALO - 1 + tm, :] + cw[2:3] * u[:, lc])
        ucs.append(jnp.concatenate(pieces, axis=-1))
        tail_ref[part, f] = u[tm - HALO:]
        for r in range(CONV_W - 1):
            cs_ref[0, 0, r, part:part + 1, :] = u[tm - (CONV_W - 1) + r: tm - (CONV_W - 1) + r + 1]
    g_ref[...] = (_silu(ucs[0]) * ucs[1]).astype(g_ref.dtype)


def _ffn_up_prompt(h, w_up, conv_w, conv_b, *, batch, seq, tm, tf):
    m, d = h.shape
    ff = w_up.shape[1] // 2
    nf = ff // tf
    cb = conv_b.reshape(1, 2 * ff)
    tps = seq // tm
    body = functools.partial(_ffn_up_body, tiles_per_seq=tps)
    gate, tails = pl.pallas_call(
        body,
        grid=(m // tm, nf),
        in_specs=[pl.BlockSpec((tm, d), lambda i, f: (i, 0)),
                  pl.BlockSpec((d, tf), lambda i, f: (0, f)),
                  pl.BlockSpec((d, tf), lambda i, f: (0, nf + f)),
                  pl.BlockSpec((CONV_W, tf), lambda i, f: (0, f)),
                  pl.BlockSpec((CONV_W, tf), lambda i, f: (0, nf + f)),
                  pl.BlockSpec((1, tf), lambda i, f: (0, f)),
                  pl.BlockSpec((1, tf), lambda i, f: (0, nf + f))],
        out_specs=[pl.BlockSpec((tm, tf), lambda i, f: (i, f)),
                   pl.BlockSpec((1, 1, CONV_W - 1, 2, tf), lambda i, f: (i // tps, i % tps, 0, 0, f))],
        out_shape=[jax.ShapeDtypeStruct((m, ff), BF16),
                   jax.ShapeDtypeStruct((batch, tps, CONV_W - 1, 2, ff), F32)],
        scratch_shapes=[pltpu.VMEM((2, tf // LANES, HALO + tm, LANES), F32), pltpu.VMEM((2, nf, HALO, tf), F32)],
        compiler_params=_params(("arbitrary", "arbitrary")),
        name="ffn_up_prompt",
    )(h, w_up, w_up, conv_w, conv_w, cb, cb)
    return gate, tails[:, tps - 1]


def _ffn_up_step_body(h_ref, w_ref, cw_ref, cb_ref, st_ref, g_ref, cs_ref, uca_ref, *, nf):
    j = pl.program_id(0)
    u = _dot(h_ref[...], w_ref[...].astype(BF16))
    s1 = st_ref[0, :, 1, :]
    cw = cw_ref[...]
    uc = cb_ref[...] + cw[0:1] * st_ref[0, :, 0, :] + cw[1:2] * s1 + cw[2:3] * u
    cs_ref[0, :, 0, :] = s1
    cs_ref[0, :, 1, :] = u

    @pl.when(j < nf)
    def _():
        uca_ref[j] = uc

    @pl.when(j >= nf)
    def _():
        g_ref[...] = (_silu(uca_ref[j - nf]) * uc).astype(g_ref.dtype)


def _ffn_up_step(h, w_up, conv_w, conv_b, state_conv, *, tf):
    m, d = h.shape
    ff = w_up.shape[1] // 2
    nf = ff // tf
    st_spec = pl.BlockSpec((1, m, CONV_W - 1, tf), lambda j: (0, 0, 0, j))
    return pl.pallas_call(
        functools.partial(_ffn_up_step_body, nf=nf),
        grid=(2 * nf,),
        in_specs=[pl.BlockSpec((m, d), lambda j: (0, 0)),
                  pl.BlockSpec((d, tf), lambda j: (0, j)),
                  pl.BlockSpec((CONV_W, tf), lambda j: (0, j)),
                  pl.BlockSpec((1, tf), lambda j: (0, j)),
                  st_spec],
        out_specs=[pl.BlockSpec((m, tf), lambda j: (0, jnp.maximum(j - nf, 0))), st_spec],
        out_shape=[jax.ShapeDtypeStruct((m, ff), BF16),
                   jax.ShapeDtypeStruct(state_conv.shape, F32)],
        scratch_shapes=[pltpu.VMEM((nf, m, tf), F32)],
        compiler_params=_params(("arbitrary",)),
        name="ffn_up_step",
    )(h, w_up, conv_w, conv_b.reshape(1, 2 * ff), state_conv)


def _ffn_down_body(g_ref, w_ref, x1_ref, gt_ref, gf_ref, y_ref):
    k = pl.program_id(1)

    @pl.when(k == 0)
    def _():
        y_ref[...] = jnp.zeros_like(y_ref)

    y_ref[...] += _dot(g_ref[...], w_ref[...].astype(BF16))

    @pl.when(k == pl.num_programs(1) - 1)
    def _():
        x2 = x1_ref[...] + gt_ref[0] * y_ref[...]
        y_ref[...] = _rms(x2) * gf_ref[...]


def _ffn_down(g, w_down, x1, gt, g_final, *, tm, tk, rows_per_mod):
    m, d = x1.shape
    ff = g.shape[1]
    return pl.pallas_call(
        _ffn_down_body,
        grid=(m // tm, ff // tk),
        in_specs=[pl.BlockSpec((tm, tk), lambda i, k: (i, k)),
                  pl.BlockSpec((tk, d), lambda i, k: (k, 0)),
                  pl.BlockSpec((tm, d), lambda i, k: (i, 0)),
                  pl.BlockSpec((1, gt.shape[1], d), lambda i, k: ((i * tm) // rows_per_mod, 0, 0)),
                  pl.BlockSpec((1, d), lambda i, k: (0, 0))],
        out_specs=pl.BlockSpec((tm, d), lambda i, k: (i, 0)),
        out_shape=jax.ShapeDtypeStruct((m, d), F32),
        compiler_params=_params(("arbitrary", "arbitrary")),
        name="ffn_down",
    )(g, w_down, x1, gt, g_final.reshape(1, d))


def _rope_tables(pos):
    half = RET_DK // 2
    inv = ROPE_THETA ** (-jnp.arange(half, dtype=F32) / half)
    ang = pos.astype(F32)[:, None] * inv[None, :]
    return jnp.cos(ang), jnp.sin(ang)


def kernel(x_prompt, x_sample, c_prompt, c_sample, state_ret, state_gla, state_conv, w_ada, b_ada, g_attn, w_in, w_a2, b_a2, g_ret_out, g_gla_out, w_out, g_ffn, w_up, conv_w, conv_b, w_down, g_final):
    bp, t_p, d = x_prompt.shape
    bs, t_s, _ = x_sample.shape
    assert t_s == 1 and w_ada.shape[0] == 1
    mp = bp * t_p
    w_ada, b_ada, g_attn, w_in, w_a2, b_a2, g_ret_out, g_gla_out, w_out, g_ffn, w_up, conv_w, conv_b, w_down = (
        a[0] for a in (w_ada, b_ada, g_attn, w_in, w_a2, b_a2, g_ret_out, g_gla_out, w_out, g_ffn, w_up, conv_w, conv_b, w_down))

    mod = _ada(jnp.concatenate([c_prompt, c_sample], axis=0), w_ada, b_ada)
    sh1p, sc1p, gt1p, sh2p, sc2p, gt2p = (mod[:bp, i * d:(i + 1) * d].reshape(bp, 1, d) for i in range(6))
    sh1s, sc1s, gt1s, sh2s, sc2s, gt2s = (mod[bp:, i * d:(i + 1) * d].reshape(1, bs, d) for i in range(6))

    cos_p, sin_p = _rope_tables(jnp.arange(t_p, dtype=jnp.int32))
    cos_s, sin_s = (jnp.broadcast_to(t, (bs, RET_DK // 2)) for t in _rope_tables(PAST_LEN + jnp.arange(t_s, dtype=jnp.int32)))
    w_in_b = w_in.astype(BF16)
    w_out_b = w_out.astype(BF16)

    xp = x_prompt.reshape(mp, d)
    qk_p, rest_p, la_p = _inproj(xp, sc1p, sh1p, g_attn, w_in_b, w_a2, b_a2, cos_p, sin_p, tm=1024, rows_per_mod=t_p)
    o_ret, s_ret_p = _retention_prompt(qk_p, rest_p, g_ret_out, batch=bp, seq=t_p, chunk=256)
    o_gla, s_gla_p = _gla_prompt(rest_p, la_p, g_gla_out, batch=bp, seq=t_p, chunk=256)
    x1_p, h2_p = _outproj(o_ret, o_gla, xp, gt1p, sc2p, sh2p, g_ffn, w_out_b, tm=512, rows_per_mod=t_p)
    g_p, cs_p = _ffn_up_prompt(h2_p, w_up, conv_w, conv_b, batch=bp, seq=t_p, tm=1024, tf=512)
    y_p = _ffn_down(g_p, w_down, x1_p, gt2p, g_final, tm=1024, tk=512, rows_per_mod=t_p)

    xs = x_sample.reshape(bs, d)
    qk_s, rest_s, la_s = _inproj(xs, sc1s, sh1s, g_attn, w_in_b, w_a2, b_a2, cos_s, sin_s, tm=bs, rows_per_mod=bs)
    o_s, s_ret_s, s_gla_s = _state_step(qk_s, rest_s, la_s, state_ret, state_gla, g_ret_out, g_gla_out)
    o_s = o_s.reshape(bs, RET_W + GLA_W)
    x1_s, h2_s = _outproj(o_s[:, :RET_W], o_s[:, RET_W:], xs, gt1s, sc2s, sh2s, g_ffn, w_out_b, tm=bs, rows_per_mod=bs)
    g_s, cs_s = _ffn_up_step(h2_s, w_up, conv_w, conv_b, state_conv, tf=512)
    y_s = _ffn_down(g_s, w_down, x1_s, gt2s, g_final, tm=bs, tk=512, rows_per_mod=bs)

    return (y_p.reshape(bp, t_p, d), y_s.reshape(bs, t_s, d),
            s_ret_p[None], s_ret_s, s_gla_p[None], s_gla_s,
            cs_p.reshape(1, bp, CONV_W - 1, -1), cs_s)
```

```python
import functools

import numpy as np
import jax
import jax.numpy as jnp
from jax import lax
from jax.experimental import pallas as pl
from jax.experimental.pallas import tpu as pltpu

F32 = jnp.float32
BF16 = jnp.bfloat16

RET_HEADS = 4
RET_DK = 256
RET_DV = 256
GLA_HEADS = 4
GLA_DK = 128
GLA_DV = 256
GLA_RANK = 16
GLA_TAU = 16.0
ROPE_THETA = 10000.0
PAST_LEN = 16384
CONV_W = 3
EPS = 1e-6

RET_W = RET_HEADS * RET_DV
GLA_W = GLA_HEADS * GLA_DV
QK_W = 2 * RET_HEADS * RET_DK
R_RV = 0
R_RG = R_RV + RET_W
R_GQ = R_RG + RET_W
R_GK = R_GQ + GLA_HEADS * GLA_DK
R_GV = R_GK + GLA_HEADS * GLA_DK
R_GR = R_GV + GLA_W
REST_W = R_GR + GLA_W
IN_MAIN = QK_W + REST_W

VMEM_LIMIT_BYTES = 56 * 1024 * 1024


def _params(semantics):
    return pltpu.CompilerParams(dimension_semantics=semantics, vmem_limit_bytes=VMEM_LIMIT_BYTES)


def _dot(a, b):
    return jnp.dot(a, b, preferred_element_type=F32)


def _dot_nt(a, b):
    return lax.dot_general(a, b, (((1,), (1,)), ((), ())), preferred_element_type=F32)


def _dot_tn(a, b):
    return lax.dot_general(a, b, (((0,), (0,)), ((), ())), preferred_element_type=F32)


def _silu(x):
    return x * jax.nn.sigmoid(x)


def _rms(x):
    return x * lax.rsqrt(jnp.mean(x * x, axis=-1, keepdims=True) + EPS)


def _ada_body(c_ref, w_ref, b_ref, o_ref):
    s = _silu(c_ref[...]).astype(BF16)
    o_ref[...] = _dot(s, w_ref[...].astype(BF16)) + b_ref[...]


def _ada(c_all, w_ada, b_ada, tn=512):
    r, d = c_all.shape
    n = w_ada.shape[1]
    return pl.pallas_call(
        _ada_body,
        grid=(n // tn,),
        in_specs=[pl.BlockSpec((r, d), lambda j: (0, 0)),
                  pl.BlockSpec((d, tn), lambda j: (0, j)),
                  pl.BlockSpec((1, tn), lambda j: (0, j))],
        out_specs=pl.BlockSpec((r, tn), lambda j: (0, j)),
        out_shape=jax.ShapeDtypeStruct((r, n), F32),
        compiler_params=_params(("arbitrary",)),
        name="ada_mod",
    )(c_all, w_ada, b_ada.reshape(1, n))


def _inproj_body(x_ref, sc_ref, sh_ref, g_ref, w_ref, wga_ref, wa2_ref, ba2_ref, cos_ref, sin_ref,
                 qk_ref, rest_ref, la_ref, h_ref, *, n_qk):
    j = pl.program_id(1)

    @pl.when(j == 0)
    def _():
        gain = g_ref[...] * (1.0 + sc_ref[0])
        hb = (_rms(x_ref[...]) * gain + sh_ref[0]).astype(BF16)
        h_ref[...] = hb
        ga = _dot(hb, wga_ref[...])
        z = _dot(ga.astype(BF16), wa2_ref[...].astype(BF16)) + ba2_ref[...]
        la_ref[...] = (jnp.minimum(z, 0.0) - jnp.log1p(jnp.exp(-jnp.abs(z)))) * (1.0 / GLA_TAU)

    @pl.when(j < n_qk)
    def _():
        acc = _dot(h_ref[...], w_ref[...])
        cos, sin = cos_ref[...], sin_ref[...]
        half = RET_DK // 2
        scale = jnp.where(j >= n_qk // 2, RET_DK ** -0.5, 1.0)
        outs = []
        for hh in range(acc.shape[1] // RET_DK):
            x1 = acc[:, hh * RET_DK: hh * RET_DK + half]
            x2 = acc[:, hh * RET_DK + half: (hh + 1) * RET_DK]
            outs += [x1 * cos - x2 * sin, x1 * sin + x2 * cos]
        qk_ref[...] = (jnp.concatenate(outs, axis=-1) * scale).astype(qk_ref.dtype)

    @pl.when(j >= n_qk)
    def _():
        rest_ref[...] = _dot(h_ref[...], w_ref[...])


def _inproj(x, sc, sh, g_attn, w_in_b, w_a2, b_a2, cos, sin, *, tm, rows_per_mod, tn=1024, x_buffers=2):
    m, d = x.shape
    nq = w_a2.shape[1]
    n_qk = QK_W // tn
    tab_tiles = cos.shape[0] // tm
    mod_spec = pl.BlockSpec((1, sc.shape[1], d), lambda i, j: ((i * tm) // rows_per_mod, 0, 0))
    tab_spec = pl.BlockSpec((tm, RET_DK // 2), lambda i, j: (i % tab_tiles, 0))
    return pl.pallas_call(
        functools.partial(_inproj_body, n_qk=n_qk),
        grid=(m // tm, IN_MAIN // tn),
        in_specs=[pl.BlockSpec((tm, d), lambda i, j: (i, 0), pipeline_mode=pl.Buffered(x_buffers)),
                  mod_spec, mod_spec,
                  pl.BlockSpec((1, d), lambda i, j: (0, 0)),
                  pl.BlockSpec((d, tn), lambda i, j: (0, j)),
                  pl.BlockSpec((d, GLA_RANK), lambda i, j: (0, 0)),
                  pl.BlockSpec((GLA_RANK, nq), lambda i, j: (0, 0)),
                  pl.BlockSpec((1, nq), lambda i, j: (0, 0)),
                  tab_spec, tab_spec],
        out_specs=[pl.BlockSpec((tm, tn), lambda i, j: (i, jnp.minimum(j, n_qk - 1))),
                   pl.BlockSpec((tm, tn), lambda i, j: (i, jnp.maximum(j - n_qk, 0))),
                   pl.BlockSpec((tm, nq), lambda i, j: (i, 0))],
        out_shape=[jax.ShapeDtypeStruct((m, QK_W), BF16),
                   jax.ShapeDtypeStruct((m, REST_W), F32),
                   jax.ShapeDtypeStruct((m, nq), F32)],
        scratch_shapes=[pltpu.VMEM((tm, d), BF16)],
        compiler_params=_params(("arbitrary", "arbitrary")),
        name="in_proj",
    )(x, sc, sh, g_attn.reshape(1, d), w_in_b, w_in_b[:, IN_MAIN:], w_a2, b_a2.reshape(1, nq), cos, sin)


def _ret_log_gamma():
    return jnp.log1p(-jnp.exp2(-5.0 - jnp.arange(RET_HEADS, dtype=F32)))


def _ret_decay_matrix(chunk):
    idx = jnp.arange(chunk, dtype=F32)
    rel = idx[:, None] - idx[None, :]
    lg = _ret_log_gamma()
    return jnp.where(rel[None] >= 0, jnp.exp(jnp.maximum(rel, 0.0)[None] * lg[:, None, None]), 0.0)


def _ret_body(dmat_ref, lg_ref, q_ref, k_ref, v_ref, g_ref, gout_ref, o_ref, sout_ref, s_ref):
    c = pl.program_id(1)
    cl = q_ref.shape[0]

    @pl.when(c == 0)
    def _():
        s_ref[...] = jnp.zeros_like(s_ref)

    idx = lax.broadcasted_iota(jnp.int32, (cl, 1), 0).astype(F32)
    for h in range(RET_HEADS):
        sl = slice(h * RET_DV, (h + 1) * RET_DV)
        lg = lg_ref[h][:1, :1]
        q_dec = jnp.exp((idx + 1.0) * lg)
        k_dec = jnp.exp((cl - 1.0 - idx) * lg)
        c_dec = jnp.exp(cl * lg)
        qb, kb = q_ref[:, sl], k_ref[:, sl]
        v = v_ref[:, sl]
        vb = v.astype(BF16)
        s_old = s_ref[h]
        scores = _dot_nt(qb, kb) * dmat_ref[h]
        o = _dot(scores.astype(BF16), vb) + _dot(qb, s_old.astype(BF16)) * q_dec
        s_new = s_old * c_dec + _dot_tn(kb, (v * k_dec).astype(BF16))
        s_ref[h] = s_new
        o_ref[:, sl] = (_rms(o) * gout_ref[:, sl] * _silu(g_ref[:, sl])).astype(o_ref.dtype)

    @pl.when(c == pl.num_programs(1) - 1)
    def _():
        sout_ref[0] = s_ref[...]


def _retention_prompt(qk, rest, g_out, *, batch, seq, chunk):
    nc = seq // chunk
    lg = jnp.broadcast_to(_ret_log_gamma()[:, None, None], (RET_HEADS, 8, 128))
    row = lambda blk: (lambda b, c: (b * nc + c, blk))
    return pl.pallas_call(
        _ret_body,
        grid=(batch, nc),
        in_specs=[pl.BlockSpec((RET_HEADS, chunk, chunk), lambda b, c: (0, 0, 0)),
                  pl.BlockSpec((RET_HEADS, 8, 128), lambda b, c: (0, 0, 0)),
                  pl.BlockSpec((chunk, RET_W), row(0)),
                  pl.BlockSpec((chunk, RET_W), row(1)),
                  pl.BlockSpec((chunk, RET_W), row(R_RV // RET_W)),
                  pl.BlockSpec((chunk, RET_W), row(R_RG // RET_W)),
                  pl.BlockSpec((1, RET_W), lambda b, c: (0, 0))],
        out_specs=[pl.BlockSpec((chunk, RET_W), row(0)),
                   pl.BlockSpec((1, RET_HEADS, RET_DK, RET_DV), lambda b, c: (b, 0, 0, 0))],
        out_shape=[jax.ShapeDtypeStruct((batch * seq, RET_W), BF16),
                   jax.ShapeDtypeStruct((batch, RET_HEADS, RET_DK, RET_DV), F32)],
        scratch_shapes=[pltpu.VMEM((RET_HEADS, RET_DK, RET_DV), F32)],
        compiler_params=_params(("arbitrary", "arbitrary")),
        name="retention_prompt",
    )(_ret_decay_matrix(chunk), lg, qk, qk, rest, rest, g_out.reshape(1, RET_W))


def _gla_level_map(c):
    t = np.arange(c)[:, None]
    s = np.arange(c)[None, :]
    x = np.bitwise_xor(t, s)
    lev = np.floor(np.log2(np.maximum(x, 1))).astype(np.int32)
    lev = np.where(t == s, -1, lev)
    lev = np.where(t < s, -2, lev)
    return lev.astype(np.int32)


SCORE_BLK = 128
LOG2E = 1.4426950408889634


def _gla_chunk(q, k, la2, lev, uppers):
    cl = q.shape[0]
    nblk = cl // SCORE_BLK
    blk = lambda x, i: x[i * SCORE_BLK:(i + 1) * SCORE_BLK]
    p = la2
    tot = la2
    diag = [jnp.zeros((SCORE_BLK, SCORE_BLK), F32) for _ in range(nblk)]
    off = {}
    level = 0
    half = 1
    while half < cl:
        upper = uppers[level]
        z = (jnp.where(upper, q, k) * jnp.exp2(jnp.where(upper, p, tot - p))).astype(BF16)
        if half < SCORE_BLK:
            for i in range(nblk):
                diag[i] = jnp.where(lev == level, _dot_nt(blk(z, i), blk(z, i)), diag[i])
        else:
            hb = half // SCORE_BLK
            for i in range(nblk):
                if (i // hb) % 2 == 1:
                    base = (i // (2 * hb)) * 2 * hb
                    for j in range(base, base + hb):
                        off[(i, j)] = _dot_nt(blk(z, i), blk(z, j))
        partner = jnp.where(upper, pltpu.roll(tot, half, 0), pltpu.roll(tot, cl - half, 0))
        p = p + jnp.where(upper, partner, 0.0)
        tot = tot + partner
        half *= 2
        level += 1
    dg = jnp.sum(q * k, axis=-1, keepdims=True)
    rows = []
    for i in range(nblk):
        d_i = jnp.where(lev == -1, blk(dg, i), diag[i])
        rows.append(jnp.concatenate([off[(i, j)] for j in range(i)] + [d_i], axis=-1))
    return rows, p, tot


def _gla_body(q_ref, k_ref, v_ref, g_ref, la_ref, lev_ref, gout_ref, o_ref, sout_ref, st_ref):
    c = pl.program_id(1)
    cl = q_ref.shape[0]

    @pl.when(c == 0)
    def _():
        st_ref[...] = jnp.zeros_like(st_ref)

    lev = lev_ref[...]
    row = lax.broadcasted_iota(jnp.int32, (cl, GLA_DK), 0)
    uppers = []
    half = 1
    while half < cl:
        uppers.append((row & half) != 0)
        half *= 2
    for h in range(GLA_HEADS):
        ks = slice(h * GLA_DK, (h + 1) * GLA_DK)
        vs = slice(h * GLA_DV, (h + 1) * GLA_DV)
        q = q_ref[:, ks] * (GLA_DK ** -0.5)
        k = k_ref[:, ks]
        rows, p, tot = _gla_chunk(q, k, la_ref[:, ks] * LOG2E, lev, uppers)
        vb = v_ref[:, vs].astype(BF16)
        st_old = st_ref[h]
        qt = (q * jnp.exp2(p)).astype(BF16)
        intra = jnp.concatenate([_dot(r.astype(BF16), vb[:r.shape[1]]) for r in rows], axis=0)
        o = intra + _dot_nt(qt, st_old.astype(BF16))
        kt = (k * jnp.exp2(tot - p)).astype(BF16)
        st_ref[h] = st_old * jnp.exp2(tot[0:1, :]) + _dot_tn(vb, kt)
        o_ref[:, vs] = (_rms(o) * gout_ref[:, vs] * _silu(g_ref[:, vs])).astype(o_ref.dtype)

    @pl.when(c == pl.num_programs(1) - 1)
    def _():
        for h in range(GLA_HEADS):
            sout_ref[0, h] = st_ref[h].T


def _gla_prompt(rest, la, g_out, *, batch, seq, chunk):
    nc = seq // chunk
    lev = jnp.asarray(_gla_level_map(SCORE_BLK))
    qk_w = GLA_HEADS * GLA_DK
    row = lambda blk: (lambda b, c: (b * nc + c, blk))
    return pl.pallas_call(
        _gla_body,
        grid=(batch, nc),
        in_specs=[pl.BlockSpec((chunk, qk_w), row(R_GQ // qk_w)),
                  pl.BlockSpec((chunk, qk_w), row(R_GK // qk_w)),
                  pl.BlockSpec((chunk, GLA_W), row(R_GV // GLA_W)),
                  pl.BlockSpec((chunk, GLA_W), row(R_GR // GLA_W)),
                  pl.BlockSpec((chunk, qk_w), row(0)),
                  pl.BlockSpec((SCORE_BLK, SCORE_BLK), lambda b, c: (0, 0)),
                  pl.BlockSpec((1, GLA_W), lambda b, c: (0, 0))],
        out_specs=[pl.BlockSpec((chunk, GLA_W), row(0)),
                   pl.BlockSpec((1, GLA_HEADS, GLA_DK, GLA_DV), lambda b, c: (b, 0, 0, 0))],
        out_shape=[jax.ShapeDtypeStruct((batch * seq, GLA_W), BF16),
                   jax.ShapeDtypeStruct((batch, GLA_HEADS, GLA_DK, GLA_DV), F32)],
        scratch_shapes=[pltpu.VMEM((GLA_HEADS, GLA_DV, GLA_DK), F32)],
        compiler_params=_params(("arbitrary", "arbitrary")),
        name="gla_prompt",
    )(rest, rest, rest, rest, la, lev, g_out.reshape(1, GLA_W))


def _columns(rows):
    n = rows[0].shape[1]
    pad = (-len(rows)) % 8
    stack = jnp.concatenate(rows + [jnp.zeros((pad, n), F32)] if pad else rows, axis=0)
    return stack.T


def _state_body(lg_ref, qk_ref, rest_ref, la_ref, sr_ref, sg_ref, gro_ref, ggo_ref, o_ref, sr_out, sg_out):
    nb = qk_ref.shape[0]
    for i in range(nb):
        qkrow = qk_ref[i]
        rrow = rest_ref[i]
        larow = la_ref[i]
        rcols = _columns([qkrow[:, j * RET_DK:(j + 1) * RET_DK] for j in range(2 * RET_HEADS)])
        gq = [rrow[:, R_GQ + h * GLA_DK: R_GQ + (h + 1) * GLA_DK] * (GLA_DK ** -0.5) for h in range(GLA_HEADS)]
        gk = [rrow[:, R_GK + h * GLA_DK: R_GK + (h + 1) * GLA_DK] for h in range(GLA_HEADS)]
        ga = [jnp.exp(larow[:, h * GLA_DK:(h + 1) * GLA_DK]) for h in range(GLA_HEADS)]
        gcols = _columns(gq + gk + ga)
        outs = []
        for h in range(RET_HEADS):
            v = rrow[:, R_RV + h * RET_DV: R_RV + (h + 1) * RET_DV]
            gate = rrow[:, R_RG + h * RET_DV: R_RG + (h + 1) * RET_DV]
            gamma = jnp.exp(lg_ref[h][:1, :1])
            qc = rcols[:, h:h + 1]
            kc = rcols[:, RET_HEADS + h:RET_HEADS + h + 1]
            s_new = sr_ref[0, i, h] * gamma + kc * v
            sr_out[0, i, h] = s_new
            o = jnp.sum(qc * s_new, axis=0, keepdims=True)
            outs.append(_rms(o) * gro_ref[:, h * RET_DV:(h + 1) * RET_DV] * _silu(gate))
        for h in range(GLA_HEADS):
            v = rrow[:, R_GV + h * GLA_DV: R_GV + (h + 1) * GLA_DV]
            gate = rrow[:, R_GR + h * GLA_DV: R_GR + (h + 1) * GLA_DV]
            qc = gcols[:, h:h + 1]
            kc = gcols[:, GLA_HEADS + h:GLA_HEADS + h + 1]
            ac = gcols[:, 2 * GLA_HEADS + h:2 * GLA_HEADS + h + 1]
            s_new = sg_ref[0, i, h] * ac + kc * v
            sg_out[0, i, h] = s_new
            o = jnp.sum(qc * s_new, axis=0, keepdims=True)
            outs.append(_rms(o) * ggo_ref[:, h * GLA_DV:(h + 1) * GLA_DV] * _silu(gate))
        o_ref[i] = jnp.concatenate(outs, axis=-1)


def _outproj_body(oa_ref, ob_ref, x_ref, gt_ref, sc_ref, sh_ref, g_ref, w_ref, x1_ref, h_ref):
    ka = oa_ref.shape[1]
    mix = _dot(oa_ref[...].astype(BF16), w_ref[:ka, :]) + _dot(ob_ref[...].astype(BF16), w_ref[ka:, :])
    x1 = x_ref[...] + gt_ref[0] * mix
    x1_ref[...] = x1
    h = _rms(x1) * g_ref[...]
    h_ref[...] = (h * (1.0 + sc_ref[0]) + sh_ref[0]).astype(BF16)


def _outproj(o_a, o_b, x, gt, sc, sh, g_ffn, w_out_bf16, *, tm, rows_per_mod):
    m, d = x.shape
    ka, kb = o_a.shape[1], o_b.shape[1]
    mod_spec = pl.BlockSpec((1, gt.shape[1], d), lambda i: ((i * tm) // rows_per_mod, 0, 0))
    return pl.pallas_call(
        _outproj_body,
        grid=(m // tm,),
        in_specs=[pl.BlockSpec((tm, ka), lambda i: (i, 0)),
                  pl.BlockSpec((tm, kb), lambda i: (i, 0)),
                  pl.BlockSpec((tm, d), lambda i: (i, 0)),
                  mod_spec, mod_spec, mod_spec,
                  pl.BlockSpec((1, d), lambda i: (0, 0)),
                  pl.BlockSpec((ka + kb, d), lambda i: (0, 0), pipeline_mode=pl.Buffered(1))],
        out_specs=[pl.BlockSpec((tm, d), lambda i: (i, 0)),
                   pl.BlockSpec((tm, d), lambda i: (i, 0))],
        out_shape=[jax.ShapeDtypeStruct((m, d), F32),
                   jax.ShapeDtypeStruct((m, d), BF16)],
        compiler_params=_params(("arbitrary",)),
        name="out_proj",
    )(o_a, o_b, x, gt, sc, sh, g_ffn.reshape(1, d), w_out_bf16)


LANES = 128
HALO = 8


def _ffn_up_body(h_ref, wa_ref, wb_ref, cwa_ref, cwb_ref, cba_ref, cbb_ref, g_ref, cs_ref, win_ref, tail_ref, *, tiles_per_seq):
    i, f = pl.program_id(0), pl.program_id(1)
    hb = h_ref[...]
    tm = hb.shape[0]
    first = (i % tiles_per_seq) == 0
    ucs = []
    for part, (w_ref, cw_ref, cb_ref) in enumerate(((wa_ref, cwa_ref, cba_ref), (wb_ref, cwb_ref, cbb_ref))):
        u = _dot(hb, w_ref[...].astype(BF16))
        prev = jnp.where(first, 0.0, tail_ref[part, f])
        pieces = []
        for c in range(u.shape[1] // LANES):
            lc = slice(c * LANES, (c + 1) * LANES)
            win_ref[part, c, 0:HALO, :] = prev[:, lc]
            win_ref[part, c, HALO:HALO + tm, :] = u[:, lc]
            cw, cb = cw_ref[:, lc], cb_ref[:, lc]
            pieces.append(cb + cw[0:1] * win_ref[part, c, HALO - 2:HALO - 2 + tm, :]
                          + cw[1:2] * win_ref[part, c, HALO - 1:HALO - 1 + tm, :] + cw[2:3] * u[:, lc])
        ucs.append(jnp.concatenate(pieces, axis=-1))
        tail_ref[part, f] = u[tm - HALO:]
        for r in range(CONV_W - 1):
            cs_ref[0, 0, r, part:part + 1, :] = u[tm - (CONV_W - 1) + r: tm - (CONV_W - 1) + r + 1]
    g_ref[...] = (_silu(ucs[0]) * ucs[1]).astype(g_ref.dtype)


def _ffn_up_state_body(h_ref, wa_ref, wb_ref, cwa_ref, cwb_ref, cba_ref, cbb_ref,
                       lg_ref, qk_ref, rest_ref, la_ref, sr_ref, sg_ref, gro_ref, ggo_ref,
                       g_ref, cs_ref, o_ref, sr_out, sg_out, win_ref, tail_ref, *, tiles_per_seq):
    _ffn_up_body(h_ref, wa_ref, wb_ref, cwa_ref, cwb_ref, cba_ref, cbb_ref, g_ref, cs_ref, win_ref, tail_ref,
                 tiles_per_seq=tiles_per_seq)
    _state_body(lg_ref, qk_ref, rest_ref, la_ref, sr_ref, sg_ref, gro_ref, ggo_ref, o_ref, sr_out, sg_out)


def _ffn_up_prompt(h, w_up, conv_w, conv_b, qk_s, rest_s, la_s, state_ret, state_gla, g_ret_out, g_gla_out,
                   *, batch, seq, tm, tf, nb):
    m, d = h.shape
    ff = w_up.shape[1] // 2
    nf = ff // tf
    cb = conv_b.reshape(1, 2 * ff)
    tps = seq // tm
    ms = qk_s.shape[0]
    nblk = ms // nb
    assert (m // tm) * nf >= nblk
    mix = RET_W + GLA_W
    lg = jnp.broadcast_to(_ret_log_gamma()[:, None, None], (RET_HEADS, 8, 128))
    sblk = lambda i, f: jnp.minimum(i * nf + f, nblk - 1)
    ret_spec = pl.BlockSpec((1, nb, RET_HEADS, RET_DK, RET_DV), lambda i, f: (0, sblk(i, f), 0, 0, 0))
    gla_spec = pl.BlockSpec((1, nb, GLA_HEADS, GLA_DK, GLA_DV), lambda i, f: (0, sblk(i, f), 0, 0, 0))
    row_spec = lambda w: pl.BlockSpec((nb, 1, w), lambda i, f: (sblk(i, f), 0, 0))
    body = functools.partial(_ffn_up_state_body, tiles_per_seq=tps)
    gate, tails, o_s, s_ret, s_gla = pl.pallas_call(
        body,
        grid=(m // tm, nf),
        in_specs=[pl.BlockSpec((tm, d), lambda i, f: (i, 0)),
                  pl.BlockSpec((d, tf), lambda i, f: (0, f)),
                  pl.BlockSpec((d, tf), lambda i, f: (0, nf + f)),
                  pl.BlockSpec((CONV_W, tf), lambda i, f: (0, f)),
                  pl.BlockSpec((CONV_W, tf), lambda i, f: (0, nf + f)),
                  pl.BlockSpec((1, tf), lambda i, f: (0, f)),
                  pl.BlockSpec((1, tf), lambda i, f: (0, nf + f)),
                  pl.BlockSpec((RET_HEADS, 8, 128), lambda i, f: (0, 0, 0)),
                  row_spec(QK_W), row_spec(REST_W), row_spec(la_s.shape[1]),
                  ret_spec, gla_spec,
                  pl.BlockSpec((1, RET_W), lambda i, f: (0, 0)),
                  pl.BlockSpec((1, GLA_W), lambda i, f: (0, 0))],
        out_specs=[pl.BlockSpec((tm, tf), lambda i, f: (i, f)),
                   pl.BlockSpec((1, 1, CONV_W - 1, 2, tf), lambda i, f: (i // tps, i % tps, 0, 0, f)),
                   row_spec(mix), ret_spec, gla_spec],
        out_shape=[jax.ShapeDtypeStruct((m, ff), BF16),
                   jax.ShapeDtypeStruct((batch, tps, CONV_W - 1, 2, ff), F32),
                   jax.ShapeDtypeStruct((ms, 1, mix), F32),
                   jax.ShapeDtypeStruct(state_ret.shape, F32),
                   jax.ShapeDtypeStruct(state_gla.shape, F32)],
        scratch_shapes=[pltpu.VMEM((2, tf // LANES, HALO + tm, LANES), F32), pltpu.VMEM((2, nf, HALO, tf), F32)],
        compiler_params=_params(("arbitrary", "arbitrary")),
        name="ffn_up_prompt",
    )(h, w_up, w_up, conv_w, conv_w, cb, cb,
      lg, qk_s.astype(F32).reshape(ms, 1, QK_W), rest_s.reshape(ms, 1, REST_W), la_s.reshape(ms, 1, la_s.shape[1]),
      state_ret, state_gla, g_ret_out.reshape(1, RET_W), g_gla_out.reshape(1, GLA_W))
    return gate, tails[:, tps - 1], o_s.reshape(ms, mix), s_ret, s_gla


def _ffn_up_step_body(h_ref, w_ref, cw_ref, cb_ref, st_ref, g_ref, cs_ref, uca_ref, *, nf):
    j = pl.program_id(0)
    u = _dot(h_ref[...], w_ref[...].astype(BF16))
    s1 = st_ref[0, :, 1, :]
    cw = cw_ref[...]
    uc = cb_ref[...] + cw[0:1] * st_ref[0, :, 0, :] + cw[1:2] * s1 + cw[2:3] * u
    cs_ref[0, :, 0, :] = s1
    cs_ref[0, :, 1, :] = u

    @pl.when(j < nf)
    def _():
        uca_ref[j] = uc

    @pl.when(j >= nf)
    def _():
        g_ref[...] = (_silu(uca_ref[j - nf]) * uc).astype(g_ref.dtype)


def _ffn_up_step(h, w_up, conv_w, conv_b, state_conv, *, tf):
    m, d = h.shape
    ff = w_up.shape[1] // 2
    nf = ff // tf
    st_spec = pl.BlockSpec((1, m, CONV_W - 1, tf), lambda j: (0, 0, 0, j))
    return pl.pallas_call(
        functools.partial(_ffn_up_step_body, nf=nf),
        grid=(2 * nf,),
        in_specs=[pl.BlockSpec((m, d), lambda j: (0, 0)),
                  pl.BlockSpec((d, tf), lambda j: (0, j)),
                  pl.BlockSpec((CONV_W, tf), lambda j: (0, j)),
                  pl.BlockSpec((1, tf), lambda j: (0, j)),
                  st_spec],
        out_specs=[pl.BlockSpec((m, tf), lambda j: (0, jnp.maximum(j - nf, 0))), st_spec],
        out_shape=[jax.ShapeDtypeStruct((m, ff), BF16),
                   jax.ShapeDtypeStruct(state_conv.shape, F32)],
        scratch_shapes=[pltpu.VMEM((nf, m, tf), F32)],
        compiler_params=_params(("arbitrary",)),
        name="ffn_up_step",
    )(h, w_up, conv_w, conv_b.reshape(1, 2 * ff), state_conv)


def _ffn_down_body(g_ref, w_ref, x1_ref, gt_ref, gf_ref, y_ref):
    k = pl.program_id(1)

    @pl.when(k == 0)
    def _():
        y_ref[...] = jnp.zeros_like(y_ref)

    y_ref[...] += _dot(g_ref[...], w_ref[...].astype(BF16))

    @pl.when(k == pl.num_programs(1) - 1)
    def _():
        x2 = x1_ref[...] + gt_ref[0] * y_ref[...]
        y_ref[...] = _rms(x2) * gf_ref[...]


def _ffn_down(g, w_down, x1, gt, g_final, *, tm, tk, rows_per_mod):
    m, d = x1.shape
    ff = g.shape[1]
    return pl.pallas_call(
        _ffn_down_body,
        grid=(m // tm, ff // tk),
        in_specs=[pl.BlockSpec((tm, tk), lambda i, k: (i, k)),
                  pl.BlockSpec((tk, d), lambda i, k: (k, 0)),
                  pl.BlockSpec((tm, d), lambda i, k: (i, 0)),
                  pl.BlockSpec((1, gt.shape[1], d), lambda i, k: ((i * tm) // rows_per_mod, 0, 0)),
                  pl.BlockSpec((1, d), lambda i, k: (0, 0))],
        out_specs=pl.BlockSpec((tm, d), lambda i, k: (i, 0)),
        out_shape=jax.ShapeDtypeStruct((m, d), F32),
        compiler_params=_params(("arbitrary", "arbitrary")),
        name="ffn_down",
    )(g, w_down, x1, gt, g_final.reshape(1, d))


def _rope_tables(pos):
    half = RET_DK // 2
    inv = ROPE_THETA ** (-jnp.arange(half, dtype=F32) / half)
    ang = pos.astype(F32)[:, None] * inv[None, :]
    return jnp.cos(ang), jnp.sin(ang)


def kernel(x_prompt, x_sample, c_prompt, c_sample, state_ret, state_gla, state_conv, w_ada, b_ada, g_attn, w_in, w_a2, b_a2, g_ret_out, g_gla_out, w_out, g_ffn, w_up, conv_w, conv_b, w_down, g_final):
    bp, t_p, d = x_prompt.shape
    bs, t_s, _ = x_sample.shape
    assert t_s == 1 and w_ada.shape[0] == 1
    mp = bp * t_p
    w_ada, b_ada, g_attn, w_in, w_a2, b_a2, g_ret_out, g_gla_out, w_out, g_ffn, w_up, conv_w, conv_b, w_down = (
        a[0] for a in (w_ada, b_ada, g_attn, w_in, w_a2, b_a2, g_ret_out, g_gla_out, w_out, g_ffn, w_up, conv_w, conv_b, w_down))

    mod = _ada(jnp.concatenate([c_prompt, c_sample], axis=0), w_ada, b_ada)
    sh1p, sc1p, gt1p, sh2p, sc2p, gt2p = (mod[:bp, i * d:(i + 1) * d].reshape(bp, 1, d) for i in range(6))
    sh1s, sc1s, gt1s, sh2s, sc2s, gt2s = (mod[bp:, i * d:(i + 1) * d].reshape(1, bs, d) for i in range(6))

    cos_p, sin_p = _rope_tables(jnp.arange(t_p, dtype=jnp.int32))
    cos_s, sin_s = (jnp.broadcast_to(t, (bs, RET_DK // 2)) for t in _rope_tables(PAST_LEN + jnp.arange(t_s, dtype=jnp.int32)))
    w_in_b = w_in.astype(BF16)
    w_out_b = w_out.astype(BF16)

    xp = x_prompt.reshape(mp, d)
    xs = x_sample.reshape(bs, d)
    qk_s, rest_s, la_s = _inproj(xs, sc1s, sh1s, g_attn, w_in_b, w_a2, b_a2, cos_s, sin_s, tm=bs, rows_per_mod=bs)
    qk_p, rest_p, la_p = _inproj(xp, sc1p, sh1p, g_attn, w_in_b, w_a2, b_a2, cos_p, sin_p, tm=1024, rows_per_mod=t_p)
    o_ret, s_ret_p = _retention_prompt(qk_p, rest_p, g_ret_out, batch=bp, seq=t_p, chunk=256)
    o_gla, s_gla_p = _gla_prompt(rest_p, la_p, g_gla_out, batch=bp, seq=t_p, chunk=256)
    x1_p, h2_p = _outproj(o_ret, o_gla, xp, gt1p, sc2p, sh2p, g_ffn, w_out_b, tm=512, rows_per_mod=t_p)
    g_p, cs_p, o_s, s_ret_s, s_gla_s = _ffn_up_prompt(
        h2_p, w_up, conv_w, conv_b, qk_s, rest_s, la_s, state_ret, state_gla, g_ret_out, g_gla_out,
        batch=bp, seq=t_p, tm=1024, tf=512, nb=2)
    y_p = _ffn_down(g_p, w_down, x1_p, gt2p, g_final, tm=1024, tk=512, rows_per_mod=t_p)

    x1_s, h2_s = _outproj(o_s[:, :RET_W], o_s[:, RET_W:], xs, gt1s, sc2s, sh2s, g_ffn, w_out_b, tm=bs, rows_per_mod=bs)
    g_s, cs_s = _ffn_up_step(h2_s, w_up, conv_w, conv_b, state_conv, tf=512)
    y_s = _ffn_down(g_s, w_down, x1_s, gt2s, g_final, tm=bs, tk=512, rows_per_mod=bs)

    return (y_p.reshape(bp, t_p, d), y_s.reshape(bs, t_s, d),
            s_ret_p[None], s_ret_s, s_gla_p[None], s_gla_s,
            cs_p.reshape(1, bp, CONV_W - 1, -1), cs_s)
```

```python
import functools

import numpy as np
import jax
import jax.numpy as jnp
from jax import lax
from jax.experimental import pallas as pl
from jax.experimental.pallas import tpu as pltpu

F32 = jnp.float32
BF16 = jnp.bfloat16

RET_HEADS = 4
RET_DK = 256
RET_DV = 256
GLA_HEADS = 4
GLA_DK = 128
GLA_DV = 256
GLA_RANK = 16
GLA_TAU = 16.0
ROPE_THETA = 10000.0
PAST_LEN = 16384
CONV_W = 3
EPS = 1e-6

RET_W = RET_HEADS * RET_DV
GLA_W = GLA_HEADS * GLA_DV
QK_W = 2 * RET_HEADS * RET_DK
R_RV = 0
R_RG = R_RV + RET_W
R_GQ = R_RG + RET_W
R_GK = R_GQ + GLA_HEADS * GLA_DK
R_GV = R_GK + GLA_HEADS * GLA_DK
R_GR = R_GV + GLA_W
REST_W = R_GR + GLA_W
IN_MAIN = QK_W + REST_W

VMEM_LIMIT_BYTES = 56 * 1024 * 1024


def _params(semantics):
    return pltpu.CompilerParams(dimension_semantics=semantics, vmem_limit_bytes=VMEM_LIMIT_BYTES)


def _dot(a, b):
    return jnp.dot(a, b, preferred_element_type=F32)


def _dot_nt(a, b):
    return lax.dot_general(a, b, (((1,), (1,)), ((), ())), preferred_element_type=F32)


def _dot_tn(a, b):
    return lax.dot_general(a, b, (((0,), (0,)), ((), ())), preferred_element_type=F32)


def _silu(x):
    return x * jax.nn.sigmoid(x)


def _rms(x):
    return x * lax.rsqrt(jnp.mean(x * x, axis=-1, keepdims=True) + EPS)


def _ada_body(c_ref, w_ref, b_ref, o_ref):
    s = _silu(c_ref[...]).astype(BF16)
    o_ref[...] = _dot(s, w_ref[...].astype(BF16)) + b_ref[...]


def _ada(c_all, w_ada, b_ada, tn=512):
    r, d = c_all.shape
    n = w_ada.shape[1]
    return pl.pallas_call(
        _ada_body,
        grid=(n // tn,),
        in_specs=[pl.BlockSpec((r, d), lambda j: (0, 0)),
                  pl.BlockSpec((d, tn), lambda j: (0, j)),
                  pl.BlockSpec((1, tn), lambda j: (0, j))],
        out_specs=pl.BlockSpec((r, tn), lambda j: (0, j)),
        out_shape=jax.ShapeDtypeStruct((r, n), F32),
        compiler_params=_params(("arbitrary",)),
        name="ada_mod",
    )(c_all, w_ada, b_ada.reshape(1, n))


def _inproj_body(x_ref, sc_ref, sh_ref, g_ref, w_ref, wga_ref, wa2_ref, ba2_ref, cos_ref, sin_ref,
                 qk_ref, rest_ref, la_ref, h_ref, *, n_qk):
    j = pl.program_id(1)

    @pl.when(j == 0)
    def _():
        gain = g_ref[...] * (1.0 + sc_ref[0])
        hb = (_rms(x_ref[...]) * gain + sh_ref[0]).astype(BF16)
        h_ref[...] = hb
        ga = _dot(hb, wga_ref[...])
        z = _dot(ga.astype(BF16), wa2_ref[...].astype(BF16)) + ba2_ref[...]
        la_ref[...] = (jnp.minimum(z, 0.0) - jnp.log1p(jnp.exp(-jnp.abs(z)))) * (1.0 / GLA_TAU)

    @pl.when(j < n_qk)
    def _():
        acc = _dot(h_ref[...], w_ref[...])
        cos, sin = cos_ref[...], sin_ref[...]
        half = RET_DK // 2
        scale = jnp.where(j >= n_qk // 2, RET_DK ** -0.5, 1.0)
        outs = []
        for hh in range(acc.shape[1] // RET_DK):
            x1 = acc[:, hh * RET_DK: hh * RET_DK + half]
            x2 = acc[:, hh * RET_DK + half: (hh + 1) * RET_DK]
            outs += [x1 * cos - x2 * sin, x1 * sin + x2 * cos]
        qk_ref[...] = (jnp.concatenate(outs, axis=-1) * scale).astype(qk_ref.dtype)

    @pl.when(j >= n_qk)
    def _():
        rest_ref[...] = _dot(h_ref[...], w_ref[...])


def _inproj(x, sc, sh, g_attn, w_in_b, w_a2, b_a2, cos, sin, *, tm, rows_per_mod, tn=1024, x_buffers=2):
    m, d = x.shape
    nq = w_a2.shape[1]
    n_qk = QK_W // tn
    tab_tiles = cos.shape[0] // tm
    mod_spec = pl.BlockSpec((1, sc.shape[1], d), lambda i, j: ((i * tm) // rows_per_mod, 0, 0))
    tab_spec = pl.BlockSpec((tm, RET_DK // 2), lambda i, j: (i % tab_tiles, 0))
    return pl.pallas_call(
        functools.partial(_inproj_body, n_qk=n_qk),
        grid=(m // tm, IN_MAIN // tn),
        in_specs=[pl.BlockSpec((tm, d), lambda i, j: (i, 0), pipeline_mode=pl.Buffered(x_buffers)),
                  mod_spec, mod_spec,
                  pl.BlockSpec((1, d), lambda i, j: (0, 0)),
                  pl.BlockSpec((d, tn), lambda i, j: (0, j)),
                  pl.BlockSpec((d, GLA_RANK), lambda i, j: (0, 0)),
                  pl.BlockSpec((GLA_RANK, nq), lambda i, j: (0, 0)),
                  pl.BlockSpec((1, nq), lambda i, j: (0, 0)),
                  tab_spec, tab_spec],
        out_specs=[pl.BlockSpec((tm, tn), lambda i, j: (i, jnp.minimum(j, n_qk - 1))),
                   pl.BlockSpec((tm, tn), lambda i, j: (i, jnp.maximum(j - n_qk, 0))),
                   pl.BlockSpec((tm, nq), lambda i, j: (i, 0))],
        out_shape=[jax.ShapeDtypeStruct((m, QK_W), BF16),
                   jax.ShapeDtypeStruct((m, REST_W), F32),
                   jax.ShapeDtypeStruct((m, nq), F32)],
        scratch_shapes=[pltpu.VMEM((tm, d), BF16)],
        compiler_params=_params(("arbitrary", "arbitrary")),
        name="in_proj",
    )(x, sc, sh, g_attn.reshape(1, d), w_in_b, w_in_b[:, IN_MAIN:], w_a2, b_a2.reshape(1, nq), cos, sin)


def _ret_log_gamma():
    return jnp.log1p(-jnp.exp2(-5.0 - jnp.arange(RET_HEADS, dtype=F32)))


def _ret_decay_matrix(chunk):
    idx = jnp.arange(chunk, dtype=F32)
    rel = idx[:, None] - idx[None, :]
    lg = _ret_log_gamma()
    return jnp.where(rel[None] >= 0, jnp.exp(jnp.maximum(rel, 0.0)[None] * lg[:, None, None]), 0.0)


def _ret_body(dmat_ref, lg_ref, q_ref, k_ref, v_ref, g_ref, gout_ref, o_ref, sout_ref, s_ref):
    c = pl.program_id(1)
    cl = q_ref.shape[0]

    @pl.when(c == 0)
    def _():
        s_ref[...] = jnp.zeros_like(s_ref)

    idx = lax.broadcasted_iota(jnp.int32, (cl, 1), 0).astype(F32)
    for h in range(RET_HEADS):
        sl = slice(h * RET_DV, (h + 1) * RET_DV)
        lg = lg_ref[h][:1, :1]
        q_dec = jnp.exp((idx + 1.0) * lg)
        k_dec = jnp.exp((cl - 1.0 - idx) * lg)
        c_dec = jnp.exp(cl * lg)
        qb, kb = q_ref[:, sl], k_ref[:, sl]
        v = v_ref[:, sl]
        vb = v.astype(BF16)
        s_old = s_ref[h]
        scores = _dot_nt(qb, kb) * dmat_ref[h]
        o = _dot(scores.astype(BF16), vb) + _dot(qb, s_old.astype(BF16)) * q_dec
        s_new = s_old * c_dec + _dot_tn(kb, (v * k_dec).astype(BF16))
        s_ref[h] = s_new
        o_ref[:, sl] = (_rms(o) * gout_ref[:, sl] * _silu(g_ref[:, sl])).astype(o_ref.dtype)

    @pl.when(c == pl.num_programs(1) - 1)
    def _():
        sout_ref[0] = s_ref[...]


def _retention_prompt(qk, rest, g_out, *, batch, seq, chunk):
    nc = seq // chunk
    lg = jnp.broadcast_to(_ret_log_gamma()[:, None, None], (RET_HEADS, 8, 128))
    row = lambda blk: (lambda b, c: (b * nc + c, blk))
    return pl.pallas_call(
        _ret_body,
        grid=(batch, nc),
        in_specs=[pl.BlockSpec((RET_HEADS, chunk, chunk), lambda b, c: (0, 0, 0)),
                  pl.BlockSpec((RET_HEADS, 8, 128), lambda b, c: (0, 0, 0)),
                  pl.BlockSpec((chunk, RET_W), row(0)),
                  pl.BlockSpec((chunk, RET_W), row(1)),
                  pl.BlockSpec((chunk, RET_W), row(R_RV // RET_W)),
                  pl.BlockSpec((chunk, RET_W), row(R_RG // RET_W)),
                  pl.BlockSpec((1, RET_W), lambda b, c: (0, 0))],
        out_specs=[pl.BlockSpec((chunk, RET_W), row(0)),
                   pl.BlockSpec((1, RET_HEADS, RET_DK, RET_DV), lambda b, c: (b, 0, 0, 0))],
        out_shape=[jax.ShapeDtypeStruct((batch * seq, RET_W), BF16),
                   jax.ShapeDtypeStruct((batch, RET_HEADS, RET_DK, RET_DV), F32)],
        scratch_shapes=[pltpu.VMEM((RET_HEADS, RET_DK, RET_DV), F32)],
        compiler_params=_params(("arbitrary", "arbitrary")),
        name="retention_prompt",
    )(_ret_decay_matrix(chunk), lg, qk, qk, rest, rest, g_out.reshape(1, RET_W))


def _gla_level_map(c):
    t = np.arange(c)[:, None]
    s = np.arange(c)[None, :]
    x = np.bitwise_xor(t, s)
    lev = np.floor(np.log2(np.maximum(x, 1))).astype(np.int32)
    lev = np.where(t == s, -1, lev)
    lev = np.where(t < s, -2, lev)
    return lev.astype(np.int32)


SCORE_BLK = 128
LOG2E = 1.4426950408889634


def _gla_chunk(q, k, la2, lev, uppers):
    cl = q.shape[0]
    nblk = cl // SCORE_BLK
    blk = lambda x, i: x[i * SCORE_BLK:(i + 1) * SCORE_BLK]
    p = la2
    tot = la2
    diag = [jnp.zeros((SCORE_BLK, SCORE_BLK), F32) for _ in range(nblk)]
    off = {}
    level = 0
    half = 1
    while half < cl:
        upper = uppers[level]
        z = (jnp.where(upper, q, k) * jnp.exp2(jnp.where(upper, p, tot - p))).astype(BF16)
        if half < SCORE_BLK:
            for i in range(nblk):
                diag[i] = jnp.where(lev == level, _dot_nt(blk(z, i), blk(z, i)), diag[i])
        else:
            hb = half // SCORE_BLK
            for i in range(nblk):
                if (i // hb) % 2 == 1:
                    base = (i // (2 * hb)) * 2 * hb
                    for j in range(base, base + hb):
                        off[(i, j)] = _dot_nt(blk(z, i), blk(z, j))
        partner = jnp.where(upper, pltpu.roll(tot, half, 0), pltpu.roll(tot, cl - half, 0))
        p = p + jnp.where(upper, partner, 0.0)
        tot = tot + partner
        half *= 2
        level += 1
    dg = jnp.sum(q * k, axis=-1, keepdims=True)
    rows = []
    for i in range(nblk):
        d_i = jnp.where(lev == -1, blk(dg, i), diag[i])
        rows.append(jnp.concatenate([off[(i, j)] for j in range(i)] + [d_i], axis=-1))
    return rows, p, tot


def _gla_body(q_ref, k_ref, v_ref, g_ref, la_ref, lev_ref, gout_ref, o_ref, sout_ref, st_ref):
    c = pl.program_id(1)
    cl = q_ref.shape[0]

    @pl.when(c == 0)
    def _():
        st_ref[...] = jnp.zeros_like(st_ref)

    lev = lev_ref[...]
    row = lax.broadcasted_iota(jnp.int32, (cl, GLA_DK), 0)
    uppers = []
    half = 1
    while half < cl:
        uppers.append((row & half) != 0)
        half *= 2
    for h in range(GLA_HEADS):
        ks = slice(h * GLA_DK, (h + 1) * GLA_DK)
        vs = slice(h * GLA_DV, (h + 1) * GLA_DV)
        q = q_ref[:, ks] * (GLA_DK ** -0.5)
        k = k_ref[:, ks]
        rows, p, tot = _gla_chunk(q, k, la_ref[:, ks] * LOG2E, lev, uppers)
        vb = v_ref[:, vs].astype(BF16)
        st_old = st_ref[h]
        qt = (q * jnp.exp2(p)).astype(BF16)
        intra = jnp.concatenate([_dot(r.astype(BF16), vb[:r.shape[1]]) for r in rows], axis=0)
        o = intra + _dot_nt(qt, st_old.astype(BF16))
        kt = (k * jnp.exp2(tot - p)).astype(BF16)
        st_ref[h] = st_old * jnp.exp2(tot[0:1, :]) + _dot_tn(vb, kt)
        o_ref[:, vs] = (_rms(o) * gout_ref[:, vs] * _silu(g_ref[:, vs])).astype(o_ref.dtype)

    @pl.when(c == pl.num_programs(1) - 1)
    def _():
        for h in range(GLA_HEADS):
            sout_ref[0, h] = st_ref[h].T


def _gla_prompt(rest, la, g_out, *, batch, seq, chunk):
    nc = seq // chunk
    lev = jnp.asarray(_gla_level_map(SCORE_BLK))
    qk_w = GLA_HEADS * GLA_DK
    row = lambda blk: (lambda b, c: (b * nc + c, blk))
    return pl.pallas_call(
        _gla_body,
        grid=(batch, nc),
        in_specs=[pl.BlockSpec((chunk, qk_w), row(R_GQ // qk_w)),
                  pl.BlockSpec((chunk, qk_w), row(R_GK // qk_w)),
                  pl.BlockSpec((chunk, GLA_W), row(R_GV // GLA_W)),
                  pl.BlockSpec((chunk, GLA_W), row(R_GR // GLA_W)),
                  pl.BlockSpec((chunk, qk_w), row(0)),
                  pl.BlockSpec((SCORE_BLK, SCORE_BLK), lambda b, c: (0, 0)),
                  pl.BlockSpec((1, GLA_W), lambda b, c: (0, 0))],
        out_specs=[pl.BlockSpec((chunk, GLA_W), row(0)),
                   pl.BlockSpec((1, GLA_HEADS, GLA_DK, GLA_DV), lambda b, c: (b, 0, 0, 0))],
        out_shape=[jax.ShapeDtypeStruct((batch * seq, GLA_W), BF16),
                   jax.ShapeDtypeStruct((batch, GLA_HEADS, GLA_DK, GLA_DV), F32)],
        scratch_shapes=[pltpu.VMEM((GLA_HEADS, GLA_DV, GLA_DK), F32)],
        compiler_params=_params(("arbitrary", "arbitrary")),
        name="gla_prompt",
    )(rest, rest, rest, rest, la, lev, g_out.reshape(1, GLA_W))


def _columns(rows):
    n = rows[0].shape[1]
    pad = (-len(rows)) % 8
    stack = jnp.concatenate(rows + [jnp.zeros((pad, n), F32)] if pad else rows, axis=0)
    return stack.T


def _state_body(lg_ref, qk_ref, rest_ref, la_ref, sr_ref, sg_ref, gro_ref, ggo_ref, o_ref, sr_out, sg_out):
    nb = qk_ref.shape[0]
    for i in range(nb):
        qkrow = qk_ref[i]
        rrow = rest_ref[i]
        larow = la_ref[i]
        rcols = _columns([qkrow[:, j * RET_DK:(j + 1) * RET_DK] for j in range(2 * RET_HEADS)])
        gq = [rrow[:, R_GQ + h * GLA_DK: R_GQ + (h + 1) * GLA_DK] * (GLA_DK ** -0.5) for h in range(GLA_HEADS)]
        gk = [rrow[:, R_GK + h * GLA_DK: R_GK + (h + 1) * GLA_DK] for h in range(GLA_HEADS)]
        ga = [jnp.exp(larow[:, h * GLA_DK:(h + 1) * GLA_DK]) for h in range(GLA_HEADS)]
        gcols = _columns(gq + gk + ga)
        outs = []
        for h in range(RET_HEADS):
            v = rrow[:, R_RV + h * RET_DV: R_RV + (h + 1) * RET_DV]
            gate = rrow[:, R_RG + h * RET_DV: R_RG + (h + 1) * RET_DV]
            gamma = jnp.exp(lg_ref[h][:1, :1])
            qc = rcols[:, h:h + 1]
            kc = rcols[:, RET_HEADS + h:RET_HEADS + h + 1]
            s_new = sr_ref[0, i, h] * gamma + kc * v
            sr_out[0, i, h] = s_new
            o = jnp.sum(qc * s_new, axis=0, keepdims=True)
            outs.append(_rms(o) * gro_ref[:, h * RET_DV:(h + 1) * RET_DV] * _silu(gate))
        for h in range(GLA_HEADS):
            v = rrow[:, R_GV + h * GLA_DV: R_GV + (h + 1) * GLA_DV]
            gate = rrow[:, R_GR + h * GLA_DV: R_GR + (h + 1) * GLA_DV]
            qc = gcols[:, h:h + 1]
            kc = gcols[:, GLA_HEADS + h:GLA_HEADS + h + 1]
            ac = gcols[:, 2 * GLA_HEADS + h:2 * GLA_HEADS + h + 1]
            s_new = sg_ref[0, i, h] * ac + kc * v
            sg_out[0, i, h] = s_new
            o = jnp.sum(qc * s_new, axis=0, keepdims=True)
            outs.append(_rms(o) * ggo_ref[:, h * GLA_DV:(h + 1) * GLA_DV] * _silu(gate))
        o_ref[i] = jnp.concatenate(outs, axis=-1)


def _outproj_body(oa_ref, ob_ref, x_ref, gt_ref, sc_ref, sh_ref, g_ref, w_ref, x1_ref, h_ref):
    ka = oa_ref.shape[1]
    mix = _dot(oa_ref[...].astype(BF16), w_ref[:ka, :]) + _dot(ob_ref[...].astype(BF16), w_ref[ka:, :])
    x1 = x_ref[...] + gt_ref[0] * mix
    x1_ref[...] = x1
    h = _rms(x1) * g_ref[...]
    h_ref[...] = (h * (1.0 + sc_ref[0]) + sh_ref[0]).astype(BF16)


def _outproj(o_a, o_b, x, gt, sc, sh, g_ffn, w_out_bf16, *, tm, rows_per_mod):
    m, d = x.shape
    ka, kb = o_a.shape[1], o_b.shape[1]
    mod_spec = pl.BlockSpec((1, gt.shape[1], d), lambda i: ((i * tm) // rows_per_mod, 0, 0))
    return pl.pallas_call(
        _outproj_body,
        grid=(m // tm,),
        in_specs=[pl.BlockSpec((tm, ka), lambda i: (i, 0)),
                  pl.BlockSpec((tm, kb), lambda i: (i, 0)),
                  pl.BlockSpec((tm, d), lambda i: (i, 0)),
                  mod_spec, mod_spec, mod_spec,
                  pl.BlockSpec((1, d), lambda i: (0, 0)),
                  pl.BlockSpec((ka + kb, d), lambda i: (0, 0), pipeline_mode=pl.Buffered(1))],
        out_specs=[pl.BlockSpec((tm, d), lambda i: (i, 0)),
                   pl.BlockSpec((tm, d), lambda i: (i, 0))],
        out_shape=[jax.ShapeDtypeStruct((m, d), F32),
                   jax.ShapeDtypeStruct((m, d), BF16)],
        compiler_params=_params(("arbitrary",)),
        name="out_proj",
    )(o_a, o_b, x, gt, sc, sh, g_ffn.reshape(1, d), w_out_bf16)


LANES = 128
HALO = 8


def _ffn_up_body(h_ref, wa_ref, wb_ref, cwa_ref, cwb_ref, cba_ref, cbb_ref, g_ref, cs_ref, win_ref, tail_ref, *, tiles_per_seq):
    i, f = pl.program_id(0), pl.program_id(1)
    hb = h_ref[...]
    tm = hb.shape[0]
    first = (i % tiles_per_seq) == 0
    ucs = []
    for part, (w_ref, cw_ref, cb_ref) in enumerate(((wa_ref, cwa_ref, cba_ref), (wb_ref, cwb_ref, cbb_ref))):
        u = _dot(hb, w_ref[...])
        prev = jnp.where(first, 0.0, tail_ref[part, f])
        pieces = []
        for c in range(u.shape[1] // LANES):
            lc = slice(c * LANES, (c + 1) * LANES)
            win_ref[part, c, 0:HALO, :] = prev[:, lc]
            win_ref[part, c, HALO:HALO + tm, :] = u[:, lc]
            cw, cb = cw_ref[:, lc], cb_ref[:, lc]
            pieces.append(cb + cw[0:1] * win_ref[part, c, HALO - 2:HALO - 2 + tm, :]
                          + cw[1:2] * win_ref[part, c, HALO - 1:HALO - 1 + tm, :] + cw[2:3] * u[:, lc])
        ucs.append(jnp.concatenate(pieces, axis=-1))
        tail_ref[part, f] = u[tm - HALO:]
        for r in range(CONV_W - 1):
            cs_ref[0, 0, r, part:part + 1, :] = u[tm - (CONV_W - 1) + r: tm - (CONV_W - 1) + r + 1]
    g_ref[...] = (_silu(ucs[0]) * ucs[1]).astype(g_ref.dtype)


def _ffn_up_state_body(h_ref, wa_ref, wb_ref, cwa_ref, cwb_ref, cba_ref, cbb_ref,
                       lg_ref, qk_ref, rest_ref, la_ref, sr_ref, sg_ref, gro_ref, ggo_ref,
                       g_ref, cs_ref, o_ref, sr_out, sg_out, win_ref, tail_ref, *, tiles_per_seq):
    _ffn_up_body(h_ref, wa_ref, wb_ref, cwa_ref, cwb_ref, cba_ref, cbb_ref, g_ref, cs_ref, win_ref, tail_ref,
                 tiles_per_seq=tiles_per_seq)
    _state_body(lg_ref, qk_ref, rest_ref, la_ref, sr_ref, sg_ref, gro_ref, ggo_ref, o_ref, sr_out, sg_out)


def _ffn_up_prompt(h, w_up, conv_w, conv_b, qk_s, rest_s, la_s, state_ret, state_gla, g_ret_out, g_gla_out,
                   *, batch, seq, tm, tf, nb):
    m, d = h.shape
    ff = w_up.shape[1] // 2
    nf = ff // tf
    cb = conv_b.reshape(1, 2 * ff)
    tps = seq // tm
    ms = qk_s.shape[0]
    nblk = ms // nb
    assert (m // tm) * nf >= nblk
    mix = RET_W + GLA_W
    lg = jnp.broadcast_to(_ret_log_gamma()[:, None, None], (RET_HEADS, 8, 128))
    sblk = lambda i, f: jnp.minimum(i * nf + f, nblk - 1)
    ret_spec = pl.BlockSpec((1, nb, RET_HEADS, RET_DK, RET_DV), lambda i, f: (0, sblk(i, f), 0, 0, 0))
    gla_spec = pl.BlockSpec((1, nb, GLA_HEADS, GLA_DK, GLA_DV), lambda i, f: (0, sblk(i, f), 0, 0, 0))
    row_spec = lambda w: pl.BlockSpec((nb, 1, w), lambda i, f: (sblk(i, f), 0, 0))
    body = functools.partial(_ffn_up_state_body, tiles_per_seq=tps)
    gate, tails, o_s, s_ret, s_gla = pl.pallas_call(
        body,
        grid=(m // tm, nf),
        in_specs=[pl.BlockSpec((tm, d), lambda i, f: (i, 0)),
                  pl.BlockSpec((d, tf), lambda i, f: (0, f)),
                  pl.BlockSpec((d, tf), lambda i, f: (0, nf + f)),
                  pl.BlockSpec((CONV_W, tf), lambda i, f: (0, f)),
                  pl.BlockSpec((CONV_W, tf), lambda i, f: (0, nf + f)),
                  pl.BlockSpec((1, tf), lambda i, f: (0, f)),
                  pl.BlockSpec((1, tf), lambda i, f: (0, nf + f)),
                  pl.BlockSpec((RET_HEADS, 8, 128), lambda i, f: (0, 0, 0)),
                  row_spec(QK_W), row_spec(REST_W), row_spec(la_s.shape[1]),
                  ret_spec, gla_spec,
                  pl.BlockSpec((1, RET_W), lambda i, f: (0, 0)),
                  pl.BlockSpec((1, GLA_W), lambda i, f: (0, 0))],
        out_specs=[pl.BlockSpec((tm, tf), lambda i, f: (i, f)),
                   pl.BlockSpec((1, 1, CONV_W - 1, 2, tf), lambda i, f: (i // tps, i % tps, 0, 0, f)),
                   row_spec(mix), ret_spec, gla_spec],
        out_shape=[jax.ShapeDtypeStruct((m, ff), BF16),
                   jax.ShapeDtypeStruct((batch, tps, CONV_W - 1, 2, ff), F32),
                   jax.ShapeDtypeStruct((ms, 1, mix), F32),
                   jax.ShapeDtypeStruct(state_ret.shape, F32),
                   jax.ShapeDtypeStruct(state_gla.shape, F32)],
        scratch_shapes=[pltpu.VMEM((2, tf // LANES, HALO + tm, LANES), F32), pltpu.VMEM((2, nf, HALO, tf), F32)],
        compiler_params=_params(("arbitrary", "arbitrary")),
        name="ffn_up_prompt",
    )(h, w_up, w_up, conv_w, conv_w, cb, cb,
      lg, qk_s.astype(F32).reshape(ms, 1, QK_W), rest_s.reshape(ms, 1, REST_W), la_s.reshape(ms, 1, la_s.shape[1]),
      state_ret, state_gla, g_ret_out.reshape(1, RET_W), g_gla_out.reshape(1, GLA_W))
    return gate, tails[:, tps - 1], o_s.reshape(ms, mix), s_ret, s_gla


def _ffn_up_step_body(h_ref, w_ref, cw_ref, cb_ref, st_ref, g_ref, cs_ref, uca_ref, *, nf):
    j = pl.program_id(0)
    u = _dot(h_ref[...], w_ref[...])
    s1 = st_ref[0, :, 1, :]
    cw = cw_ref[...]
    uc = cb_ref[...] + cw[0:1] * st_ref[0, :, 0, :] + cw[1:2] * s1 + cw[2:3] * u
    cs_ref[0, :, 0, :] = s1
    cs_ref[0, :, 1, :] = u

    @pl.when(j < nf)
    def _():
        uca_ref[j] = uc

    @pl.when(j >= nf)
    def _():
        g_ref[...] = (_silu(uca_ref[j - nf]) * uc).astype(g_ref.dtype)


def _ffn_up_step(h, w_up, conv_w, conv_b, state_conv, *, tf):
    m, d = h.shape
    ff = w_up.shape[1] // 2
    nf = ff // tf
    st_spec = pl.BlockSpec((1, m, CONV_W - 1, tf), lambda j: (0, 0, 0, j))
    return pl.pallas_call(
        functools.partial(_ffn_up_step_body, nf=nf),
        grid=(2 * nf,),
        in_specs=[pl.BlockSpec((m, d), lambda j: (0, 0)),
                  pl.BlockSpec((d, tf), lambda j: (0, j)),
                  pl.BlockSpec((CONV_W, tf), lambda j: (0, j)),
                  pl.BlockSpec((1, tf), lambda j: (0, j)),
                  st_spec],
        out_specs=[pl.BlockSpec((m, tf), lambda j: (0, jnp.maximum(j - nf, 0))), st_spec],
        out_shape=[jax.ShapeDtypeStruct((m, ff), BF16),
                   jax.ShapeDtypeStruct(state_conv.shape, F32)],
        scratch_shapes=[pltpu.VMEM((nf, m, tf), F32)],
        compiler_params=_params(("arbitrary",)),
        name="ffn_up_step",
    )(h, w_up, conv_w, conv_b.reshape(1, 2 * ff), state_conv)


def _ffn_down_body(g_ref, w_ref, x1_ref, gt_ref, gf_ref, y_ref):
    k = pl.program_id(1)

    @pl.when(k == 0)
    def _():
        y_ref[...] = jnp.zeros_like(y_ref)

    y_ref[...] += _dot(g_ref[...], w_ref[...])

    @pl.when(k == pl.num_programs(1) - 1)
    def _():
        x2 = x1_ref[...] + gt_ref[0] * y_ref[...]
        y_ref[...] = _rms(x2) * gf_ref[...]


def _ffn_down(g, w_down, x1, gt, g_final, *, tm, tk, rows_per_mod, x1_buffers=2):
    m, d = x1.shape
    ff = g.shape[1]
    return pl.pallas_call(
        _ffn_down_body,
        grid=(m // tm, ff // tk),
        in_specs=[pl.BlockSpec((tm, tk), lambda i, k: (i, k)),
                  pl.BlockSpec((tk, d), lambda i, k: (k, 0)),
                  pl.BlockSpec((tm, d), lambda i, k: (i, 0), pipeline_mode=pl.Buffered(x1_buffers)),
                  pl.BlockSpec((1, gt.shape[1], d), lambda i, k: ((i * tm) // rows_per_mod, 0, 0)),
                  pl.BlockSpec((1, d), lambda i, k: (0, 0))],
        out_specs=pl.BlockSpec((tm, d), lambda i, k: (i, 0)),
        out_shape=jax.ShapeDtypeStruct((m, d), F32),
        compiler_params=_params(("arbitrary", "arbitrary")),
        name="ffn_down",
    )(g, w_down, x1, gt, g_final.reshape(1, d))


def _rope_tables(pos):
    half = RET_DK // 2
    inv = ROPE_THETA ** (-jnp.arange(half, dtype=F32) / half)
    ang = pos.astype(F32)[:, None] * inv[None, :]
    return jnp.cos(ang), jnp.sin(ang)


def kernel(x_prompt, x_sample, c_prompt, c_sample, state_ret, state_gla, state_conv, w_ada, b_ada, g_attn, w_in, w_a2, b_a2, g_ret_out, g_gla_out, w_out, g_ffn, w_up, conv_w, conv_b, w_down, g_final):
    bp, t_p, d = x_prompt.shape
    bs, t_s, _ = x_sample.shape
    assert t_s == 1 and w_ada.shape[0] == 1
    mp = bp * t_p
    w_ada, b_ada, g_attn, w_in, w_a2, b_a2, g_ret_out, g_gla_out, w_out, g_ffn, w_up, conv_w, conv_b, w_down = (
        a[0] for a in (w_ada, b_ada, g_attn, w_in, w_a2, b_a2, g_ret_out, g_gla_out, w_out, g_ffn, w_up, conv_w, conv_b, w_down))

    mod = _ada(jnp.concatenate([c_prompt, c_sample], axis=0), w_ada, b_ada)
    sh1p, sc1p, gt1p, sh2p, sc2p, gt2p = (mod[:bp, i * d:(i + 1) * d].reshape(bp, 1, d) for i in range(6))
    sh1s, sc1s, gt1s, sh2s, sc2s, gt2s = (mod[bp:, i * d:(i + 1) * d].reshape(1, bs, d) for i in range(6))

    cos_p, sin_p = _rope_tables(jnp.arange(t_p, dtype=jnp.int32))
    cos_s, sin_s = (jnp.broadcast_to(t, (bs, RET_DK // 2)) for t in _rope_tables(PAST_LEN + jnp.arange(t_s, dtype=jnp.int32)))
    w_in_b, w_out_b, w_up_b, w_down_b = (w.astype(BF16) for w in (w_in, w_out, w_up, w_down))

    xp = x_prompt.reshape(mp, d)
    xs = x_sample.reshape(bs, d)
    qk_s, rest_s, la_s = _inproj(xs, sc1s, sh1s, g_attn, w_in_b, w_a2, b_a2, cos_s, sin_s, tm=bs, rows_per_mod=bs)
    qk_p, rest_p, la_p = _inproj(xp, sc1p, sh1p, g_attn, w_in_b, w_a2, b_a2, cos_p, sin_p, tm=1024, rows_per_mod=t_p)
    o_ret, s_ret_p = _retention_prompt(qk_p, rest_p, g_ret_out, batch=bp, seq=t_p, chunk=256)
    o_gla, s_gla_p = _gla_prompt(rest_p, la_p, g_gla_out, batch=bp, seq=t_p, chunk=256)
    x1_p, h2_p = _outproj(o_ret, o_gla, xp, gt1p, sc2p, sh2p, g_ffn, w_out_b, tm=512, rows_per_mod=t_p)
    g_p, cs_p, o_s, s_ret_s, s_gla_s = _ffn_up_prompt(
        h2_p, w_up_b, conv_w, conv_b, qk_s, rest_s, la_s, state_ret, state_gla, g_ret_out, g_gla_out,
        batch=bp, seq=t_p, tm=1024, tf=512, nb=2)
    y_p = _ffn_down(g_p, w_down_b, x1_p, gt2p, g_final, tm=1024, tk=512, rows_per_mod=t_p)

    x1_s, h2_s = _outproj(o_s[:, :RET_W], o_s[:, RET_W:], xs, gt1s, sc2s, sh2s, g_ffn, w_out_b, tm=bs, rows_per_mod=bs)
    g_s, cs_s = _ffn_up_step(h2_s, w_up_b, conv_w, conv_b, state_conv, tf=512)
    y_s = _ffn_down(g_s, w_down_b, x1_s, gt2s, g_final, tm=bs, tk=1408, rows_per_mod=bs)

    return (y_p.reshape(bp, t_p, d), y_s.reshape(bs, t_s, d),
            s_ret_p[None], s_ret_s, s_gla_p[None], s_gla_s,
            cs_p.reshape(1, bp, CONV_W - 1, -1), cs_s)
```

```python
import functools

import numpy as np
import jax
import jax.numpy as jnp
from jax import lax
from jax.experimental import pallas as pl
from jax.experimental.pallas import tpu as pltpu

F32 = jnp.float32
BF16 = jnp.bfloat16

RET_HEADS = 4
RET_DK = 256
RET_DV = 256
GLA_HEADS = 4
GLA_DK = 128
GLA_DV = 256
GLA_RANK = 16
GLA_TAU = 16.0
ROPE_THETA = 10000.0
PAST_LEN = 16384
CONV_W = 3
EPS = 1e-6

RET_W = RET_HEADS * RET_DV
GLA_W = GLA_HEADS * GLA_DV
QK_W = 2 * RET_HEADS * RET_DK
R_RV = 0
R_RG = R_RV + RET_W
R_GQ = R_RG + RET_W
R_GK = R_GQ + GLA_HEADS * GLA_DK
R_GV = R_GK + GLA_HEADS * GLA_DK
R_GR = R_GV + GLA_W
REST_W = R_GR + GLA_W
IN_MAIN = QK_W + REST_W

VMEM_LIMIT_BYTES = 56 * 1024 * 1024


def _params(semantics):
    return pltpu.CompilerParams(dimension_semantics=semantics, vmem_limit_bytes=VMEM_LIMIT_BYTES)


def _dot(a, b):
    return jnp.dot(a, b, preferred_element_type=F32)


def _dot_nt(a, b):
    return lax.dot_general(a, b, (((1,), (1,)), ((), ())), preferred_element_type=F32)


def _dot_tn(a, b):
    return lax.dot_general(a, b, (((0,), (0,)), ((), ())), preferred_element_type=F32)


def _silu(x):
    return x * jax.nn.sigmoid(x)


def _rms(x):
    return x * lax.rsqrt(jnp.mean(x * x, axis=-1, keepdims=True) + EPS)


def _ada_body(c_ref, w_ref, b_ref, o_ref):
    s = _silu(c_ref[...]).astype(BF16)
    o_ref[...] = _dot(s, w_ref[...].astype(BF16)) + b_ref[...]


def _ada(c_all, w_ada, b_ada, tn=512):
    r, d = c_all.shape
    n = w_ada.shape[1]
    return pl.pallas_call(
        _ada_body,
        grid=(n // tn,),
        in_specs=[pl.BlockSpec((r, d), lambda j: (0, 0)),
                  pl.BlockSpec((d, tn), lambda j: (0, j)),
                  pl.BlockSpec((1, tn), lambda j: (0, j))],
        out_specs=pl.BlockSpec((r, tn), lambda j: (0, j)),
        out_shape=jax.ShapeDtypeStruct((r, n), F32),
        compiler_params=_params(("arbitrary",)),
        name="ada_mod",
    )(c_all, w_ada, b_ada.reshape(1, n))


def _inproj_body(x_ref, sc_ref, sh_ref, g_ref, w_ref, wga_ref, wa2_ref, ba2_ref, cos_ref, sin_ref,
                 qk_ref, rest_ref, la_ref, h_ref, *, n_qk):
    j = pl.program_id(1)

    @pl.when(j == 0)
    def _():
        gain = g_ref[...] * (1.0 + sc_ref[0])
        hb = (_rms(x_ref[...]) * gain + sh_ref[0]).astype(BF16)
        h_ref[...] = hb
        ga = _dot(hb, wga_ref[...])
        z = _dot(ga.astype(BF16), wa2_ref[...].astype(BF16)) + ba2_ref[...]
        la_ref[...] = (jnp.minimum(z, 0.0) - jnp.log1p(jnp.exp(-jnp.abs(z)))) * (1.0 / GLA_TAU)

    @pl.when(j < n_qk)
    def _():
        acc = _dot(h_ref[...], w_ref[...])
        cos, sin = cos_ref[...], sin_ref[...]
        half = RET_DK // 2
        scale = jnp.where(j >= n_qk // 2, RET_DK ** -0.5, 1.0)
        outs = []
        for hh in range(acc.shape[1] // RET_DK):
            x1 = acc[:, hh * RET_DK: hh * RET_DK + half]
            x2 = acc[:, hh * RET_DK + half: (hh + 1) * RET_DK]
            outs += [x1 * cos - x2 * sin, x1 * sin + x2 * cos]
        qk_ref[...] = (jnp.concatenate(outs, axis=-1) * scale).astype(qk_ref.dtype)

    @pl.when(j >= n_qk)
    def _():
        rest_ref[...] = _dot(h_ref[...], w_ref[...])


def _inproj(x, sc, sh, g_attn, w_in_b, w_a2, b_a2, cos, sin, *, tm, rows_per_mod, tn=1024, x_buffers=2):
    m, d = x.shape
    nq = w_a2.shape[1]
    n_qk = QK_W // tn
    tab_tiles = cos.shape[0] // tm
    mod_spec = pl.BlockSpec((1, sc.shape[1], d), lambda i, j: ((i * tm) // rows_per_mod, 0, 0))
    tab_spec = pl.BlockSpec((tm, RET_DK // 2), lambda i, j: (i % tab_tiles, 0))
    return pl.pallas_call(
        functools.partial(_inproj_body, n_qk=n_qk),
        grid=(m // tm, IN_MAIN // tn),
        in_specs=[pl.BlockSpec((tm, d), lambda i, j: (i, 0), pipeline_mode=pl.Buffered(x_buffers)),
                  mod_spec, mod_spec,
                  pl.BlockSpec((1, d), lambda i, j: (0, 0)),
                  pl.BlockSpec((d, tn), lambda i, j: (0, j)),
                  pl.BlockSpec((d, GLA_RANK), lambda i, j: (0, 0)),
                  pl.BlockSpec((GLA_RANK, nq), lambda i, j: (0, 0)),
                  pl.BlockSpec((1, nq), lambda i, j: (0, 0)),
                  tab_spec, tab_spec],
        out_specs=[pl.BlockSpec((tm, tn), lambda i, j: (i, jnp.minimum(j, n_qk - 1))),
                   pl.BlockSpec((tm, tn), lambda i, j: (i, jnp.maximum(j - n_qk, 0))),
                   pl.BlockSpec((tm, nq), lambda i, j: (i, 0))],
        out_shape=[jax.ShapeDtypeStruct((m, QK_W), BF16),
                   jax.ShapeDtypeStruct((m, REST_W), F32),
                   jax.ShapeDtypeStruct((m, nq), F32)],
        scratch_shapes=[pltpu.VMEM((tm, d), BF16)],
        compiler_params=_params(("arbitrary", "arbitrary")),
        name="in_proj",
    )(x, sc, sh, g_attn.reshape(1, d), w_in_b, w_in_b[:, IN_MAIN:], w_a2, b_a2.reshape(1, nq), cos, sin)


def _ret_log_gamma():
    return jnp.log1p(-jnp.exp2(-5.0 - jnp.arange(RET_HEADS, dtype=F32)))


def _ret_decay_matrix(chunk):
    idx = jnp.arange(chunk, dtype=F32)
    rel = idx[:, None] - idx[None, :]
    lg = _ret_log_gamma()
    return jnp.where(rel[None] >= 0, jnp.exp(jnp.maximum(rel, 0.0)[None] * lg[:, None, None]), 0.0)


def _ret_heads(dmat_ref, lg_ref, q_ref, k_ref, v_ref, g_ref, gout_ref, s_ref, new_ref, fresh, valid):
    cl = q_ref.shape[0]
    idx = lax.broadcasted_iota(jnp.int32, (cl, 1), 0).astype(F32)
    for h in range(RET_HEADS):
        sl = slice(h * RET_DV, (h + 1) * RET_DV)
        lg = lg_ref[h][:1, :1]
        q_dec = jnp.exp((idx + 1.0) * lg)
        k_dec = jnp.exp((cl - 1.0 - idx) * lg)
        c_dec = jnp.exp(cl * lg)
        qb, kb = q_ref[:, sl], k_ref[:, sl]
        v = v_ref[:, sl]
        vb = v.astype(BF16)
        s_old = jnp.where(fresh, 0.0, s_ref[h])
        scores = _dot_nt(qb, kb) * dmat_ref[h]
        o = _dot(scores.astype(BF16), vb) + _dot(qb, s_old.astype(BF16)) * q_dec
        s_new = s_old * c_dec + _dot_tn(kb, (v * k_dec).astype(BF16))
        s_ref[h] = jnp.where(valid, s_new, s_old)
        new_ref[:, sl] = (_rms(o) * gout_ref[:, sl] * _silu(g_ref[:, sl])).astype(new_ref.dtype)


def _gla_level_map(c):
    t = np.arange(c)[:, None]
    s = np.arange(c)[None, :]
    x = np.bitwise_xor(t, s)
    lev = np.floor(np.log2(np.maximum(x, 1))).astype(np.int32)
    lev = np.where(t == s, -1, lev)
    lev = np.where(t < s, -2, lev)
    return lev.astype(np.int32)


SCORE_BLK = 128
LOG2E = 1.4426950408889634


def _gla_chunk(q, k, la2, lev, uppers):
    cl = q.shape[0]
    nblk = cl // SCORE_BLK
    blk = lambda x, i: x[i * SCORE_BLK:(i + 1) * SCORE_BLK]
    p = la2
    tot = la2
    diag = [jnp.zeros((SCORE_BLK, SCORE_BLK), F32) for _ in range(nblk)]
    off = {}
    level = 0
    half = 1
    while half < cl:
        upper = uppers[level]
        z = (jnp.where(upper, q, k) * jnp.exp2(jnp.where(upper, p, tot - p))).astype(BF16)
        if half < SCORE_BLK:
            for i in range(nblk):
                diag[i] = jnp.where(lev == level, _dot_nt(blk(z, i), blk(z, i)), diag[i])
        else:
            hb = half // SCORE_BLK
            for i in range(nblk):
                if (i // hb) % 2 == 1:
                    base = (i // (2 * hb)) * 2 * hb
                    for j in range(base, base + hb):
                        off[(i, j)] = _dot_nt(blk(z, i), blk(z, j))
        partner = jnp.where(upper, pltpu.roll(tot, half, 0), pltpu.roll(tot, cl - half, 0))
        p = p + jnp.where(upper, partner, 0.0)
        tot = tot + partner
        half *= 2
        level += 1
    dg = jnp.sum(q * k, axis=-1, keepdims=True)
    rows = []
    for i in range(nblk):
        d_i = jnp.where(lev == -1, blk(dg, i), diag[i])
        rows.append(jnp.concatenate([off[(i, j)] for j in range(i)] + [d_i], axis=-1))
    return rows, p, tot


def _gla_heads(q_ref, k_ref, v_ref, g_ref, la_ref, lev_ref, gout_ref, st_ref, new_ref, fresh, valid):
    cl = q_ref.shape[0]
    lev = lev_ref[...]
    row = lax.broadcasted_iota(jnp.int32, (cl, GLA_DK), 0)
    uppers = []
    half = 1
    while half < cl:
        uppers.append((row & half) != 0)
        half *= 2
    for h in range(GLA_HEADS):
        ks = slice(h * GLA_DK, (h + 1) * GLA_DK)
        vs = slice(h * GLA_DV, (h + 1) * GLA_DV)
        q = q_ref[:, ks] * (GLA_DK ** -0.5)
        k = k_ref[:, ks]
        rows, p, tot = _gla_chunk(q, k, la_ref[:, ks] * LOG2E, lev, uppers)
        vb = v_ref[:, vs].astype(BF16)
        st_old = jnp.where(fresh, 0.0, st_ref[h])
        qt = (q * jnp.exp2(p)).astype(BF16)
        intra = jnp.concatenate([_dot(r.astype(BF16), vb[:r.shape[1]]) for r in rows], axis=0)
        o = intra + _dot_nt(qt, st_old.astype(BF16))
        kt = (k * jnp.exp2(tot - p)).astype(BF16)
        st_new = st_old * jnp.exp2(tot[0:1, :]) + _dot_tn(vb, kt)
        st_ref[h] = jnp.where(valid, st_new, st_old)
        new_ref[:, RET_W + h * GLA_DV: RET_W + (h + 1) * GLA_DV] = (
            _rms(o) * gout_ref[:, vs] * _silu(g_ref[:, vs])).astype(new_ref.dtype)


def _mixer_body(dmat_ref, lg_ref, q_ref, k_ref, rv_ref, rg_ref, gq_ref, gk_ref, gv_ref, gr_ref, la_ref, lev_ref,
                gro_ref, ggo_ref, x_ref, gt_ref, sc_ref, sh_ref, gf_ref, w_ref,
                x1_ref, h_ref, sr_out, sg_out, s_ref, st_ref, mixa_ref, mixb_ref, *, nc, n_chunks):
    s = pl.program_id(0)
    valid = s < n_chunks
    c = jnp.minimum(s, n_chunks - 1) % nc
    fresh = c == 0

    @pl.when(s == 0)
    def _():
        mixb_ref[...] = jnp.zeros_like(mixb_ref)

    def step(old_ref, new_ref):
        x1 = x_ref[...] + gt_ref[0] * _dot(old_ref[...], w_ref[...])
        x1_ref[...] = x1
        hn = _rms(x1) * gf_ref[...]
        h_ref[...] = (hn * (1.0 + sc_ref[0]) + sh_ref[0]).astype(h_ref.dtype)
        _ret_heads(dmat_ref, lg_ref, q_ref, k_ref, rv_ref, rg_ref, gro_ref, s_ref, new_ref, fresh, valid)
        _gla_heads(gq_ref, gk_ref, gv_ref, gr_ref, la_ref, lev_ref, ggo_ref, st_ref, new_ref, fresh, valid)

    @pl.when(s % 2 == 0)
    def _():
        step(mixb_ref, mixa_ref)

    @pl.when(s % 2 == 1)
    def _():
        step(mixa_ref, mixb_ref)

    @pl.when(valid & (c == nc - 1))
    def _():
        sr_out[0] = s_ref[...]
        for h in range(GLA_HEADS):
            sg_out[0, h] = st_ref[h].T


def _mixer_prompt(qk, rest, la, x, gt, sc, sh, g_ret_out, g_gla_out, g_ffn, w_out_b, *, batch, seq, chunk):
    m, d = x.shape
    nc = seq // chunk
    n_chunks = batch * nc
    lg = jnp.broadcast_to(_ret_log_gamma()[:, None, None], (RET_HEADS, 8, 128))
    lev = jnp.asarray(_gla_level_map(SCORE_BLK))
    gqk_w = GLA_HEADS * GLA_DK
    cur = lambda s: jnp.minimum(s, n_chunks - 1)
    prv = lambda s: jnp.maximum(s - 1, 0)
    rowc = lambda w, blk: pl.BlockSpec((chunk, w), lambda s: (cur(s), blk))
    const = lambda shape: pl.BlockSpec(shape, lambda s: (0,) * len(shape))
    mod_spec = pl.BlockSpec((1, 1, d), lambda s: (prv(s) // nc, 0, 0))
    return pl.pallas_call(
        functools.partial(_mixer_body, nc=nc, n_chunks=n_chunks),
        grid=(n_chunks + 1,),
        in_specs=[const((RET_HEADS, chunk, chunk)), const((RET_HEADS, 8, 128)),
                  rowc(RET_W, 0), rowc(RET_W, 1),
                  rowc(RET_W, R_RV // RET_W), rowc(RET_W, R_RG // RET_W),
                  rowc(gqk_w, R_GQ // gqk_w), rowc(gqk_w, R_GK // gqk_w),
                  rowc(GLA_W, R_GV // GLA_W), rowc(GLA_W, R_GR // GLA_W),
                  rowc(gqk_w, 0),
                  const((SCORE_BLK, SCORE_BLK)), const((1, RET_W)), const((1, GLA_W)),
                  pl.BlockSpec((chunk, d), lambda s: (prv(s), 0)),
                  mod_spec, mod_spec, mod_spec,
                  const((1, d)),
                  pl.BlockSpec((RET_W + GLA_W, d), lambda s: (0, 0), pipeline_mode=pl.Buffered(1))],
        out_specs=[pl.BlockSpec((chunk, d), lambda s: (prv(s), 0)),
                   pl.BlockSpec((chunk, d), lambda s: (prv(s), 0)),
                   pl.BlockSpec((1, RET_HEADS, RET_DK, RET_DV), lambda s: (cur(s) // nc, 0, 0, 0)),
                   pl.BlockSpec((1, GLA_HEADS, GLA_DK, GLA_DV), lambda s: (cur(s) // nc, 0, 0, 0))],
        out_shape=[jax.ShapeDtypeStruct((m, d), F32),
                   jax.ShapeDtypeStruct((m, d), BF16),
                   jax.ShapeDtypeStruct((batch, RET_HEADS, RET_DK, RET_DV), F32),
                   jax.ShapeDtypeStruct((batch, GLA_HEADS, GLA_DK, GLA_DV), F32)],
        scratch_shapes=[pltpu.VMEM((RET_HEADS, RET_DK, RET_DV), F32),
                        pltpu.VMEM((GLA_HEADS, GLA_DV, GLA_DK), F32),
                        pltpu.VMEM((chunk, RET_W + GLA_W), BF16),
                        pltpu.VMEM((chunk, RET_W + GLA_W), BF16)],
        compiler_params=_params(("arbitrary",)),
        name="mixer_prompt",
    )(_ret_decay_matrix(chunk), lg, qk, qk, rest, rest, rest, rest, rest, rest, la, lev,
      g_ret_out.reshape(1, RET_W), g_gla_out.reshape(1, GLA_W), x, gt, sc, sh, g_ffn.reshape(1, d), w_out_b)


def _columns(rows):
    n = rows[0].shape[1]
    pad = (-len(rows)) % 8
    stack = jnp.concatenate(rows + [jnp.zeros((pad, n), F32)] if pad else rows, axis=0)
    return stack.T


def _state_body(lg_ref, qk_ref, rest_ref, la_ref, sr_ref, sg_ref, gro_ref, ggo_ref, o_ref, sr_out, sg_out):
    nb = qk_ref.shape[0]
    for i in range(nb):
        qkrow = qk_ref[i]
        rrow = rest_ref[i]
        larow = la_ref[i]
        rcols = _columns([qkrow[:, j * RET_DK:(j + 1) * RET_DK] for j in range(2 * RET_HEADS)])
        gq = [rrow[:, R_GQ + h * GLA_DK: R_GQ + (h + 1) * GLA_DK] * (GLA_DK ** -0.5) for h in range(GLA_HEADS)]
        gk = [rrow[:, R_GK + h * GLA_DK: R_GK + (h + 1) * GLA_DK] for h in range(GLA_HEADS)]
        ga = [jnp.exp(larow[:, h * GLA_DK:(h + 1) * GLA_DK]) for h in range(GLA_HEADS)]
        gcols = _columns(gq + gk + ga)
        outs = []
        for h in range(RET_HEADS):
            v = rrow[:, R_RV + h * RET_DV: R_RV + (h + 1) * RET_DV]
            gate = rrow[:, R_RG + h * RET_DV: R_RG + (h + 1) * RET_DV]
            gamma = jnp.exp(lg_ref[h][:1, :1])
            qc = rcols[:, h:h + 1]
            kc = rcols[:, RET_HEADS + h:RET_HEADS + h + 1]
            s_new = sr_ref[0, i, h] * gamma + kc * v
            sr_out[0, i, h] = s_new
            o = jnp.sum(qc * s_new, axis=0, keepdims=True)
            outs.append(_rms(o) * gro_ref[:, h * RET_DV:(h + 1) * RET_DV] * _silu(gate))
        for h in range(GLA_HEADS):
            v = rrow[:, R_GV + h * GLA_DV: R_GV + (h + 1) * GLA_DV]
            gate = rrow[:, R_GR + h * GLA_DV: R_GR + (h + 1) * GLA_DV]
            qc = gcols[:, h:h + 1]
            kc = gcols[:, GLA_HEADS + h:GLA_HEADS + h + 1]
            ac = gcols[:, 2 * GLA_HEADS + h:2 * GLA_HEADS + h + 1]
            s_new = sg_ref[0, i, h] * ac + kc * v
            sg_out[0, i, h] = s_new
            o = jnp.sum(qc * s_new, axis=0, keepdims=True)
            outs.append(_rms(o) * ggo_ref[:, h * GLA_DV:(h + 1) * GLA_DV] * _silu(gate))
        o_ref[i] = jnp.concatenate(outs, axis=-1)


def _outproj_body(oa_ref, ob_ref, x_ref, gt_ref, sc_ref, sh_ref, g_ref, w_ref, x1_ref, h_ref):
    ka = oa_ref.shape[1]
    mix = _dot(oa_ref[...].astype(BF16), w_ref[:ka, :]) + _dot(ob_ref[...].astype(BF16), w_ref[ka:, :])
    x1 = x_ref[...] + gt_ref[0] * mix
    x1_ref[...] = x1
    h = _rms(x1) * g_ref[...]
    h_ref[...] = (h * (1.0 + sc_ref[0]) + sh_ref[0]).astype(BF16)


def _outproj(o_a, o_b, x, gt, sc, sh, g_ffn, w_out_bf16, *, tm, rows_per_mod):
    m, d = x.shape
    ka, kb = o_a.shape[1], o_b.shape[1]
    mod_spec = pl.BlockSpec((1, gt.shape[1], d), lambda i: ((i * tm) // rows_per_mod, 0, 0))
    return pl.pallas_call(
        _outproj_body,
        grid=(m // tm,),
        in_specs=[pl.BlockSpec((tm, ka), lambda i: (i, 0)),
                  pl.BlockSpec((tm, kb), lambda i: (i, 0)),
                  pl.BlockSpec((tm, d), lambda i: (i, 0)),
                  mod_spec, mod_spec, mod_spec,
                  pl.BlockSpec((1, d), lambda i: (0, 0)),
                  pl.BlockSpec((ka + kb, d), lambda i: (0, 0), pipeline_mode=pl.Buffered(1))],
        out_specs=[pl.BlockSpec((tm, d), lambda i: (i, 0)),
                   pl.BlockSpec((tm, d), lambda i: (i, 0))],
        out_shape=[jax.ShapeDtypeStruct((m, d), F32),
                   jax.ShapeDtypeStruct((m, d), BF16)],
        compiler_params=_params(("arbitrary",)),
        name="out_proj",
    )(o_a, o_b, x, gt, sc, sh, g_ffn.reshape(1, d), w_out_bf16)


LANES = 128
HALO = 8


def _ffn_up_body(h_ref, wa_ref, wb_ref, cwa_ref, cwb_ref, cba_ref, cbb_ref, g_ref, cs_ref, win_ref, tail_ref, *, tiles_per_seq):
    i, f = pl.program_id(0), pl.program_id(1)
    hb = h_ref[...]
    tm = hb.shape[0]
    first = (i % tiles_per_seq) == 0
    ucs = []
    for part, (w_ref, cw_ref, cb_ref) in enumerate(((wa_ref, cwa_ref, cba_ref), (wb_ref, cwb_ref, cbb_ref))):
        u = _dot(hb, w_ref[...])
        prev = jnp.where(first, 0.0, tail_ref[part, f])
        pieces = []
        for c in range(u.shape[1] // LANES):
            lc = slice(c * LANES, (c + 1) * LANES)
            win_ref[part, c, 0:HALO, :] = prev[:, lc]
            win_ref[part, c, HALO:HALO + tm, :] = u[:, lc]
            cw, cb = cw_ref[:, lc], cb_ref[:, lc]
            pieces.append(cb + cw[0:1] * win_ref[part, c, HALO - 2:HALO - 2 + tm, :]
                          + cw[1:2] * win_ref[part, c, HALO - 1:HALO - 1 + tm, :] + cw[2:3] * u[:, lc])
        ucs.append(jnp.concatenate(pieces, axis=-1))
        tail_ref[part, f] = u[tm - HALO:]
        for r in range(CONV_W - 1):
            cs_ref[0, 0, r, part:part + 1, :] = u[tm - (CONV_W - 1) + r: tm - (CONV_W - 1) + r + 1]
    g_ref[...] = (_silu(ucs[0]) * ucs[1]).astype(g_ref.dtype)


def _ffn_up_state_body(h_ref, wa_ref, wb_ref, cwa_ref, cwb_ref, cba_ref, cbb_ref,
                       lg_ref, qk_ref, rest_ref, la_ref, sr_ref, sg_ref, gro_ref, ggo_ref, wd_ref,
                       g_ref, cs_ref, o_ref, sr_out, sg_out, wdb_ref, win_ref, tail_ref, *, tiles_per_seq):
    _ffn_up_body(h_ref, wa_ref, wb_ref, cwa_ref, cwb_ref, cba_ref, cbb_ref, g_ref, cs_ref, win_ref, tail_ref,
                 tiles_per_seq=tiles_per_seq)
    _state_body(lg_ref, qk_ref, rest_ref, la_ref, sr_ref, sg_ref, gro_ref, ggo_ref, o_ref, sr_out, sg_out)
    wdb_ref[...] = wd_ref[...].astype(wdb_ref.dtype)


def _ffn_up_prompt(h, w_up, conv_w, conv_b, qk_s, rest_s, la_s, state_ret, state_gla, g_ret_out, g_gla_out, w_down,
                   *, batch, seq, tm, tf, nb, cast_rows):
    m, d = h.shape
    ff = w_up.shape[1] // 2
    nf = ff // tf
    cb = conv_b.reshape(1, 2 * ff)
    tps = seq // tm
    ms = qk_s.shape[0]
    nblk = ms // nb
    ncast = ff // cast_rows
    assert (m // tm) * nf >= nblk + ncast
    mix = RET_W + GLA_W
    lg = jnp.broadcast_to(_ret_log_gamma()[:, None, None], (RET_HEADS, 8, 128))
    sblk = lambda i, f: jnp.minimum(i * nf + f, nblk - 1)
    ret_spec = pl.BlockSpec((1, nb, RET_HEADS, RET_DK, RET_DV), lambda i, f: (0, sblk(i, f), 0, 0, 0))
    gla_spec = pl.BlockSpec((1, nb, GLA_HEADS, GLA_DK, GLA_DV), lambda i, f: (0, sblk(i, f), 0, 0, 0))
    row_spec = lambda w: pl.BlockSpec((nb, 1, w), lambda i, f: (sblk(i, f), 0, 0))
    cast_spec = pl.BlockSpec((cast_rows, d), lambda i, f: (jnp.clip(i * nf + f - nblk, 0, ncast - 1), 0))
    body = functools.partial(_ffn_up_state_body, tiles_per_seq=tps)
    gate, tails, o_s, s_ret, s_gla, w_down_b = pl.pallas_call(
        body,
        grid=(m // tm, nf),
        in_specs=[pl.BlockSpec((tm, d), lambda i, f: (i, 0)),
                  pl.BlockSpec((d, tf), lambda i, f: (0, f)),
                  pl.BlockSpec((d, tf), lambda i, f: (0, nf + f)),
                  pl.BlockSpec((CONV_W, tf), lambda i, f: (0, f)),
                  pl.BlockSpec((CONV_W, tf), lambda i, f: (0, nf + f)),
                  pl.BlockSpec((1, tf), lambda i, f: (0, f)),
                  pl.BlockSpec((1, tf), lambda i, f: (0, nf + f)),
                  pl.BlockSpec((RET_HEADS, 8, 128), lambda i, f: (0, 0, 0)),
                  row_spec(QK_W), row_spec(REST_W), row_spec(la_s.shape[1]),
                  ret_spec, gla_spec,
                  pl.BlockSpec((1, RET_W), lambda i, f: (0, 0)),
                  pl.BlockSpec((1, GLA_W), lambda i, f: (0, 0)),
                  cast_spec],
        out_specs=[pl.BlockSpec((tm, tf), lambda i, f: (i, f)),
                   pl.BlockSpec((1, 1, CONV_W - 1, 2, tf), lambda i, f: (i // tps, i % tps, 0, 0, f)),
                   row_spec(mix), ret_spec, gla_spec, cast_spec],
        out_shape=[jax.ShapeDtypeStruct((m, ff), BF16),
                   jax.ShapeDtypeStruct((batch, tps, CONV_W - 1, 2, ff), F32),
                   jax.ShapeDtypeStruct((ms, 1, mix), F32),
                   jax.ShapeDtypeStruct(state_ret.shape, F32),
                   jax.ShapeDtypeStruct(state_gla.shape, F32),
                   jax.ShapeDtypeStruct(w_down.shape, BF16)],
        scratch_shapes=[pltpu.VMEM((2, tf // LANES, HALO + tm, LANES), F32), pltpu.VMEM((2, nf, HALO, tf), F32)],
        compiler_params=_params(("arbitrary", "arbitrary")),
        name="ffn_up_prompt",
    )(h, w_up, w_up, conv_w, conv_w, cb, cb,
      lg, qk_s.astype(F32).reshape(ms, 1, QK_W), rest_s.reshape(ms, 1, REST_W), la_s.reshape(ms, 1, la_s.shape[1]),
      state_ret, state_gla, g_ret_out.reshape(1, RET_W), g_gla_out.reshape(1, GLA_W), w_down)
    return gate, tails[:, tps - 1], o_s.reshape(ms, mix), s_ret, s_gla, w_down_b


def _ffn_up_step_body(h_ref, w_ref, cw_ref, cb_ref, st_ref, g_ref, cs_ref, uca_ref, *, nf):
    j = pl.program_id(0)
    u = _dot(h_ref[...], w_ref[...])
    s1 = st_ref[0, :, 1, :]
    cw = cw_ref[...]
    uc = cb_ref[...] + cw[0:1] * st_ref[0, :, 0, :] + cw[1:2] * s1 + cw[2:3] * u
    cs_ref[0, :, 0, :] = s1
    cs_ref[0, :, 1, :] = u

    @pl.when(j < nf)
    def _():
        uca_ref[j] = uc

    @pl.when(j >= nf)
    def _():
        g_ref[...] = (_silu(uca_ref[j - nf]) * uc).astype(g_ref.dtype)


def _ffn_up_step(h, w_up, conv_w, conv_b, state_conv, *, tf):
    m, d = h.shape
    ff = w_up.shape[1] // 2
    nf = ff // tf
    st_spec = pl.BlockSpec((1, m, CONV_W - 1, tf), lambda j: (0, 0, 0, j))
    return pl.pallas_call(
        functools.partial(_ffn_up_step_body, nf=nf),
        grid=(2 * nf,),
        in_specs=[pl.BlockSpec((m, d), lambda j: (0, 0)),
                  pl.BlockSpec((d, tf), lambda j: (0, j)),
                  pl.BlockSpec((CONV_W, tf), lambda j: (0, j)),
                  pl.BlockSpec((1, tf), lambda j: (0, j)),
                  st_spec],
        out_specs=[pl.BlockSpec((m, tf), lambda j: (0, jnp.maximum(j - nf, 0))), st_spec],
        out_shape=[jax.ShapeDtypeStruct((m, ff), BF16),
                   jax.ShapeDtypeStruct(state_conv.shape, F32)],
        scratch_shapes=[pltpu.VMEM((nf, m, tf), F32)],
        compiler_params=_params(("arbitrary",)),
        name="ffn_up_step",
    )(h, w_up, conv_w, conv_b.reshape(1, 2 * ff), state_conv)


def _ffn_down_body(g_ref, w_ref, x1_ref, gt_ref, gf_ref, y_ref):
    k = pl.program_id(1)

    @pl.when(k == 0)
    def _():
        y_ref[...] = jnp.zeros_like(y_ref)

    y_ref[...] += _dot(g_ref[...], w_ref[...])

    @pl.when(k == pl.num_programs(1) - 1)
    def _():
        x2 = x1_ref[...] + gt_ref[0] * y_ref[...]
        y_ref[...] = _rms(x2) * gf_ref[...]


def _ffn_down(g, w_down, x1, gt, g_final, *, tm, tk, rows_per_mod, x1_buffers=2):
    m, d = x1.shape
    ff = g.shape[1]
    return pl.pallas_call(
        _ffn_down_body,
        grid=(m // tm, ff // tk),
        in_specs=[pl.BlockSpec((tm, tk), lambda i, k: (i, k)),
                  pl.BlockSpec((tk, d), lambda i, k: (k, 0)),
                  pl.BlockSpec((tm, d), lambda i, k: (i, 0), pipeline_mode=pl.Buffered(x1_buffers)),
                  pl.BlockSpec((1, gt.shape[1], d), lambda i, k: ((i * tm) // rows_per_mod, 0, 0)),
                  pl.BlockSpec((1, d), lambda i, k: (0, 0))],
        out_specs=pl.BlockSpec((tm, d), lambda i, k: (i, 0)),
        out_shape=jax.ShapeDtypeStruct((m, d), F32),
        compiler_params=_params(("arbitrary", "arbitrary")),
        name="ffn_down",
    )(g, w_down, x1, gt, g_final.reshape(1, d))


def _rope_tables(pos):
    half = RET_DK // 2
    inv = ROPE_THETA ** (-jnp.arange(half, dtype=F32) / half)
    ang = pos.astype(F32)[:, None] * inv[None, :]
    return jnp.cos(ang), jnp.sin(ang)


def kernel(x_prompt, x_sample, c_prompt, c_sample, state_ret, state_gla, state_conv, w_ada, b_ada, g_attn, w_in, w_a2, b_a2, g_ret_out, g_gla_out, w_out, g_ffn, w_up, conv_w, conv_b, w_down, g_final):
    bp, t_p, d = x_prompt.shape
    bs, t_s, _ = x_sample.shape
    assert t_s == 1 and w_ada.shape[0] == 1
    mp = bp * t_p
    w_ada, b_ada, g_attn, w_in, w_a2, b_a2, g_ret_out, g_gla_out, w_out, g_ffn, w_up, conv_w, conv_b, w_down = (
        a[0] for a in (w_ada, b_ada, g_attn, w_in, w_a2, b_a2, g_ret_out, g_gla_out, w_out, g_ffn, w_up, conv_w, conv_b, w_down))

    mod = _ada(jnp.concatenate([c_prompt, c_sample], axis=0), w_ada, b_ada)
    sh1p, sc1p, gt1p, sh2p, sc2p, gt2p = (mod[:bp, i * d:(i + 1) * d].reshape(bp, 1, d) for i in range(6))
    sh1s, sc1s, gt1s, sh2s, sc2s, gt2s = (mod[bp:, i * d:(i + 1) * d].reshape(1, bs, d) for i in range(6))

    cos_p, sin_p = _rope_tables(jnp.arange(t_p, dtype=jnp.int32))
    cos_s, sin_s = (jnp.broadcast_to(t, (bs, RET_DK // 2)) for t in _rope_tables(PAST_LEN + jnp.arange(t_s, dtype=jnp.int32)))
    w_in_b, w_out_b, w_up_b = (w.astype(BF16) for w in (w_in, w_out, w_up))

    xp = x_prompt.reshape(mp, d)
    xs = x_sample.reshape(bs, d)
    qk_s, rest_s, la_s = _inproj(xs, sc1s, sh1s, g_attn, w_in_b, w_a2, b_a2, cos_s, sin_s, tm=bs, rows_per_mod=bs)
    qk_p, rest_p, la_p = _inproj(xp, sc1p, sh1p, g_attn, w_in_b, w_a2, b_a2, cos_p, sin_p, tm=1024, rows_per_mod=t_p)
    x1_p, h2_p, s_ret_p, s_gla_p = _mixer_prompt(qk_p, rest_p, la_p, xp, gt1p, sc2p, sh2p, g_ret_out, g_gla_out, g_ffn,
                                                 w_out_b, batch=bp, seq=t_p, chunk=256)
    g_p, cs_p, o_s, s_ret_s, s_gla_s, w_down_b = _ffn_up_prompt(
        h2_p, w_up_b, conv_w, conv_b, qk_s, rest_s, la_s, state_ret, state_gla, g_ret_out, g_gla_out, w_down,
        batch=bp, seq=t_p, tm=1024, tf=512, nb=2, cast_rows=256)
    y_p = _ffn_down(g_p, w_down_b, x1_p, gt2p, g_final, tm=1024, tk=512, rows_per_mod=t_p)

    x1_s, h2_s = _outproj(o_s[:, :RET_W], o_s[:, RET_W:], xs, gt1s, sc2s, sh2s, g_ffn, w_out_b, tm=bs, rows_per_mod=bs)
    g_s, cs_s = _ffn_up_step(h2_s, w_up_b, conv_w, conv_b, state_conv, tf=512)
    y_s = _ffn_down(g_s, w_down_b, x1_s, gt2s, g_final, tm=bs, tk=1408, rows_per_mod=bs)

    return (y_p.reshape(bp, t_p, d), y_s.reshape(bs, t_s, d),
            s_ret_p[None], s_ret_s, s_gla_p[None], s_gla_s,
            cs_p.reshape(1, bp, CONV_W - 1, -1), cs_s)
```

```python
import functools

import numpy as np
import jax
import jax.numpy as jnp
from jax import lax
from jax.experimental import pallas as pl
from jax.experimental.pallas import tpu as pltpu

F32 = jnp.float32
BF16 = jnp.bfloat16

RET_HEADS = 4
RET_DK = 256
RET_DV = 256
GLA_HEADS = 4
GLA_DK = 128
GLA_DV = 256
GLA_RANK = 16
GLA_TAU = 16.0
ROPE_THETA = 10000.0
PAST_LEN = 16384
CONV_W = 3
EPS = 1e-6

RET_W = RET_HEADS * RET_DV
GLA_W = GLA_HEADS * GLA_DV
QK_W = 2 * RET_HEADS * RET_DK
R_RV = 0
R_RG = R_RV + RET_W
R_GQ = R_RG + RET_W
R_GK = R_GQ + GLA_HEADS * GLA_DK
R_GV = R_GK + GLA_HEADS * GLA_DK
R_GR = R_GV + GLA_W
REST_W = R_GR + GLA_W
IN_MAIN = QK_W + REST_W

VMEM_LIMIT_BYTES = 56 * 1024 * 1024


def _params(semantics):
    return pltpu.CompilerParams(dimension_semantics=semantics, vmem_limit_bytes=VMEM_LIMIT_BYTES)


def _dot(a, b):
    return jnp.dot(a, b, preferred_element_type=F32)


def _dot_nt(a, b):
    return lax.dot_general(a, b, (((1,), (1,)), ((), ())), preferred_element_type=F32)


def _dot_tn(a, b):
    return lax.dot_general(a, b, (((0,), (0,)), ((), ())), preferred_element_type=F32)


def _silu(x):
    return x * jax.nn.sigmoid(x)


def _rms(x):
    return x * lax.rsqrt(jnp.mean(x * x, axis=-1, keepdims=True) + EPS)


def _ada_body(c_ref, w_ref, b_ref, o_ref):
    s = _silu(c_ref[...]).astype(BF16)
    o_ref[...] = _dot(s, w_ref[...].astype(BF16)) + b_ref[...]


def _ada(c_all, w_ada, b_ada, tn=512):
    r, d = c_all.shape
    n = w_ada.shape[1]
    return pl.pallas_call(
        _ada_body,
        grid=(n // tn,),
        in_specs=[pl.BlockSpec((r, d), lambda j: (0, 0)),
                  pl.BlockSpec((d, tn), lambda j: (0, j)),
                  pl.BlockSpec((1, tn), lambda j: (0, j))],
        out_specs=pl.BlockSpec((r, tn), lambda j: (0, j)),
        out_shape=jax.ShapeDtypeStruct((r, n), F32),
        compiler_params=_params(("arbitrary",)),
        name="ada_mod",
    )(c_all, w_ada, b_ada.reshape(1, n))


def _inproj_body(x_ref, sc_ref, sh_ref, g_ref, w_ref, wga_ref, wa2_ref, ba2_ref, cos_ref, sin_ref,
                 qk_ref, rest_ref, la_ref, *more, n_qk, cast_w):
    h_ref = more[-1]
    j = pl.program_id(1)

    @pl.when(j == 0)
    def _():
        gain = g_ref[...] * (1.0 + sc_ref[0])
        hb = (_rms(x_ref[...]) * gain + sh_ref[0]).astype(BF16)
        h_ref[...] = hb
        ga = _dot_nt(hb, wga_ref[...].astype(BF16))
        z = _dot(ga.astype(BF16), wa2_ref[...].astype(BF16)) + ba2_ref[...]
        la_ref[...] = (jnp.minimum(z, 0.0) - jnp.log1p(jnp.exp(-jnp.abs(z)))) * (1.0 / GLA_TAU)

    if cast_w:
        w_tile = w_ref[...].T.astype(BF16)
        more[0][...] = w_tile
        weight = lambda: w_tile
    else:
        weight = lambda: w_ref[...]

    @pl.when(j < n_qk)
    def _():
        acc = _dot(h_ref[...], weight())
        cos, sin = cos_ref[...], sin_ref[...]
        half = RET_DK // 2
        scale = jnp.where(j >= n_qk // 2, RET_DK ** -0.5, 1.0)
        outs = []
        for hh in range(acc.shape[1] // RET_DK):
            x1 = acc[:, hh * RET_DK: hh * RET_DK + half]
            x2 = acc[:, hh * RET_DK + half: (hh + 1) * RET_DK]
            outs += [x1 * cos - x2 * sin, x1 * sin + x2 * cos]
        qk_ref[...] = (jnp.concatenate(outs, axis=-1) * scale).astype(qk_ref.dtype)

    @pl.when(j >= n_qk)
    def _():
        rest_ref[...] = _dot(h_ref[...], weight())


def _inproj(x, sc, sh, g_attn, w, w_gate_t, w_a2, b_a2, cos, sin, *, tm, rows_per_mod, cast_w=False, tn=1024):
    m, d = x.shape
    nq = w_a2.shape[1]
    n_qk = QK_W // tn
    tab_tiles = cos.shape[0] // tm
    mod_spec = pl.BlockSpec((1, sc.shape[1], d), lambda i, j: ((i * tm) // rows_per_mod, 0, 0))
    tab_spec = pl.BlockSpec((tm, RET_DK // 2), lambda i, j: (i % tab_tiles, 0))
    out_specs = [pl.BlockSpec((tm, tn), lambda i, j: (i, jnp.minimum(j, n_qk - 1))),
                 pl.BlockSpec((tm, tn), lambda i, j: (i, jnp.maximum(j - n_qk, 0))),
                 pl.BlockSpec((tm, nq), lambda i, j: (i, 0))]
    out_shape = [jax.ShapeDtypeStruct((m, QK_W), BF16),
                 jax.ShapeDtypeStruct((m, REST_W), F32),
                 jax.ShapeDtypeStruct((m, nq), F32)]
    if cast_w:
        assert m == tm
        w_spec = pl.BlockSpec((tn, d), lambda i, j: (j, 0))
        out_specs.append(pl.BlockSpec((d, tn), lambda i, j: (0, j)))
        out_shape.append(jax.ShapeDtypeStruct((d, IN_MAIN), BF16))
    else:
        w_spec = pl.BlockSpec((d, tn), lambda i, j: (0, j))
    return pl.pallas_call(
        functools.partial(_inproj_body, n_qk=n_qk, cast_w=cast_w),
        grid=(m // tm, IN_MAIN // tn),
        in_specs=[pl.BlockSpec((tm, d), lambda i, j: (i, 0)),
                  mod_spec, mod_spec,
                  pl.BlockSpec((1, d), lambda i, j: (0, 0)),
                  w_spec,
                  pl.BlockSpec((GLA_RANK, d), lambda i, j: (0, 0)),
                  pl.BlockSpec((GLA_RANK, nq), lambda i, j: (0, 0)),
                  pl.BlockSpec((1, nq), lambda i, j: (0, 0)),
                  tab_spec, tab_spec],
        out_specs=out_specs,
        out_shape=out_shape,
        scratch_shapes=[pltpu.VMEM((tm, d), BF16)],
        compiler_params=_params(("arbitrary", "arbitrary")),
        name="in_proj",
    )(x, sc, sh, g_attn.reshape(1, d), w, w_gate_t, w_a2, b_a2.reshape(1, nq), cos, sin)


def _ret_log_gamma():
    return jnp.log1p(-jnp.exp2(-5.0 - jnp.arange(RET_HEADS, dtype=F32)))


def _ret_decay_matrix(chunk):
    idx = jnp.arange(chunk, dtype=F32)
    rel = idx[:, None] - idx[None, :]
    lg = _ret_log_gamma()
    return jnp.where(rel[None] >= 0, jnp.exp(jnp.maximum(rel, 0.0)[None] * lg[:, None, None]), 0.0)


def _ret_heads(dmat_ref, lg_ref, q_ref, k_ref, v_ref, g_ref, gout_ref, s_ref, new_ref, fresh, valid):
    cl = q_ref.shape[0]
    idx = lax.broadcasted_iota(jnp.int32, (cl, 1), 0).astype(F32)
    for h in range(RET_HEADS):
        sl = slice(h * RET_DV, (h + 1) * RET_DV)
        lg = lg_ref[h][:1, :1]
        q_dec = jnp.exp((idx + 1.0) * lg)
        k_dec = jnp.exp((cl - 1.0 - idx) * lg)
        c_dec = jnp.exp(cl * lg)
        qb, kb = q_ref[:, sl], k_ref[:, sl]
        v = v_ref[:, sl]
        vb = v.astype(BF16)
        s_old = jnp.where(fresh, 0.0, s_ref[h])
        scores = _dot_nt(qb, kb) * dmat_ref[h]
        o = _dot(scores.astype(BF16), vb) + _dot(qb, s_old.astype(BF16)) * q_dec
        s_new = s_old * c_dec + _dot_tn(kb, (v * k_dec).astype(BF16))
        s_ref[h] = jnp.where(valid, s_new, s_old)
        new_ref[:, sl] = (_rms(o) * gout_ref[:, sl] * _silu(g_ref[:, sl])).astype(new_ref.dtype)


def _gla_level_map(c):
    t = np.arange(c)[:, None]
    s = np.arange(c)[None, :]
    x = np.bitwise_xor(t, s)
    lev = np.floor(np.log2(np.maximum(x, 1))).astype(np.int32)
    lev = np.where(t == s, -1, lev)
    lev = np.where(t < s, -2, lev)
    return lev.astype(np.int32)


SCORE_BLK = 128
LOG2E = 1.4426950408889634


def _gla_chunk(q, k, la2, lev, uppers):
    cl = q.shape[0]
    nblk = cl // SCORE_BLK
    blk = lambda x, i: x[i * SCORE_BLK:(i + 1) * SCORE_BLK]
    p = la2
    tot = la2
    diag = [jnp.zeros((SCORE_BLK, SCORE_BLK), F32) for _ in range(nblk)]
    off = {}
    level = 0
    half = 1
    while half < cl:
        upper = uppers[level]
        z = (jnp.where(upper, q, k) * jnp.exp2(jnp.where(upper, p, tot - p))).astype(BF16)
        if half < SCORE_BLK:
            for i in range(nblk):
                diag[i] = jnp.where(lev == level, _dot_nt(blk(z, i), blk(z, i)), diag[i])
        else:
            hb = half // SCORE_BLK
            for i in range(nblk):
                if (i // hb) % 2 == 1:
                    base = (i // (2 * hb)) * 2 * hb
                    for j in range(base, base + hb):
                        off[(i, j)] = _dot_nt(blk(z, i), blk(z, j))
        partner = jnp.where(upper, pltpu.roll(tot, half, 0), pltpu.roll(tot, cl - half, 0))
        p = p + jnp.where(upper, partner, 0.0)
        tot = tot + partner
        half *= 2
        level += 1
    dg = jnp.sum(q * k, axis=-1, keepdims=True)
    rows = []
    for i in range(nblk):
        d_i = jnp.where(lev == -1, blk(dg, i), diag[i])
        rows.append(jnp.concatenate([off[(i, j)] for j in range(i)] + [d_i], axis=-1))
    return rows, p, tot


def _gla_heads(q_ref, k_ref, v_ref, g_ref, la_ref, lev_ref, gout_ref, st_ref, new_ref, fresh, valid):
    cl = q_ref.shape[0]
    lev = lev_ref[...]
    row = lax.broadcasted_iota(jnp.int32, (cl, GLA_DK), 0)
    uppers = []
    half = 1
    while half < cl:
        uppers.append((row & half) != 0)
        half *= 2
    for h in range(GLA_HEADS):
        ks = slice(h * GLA_DK, (h + 1) * GLA_DK)
        vs = slice(h * GLA_DV, (h + 1) * GLA_DV)
        q = q_ref[:, ks] * (GLA_DK ** -0.5)
        k = k_ref[:, ks]
        rows, p, tot = _gla_chunk(q, k, la_ref[:, ks] * LOG2E, lev, uppers)
        vb = v_ref[:, vs].astype(BF16)
        st_old = jnp.where(fresh, 0.0, st_ref[h])
        qt = (q * jnp.exp2(p)).astype(BF16)
        intra = jnp.concatenate([_dot(r.astype(BF16), vb[:r.shape[1]]) for r in rows], axis=0)
        o = intra + _dot_nt(qt, st_old.astype(BF16))
        kt = (k * jnp.exp2(tot - p)).astype(BF16)
        st_new = st_old * jnp.exp2(tot[0:1, :]) + _dot_tn(vb, kt)
        st_ref[h] = jnp.where(valid, st_new, st_old)
        new_ref[:, RET_W + h * GLA_DV: RET_W + (h + 1) * GLA_DV] = (
            _rms(o) * gout_ref[:, vs] * _silu(g_ref[:, vs])).astype(new_ref.dtype)


def _mixer_body(dmat_ref, lg_ref, q_ref, k_ref, rv_ref, rg_ref, gq_ref, gk_ref, gv_ref, gr_ref, la_ref, lev_ref,
                gro_ref, ggo_ref, x_ref, gt_ref, sc_ref, sh_ref, gf_ref, w_ref,
                x1_ref, h_ref, sr_out, sg_out, s_ref, st_ref, mixa_ref, mixb_ref, *, nc, n_chunks):
    s = pl.program_id(0)
    valid = s < n_chunks
    c = jnp.minimum(s, n_chunks - 1) % nc
    fresh = c == 0

    @pl.when(s == 0)
    def _():
        mixb_ref[...] = jnp.zeros_like(mixb_ref)

    def step(old_ref, new_ref):
        x1 = x_ref[...] + gt_ref[0] * _dot(old_ref[...], w_ref[...])
        x1_ref[...] = x1
        hn = _rms(x1) * gf_ref[...]
        h_ref[...] = (hn * (1.0 + sc_ref[0]) + sh_ref[0]).astype(h_ref.dtype)
        _ret_heads(dmat_ref, lg_ref, q_ref, k_ref, rv_ref, rg_ref, gro_ref, s_ref, new_ref, fresh, valid)
        _gla_heads(gq_ref, gk_ref, gv_ref, gr_ref, la_ref, lev_ref, ggo_ref, st_ref, new_ref, fresh, valid)

    @pl.when(s % 2 == 0)
    def _():
        step(mixb_ref, mixa_ref)

    @pl.when(s % 2 == 1)
    def _():
        step(mixa_ref, mixb_ref)

    @pl.when(valid & (c == nc - 1))
    def _():
        sr_out[0] = s_ref[...]
        for h in range(GLA_HEADS):
            sg_out[0, h] = st_ref[h].T


def _mixer_prompt(qk, rest, la, x, gt, sc, sh, g_ret_out, g_gla_out, g_ffn, w_out_b, *, batch, seq, chunk):
    m, d = x.shape
    nc = seq // chunk
    n_chunks = batch * nc
    lg = jnp.broadcast_to(_ret_log_gamma()[:, None, None], (RET_HEADS, 8, 128))
    lev = jnp.asarray(_gla_level_map(SCORE_BLK))
    gqk_w = GLA_HEADS * GLA_DK
    cur = lambda s: jnp.minimum(s, n_chunks - 1)
    prv = lambda s: jnp.maximum(s - 1, 0)
    rowc = lambda w, blk: pl.BlockSpec((chunk, w), lambda s: (cur(s), blk))
    const = lambda shape: pl.BlockSpec(shape, lambda s: (0,) * len(shape))
    mod_spec = pl.BlockSpec((1, 1, d), lambda s: (prv(s) // nc, 0, 0))
    return pl.pallas_call(
        functools.partial(_mixer_body, nc=nc, n_chunks=n_chunks),
        grid=(n_chunks + 1,),
        in_specs=[const((RET_HEADS, chunk, chunk)), const((RET_HEADS, 8, 128)),
                  rowc(RET_W, 0), rowc(RET_W, 1),
                  rowc(RET_W, R_RV // RET_W), rowc(RET_W, R_RG // RET_W),
                  rowc(gqk_w, R_GQ // gqk_w), rowc(gqk_w, R_GK // gqk_w),
                  rowc(GLA_W, R_GV // GLA_W), rowc(GLA_W, R_GR // GLA_W),
                  rowc(gqk_w, 0),
                  const((SCORE_BLK, SCORE_BLK)), const((1, RET_W)), const((1, GLA_W)),
                  pl.BlockSpec((chunk, d), lambda s: (prv(s), 0)),
                  mod_spec, mod_spec, mod_spec,
                  const((1, d)),
                  pl.BlockSpec((RET_W + GLA_W, d), lambda s: (0, 0), pipeline_mode=pl.Buffered(1))],
        out_specs=[pl.BlockSpec((chunk, d), lambda s: (prv(s), 0)),
                   pl.BlockSpec((chunk, d), lambda s: (prv(s), 0)),
                   pl.BlockSpec((1, RET_HEADS, RET_DK, RET_DV), lambda s: (cur(s) // nc, 0, 0, 0)),
                   pl.BlockSpec((1, GLA_HEADS, GLA_DK, GLA_DV), lambda s: (cur(s) // nc, 0, 0, 0))],
        out_shape=[jax.ShapeDtypeStruct((m, d), F32),
                   jax.ShapeDtypeStruct((m, d), BF16),
                   jax.ShapeDtypeStruct((batch, RET_HEADS, RET_DK, RET_DV), F32),
                   jax.ShapeDtypeStruct((batch, GLA_HEADS, GLA_DK, GLA_DV), F32)],
        scratch_shapes=[pltpu.VMEM((RET_HEADS, RET_DK, RET_DV), F32),
                        pltpu.VMEM((GLA_HEADS, GLA_DV, GLA_DK), F32),
                        pltpu.VMEM((chunk, RET_W + GLA_W), BF16),
                        pltpu.VMEM((chunk, RET_W + GLA_W), BF16)],
        compiler_params=_params(("arbitrary",)),
        name="mixer_prompt",
    )(_ret_decay_matrix(chunk), lg, qk, qk, rest, rest, rest, rest, rest, rest, la, lev,
      g_ret_out.reshape(1, RET_W), g_gla_out.reshape(1, GLA_W), x, gt, sc, sh, g_ffn.reshape(1, d), w_out_b)


def _columns(rows):
    n = rows[0].shape[1]
    pad = (-len(rows)) % 8
    stack = jnp.concatenate(rows + [jnp.zeros((pad, n), F32)] if pad else rows, axis=0)
    return stack.T


def _state_body(lg_ref, qk_ref, rest_ref, la_ref, sr_ref, sg_ref, gro_ref, ggo_ref, o_ref, sr_out, sg_out):
    nb = qk_ref.shape[0]
    for i in range(nb):
        qkrow = qk_ref[i]
        rrow = rest_ref[i]
        larow = la_ref[i]
        rcols = _columns([qkrow[:, j * RET_DK:(j + 1) * RET_DK] for j in range(2 * RET_HEADS)])
        gq = [rrow[:, R_GQ + h * GLA_DK: R_GQ + (h + 1) * GLA_DK] * (GLA_DK ** -0.5) for h in range(GLA_HEADS)]
        gk = [rrow[:, R_GK + h * GLA_DK: R_GK + (h + 1) * GLA_DK] for h in range(GLA_HEADS)]
        ga = [jnp.exp(larow[:, h * GLA_DK:(h + 1) * GLA_DK]) for h in range(GLA_HEADS)]
        gcols = _columns(gq + gk + ga)
        outs = []
        for h in range(RET_HEADS):
            v = rrow[:, R_RV + h * RET_DV: R_RV + (h + 1) * RET_DV]
            gate = rrow[:, R_RG + h * RET_DV: R_RG + (h + 1) * RET_DV]
            gamma = jnp.exp(lg_ref[h][:1, :1])
            qc = rcols[:, h:h + 1]
            kc = rcols[:, RET_HEADS + h:RET_HEADS + h + 1]
            s_new = sr_ref[0, i, h] * gamma + kc * v
            sr_out[0, i, h] = s_new
            o = jnp.sum(qc * s_new, axis=0, keepdims=True)
            outs.append(_rms(o) * gro_ref[:, h * RET_DV:(h + 1) * RET_DV] * _silu(gate))
        for h in range(GLA_HEADS):
            v = rrow[:, R_GV + h * GLA_DV: R_GV + (h + 1) * GLA_DV]
            gate = rrow[:, R_GR + h * GLA_DV: R_GR + (h + 1) * GLA_DV]
            qc = gcols[:, h:h + 1]
            kc = gcols[:, GLA_HEADS + h:GLA_HEADS + h + 1]
            ac = gcols[:, 2 * GLA_HEADS + h:2 * GLA_HEADS + h + 1]
            s_new = sg_ref[0, i, h] * ac + kc * v
            sg_out[0, i, h] = s_new
            o = jnp.sum(qc * s_new, axis=0, keepdims=True)
            outs.append(_rms(o) * ggo_ref[:, h * GLA_DV:(h + 1) * GLA_DV] * _silu(gate))
        o_ref[i] = jnp.concatenate(outs, axis=-1)


def _outproj_body(oa_ref, ob_ref, x_ref, gt_ref, sc_ref, sh_ref, g_ref, w_ref, x1_ref, h_ref):
    ka = oa_ref.shape[1]
    mix = _dot(oa_ref[...].astype(BF16), w_ref[:ka, :]) + _dot(ob_ref[...].astype(BF16), w_ref[ka:, :])
    x1 = x_ref[...] + gt_ref[0] * mix
    x1_ref[...] = x1
    h = _rms(x1) * g_ref[...]
    h_ref[...] = (h * (1.0 + sc_ref[0]) + sh_ref[0]).astype(BF16)


def _outproj(o_a, o_b, x, gt, sc, sh, g_ffn, w_out_bf16, *, tm, rows_per_mod):
    m, d = x.shape
    ka, kb = o_a.shape[1], o_b.shape[1]
    mod_spec = pl.BlockSpec((1, gt.shape[1], d), lambda i: ((i * tm) // rows_per_mod, 0, 0))
    return pl.pallas_call(
        _outproj_body,
        grid=(m // tm,),
        in_specs=[pl.BlockSpec((tm, ka), lambda i: (i, 0)),
                  pl.BlockSpec((tm, kb), lambda i: (i, 0)),
                  pl.BlockSpec((tm, d), lambda i: (i, 0)),
                  mod_spec, mod_spec, mod_spec,
                  pl.BlockSpec((1, d), lambda i: (0, 0)),
                  pl.BlockSpec((ka + kb, d), lambda i: (0, 0), pipeline_mode=pl.Buffered(1))],
        out_specs=[pl.BlockSpec((tm, d), lambda i: (i, 0)),
                   pl.BlockSpec((tm, d), lambda i: (i, 0))],
        out_shape=[jax.ShapeDtypeStruct((m, d), F32),
                   jax.ShapeDtypeStruct((m, d), BF16)],
        compiler_params=_params(("arbitrary",)),
        name="out_proj",
    )(o_a, o_b, x, gt, sc, sh, g_ffn.reshape(1, d), w_out_bf16)


LANES = 128
HALO = 8


def _ffn_up_body(h_ref, wa_ref, wb_ref, cwa_ref, cwb_ref, cba_ref, cbb_ref, g_ref, cs_ref, win_ref, tail_ref, *, tiles_per_seq):
    i, f = pl.program_id(0), pl.program_id(1)
    hb = h_ref[...]
    tm = hb.shape[0]
    first = (i % tiles_per_seq) == 0
    ucs = []
    for part, (w_ref, cw_ref, cb_ref) in enumerate(((wa_ref, cwa_ref, cba_ref), (wb_ref, cwb_ref, cbb_ref))):
        u = _dot(hb, w_ref[...])
        prev = jnp.where(first, 0.0, tail_ref[part, f])
        pieces = []
        for c in range(u.shape[1] // LANES):
            lc = slice(c * LANES, (c + 1) * LANES)
            win_ref[part, c, 0:HALO, :] = prev[:, lc]
            win_ref[part, c, HALO:HALO + tm, :] = u[:, lc]
            cw, cb = cw_ref[:, lc], cb_ref[:, lc]
            pieces.append(cb + cw[0:1] * win_ref[part, c, HALO - 2:HALO - 2 + tm, :]
                          + cw[1:2] * win_ref[part, c, HALO - 1:HALO - 1 + tm, :] + cw[2:3] * u[:, lc])
        ucs.append(jnp.concatenate(pieces, axis=-1))
        tail_ref[part, f] = u[tm - HALO:]
        for r in range(CONV_W - 1):
            cs_ref[0, 0, r, part:part + 1, :] = u[tm - (CONV_W - 1) + r: tm - (CONV_W - 1) + r + 1]
    g_ref[...] = (_silu(ucs[0]) * ucs[1]).astype(g_ref.dtype)


def _ffn_up_state_body(h_ref, wa_ref, wb_ref, cwa_ref, cwb_ref, cba_ref, cbb_ref,
                       lg_ref, qk_ref, rest_ref, la_ref, sr_ref, sg_ref, gro_ref, ggo_ref, wd_ref,
                       g_ref, cs_ref, o_ref, sr_out, sg_out, wdb_ref, win_ref, tail_ref, *, tiles_per_seq):
    _ffn_up_body(h_ref, wa_ref, wb_ref, cwa_ref, cwb_ref, cba_ref, cbb_ref, g_ref, cs_ref, win_ref, tail_ref,
                 tiles_per_seq=tiles_per_seq)
    _state_body(lg_ref, qk_ref, rest_ref, la_ref, sr_ref, sg_ref, gro_ref, ggo_ref, o_ref, sr_out, sg_out)
    wdb_ref[...] = wd_ref[...].astype(wdb_ref.dtype)


def _ffn_up_prompt(h, w_up, conv_w, conv_b, qk_s, rest_s, la_s, state_ret, state_gla, g_ret_out, g_gla_out, w_down,
                   *, batch, seq, tm, tf, nb, cast_rows):
    m, d = h.shape
    ff = w_up.shape[1] // 2
    nf = ff // tf
    cb = conv_b.reshape(1, 2 * ff)
    tps = seq // tm
    ms = qk_s.shape[0]
    nblk = ms // nb
    ncast = ff // cast_rows
    assert (m // tm) * nf >= nblk + ncast
    mix = RET_W + GLA_W
    lg = jnp.broadcast_to(_ret_log_gamma()[:, None, None], (RET_HEADS, 8, 128))
    sblk = lambda i, f: jnp.minimum(i * nf + f, nblk - 1)
    ret_spec = pl.BlockSpec((1, nb, RET_HEADS, RET_DK, RET_DV), lambda i, f: (0, sblk(i, f), 0, 0, 0))
    gla_spec = pl.BlockSpec((1, nb, GLA_HEADS, GLA_DK, GLA_DV), lambda i, f: (0, sblk(i, f), 0, 0, 0))
    row_spec = lambda w: pl.BlockSpec((nb, 1, w), lambda i, f: (sblk(i, f), 0, 0))
    cast_spec = pl.BlockSpec((cast_rows, d), lambda i, f: (jnp.clip(i * nf + f - nblk, 0, ncast - 1), 0))
    body = functools.partial(_ffn_up_state_body, tiles_per_seq=tps)
    gate, tails, o_s, s_ret, s_gla, w_down_b = pl.pallas_call(
        body,
        grid=(m // tm, nf),
        in_specs=[pl.BlockSpec((tm, d), lambda i, f: (i, 0)),
                  pl.BlockSpec((d, tf), lambda i, f: (0, f)),
                  pl.BlockSpec((d, tf), lambda i, f: (0, nf + f)),
                  pl.BlockSpec((CONV_W, tf), lambda i, f: (0, f)),
                  pl.BlockSpec((CONV_W, tf), lambda i, f: (0, nf + f)),
                  pl.BlockSpec((1, tf), lambda i, f: (0, f)),
                  pl.BlockSpec((1, tf), lambda i, f: (0, nf + f)),
                  pl.BlockSpec((RET_HEADS, 8, 128), lambda i, f: (0, 0, 0)),
                  row_spec(QK_W), row_spec(REST_W), row_spec(la_s.shape[1]),
                  ret_spec, gla_spec,
                  pl.BlockSpec((1, RET_W), lambda i, f: (0, 0)),
                  pl.BlockSpec((1, GLA_W), lambda i, f: (0, 0)),
                  cast_spec],
        out_specs=[pl.BlockSpec((tm, tf), lambda i, f: (i, f)),
                   pl.BlockSpec((1, 1, CONV_W - 1, 2, tf), lambda i, f: (i // tps, i % tps, 0, 0, f)),
                   row_spec(mix), ret_spec, gla_spec, cast_spec],
        out_shape=[jax.ShapeDtypeStruct((m, ff), BF16),
                   jax.ShapeDtypeStruct((batch, tps, CONV_W - 1, 2, ff), F32),
                   jax.ShapeDtypeStruct((ms, 1, mix), F32),
                   jax.ShapeDtypeStruct(state_ret.shape, F32),
                   jax.ShapeDtypeStruct(state_gla.shape, F32),
                   jax.ShapeDtypeStruct(w_down.shape, BF16)],
        scratch_shapes=[pltpu.VMEM((2, tf // LANES, HALO + tm, LANES), F32), pltpu.VMEM((2, nf, HALO, tf), F32)],
        compiler_params=_params(("arbitrary", "arbitrary")),
        name="ffn_up_prompt",
    )(h, w_up, w_up, conv_w, conv_w, cb, cb,
      lg, qk_s.astype(F32).reshape(ms, 1, QK_W), rest_s.reshape(ms, 1, REST_W), la_s.reshape(ms, 1, la_s.shape[1]),
      state_ret, state_gla, g_ret_out.reshape(1, RET_W), g_gla_out.reshape(1, GLA_W), w_down)
    return gate, tails[:, tps - 1], o_s.reshape(ms, mix), s_ret, s_gla, w_down_b


def _ffn_up_step_body(h_ref, w_ref, cw_ref, cb_ref, st_ref, g_ref, cs_ref, uca_ref, *, nf):
    j = pl.program_id(0)
    u = _dot(h_ref[...], w_ref[...])
    s1 = st_ref[0, :, 1, :]
    cw = cw_ref[...]
    uc = cb_ref[...] + cw[0:1] * st_ref[0, :, 0, :] + cw[1:2] * s1 + cw[2:3] * u
    cs_ref[0, :, 0, :] = s1
    cs_ref[0, :, 1, :] = u

    @pl.when(j < nf)
    def _():
        uca_ref[j] = uc

    @pl.when(j >= nf)
    def _():
        g_ref[...] = (_silu(uca_ref[j - nf]) * uc).astype(g_ref.dtype)


def _ffn_up_step(h, w_up, conv_w, conv_b, state_conv, *, tf):
    m, d = h.shape
    ff = w_up.shape[1] // 2
    nf = ff // tf
    st_spec = pl.BlockSpec((1, m, CONV_W - 1, tf), lambda j: (0, 0, 0, j))
    return pl.pallas_call(
        functools.partial(_ffn_up_step_body, nf=nf),
        grid=(2 * nf,),
        in_specs=[pl.BlockSpec((m, d), lambda j: (0, 0)),
                  pl.BlockSpec((d, tf), lambda j: (0, j)),
                  pl.BlockSpec((CONV_W, tf), lambda j: (0, j)),
                  pl.BlockSpec((1, tf), lambda j: (0, j)),
                  st_spec],
        out_specs=[pl.BlockSpec((m, tf), lambda j: (0, jnp.maximum(j - nf, 0))), st_spec],
        out_shape=[jax.ShapeDtypeStruct((m, ff), BF16),
                   jax.ShapeDtypeStruct(state_conv.shape, F32)],
        scratch_shapes=[pltpu.VMEM((nf, m, tf), F32)],
        compiler_params=_params(("arbitrary",)),
        name="ffn_up_step",
    )(h, w_up, conv_w, conv_b.reshape(1, 2 * ff), state_conv)


def _ffn_down_body(g_ref, w_ref, x1_ref, gt_ref, gf_ref, y_ref):
    k = pl.program_id(1)

    @pl.when(k == 0)
    def _():
        y_ref[...] = jnp.zeros_like(y_ref)

    y_ref[...] += _dot(g_ref[...], w_ref[...])

    @pl.when(k == pl.num_programs(1) - 1)
    def _():
        x2 = x1_ref[...] + gt_ref[0] * y_ref[...]
        y_ref[...] = _rms(x2) * gf_ref[...]


def _ffn_down(g, w_down, x1, gt, g_final, *, tm, tk, rows_per_mod, x1_buffers=2):
    m, d = x1.shape
    ff = g.shape[1]
    return pl.pallas_call(
        _ffn_down_body,
        grid=(m // tm, ff // tk),
        in_specs=[pl.BlockSpec((tm, tk), lambda i, k: (i, k)),
                  pl.BlockSpec((tk, d), lambda i, k: (k, 0)),
                  pl.BlockSpec((tm, d), lambda i, k: (i, 0), pipeline_mode=pl.Buffered(x1_buffers)),
                  pl.BlockSpec((1, gt.shape[1], d), lambda i, k: ((i * tm) // rows_per_mod, 0, 0)),
                  pl.BlockSpec((1, d), lambda i, k: (0, 0))],
        out_specs=pl.BlockSpec((tm, d), lambda i, k: (i, 0)),
        out_shape=jax.ShapeDtypeStruct((m, d), F32),
        compiler_params=_params(("arbitrary", "arbitrary")),
        name="ffn_down",
    )(g, w_down, x1, gt, g_final.reshape(1, d))


def _rope_tables(pos):
    half = RET_DK // 2
    inv = ROPE_THETA ** (-jnp.arange(half, dtype=F32) / half)
    ang = pos.astype(F32)[:, None] * inv[None, :]
    return jnp.cos(ang), jnp.sin(ang)


def kernel(x_prompt, x_sample, c_prompt, c_sample, state_ret, state_gla, state_conv, w_ada, b_ada, g_attn, w_in, w_a2, b_a2, g_ret_out, g_gla_out, w_out, g_ffn, w_up, conv_w, conv_b, w_down, g_final):
    bp, t_p, d = x_prompt.shape
    bs, t_s, _ = x_sample.shape
    assert t_s == 1 and w_ada.shape[0] == 1
    mp = bp * t_p
    w_ada, b_ada, g_attn, w_in, w_a2, b_a2, g_ret_out, g_gla_out, w_out, g_ffn, w_up, conv_w, conv_b, w_down = (
        a[0] for a in (w_ada, b_ada, g_attn, w_in, w_a2, b_a2, g_ret_out, g_gla_out, w_out, g_ffn, w_up, conv_w, conv_b, w_down))

    mod = _ada(jnp.concatenate([c_prompt, c_sample], axis=0), w_ada, b_ada)
    sh1p, sc1p, gt1p, sh2p, sc2p, gt2p = (mod[:bp, i * d:(i + 1) * d].reshape(bp, 1, d) for i in range(6))
    sh1s, sc1s, gt1s, sh2s, sc2s, gt2s = (mod[bp:, i * d:(i + 1) * d].reshape(1, bs, d) for i in range(6))

    cos_p, sin_p = _rope_tables(jnp.arange(t_p, dtype=jnp.int32))
    cos_s, sin_s = (jnp.broadcast_to(t, (bs, RET_DK // 2)) for t in _rope_tables(PAST_LEN + jnp.arange(t_s, dtype=jnp.int32)))
    w_out_b, w_up_b = (w.astype(BF16) for w in (w_out, w_up))
    w_in_t = w_in.T
    w_gate_t = w_in_t[IN_MAIN:]

    xp = x_prompt.reshape(mp, d)
    xs = x_sample.reshape(bs, d)
    qk_s, rest_s, la_s, w_in_b = _inproj(xs, sc1s, sh1s, g_attn, w_in_t, w_gate_t, w_a2, b_a2, cos_s, sin_s,
                                         tm=bs, rows_per_mod=bs, cast_w=True)
    qk_p, rest_p, la_p = _inproj(xp, sc1p, sh1p, g_attn, w_in_b, w_gate_t, w_a2, b_a2, cos_p, sin_p,
                                 tm=1024, rows_per_mod=t_p)
    x1_p, h2_p, s_ret_p, s_gla_p = _mixer_prompt(qk_p, rest_p, la_p, xp, gt1p, sc2p, sh2p, g_ret_out, g_gla_out, g_ffn,
                                                 w_out_b, batch=bp, seq=t_p, chunk=256)
    g_p, cs_p, o_s, s_ret_s, s_gla_s, w_down_b = _ffn_up_prompt(
        h2_p, w_up_b, conv_w, conv_b, qk_s, rest_s, la_s, state_ret, state_gla, g_ret_out, g_gla_out, w_down,
        batch=bp, seq=t_p, tm=1024, tf=512, nb=2, cast_rows=256)
    y_p = _ffn_down(g_p, w_down_b, x1_p, gt2p, g_final, tm=1024, tk=512, rows_per_mod=t_p)

    x1_s, h2_s = _outproj(o_s[:, :RET_W], o_s[:, RET_W:], xs, gt1s, sc2s, sh2s, g_ffn, w_out_b, tm=bs, rows_per_mod=bs)
    g_s, cs_s = _ffn_up_step(h2_s, w_up_b, conv_w, conv_b, state_conv, tf=512)
    y_s = _ffn_down(g_s, w_down_b, x1_s, gt2s, g_final, tm=bs, tk=1408, rows_per_mod=bs)

    return (y_p.reshape(bp, t_p, d), y_s.reshape(bs, t_s, d),
            s_ret_p[None], s_ret_s, s_gla_p[None], s_gla_s,
            cs_p.reshape(1, bp, CONV_W - 1, -1), cs_s)
```

```python
import functools

import numpy as np
import jax
import jax.numpy as jnp
from jax import lax
from jax.experimental import pallas as pl
from jax.experimental.pallas import tpu as pltpu

F32 = jnp.float32
BF16 = jnp.bfloat16

RET_HEADS = 4
RET_DK = 256
RET_DV = 256
GLA_HEADS = 4
GLA_DK = 128
GLA_DV = 256
GLA_RANK = 16
GLA_TAU = 16.0
ROPE_THETA = 10000.0
PAST_LEN = 16384
CONV_W = 3
EPS = 1e-6

RET_W = RET_HEADS * RET_DV
GLA_W = GLA_HEADS * GLA_DV
QK_W = 2 * RET_HEADS * RET_DK
R_RV = 0
R_RG = R_RV + RET_W
R_GQ = R_RG + RET_W
R_GK = R_GQ + GLA_HEADS * GLA_DK
R_GV = R_GK + GLA_HEADS * GLA_DK
R_GR = R_GV + GLA_W
REST_W = R_GR + GLA_W
IN_MAIN = QK_W + REST_W

VMEM_LIMIT_BYTES = 56 * 1024 * 1024


def _params(semantics):
    return pltpu.CompilerParams(dimension_semantics=semantics, vmem_limit_bytes=VMEM_LIMIT_BYTES)


def _dot(a, b):
    return jnp.dot(a, b, preferred_element_type=F32)


def _dot_nt(a, b):
    return lax.dot_general(a, b, (((1,), (1,)), ((), ())), preferred_element_type=F32)


def _dot_tn(a, b):
    return lax.dot_general(a, b, (((0,), (0,)), ((), ())), preferred_element_type=F32)


def _silu(x):
    return x * jax.nn.sigmoid(x)


def _rms(x):
    return x * lax.rsqrt(jnp.mean(x * x, axis=-1, keepdims=True) + EPS)


def _ada_body(c_ref, w_ref, b_ref, wu_ref, o_ref, wub_ref):
    s = _silu(c_ref[...]).astype(BF16)
    o_ref[...] = _dot(s, w_ref[...].astype(BF16)) + b_ref[...]
    wub_ref[...] = wu_ref[...].astype(wub_ref.dtype)


def _ada(c_all, w_ada, b_ada, w_up, tn=512):
    r, d = c_all.shape
    n = w_ada.shape[1]
    ncast = w_up.shape[1] // tn
    assert n // tn >= ncast
    cast_spec = pl.BlockSpec((w_up.shape[0], tn), lambda j: (0, jnp.minimum(j, ncast - 1)))
    return pl.pallas_call(
        _ada_body,
        grid=(n // tn,),
        in_specs=[pl.BlockSpec((r, d), lambda j: (0, 0)),
                  pl.BlockSpec((d, tn), lambda j: (0, j)),
                  pl.BlockSpec((1, tn), lambda j: (0, j)),
                  cast_spec],
        out_specs=[pl.BlockSpec((r, tn), lambda j: (0, j)), cast_spec],
        out_shape=[jax.ShapeDtypeStruct((r, n), F32), jax.ShapeDtypeStruct(w_up.shape, BF16)],
        compiler_params=_params(("arbitrary",)),
        name="ada_mod",
    )(c_all, w_ada, b_ada.reshape(1, n), w_up)


def _inproj_body(x_ref, sc_ref, sh_ref, g_ref, w_ref, wga_ref, wa2_ref, ba2_ref, cos_ref, sin_ref,
                 qk_ref, rest_ref, la_ref, *more, n_qk, cast_w):
    h_ref = more[-1]
    j = pl.program_id(1)

    @pl.when(j == 0)
    def _():
        gain = g_ref[...] * (1.0 + sc_ref[0])
        hb = (_rms(x_ref[...]) * gain + sh_ref[0]).astype(BF16)
        h_ref[...] = hb
        ga = _dot_nt(hb, wga_ref[...].astype(BF16))
        z = _dot(ga.astype(BF16), wa2_ref[...].astype(BF16)) + ba2_ref[...]
        la_ref[...] = (jnp.minimum(z, 0.0) - jnp.log1p(jnp.exp(-jnp.abs(z)))) * (1.0 / GLA_TAU)

    if cast_w:
        w_tile = w_ref[...].T.astype(BF16)
        more[0][...] = w_tile
        weight = lambda: w_tile
    else:
        weight = lambda: w_ref[...]

    @pl.when(j < n_qk)
    def _():
        acc = _dot(h_ref[...], weight())
        cos, sin = cos_ref[...], sin_ref[...]
        half = RET_DK // 2
        scale = jnp.where(j >= n_qk // 2, RET_DK ** -0.5, 1.0)
        outs = []
        for hh in range(acc.shape[1] // RET_DK):
            x1 = acc[:, hh * RET_DK: hh * RET_DK + half]
            x2 = acc[:, hh * RET_DK + half: (hh + 1) * RET_DK]
            outs += [x1 * cos - x2 * sin, x1 * sin + x2 * cos]
        qk_ref[...] = (jnp.concatenate(outs, axis=-1) * scale).astype(qk_ref.dtype)

    @pl.when(j >= n_qk)
    def _():
        rest_ref[...] = _dot(h_ref[...], weight())


def _inproj(x, sc, sh, g_attn, w, w_gate_t, w_a2, b_a2, cos, sin, *, tm, rows_per_mod, cast_w=False, tn=1024):
    m, d = x.shape
    nq = w_a2.shape[1]
    n_qk = QK_W // tn
    tab_tiles = cos.shape[0] // tm
    mod_spec = pl.BlockSpec((1, sc.shape[1], d), lambda i, j: ((i * tm) // rows_per_mod, 0, 0))
    tab_spec = pl.BlockSpec((tm, RET_DK // 2), lambda i, j: (i % tab_tiles, 0))
    out_specs = [pl.BlockSpec((tm, tn), lambda i, j: (i, jnp.minimum(j, n_qk - 1))),
                 pl.BlockSpec((tm, tn), lambda i, j: (i, jnp.maximum(j - n_qk, 0))),
                 pl.BlockSpec((tm, nq), lambda i, j: (i, 0))]
    out_shape = [jax.ShapeDtypeStruct((m, QK_W), BF16),
                 jax.ShapeDtypeStruct((m, REST_W), F32),
                 jax.ShapeDtypeStruct((m, nq), F32)]
    if cast_w:
        assert m == tm
        w_spec = pl.BlockSpec((tn, d), lambda i, j: (j, 0))
        out_specs.append(pl.BlockSpec((d, tn), lambda i, j: (0, j)))
        out_shape.append(jax.ShapeDtypeStruct((d, IN_MAIN), BF16))
    else:
        w_spec = pl.BlockSpec((d, tn), lambda i, j: (0, j))
    return pl.pallas_call(
        functools.partial(_inproj_body, n_qk=n_qk, cast_w=cast_w),
        grid=(m // tm, IN_MAIN // tn),
        in_specs=[pl.BlockSpec((tm, d), lambda i, j: (i, 0)),
                  mod_spec, mod_spec,
                  pl.BlockSpec((1, d), lambda i, j: (0, 0)),
                  w_spec,
                  pl.BlockSpec((GLA_RANK, d), lambda i, j: (0, 0)),
                  pl.BlockSpec((GLA_RANK, nq), lambda i, j: (0, 0)),
                  pl.BlockSpec((1, nq), lambda i, j: (0, 0)),
                  tab_spec, tab_spec],
        out_specs=out_specs,
        out_shape=out_shape,
        scratch_shapes=[pltpu.VMEM((tm, d), BF16)],
        compiler_params=_params(("arbitrary", "arbitrary")),
        name="in_proj",
    )(x, sc, sh, g_attn.reshape(1, d), w, w_gate_t, w_a2, b_a2.reshape(1, nq), cos, sin)


def _ret_log_gamma():
    return jnp.log1p(-jnp.exp2(-5.0 - jnp.arange(RET_HEADS, dtype=F32)))


def _ret_decay_matrix(chunk):
    idx = jnp.arange(chunk, dtype=F32)
    rel = idx[:, None] - idx[None, :]
    lg = _ret_log_gamma()
    return jnp.where(rel[None] >= 0, jnp.exp(jnp.maximum(rel, 0.0)[None] * lg[:, None, None]), 0.0)


def _ret_heads(dmat_ref, lg_ref, q_ref, k_ref, v_ref, g_ref, gout_ref, s_ref, new_ref, fresh, valid,
               heads=range(RET_HEADS)):
    cl = q_ref.shape[0]
    idx = lax.broadcasted_iota(jnp.int32, (cl, 1), 0).astype(F32)
    for h in heads:
        sl = slice(h * RET_DV, (h + 1) * RET_DV)
        lg = lg_ref[h][:1, :1]
        q_dec = jnp.exp((idx + 1.0) * lg)
        k_dec = jnp.exp((cl - 1.0 - idx) * lg)
        c_dec = jnp.exp(cl * lg)
        qb, kb = q_ref[:, sl], k_ref[:, sl]
        v = v_ref[:, sl]
        vb = v.astype(BF16)
        s_old = jnp.where(fresh, 0.0, s_ref[h])
        scores = _dot_nt(qb, kb) * dmat_ref[h]
        o = _dot(scores.astype(BF16), vb) + _dot(qb, s_old.astype(BF16)) * q_dec
        s_new = s_old * c_dec + _dot_tn(kb, (v * k_dec).astype(BF16))
        s_ref[h] = jnp.where(valid, s_new, s_old)
        new_ref[:, sl] = (_rms(o) * gout_ref[:, sl] * _silu(g_ref[:, sl])).astype(new_ref.dtype)


def _gla_level_map(c):
    t = np.arange(c)[:, None]
    s = np.arange(c)[None, :]
    x = np.bitwise_xor(t, s)
    lev = np.floor(np.log2(np.maximum(x, 1))).astype(np.int32)
    lev = np.where(t == s, -1, lev)
    lev = np.where(t < s, -2, lev)
    return lev.astype(np.int32)


SCORE_BLK = 128
LOG2E = 1.4426950408889634


def _gla_chunk(q, k, la2, lev, uppers):
    cl = q.shape[0]
    nblk = cl // SCORE_BLK
    blk = lambda x, i: x[i * SCORE_BLK:(i + 1) * SCORE_BLK]
    p = la2
    tot = la2
    diag = [jnp.zeros((SCORE_BLK, SCORE_BLK), F32) for _ in range(nblk)]
    off = {}
    level = 0
    half = 1
    while half < cl:
        upper = uppers[level]
        z = (jnp.where(upper, q, k) * jnp.exp2(jnp.where(upper, p, tot - p))).astype(BF16)
        if half < SCORE_BLK:
            for i in range(nblk):
                diag[i] = jnp.where(lev == level, _dot_nt(blk(z, i), blk(z, i)), diag[i])
        else:
            hb = half // SCORE_BLK
            for i in range(nblk):
                if (i // hb) % 2 == 1:
                    base = (i // (2 * hb)) * 2 * hb
                    for j in range(base, base + hb):
                        off[(i, j)] = _dot_nt(blk(z, i), blk(z, j))
        partner = jnp.where(upper, pltpu.roll(tot, half, 0), pltpu.roll(tot, cl - half, 0))
        p = p + jnp.where(upper, partner, 0.0)
        tot = tot + partner
        half *= 2
        level += 1
    dg = jnp.sum(q * k, axis=-1, keepdims=True)
    rows = []
    for i in range(nblk):
        d_i = jnp.where(lev == -1, blk(dg, i), diag[i])
        rows.append(jnp.concatenate([off[(i, j)] for j in range(i)] + [d_i], axis=-1))
    return rows, p, tot


def _gla_heads(q_ref, k_ref, v_ref, g_ref, la_ref, lev_ref, gout_ref, st_ref, new_ref, fresh, valid, after_head=None):
    cl = q_ref.shape[0]
    lev = lev_ref[...]
    row = lax.broadcasted_iota(jnp.int32, (cl, GLA_DK), 0)
    uppers = []
    half = 1
    while half < cl:
        uppers.append((row & half) != 0)
        half *= 2
    for h in range(GLA_HEADS):
        if after_head is not None and h > 0:
            after_head(h - 1)
        ks = slice(h * GLA_DK, (h + 1) * GLA_DK)
        vs = slice(h * GLA_DV, (h + 1) * GLA_DV)
        q = q_ref[:, ks] * (GLA_DK ** -0.5)
        k = k_ref[:, ks]
        rows, p, tot = _gla_chunk(q, k, la_ref[:, ks] * LOG2E, lev, uppers)
        vb = v_ref[:, vs].astype(BF16)
        st_old = jnp.where(fresh, 0.0, st_ref[h])
        qt = (q * jnp.exp2(p)).astype(BF16)
        intra = jnp.concatenate([_dot(r.astype(BF16), vb[:r.shape[1]]) for r in rows], axis=0)
        o = intra + _dot_nt(qt, st_old.astype(BF16))
        kt = (k * jnp.exp2(tot - p)).astype(BF16)
        st_new = st_old * jnp.exp2(tot[0:1, :]) + _dot_tn(vb, kt)
        st_ref[h] = jnp.where(valid, st_new, st_old)
        new_ref[:, RET_W + h * GLA_DV: RET_W + (h + 1) * GLA_DV] = (
            _rms(o) * gout_ref[:, vs] * _silu(g_ref[:, vs])).astype(new_ref.dtype)


PROJ_BLK = 256

def _mixer_body(dmat_ref, lg_ref, q_ref, k_ref, rv_ref, rg_ref, gq_ref, gk_ref, gv_ref, gr_ref, la_ref, lev_ref,
                gro_ref, ggo_ref, x_ref, gt_ref, sc_ref, sh_ref, gf_ref, w_ref,
                x1_ref, h_ref, sr_out, sg_out, s_ref, st_ref, mixa_ref, mixb_ref, *, nc, n_chunks):
    s = pl.program_id(0)
    valid = s < n_chunks
    c = jnp.minimum(s, n_chunks - 1) % nc
    fresh = c == 0

    @pl.when(s == 0)
    def _():
        mixb_ref[...] = jnp.zeros_like(mixb_ref)

    def step(old_ref, new_ref):
        old = old_ref[...]
        d = x_ref.shape[1]
        nblk = d // PROJ_BLK
        ssq = [jnp.zeros((x_ref.shape[0], 1), F32)]

        def project(blocks):
            for cb in blocks:
                cs = slice(cb * PROJ_BLK, (cb + 1) * PROJ_BLK)
                x1 = x_ref[:, cs] + gt_ref[0][:, cs] * _dot(old, w_ref[:, cs])
                x1_ref[:, cs] = x1
                ssq[0] = ssq[0] + jnp.sum(x1 * x1, axis=-1, keepdims=True)

        per_head = nblk // GLA_HEADS
        _gla_heads(gq_ref, gk_ref, gv_ref, gr_ref, la_ref, lev_ref, ggo_ref, st_ref, new_ref, fresh, valid,
                   after_head=lambda h: project(range(h * per_head, (h + 1) * per_head)))
        project(range((GLA_HEADS - 1) * per_head, nblk))
        _ret_heads(dmat_ref, lg_ref, q_ref, k_ref, rv_ref, rg_ref, gro_ref, s_ref, new_ref, fresh, valid)
        inv = lax.rsqrt(ssq[0] * (1.0 / d) + EPS)
        gain = gf_ref[...] * (1.0 + sc_ref[0])
        for cb in range(nblk):
            cs = slice(cb * PROJ_BLK, (cb + 1) * PROJ_BLK)
            h_ref[:, cs] = (x1_ref[:, cs] * inv * gain[:, cs] + sh_ref[0][:, cs]).astype(h_ref.dtype)

    @pl.when(s % 2 == 0)
    def _():
        step(mixb_ref, mixa_ref)

    @pl.when(s % 2 == 1)
    def _():
        step(mixa_ref, mixb_ref)

    @pl.when(valid & (c == nc - 1))
    def _():
        sr_out[0] = s_ref[...]
        for h in range(GLA_HEADS):
            sg_out[0, h] = st_ref[h].T


def _mixer_prompt(qk, rest, la, x, gt, sc, sh, g_ret_out, g_gla_out, g_ffn, w_out_b, *, batch, seq, chunk):
    m, d = x.shape
    nc = seq // chunk
    n_chunks = batch * nc
    lg = jnp.broadcast_to(_ret_log_gamma()[:, None, None], (RET_HEADS, 8, 128))
    lev = jnp.asarray(_gla_level_map(SCORE_BLK))
    gqk_w = GLA_HEADS * GLA_DK
    cur = lambda s: jnp.minimum(s, n_chunks - 1)
    prv = lambda s: jnp.maximum(s - 1, 0)
    rowc = lambda w, blk: pl.BlockSpec((chunk, w), lambda s: (cur(s), blk))
    const = lambda shape: pl.BlockSpec(shape, lambda s: (0,) * len(shape))
    mod_spec = pl.BlockSpec((1, 1, d), lambda s: (prv(s) // nc, 0, 0))
    return pl.pallas_call(
        functools.partial(_mixer_body, nc=nc, n_chunks=n_chunks),
        grid=(n_chunks + 1,),
        in_specs=[const((RET_HEADS, chunk, chunk)), const((RET_HEADS, 8, 128)),
                  rowc(RET_W, 0), rowc(RET_W, 1),
                  rowc(RET_W, R_RV // RET_W), rowc(RET_W, R_RG // RET_W),
                  rowc(gqk_w, R_GQ // gqk_w), rowc(gqk_w, R_GK // gqk_w),
                  rowc(GLA_W, R_GV // GLA_W), rowc(GLA_W, R_GR // GLA_W),
                  rowc(gqk_w, 0),
                  const((SCORE_BLK, SCORE_BLK)), const((1, RET_W)), const((1, GLA_W)),
                  pl.BlockSpec((chunk, d), lambda s: (prv(s), 0)),
                  mod_spec, mod_spec, mod_spec,
                  const((1, d)),
                  pl.BlockSpec((RET_W + GLA_W, d), lambda s: (0, 0), pipeline_mode=pl.Buffered(1))],
        out_specs=[pl.BlockSpec((chunk, d), lambda s: (prv(s), 0)),
                   pl.BlockSpec((chunk, d), lambda s: (prv(s), 0)),
                   pl.BlockSpec((1, RET_HEADS, RET_DK, RET_DV), lambda s: (cur(s) // nc, 0, 0, 0)),
                   pl.BlockSpec((1, GLA_HEADS, GLA_DK, GLA_DV), lambda s: (cur(s) // nc, 0, 0, 0))],
        out_shape=[jax.ShapeDtypeStruct((m, d), F32),
                   jax.ShapeDtypeStruct((m, d), BF16),
                   jax.ShapeDtypeStruct((batch, RET_HEADS, RET_DK, RET_DV), F32),
                   jax.ShapeDtypeStruct((batch, GLA_HEADS, GLA_DK, GLA_DV), F32)],
        scratch_shapes=[pltpu.VMEM((RET_HEADS, RET_DK, RET_DV), F32),
                        pltpu.VMEM((GLA_HEADS, GLA_DV, GLA_DK), F32),
                        pltpu.VMEM((chunk, RET_W + GLA_W), BF16),
                        pltpu.VMEM((chunk, RET_W + GLA_W), BF16)],
        compiler_params=_params(("arbitrary",)),
        name="mixer_prompt",
    )(_ret_decay_matrix(chunk), lg, qk, qk, rest, rest, rest, rest, rest, rest, la, lev,
      g_ret_out.reshape(1, RET_W), g_gla_out.reshape(1, GLA_W), x, gt, sc, sh, g_ffn.reshape(1, d), w_out_b)


def _columns(rows):
    n = rows[0].shape[1]
    pad = (-len(rows)) % 8
    stack = jnp.concatenate(rows + [jnp.zeros((pad, n), F32)] if pad else rows, axis=0)
    return stack.T


def _state_body(lg_ref, qk_ref, rest_ref, la_ref, sr_ref, sg_ref, gro_ref, ggo_ref, o_ref, sr_out, sg_out):
    nb = qk_ref.shape[0]
    for i in range(nb):
        qkrow = qk_ref[i]
        rrow = rest_ref[i]
        larow = la_ref[i]
        rcols = _columns([qkrow[:, j * RET_DK:(j + 1) * RET_DK] for j in range(2 * RET_HEADS)])
        gq = [rrow[:, R_GQ + h * GLA_DK: R_GQ + (h + 1) * GLA_DK] * (GLA_DK ** -0.5) for h in range(GLA_HEADS)]
        gk = [rrow[:, R_GK + h * GLA_DK: R_GK + (h + 1) * GLA_DK] for h in range(GLA_HEADS)]
        ga = [jnp.exp(larow[:, h * GLA_DK:(h + 1) * GLA_DK]) for h in range(GLA_HEADS)]
        gcols = _columns(gq + gk + ga)
        outs = []
        for h in range(RET_HEADS):
            v = rrow[:, R_RV + h * RET_DV: R_RV + (h + 1) * RET_DV]
            gate = rrow[:, R_RG + h * RET_DV: R_RG + (h + 1) * RET_DV]
            gamma = jnp.exp(lg_ref[h][:1, :1])
            qc = rcols[:, h:h + 1]
            kc = rcols[:, RET_HEADS + h:RET_HEADS + h + 1]
            s_new = sr_ref[0, i, h] * gamma + kc * v
            sr_out[0, i, h] = s_new
            o = jnp.sum(qc * s_new, axis=0, keepdims=True)
            outs.append(_rms(o) * gro_ref[:, h * RET_DV:(h + 1) * RET_DV] * _silu(gate))
        for h in range(GLA_HEADS):
            v = rrow[:, R_GV + h * GLA_DV: R_GV + (h + 1) * GLA_DV]
            gate = rrow[:, R_GR + h * GLA_DV: R_GR + (h + 1) * GLA_DV]
            qc = gcols[:, h:h + 1]
            kc = gcols[:, GLA_HEADS + h:GLA_HEADS + h + 1]
            ac = gcols[:, 2 * GLA_HEADS + h:2 * GLA_HEADS + h + 1]
            s_new = sg_ref[0, i, h] * ac + kc * v
            sg_out[0, i, h] = s_new
            o = jnp.sum(qc * s_new, axis=0, keepdims=True)
            outs.append(_rms(o) * ggo_ref[:, h * GLA_DV:(h + 1) * GLA_DV] * _silu(gate))
        o_ref[i] = jnp.concatenate(outs, axis=-1)


def _outproj_body(oa_ref, ob_ref, x_ref, gt_ref, sc_ref, sh_ref, g_ref, w_ref, x1_ref, h_ref):
    ka = oa_ref.shape[1]
    mix = _dot(oa_ref[...].astype(BF16), w_ref[:ka, :]) + _dot(ob_ref[...].astype(BF16), w_ref[ka:, :])
    x1 = x_ref[...] + gt_ref[0] * mix
    x1_ref[...] = x1
    h = _rms(x1) * g_ref[...]
    h_ref[...] = (h * (1.0 + sc_ref[0]) + sh_ref[0]).astype(BF16)


def _outproj(o_a, o_b, x, gt, sc, sh, g_ffn, w_out_bf16, *, tm, rows_per_mod):
    m, d = x.shape
    ka, kb = o_a.shape[1], o_b.shape[1]
    mod_spec = pl.BlockSpec((1, gt.shape[1], d), lambda i: ((i * tm) // rows_per_mod, 0, 0))
    return pl.pallas_call(
        _outproj_body,
        grid=(m // tm,),
        in_specs=[pl.BlockSpec((tm, ka), lambda i: (i, 0)),
                  pl.BlockSpec((tm, kb), lambda i: (i, 0)),
                  pl.BlockSpec((tm, d), lambda i: (i, 0)),
                  mod_spec, mod_spec, mod_spec,
                  pl.BlockSpec((1, d), lambda i: (0, 0)),
                  pl.BlockSpec((ka + kb, d), lambda i: (0, 0), pipeline_mode=pl.Buffered(1))],
        out_specs=[pl.BlockSpec((tm, d), lambda i: (i, 0)),
                   pl.BlockSpec((tm, d), lambda i: (i, 0))],
        out_shape=[jax.ShapeDtypeStruct((m, d), F32),
                   jax.ShapeDtypeStruct((m, d), BF16)],
        compiler_params=_params(("arbitrary",)),
        name="out_proj",
    )(o_a, o_b, x, gt, sc, sh, g_ffn.reshape(1, d), w_out_bf16)


LANES = 128
HALO = 8


def _ffn_up_body(h_ref, wa_ref, wb_ref, cwa_ref, cwb_ref, cba_ref, cbb_ref, g_ref, cs_ref, win_ref, tail_ref, *, tiles_per_seq):
    i, f = pl.program_id(0), pl.program_id(1)
    hb = h_ref[...]
    tm = hb.shape[0]
    first = (i % tiles_per_seq) == 0
    ucs = []
    for part, (w_ref, cw_ref, cb_ref) in enumerate(((wa_ref, cwa_ref, cba_ref), (wb_ref, cwb_ref, cbb_ref))):
        u = _dot(hb, w_ref[...])
        prev = jnp.where(first, 0.0, tail_ref[part, f])
        pieces = []
        for c in range(u.shape[1] // LANES):
            lc = slice(c * LANES, (c + 1) * LANES)
            win_ref[part, c, 0:HALO, :] = prev[:, lc]
            win_ref[part, c, HALO:HALO + tm, :] = u[:, lc]
            cw, cb = cw_ref[:, lc], cb_ref[:, lc]
            pieces.append(cb + cw[0:1] * win_ref[part, c, HALO - 2:HALO - 2 + tm, :]
                          + cw[1:2] * win_ref[part, c, HALO - 1:HALO - 1 + tm, :] + cw[2:3] * u[:, lc])
        ucs.append(jnp.concatenate(pieces, axis=-1))
        tail_ref[part, f] = u[tm - HALO:]
        for r in range(CONV_W - 1):
            cs_ref[0, 0, r, part:part + 1, :] = u[tm - (CONV_W - 1) + r: tm - (CONV_W - 1) + r + 1]
    g_ref[...] = (_silu(ucs[0]) * ucs[1]).astype(g_ref.dtype)


def _ffn_up_state_body(h_ref, wa_ref, wb_ref, cwa_ref, cwb_ref, cba_ref, cbb_ref,
                       lg_ref, qk_ref, rest_ref, la_ref, sr_ref, sg_ref, gro_ref, ggo_ref, wd_ref,
                       g_ref, cs_ref, o_ref, sr_out, sg_out, wdb_ref, win_ref, tail_ref, *, tiles_per_seq):
    _ffn_up_body(h_ref, wa_ref, wb_ref, cwa_ref, cwb_ref, cba_ref, cbb_ref, g_ref, cs_ref, win_ref, tail_ref,
                 tiles_per_seq=tiles_per_seq)
    _state_body(lg_ref, qk_ref, rest_ref, la_ref, sr_ref, sg_ref, gro_ref, ggo_ref, o_ref, sr_out, sg_out)
    wdb_ref[...] = wd_ref[...].astype(wdb_ref.dtype)


def _ffn_up_prompt(h, w_up, conv_w, conv_b, qk_s, rest_s, la_s, state_ret, state_gla, g_ret_out, g_gla_out, w_down,
                   *, batch, seq, tm, tf, nb, cast_rows):
    m, d = h.shape
    ff = w_up.shape[1] // 2
    nf = ff // tf
    cb = conv_b.reshape(1, 2 * ff)
    tps = seq // tm
    ms = qk_s.shape[0]
    nblk = ms // nb
    ncast = ff // cast_rows
    assert (m // tm) * nf >= nblk + ncast
    mix = RET_W + GLA_W
    lg = jnp.broadcast_to(_ret_log_gamma()[:, None, None], (RET_HEADS, 8, 128))
    sblk = lambda i, f: jnp.minimum(i * nf + f, nblk - 1)
    ret_spec = pl.BlockSpec((1, nb, RET_HEADS, RET_DK, RET_DV), lambda i, f: (0, sblk(i, f), 0, 0, 0))
    gla_spec = pl.BlockSpec((1, nb, GLA_HEADS, GLA_DK, GLA_DV), lambda i, f: (0, sblk(i, f), 0, 0, 0))
    row_spec = lambda w: pl.BlockSpec((nb, 1, w), lambda i, f: (sblk(i, f), 0, 0))
    cast_spec = pl.BlockSpec((cast_rows, d), lambda i, f: (jnp.clip(i * nf + f - nblk, 0, ncast - 1), 0))
    body = functools.partial(_ffn_up_state_body, tiles_per_seq=tps)
    gate, tails, o_s, s_ret, s_gla, w_down_b = pl.pallas_call(
        body,
        grid=(m // tm, nf),
        in_specs=[pl.BlockSpec((tm, d), lambda i, f: (i, 0)),
                  pl.BlockSpec((d, tf), lambda i, f: (0, f)),
                  pl.BlockSpec((d, tf), lambda i, f: (0, nf + f)),
                  pl.BlockSpec((CONV_W, tf), lambda i, f: (0, f)),
                  pl.BlockSpec((CONV_W, tf), lambda i, f: (0, nf + f)),
                  pl.BlockSpec((1, tf), lambda i, f: (0, f)),
                  pl.BlockSpec((1, tf), lambda i, f: (0, nf + f)),
                  pl.BlockSpec((RET_HEADS, 8, 128), lambda i, f: (0, 0, 0)),
                  row_spec(QK_W), row_spec(REST_W), row_spec(la_s.shape[1]),
                  ret_spec, gla_spec,
                  pl.BlockSpec((1, RET_W), lambda i, f: (0, 0)),
                  pl.BlockSpec((1, GLA_W), lambda i, f: (0, 0)),
                  cast_spec],
        out_specs=[pl.BlockSpec((tm, tf), lambda i, f: (i, f)),
                   pl.BlockSpec((1, 1, CONV_W - 1, 2, tf), lambda i, f: (i // tps, i % tps, 0, 0, f)),
                   row_spec(mix), ret_spec, gla_spec, cast_spec],
        out_shape=[jax.ShapeDtypeStruct((m, ff), BF16),
                   jax.ShapeDtypeStruct((batch, tps, CONV_W - 1, 2, ff), F32),
                   jax.ShapeDtypeStruct((ms, 1, mix), F32),
                   jax.ShapeDtypeStruct(state_ret.shape, F32),
                   jax.ShapeDtypeStruct(state_gla.shape, F32),
                   jax.ShapeDtypeStruct(w_down.shape, BF16)],
        scratch_shapes=[pltpu.VMEM((2, tf // LANES, HALO + tm, LANES), F32), pltpu.VMEM((2, nf, HALO, tf), F32)],
        compiler_params=_params(("arbitrary", "arbitrary")),
        name="ffn_up_prompt",
    )(h, w_up, w_up, conv_w, conv_w, cb, cb,
      lg, qk_s.astype(F32).reshape(ms, 1, QK_W), rest_s.reshape(ms, 1, REST_W), la_s.reshape(ms, 1, la_s.shape[1]),
      state_ret, state_gla, g_ret_out.reshape(1, RET_W), g_gla_out.reshape(1, GLA_W), w_down)
    return gate, tails[:, tps - 1], o_s.reshape(ms, mix), s_ret, s_gla, w_down_b


def _ffn_up_step_body(h_ref, w_ref, cw_ref, cb_ref, st_ref, g_ref, cs_ref, uca_ref, *, nf):
    j = pl.program_id(0)
    u = _dot(h_ref[...], w_ref[...])
    s1 = st_ref[0, :, 1, :]
    cw = cw_ref[...]
    uc = cb_ref[...] + cw[0:1] * st_ref[0, :, 0, :] + cw[1:2] * s1 + cw[2:3] * u
    cs_ref[0, :, 0, :] = s1
    cs_ref[0, :, 1, :] = u

    @pl.when(j < nf)
    def _():
        uca_ref[j] = uc

    @pl.when(j >= nf)
    def _():
        g_ref[...] = (_silu(uca_ref[j - nf]) * uc).astype(g_ref.dtype)


def _ffn_up_step(h, w_up, conv_w, conv_b, state_conv, *, tf):
    m, d = h.shape
    ff = w_up.shape[1] // 2
    nf = ff // tf
    st_spec = pl.BlockSpec((1, m, CONV_W - 1, tf), lambda j: (0, 0, 0, j))
    return pl.pallas_call(
        functools.partial(_ffn_up_step_body, nf=nf),
        grid=(2 * nf,),
        in_specs=[pl.BlockSpec((m, d), lambda j: (0, 0)),
                  pl.BlockSpec((d, tf), lambda j: (0, j)),
                  pl.BlockSpec((CONV_W, tf), lambda j: (0, j)),
                  pl.BlockSpec((1, tf), lambda j: (0, j)),
                  st_spec],
        out_specs=[pl.BlockSpec((m, tf), lambda j: (0, jnp.maximum(j - nf, 0))), st_spec],
        out_shape=[jax.ShapeDtypeStruct((m, ff), BF16),
                   jax.ShapeDtypeStruct(state_conv.shape, F32)],
        scratch_shapes=[pltpu.VMEM((nf, m, tf), F32)],
        compiler_params=_params(("arbitrary",)),
        name="ffn_up_step",
    )(h, w_up, conv_w, conv_b.reshape(1, 2 * ff), state_conv)


def _ffn_down_body(g_ref, w_ref, x1_ref, gt_ref, gf_ref, y_ref):
    k = pl.program_id(1)

    @pl.when(k == 0)
    def _():
        y_ref[...] = jnp.zeros_like(y_ref)

    y_ref[...] += _dot(g_ref[...], w_ref[...])

    @pl.when(k == pl.num_programs(1) - 1)
    def _():
        x2 = x1_ref[...] + gt_ref[0] * y_ref[...]
        y_ref[...] = _rms(x2) * gf_ref[...]


def _ffn_down(g, w_down, x1, gt, g_final, *, tm, tk, rows_per_mod, x1_buffers=2):
    m, d = x1.shape
    ff = g.shape[1]
    return pl.pallas_call(
        _ffn_down_body,
        grid=(m // tm, ff // tk),
        in_specs=[pl.BlockSpec((tm, tk), lambda i, k: (i, k)),
                  pl.BlockSpec((tk, d), lambda i, k: (k, 0)),
                  pl.BlockSpec((tm, d), lambda i, k: (i, 0), pipeline_mode=pl.Buffered(x1_buffers)),
                  pl.BlockSpec((1, gt.shape[1], d), lambda i, k: ((i * tm) // rows_per_mod, 0, 0)),
                  pl.BlockSpec((1, d), lambda i, k: (0, 0))],
        out_specs=pl.BlockSpec((tm, d), lambda i, k: (i, 0)),
        out_shape=jax.ShapeDtypeStruct((m, d), F32),
        compiler_params=_params(("arbitrary", "arbitrary")),
        name="ffn_down",
    )(g, w_down, x1, gt, g_final.reshape(1, d))


def _rope_tables(pos):
    half = RET_DK // 2
    inv = ROPE_THETA ** (-jnp.arange(half, dtype=F32) / half)
    ang = pos.astype(F32)[:, None] * inv[None, :]
    return jnp.cos(ang), jnp.sin(ang)


def kernel(x_prompt, x_sample, c_prompt, c_sample, state_ret, state_gla, state_conv, w_ada, b_ada, g_attn, w_in, w_a2, b_a2, g_ret_out, g_gla_out, w_out, g_ffn, w_up, conv_w, conv_b, w_down, g_final):
    bp, t_p, d = x_prompt.shape
    bs, t_s, _ = x_sample.shape
    assert t_s == 1 and w_ada.shape[0] == 1
    mp = bp * t_p
    w_ada, b_ada, g_attn, w_in, w_a2, b_a2, g_ret_out, g_gla_out, w_out, g_ffn, w_up, conv_w, conv_b, w_down = (
        a[0] for a in (w_ada, b_ada, g_attn, w_in, w_a2, b_a2, g_ret_out, g_gla_out, w_out, g_ffn, w_up, conv_w, conv_b, w_down))

    mod, w_up_b = _ada(jnp.concatenate([c_prompt, c_sample], axis=0), w_ada, b_ada, w_up)
    sh1p, sc1p, gt1p, sh2p, sc2p, gt2p = (mod[:bp, i * d:(i + 1) * d].reshape(bp, 1, d) for i in range(6))
    sh1s, sc1s, gt1s, sh2s, sc2s, gt2s = (mod[bp:, i * d:(i + 1) * d].reshape(1, bs, d) for i in range(6))

    cos_p, sin_p = _rope_tables(jnp.arange(t_p, dtype=jnp.int32))
    cos_s, sin_s = (jnp.broadcast_to(t, (bs, RET_DK // 2)) for t in _rope_tables(PAST_LEN + jnp.arange(t_s, dtype=jnp.int32)))
    w_out_b = w_out.astype(BF16)
    w_in_t = w_in.T
    w_gate_t = w_in_t[IN_MAIN:]

    xp = x_prompt.reshape(mp, d)
    xs = x_sample.reshape(bs, d)
    qk_s, rest_s, la_s, w_in_b = _inproj(xs, sc1s, sh1s, g_attn, w_in_t, w_gate_t, w_a2, b_a2, cos_s, sin_s,
                                         tm=bs, rows_per_mod=bs, cast_w=True)
    qk_p, rest_p, la_p = _inproj(xp, sc1p, sh1p, g_attn, w_in_b, w_gate_t, w_a2, b_a2, cos_p, sin_p,
                                 tm=1024, rows_per_mod=t_p)
    x1_p, h2_p, s_ret_p, s_gla_p = _mixer_prompt(qk_p, rest_p, la_p, xp, gt1p, sc2p, sh2p, g_ret_out, g_gla_out, g_ffn,
                                                 w_out_b, batch=bp, seq=t_p, chunk=256)
    g_p, cs_p, o_s, s_ret_s, s_gla_s, w_down_b = _ffn_up_prompt(
        h2_p, w_up_b, conv_w, conv_b, qk_s, rest_s, la_s, state_ret, state_gla, g_ret_out, g_gla_out, w_down,
        batch=bp, seq=t_p, tm=1024, tf=512, nb=2, cast_rows=256)
    y_p = _ffn_down(g_p, w_down_b, x1_p, gt2p, g_final, tm=1024, tk=512, rows_per_mod=t_p)

    x1_s, h2_s = _outproj(o_s[:, :RET_W], o_s[:, RET_W:], xs, gt1s, sc2s, sh2s, g_ffn, w_out_b, tm=bs, rows_per_mod=bs)
    g_s, cs_s = _ffn_up_step(h2_s, w_up_b, conv_w, conv_b, state_conv, tf=512)
    y_s = _ffn_down(g_s, w_down_b, x1_s, gt2s, g_final, tm=bs, tk=1408, rows_per_mod=bs)

    return (y_p.reshape(bp, t_p, d), y_s.reshape(bs, t_s, d),
            s_ret_p[None], s_ret_s, s_gla_p[None], s_gla_s,
            cs_p.reshape(1, bp, CONV_W - 1, -1), cs_s)
```

```python
import functools

import numpy as np
import jax
import jax.numpy as jnp
from jax import lax
from jax.experimental import pallas as pl
from jax.experimental.pallas import tpu as pltpu

F32 = jnp.float32
BF16 = jnp.bfloat16

RET_HEADS = 4
RET_DK = 256
RET_DV = 256
GLA_HEADS = 4
GLA_DK = 128
GLA_DV = 256
GLA_RANK = 16
GLA_TAU = 16.0
ROPE_THETA = 10000.0
PAST_LEN = 16384
CONV_W = 3
EPS = 1e-6

RET_W = RET_HEADS * RET_DV
GLA_W = GLA_HEADS * GLA_DV
QK_W = 2 * RET_HEADS * RET_DK
R_RV = 0
R_RG = R_RV + RET_W
R_GQ = R_RG + RET_W
R_GK = R_GQ + GLA_HEADS * GLA_DK
R_GV = R_GK + GLA_HEADS * GLA_DK
R_GR = R_GV + GLA_W
REST_W = R_GR + GLA_W
IN_MAIN = QK_W + REST_W

VMEM_LIMIT_BYTES = 56 * 1024 * 1024
PROJ_BLK = 256


def _params(semantics):
    return pltpu.CompilerParams(dimension_semantics=semantics, vmem_limit_bytes=VMEM_LIMIT_BYTES)


def _dot(a, b):
    return jnp.dot(a, b, preferred_element_type=F32)


def _dot_nt(a, b):
    return lax.dot_general(a, b, (((1,), (1,)), ((), ())), preferred_element_type=F32)


def _dot_tn(a, b):
    return lax.dot_general(a, b, (((0,), (0,)), ((), ())), preferred_element_type=F32)


def _silu(x):
    return x * jax.nn.sigmoid(x)


def _rms(x):
    return x * lax.rsqrt(jnp.mean(x * x, axis=-1, keepdims=True) + EPS)


def _ada_body(c_ref, w_ref, b_ref, wu_ref, o_ref, wub_ref):
    s = _silu(c_ref[...]).astype(BF16)
    o_ref[...] = _dot(s, w_ref[...].astype(BF16)) + b_ref[...]
    wub_ref[...] = wu_ref[...].astype(wub_ref.dtype)


def _ada(c_all, w_ada, b_ada, w_up, tn=512):
    r, d = c_all.shape
    n = w_ada.shape[1]
    ncast = w_up.shape[1] // tn
    assert n // tn >= ncast
    cast_spec = pl.BlockSpec((w_up.shape[0], tn), lambda j: (0, jnp.minimum(j, ncast - 1)))
    return pl.pallas_call(
        _ada_body,
        grid=(n // tn,),
        in_specs=[pl.BlockSpec((r, d), lambda j: (0, 0)),
                  pl.BlockSpec((d, tn), lambda j: (0, j)),
                  pl.BlockSpec((1, tn), lambda j: (0, j)),
                  cast_spec],
        out_specs=[pl.BlockSpec((r, tn), lambda j: (0, j)), cast_spec],
        out_shape=[jax.ShapeDtypeStruct((r, n), F32), jax.ShapeDtypeStruct(w_up.shape, BF16)],
        compiler_params=_params(("arbitrary",)),
        name="ada_mod",
    )(c_all, w_ada, b_ada.reshape(1, n), w_up)


def _inproj_body(x_ref, sc_ref, sh_ref, g_ref, w_ref, wga_ref, wa2_ref, ba2_ref, cos_ref, sin_ref,
                 qk_ref, rest_ref, la_ref, *more, n_qk, cast_w):
    h_ref = more[-1]
    j = pl.program_id(1)

    @pl.when(j == 0)
    def _():
        gain = g_ref[...] * (1.0 + sc_ref[0])
        hb = (_rms(x_ref[...]) * gain + sh_ref[0]).astype(BF16)
        h_ref[...] = hb
        ga = _dot_nt(hb, wga_ref[...].astype(BF16))
        z = _dot(ga.astype(BF16), wa2_ref[...].astype(BF16)) + ba2_ref[...]
        la_ref[...] = (jnp.minimum(z, 0.0) - jnp.log1p(jnp.exp(-jnp.abs(z)))) * (1.0 / GLA_TAU)

    if cast_w:
        w_tile = w_ref[...].T.astype(BF16)
        more[0][...] = w_tile
        weight = lambda: w_tile
    else:
        weight = lambda: w_ref[...]

    @pl.when(j < n_qk)
    def _():
        acc = _dot(h_ref[...], weight())
        cos, sin = cos_ref[...], sin_ref[...]
        half = RET_DK // 2
        scale = jnp.where(j >= n_qk // 2, RET_DK ** -0.5, 1.0)
        outs = []
        for hh in range(acc.shape[1] // RET_DK):
            x1 = acc[:, hh * RET_DK: hh * RET_DK + half]
            x2 = acc[:, hh * RET_DK + half: (hh + 1) * RET_DK]
            outs += [x1 * cos - x2 * sin, x1 * sin + x2 * cos]
        qk_ref[...] = (jnp.concatenate(outs, axis=-1) * scale).astype(qk_ref.dtype)

    @pl.when(j >= n_qk)
    def _():
        rest_ref[...] = _dot(h_ref[...], weight())


def _inproj(x, sc, sh, g_attn, w, w_gate_t, w_a2, b_a2, cos, sin, *, tm, rows_per_mod, cast_w=False, tn=1024):
    m, d = x.shape
    nq = w_a2.shape[1]
    n_qk = QK_W // tn
    tab_tiles = cos.shape[0] // tm
    mod_spec = pl.BlockSpec((1, sc.shape[1], d), lambda i, j: ((i * tm) // rows_per_mod, 0, 0))
    tab_spec = pl.BlockSpec((tm, RET_DK // 2), lambda i, j: (i % tab_tiles, 0))
    out_specs = [pl.BlockSpec((tm, tn), lambda i, j: (i, jnp.minimum(j, n_qk - 1))),
                 pl.BlockSpec((tm, tn), lambda i, j: (i, jnp.maximum(j - n_qk, 0))),
                 pl.BlockSpec((tm, nq), lambda i, j: (i, 0))]
    out_shape = [jax.ShapeDtypeStruct((m, QK_W), BF16),
                 jax.ShapeDtypeStruct((m, REST_W), F32),
                 jax.ShapeDtypeStruct((m, nq), F32)]
    if cast_w:
        assert m == tm
        w_spec = pl.BlockSpec((tn, d), lambda i, j: (j, 0))
        out_specs.append(pl.BlockSpec((d, tn), lambda i, j: (0, j)))
        out_shape.append(jax.ShapeDtypeStruct((d, IN_MAIN), BF16))
    else:
        w_spec = pl.BlockSpec((d, tn), lambda i, j: (0, j))
    return pl.pallas_call(
        functools.partial(_inproj_body, n_qk=n_qk, cast_w=cast_w),
        grid=(m // tm, IN_MAIN // tn),
        in_specs=[pl.BlockSpec((tm, d), lambda i, j: (i, 0)),
                  mod_spec, mod_spec,
                  pl.BlockSpec((1, d), lambda i, j: (0, 0)),
                  w_spec,
                  pl.BlockSpec((GLA_RANK, d), lambda i, j: (0, 0)),
                  pl.BlockSpec((GLA_RANK, nq), lambda i, j: (0, 0)),
                  pl.BlockSpec((1, nq), lambda i, j: (0, 0)),
                  tab_spec, tab_spec],
        out_specs=out_specs,
        out_shape=out_shape,
        scratch_shapes=[pltpu.VMEM((tm, d), BF16)],
        compiler_params=_params(("arbitrary", "arbitrary")),
        name="in_proj",
    )(x, sc, sh, g_attn.reshape(1, d), w, w_gate_t, w_a2, b_a2.reshape(1, nq), cos, sin)


def _ret_log_gamma():
    return jnp.log1p(-jnp.exp2(-5.0 - jnp.arange(RET_HEADS, dtype=F32)))


def _ret_decay_matrix(chunk):
    idx = jnp.arange(chunk, dtype=F32)
    rel = idx[:, None] - idx[None, :]
    lg = _ret_log_gamma()
    return jnp.where(rel[None] >= 0, jnp.exp(jnp.maximum(rel, 0.0)[None] * lg[:, None, None]), 0.0)


def _ret_heads(dmat_ref, lg_ref, q_ref, k_ref, v_ref, g_ref, gout_ref, s_ref, new_ref, fresh, valid,
               heads=range(RET_HEADS)):
    cl = q_ref.shape[0]
    idx = lax.broadcasted_iota(jnp.int32, (cl, 1), 0).astype(F32)
    for h in heads:
        sl = slice(h * RET_DV, (h + 1) * RET_DV)
        lg = lg_ref[h][:1, :1]
        q_dec = jnp.exp((idx + 1.0) * lg)
        k_dec = jnp.exp((cl - 1.0 - idx) * lg)
        c_dec = jnp.exp(cl * lg)
        qb, kb = q_ref[:, sl], k_ref[:, sl]
        v = v_ref[:, sl]
        vb = v.astype(BF16)
        s_old = jnp.where(fresh, 0.0, s_ref[h])
        scores = _dot_nt(qb, kb) * dmat_ref[h]
        o = _dot(scores.astype(BF16), vb) + _dot(qb, s_old.astype(BF16)) * q_dec
        s_new = s_old * c_dec + _dot_tn(kb, (v * k_dec).astype(BF16))
        s_ref[h] = jnp.where(valid, s_new, s_old)
        new_ref[:, sl] = (_rms(o) * gout_ref[:, sl] * _silu(g_ref[:, sl])).astype(new_ref.dtype)


def _gla_level_map(c):
    t = np.arange(c)[:, None]
    s = np.arange(c)[None, :]
    x = np.bitwise_xor(t, s)
    lev = np.floor(np.log2(np.maximum(x, 1))).astype(np.int32)
    lev = np.where(t == s, -1, lev)
    lev = np.where(t < s, -2, lev)
    return lev.astype(np.int32)


SCORE_BLK = 128
LOG2E = 1.4426950408889634


def _gla_chunk(q, k, la2, lev, uppers, at_level=None):
    cl = q.shape[0]
    nblk = cl // SCORE_BLK
    blk = lambda x, i: x[i * SCORE_BLK:(i + 1) * SCORE_BLK]
    p = la2
    tot = la2
    diag = [jnp.zeros((SCORE_BLK, SCORE_BLK), F32) for _ in range(nblk)]
    off = {}
    level = 0
    half = 1
    while half < cl:
        if at_level is not None:
            at_level(level)
        upper = uppers[level]
        z = (jnp.where(upper, q, k) * jnp.exp2(jnp.where(upper, p, tot - p))).astype(BF16)
        if half < SCORE_BLK:
            for i in range(nblk):
                diag[i] = jnp.where(lev == level, _dot_nt(blk(z, i), blk(z, i)), diag[i])
        else:
            hb = half // SCORE_BLK
            for i in range(nblk):
                if (i // hb) % 2 == 1:
                    base = (i // (2 * hb)) * 2 * hb
                    for j in range(base, base + hb):
                        off[(i, j)] = _dot_nt(blk(z, i), blk(z, j))
        partner = jnp.where(upper, pltpu.roll(tot, half, 0), pltpu.roll(tot, cl - half, 0))
        p = p + jnp.where(upper, partner, 0.0)
        tot = tot + partner
        half *= 2
        level += 1
    dg = jnp.sum(q * k, axis=-1, keepdims=True)
    rows = []
    for i in range(nblk):
        d_i = jnp.where(lev == -1, blk(dg, i), diag[i])
        rows.append(jnp.concatenate([off[(i, j)] for j in range(i)] + [d_i], axis=-1))
    return rows, p, tot


def _gla_heads(q_ref, k_ref, v_ref, g_ref, la_ref, lev_ref, gout_ref, st_ref, new_ref, fresh, valid, after_head=None):
    cl = q_ref.shape[0]
    lev = lev_ref[...]
    row = lax.broadcasted_iota(jnp.int32, (cl, GLA_DK), 0)
    uppers = []
    half = 1
    while half < cl:
        uppers.append((row & half) != 0)
        half *= 2
    nlev = len(uppers)
    for h in range(GLA_HEADS):
        ks = slice(h * GLA_DK, (h + 1) * GLA_DK)
        vs = slice(h * GLA_DV, (h + 1) * GLA_DV)
        q = q_ref[:, ks] * (GLA_DK ** -0.5)
        k = k_ref[:, ks]
        hook = None if after_head is None else (lambda level, h=h: after_head(h * nlev + level))
        rows, p, tot = _gla_chunk(q, k, la_ref[:, ks] * LOG2E, lev, uppers, hook)
        vb = v_ref[:, vs].astype(BF16)
        st_old = jnp.where(fresh, 0.0, st_ref[h])
        qt = (q * jnp.exp2(p)).astype(BF16)
        intra = jnp.concatenate([_dot(r.astype(BF16), vb[:r.shape[1]]) for r in rows], axis=0)
        o = intra + _dot_nt(qt, st_old.astype(BF16))
        kt = (k * jnp.exp2(tot - p)).astype(BF16)
        st_new = st_old * jnp.exp2(tot[0:1, :]) + _dot_tn(vb, kt)
        st_ref[h] = jnp.where(valid, st_new, st_old)
        new_ref[:, RET_W + h * GLA_DV: RET_W + (h + 1) * GLA_DV] = (
            _rms(o) * gout_ref[:, vs] * _silu(g_ref[:, vs])).astype(new_ref.dtype)


def _mixer_body(dmat_ref, lg_ref, q_ref, k_ref, rv_ref, rg_ref, gq_ref, gk_ref, gv_ref, gr_ref, la_ref, lev_ref,
                gro_ref, ggo_ref, x_ref, gt_ref, sc_ref, sh_ref, gf_ref, w_ref,
                x1_ref, h_ref, sr_out, sg_out, s_ref, st_ref, mixa_ref, mixb_ref, *, nc, n_chunks):
    s = pl.program_id(0)
    valid = s < n_chunks
    c = jnp.minimum(s, n_chunks - 1) % nc
    fresh = c == 0

    @pl.when(s == 0)
    def _():
        mixb_ref[...] = jnp.zeros_like(mixb_ref)

    def step(old_ref, new_ref):
        old = old_ref[...]
        d = x_ref.shape[1]
        nblk = d // PROJ_BLK
        ssq = [jnp.zeros((x_ref.shape[0], 1), F32)]

        def project(blocks):
            for cb in blocks:
                cs = slice(cb * PROJ_BLK, (cb + 1) * PROJ_BLK)
                x1 = x_ref[:, cs] + gt_ref[0][:, cs] * _dot(old, w_ref[:, cs])
                x1_ref[:, cs] = x1
                ssq[0] = ssq[0] + jnp.sum(x1 * x1, axis=-1, keepdims=True)

        nslots = GLA_HEADS * (x_ref.shape[0].bit_length() - 1)
        due = {(cb * nslots) // nblk: cb for cb in range(nblk)}
        _gla_heads(gq_ref, gk_ref, gv_ref, gr_ref, la_ref, lev_ref, ggo_ref, st_ref, new_ref, fresh, valid,
                   after_head=lambda slot: project([due[slot]]) if slot in due else None)
        _ret_heads(dmat_ref, lg_ref, q_ref, k_ref, rv_ref, rg_ref, gro_ref, s_ref, new_ref, fresh, valid)
        inv = lax.rsqrt(ssq[0] * (1.0 / d) + EPS)
        gain = gf_ref[...] * (1.0 + sc_ref[0])
        for cb in range(nblk):
            cs = slice(cb * PROJ_BLK, (cb + 1) * PROJ_BLK)
            h_ref[:, cs] = (x1_ref[:, cs] * inv * gain[:, cs] + sh_ref[0][:, cs]).astype(h_ref.dtype)

    @pl.when(s % 2 == 0)
    def _():
        step(mixb_ref, mixa_ref)

    @pl.when(s % 2 == 1)
    def _():
        step(mixa_ref, mixb_ref)

    @pl.when(valid & (c == nc - 1))
    def _():
        sr_out[0] = s_ref[...]
        for h in range(GLA_HEADS):
            sg_out[0, h] = st_ref[h].T


def _mixer_prompt(qk, rest, la, x, gt, sc, sh, g_ret_out, g_gla_out, g_ffn, w_out_b, *, batch, seq, chunk):
    m, d = x.shape
    nc = seq // chunk
    n_chunks = batch * nc
    lg = jnp.broadcast_to(_ret_log_gamma()[:, None, None], (RET_HEADS, 8, 128))
    lev = jnp.asarray(_gla_level_map(SCORE_BLK))
    gqk_w = GLA_HEADS * GLA_DK
    cur = lambda s: jnp.minimum(s, n_chunks - 1)
    prv = lambda s: jnp.maximum(s - 1, 0)
    rowc = lambda w, blk: pl.BlockSpec((chunk, w), lambda s: (cur(s), blk))
    const = lambda shape: pl.BlockSpec(shape, lambda s: (0,) * len(shape))
    mod_spec = pl.BlockSpec((1, 1, d), lambda s: (prv(s) // nc, 0, 0))
    return pl.pallas_call(
        functools.partial(_mixer_body, nc=nc, n_chunks=n_chunks),
        grid=(n_chunks + 1,),
        in_specs=[const((RET_HEADS, chunk, chunk)), const((RET_HEADS, 8, 128)),
                  rowc(RET_W, 0), rowc(RET_W, 1),
                  rowc(RET_W, R_RV // RET_W), rowc(RET_W, R_RG // RET_W),
                  rowc(gqk_w, R_GQ // gqk_w), rowc(gqk_w, R_GK // gqk_w),
                  rowc(GLA_W, R_GV // GLA_W), rowc(GLA_W, R_GR // GLA_W),
                  rowc(gqk_w, 0),
                  const((SCORE_BLK, SCORE_BLK)), const((1, RET_W)), const((1, GLA_W)),
                  pl.BlockSpec((chunk, d), lambda s: (prv(s), 0)),
                  mod_spec, mod_spec, mod_spec,
                  const((1, d)),
                  pl.BlockSpec((RET_W + GLA_W, d), lambda s: (0, 0), pipeline_mode=pl.Buffered(1))],
        out_specs=[pl.BlockSpec((chunk, d), lambda s: (prv(s), 0)),
                   pl.BlockSpec((chunk, d), lambda s: (prv(s), 0)),
                   pl.BlockSpec((1, RET_HEADS, RET_DK, RET_DV), lambda s: (cur(s) // nc, 0, 0, 0)),
                   pl.BlockSpec((1, GLA_HEADS, GLA_DK, GLA_DV), lambda s: (cur(s) // nc, 0, 0, 0))],
        out_shape=[jax.ShapeDtypeStruct((m, d), F32),
                   jax.ShapeDtypeStruct((m, d), BF16),
                   jax.ShapeDtypeStruct((batch, RET_HEADS, RET_DK, RET_DV), F32),
                   jax.ShapeDtypeStruct((batch, GLA_HEADS, GLA_DK, GLA_DV), F32)],
        scratch_shapes=[pltpu.VMEM((RET_HEADS, RET_DK, RET_DV), F32),
                        pltpu.VMEM((GLA_HEADS, GLA_DV, GLA_DK), F32),
                        pltpu.VMEM((chunk, RET_W + GLA_W), BF16),
                        pltpu.VMEM((chunk, RET_W + GLA_W), BF16)],
        compiler_params=_params(("arbitrary",)),
        name="mixer_prompt",
    )(_ret_decay_matrix(chunk), lg, qk, qk, rest, rest, rest, rest, rest, rest, la, lev,
      g_ret_out.reshape(1, RET_W), g_gla_out.reshape(1, GLA_W), x, gt, sc, sh, g_ffn.reshape(1, d), w_out_b)


def _columns(rows):
    n = rows[0].shape[1]
    pad = (-len(rows)) % 8
    stack = jnp.concatenate(rows + [jnp.zeros((pad, n), F32)] if pad else rows, axis=0)
    return stack.T


def _state_body(lg_ref, qk_ref, rest_ref, la_ref, sr_ref, sg_ref, gro_ref, ggo_ref, o_ref, sr_out, sg_out):
    nb = qk_ref.shape[0]
    for i in range(nb):
        qkrow = qk_ref[i]
        rrow = rest_ref[i]
        larow = la_ref[i]
        rcols = _columns([qkrow[:, j * RET_DK:(j + 1) * RET_DK] for j in range(2 * RET_HEADS)])
        gq = [rrow[:, R_GQ + h * GLA_DK: R_GQ + (h + 1) * GLA_DK] * (GLA_DK ** -0.5) for h in range(GLA_HEADS)]
        gk = [rrow[:, R_GK + h * GLA_DK: R_GK + (h + 1) * GLA_DK] for h in range(GLA_HEADS)]
        ga = [jnp.exp(larow[:, h * GLA_DK:(h + 1) * GLA_DK]) for h in range(GLA_HEADS)]
        gcols = _columns(gq + gk + ga)
        outs = []
        for h in range(RET_HEADS):
            v = rrow[:, R_RV + h * RET_DV: R_RV + (h + 1) * RET_DV]
            gate = rrow[:, R_RG + h * RET_DV: R_RG + (h + 1) * RET_DV]
            gamma = jnp.exp(lg_ref[h][:1, :1])
            qc = rcols[:, h:h + 1]
            kc = rcols[:, RET_HEADS + h:RET_HEADS + h + 1]
            s_new = sr_ref[0, i, h] * gamma + kc * v
            sr_out[0, i, h] = s_new
            o = jnp.sum(qc * s_new, axis=0, keepdims=True)
            outs.append(_rms(o) * gro_ref[:, h * RET_DV:(h + 1) * RET_DV] * _silu(gate))
        for h in range(GLA_HEADS):
            v = rrow[:, R_GV + h * GLA_DV: R_GV + (h + 1) * GLA_DV]
            gate = rrow[:, R_GR + h * GLA_DV: R_GR + (h + 1) * GLA_DV]
            qc = gcols[:, h:h + 1]
            kc = gcols[:, GLA_HEADS + h:GLA_HEADS + h + 1]
            ac = gcols[:, 2 * GLA_HEADS + h:2 * GLA_HEADS + h + 1]
            s_new = sg_ref[0, i, h] * ac + kc * v
            sg_out[0, i, h] = s_new
            o = jnp.sum(qc * s_new, axis=0, keepdims=True)
            outs.append(_rms(o) * ggo_ref[:, h * GLA_DV:(h + 1) * GLA_DV] * _silu(gate))
        o_ref[i] = jnp.concatenate(outs, axis=-1)


def _outproj_body(oa_ref, ob_ref, x_ref, gt_ref, sc_ref, sh_ref, g_ref, w_ref, x1_ref, h_ref):
    ka = oa_ref.shape[1]
    mix = _dot(oa_ref[...].astype(BF16), w_ref[:ka, :]) + _dot(ob_ref[...].astype(BF16), w_ref[ka:, :])
    x1 = x_ref[...] + gt_ref[0] * mix
    x1_ref[...] = x1
    h = _rms(x1) * g_ref[...]
    h_ref[...] = (h * (1.0 + sc_ref[0]) + sh_ref[0]).astype(BF16)


def _outproj(o_a, o_b, x, gt, sc, sh, g_ffn, w_out_bf16, *, tm, rows_per_mod):
    m, d = x.shape
    ka, kb = o_a.shape[1], o_b.shape[1]
    mod_spec = pl.BlockSpec((1, gt.shape[1], d), lambda i: ((i * tm) // rows_per_mod, 0, 0))
    return pl.pallas_call(
        _outproj_body,
        grid=(m // tm,),
        in_specs=[pl.BlockSpec((tm, ka), lambda i: (i, 0)),
                  pl.BlockSpec((tm, kb), lambda i: (i, 0)),
                  pl.BlockSpec((tm, d), lambda i: (i, 0)),
                  mod_spec, mod_spec, mod_spec,
                  pl.BlockSpec((1, d), lambda i: (0, 0)),
                  pl.BlockSpec((ka + kb, d), lambda i: (0, 0), pipeline_mode=pl.Buffered(1))],
        out_specs=[pl.BlockSpec((tm, d), lambda i: (i, 0)),
                   pl.BlockSpec((tm, d), lambda i: (i, 0))],
        out_shape=[jax.ShapeDtypeStruct((m, d), F32),
                   jax.ShapeDtypeStruct((m, d), BF16)],
        compiler_params=_params(("arbitrary",)),
        name="out_proj",
    )(o_a, o_b, x, gt, sc, sh, g_ffn.reshape(1, d), w_out_bf16)


LANES = 128
HALO = 8


def _ffn_up_body(h_ref, wa_ref, wb_ref, cwa_ref, cwb_ref, cba_ref, cbb_ref, g_ref, cs_ref, win_ref, tail_ref, *, tiles_per_seq):
    i, f = pl.program_id(0), pl.program_id(1)
    hb = h_ref[...]
    tm = hb.shape[0]
    first = (i % tiles_per_seq) == 0
    ucs = []
    for part, (w_ref, cw_ref, cb_ref) in enumerate(((wa_ref, cwa_ref, cba_ref), (wb_ref, cwb_ref, cbb_ref))):
        u = _dot(hb, w_ref[...])
        prev = jnp.where(first, 0.0, tail_ref[part, f])
        pieces = []
        for c in range(u.shape[1] // LANES):
            lc = slice(c * LANES, (c + 1) * LANES)
            win_ref[part, c, 0:HALO, :] = prev[:, lc]
            win_ref[part, c, HALO:HALO + tm, :] = u[:, lc]
            cw, cb = cw_ref[:, lc], cb_ref[:, lc]
            pieces.append(cb + cw[0:1] * win_ref[part, c, HALO - 2:HALO - 2 + tm, :]
                          + cw[1:2] * win_ref[part, c, HALO - 1:HALO - 1 + tm, :] + cw[2:3] * u[:, lc])
        ucs.append(jnp.concatenate(pieces, axis=-1))
        tail_ref[part, f] = u[tm - HALO:]
        for r in range(CONV_W - 1):
            cs_ref[0, 0, r, part:part + 1, :] = u[tm - (CONV_W - 1) + r: tm - (CONV_W - 1) + r + 1]
    g_ref[...] = (_silu(ucs[0]) * ucs[1]).astype(g_ref.dtype)


def _ffn_up_state_body(h_ref, wa_ref, wb_ref, cwa_ref, cwb_ref, cba_ref, cbb_ref,
                       lg_ref, qk_ref, rest_ref, la_ref, sr_ref, sg_ref, gro_ref, ggo_ref, wd_ref,
                       g_ref, cs_ref, o_ref, sr_out, sg_out, wdb_ref, win_ref, tail_ref, *, tiles_per_seq):
    _ffn_up_body(h_ref, wa_ref, wb_ref, cwa_ref, cwb_ref, cba_ref, cbb_ref, g_ref, cs_ref, win_ref, tail_ref,
                 tiles_per_seq=tiles_per_seq)
    _state_body(lg_ref, qk_ref, rest_ref, la_ref, sr_ref, sg_ref, gro_ref, ggo_ref, o_ref, sr_out, sg_out)
    wdb_ref[...] = wd_ref[...].astype(wdb_ref.dtype)


def _ffn_up_prompt(h, w_up, conv_w, conv_b, qk_s, rest_s, la_s, state_ret, state_gla, g_ret_out, g_gla_out, w_down,
                   *, batch, seq, tm, tf, nb, cast_rows):
    m, d = h.shape
    ff = w_up.shape[1] // 2
    nf = ff // tf
    cb = conv_b.reshape(1, 2 * ff)
    tps = seq // tm
    ms = qk_s.shape[0]
    nblk = ms // nb
    ncast = ff // cast_rows
    assert (m // tm) * nf >= nblk + ncast
    mix = RET_W + GLA_W
    lg = jnp.broadcast_to(_ret_log_gamma()[:, None, None], (RET_HEADS, 8, 128))
    sblk = lambda i, f: jnp.minimum(i * nf + f, nblk - 1)
    ret_spec = pl.BlockSpec((1, nb, RET_HEADS, RET_DK, RET_DV), lambda i, f: (0, sblk(i, f), 0, 0, 0))
    gla_spec = pl.BlockSpec((1, nb, GLA_HEADS, GLA_DK, GLA_DV), lambda i, f: (0, sblk(i, f), 0, 0, 0))
    row_spec = lambda w: pl.BlockSpec((nb, 1, w), lambda i, f: (sblk(i, f), 0, 0))
    cast_spec = pl.BlockSpec((cast_rows, d), lambda i, f: (jnp.clip(i * nf + f - nblk, 0, ncast - 1), 0))
    body = functools.partial(_ffn_up_state_body, tiles_per_seq=tps)
    gate, tails, o_s, s_ret, s_gla, w_down_b = pl.pallas_call(
        body,
        grid=(m // tm, nf),
        in_specs=[pl.BlockSpec((tm, d), lambda i, f: (i, 0)),
                  pl.BlockSpec((d, tf), lambda i, f: (0, f)),
                  pl.BlockSpec((d, tf), lambda i, f: (0, nf + f)),
                  pl.BlockSpec((CONV_W, tf), lambda i, f: (0, f)),
                  pl.BlockSpec((CONV_W, tf), lambda i, f: (0, nf + f)),
                  pl.BlockSpec((1, tf), lambda i, f: (0, f)),
                  pl.BlockSpec((1, tf), lambda i, f: (0, nf + f)),
                  pl.BlockSpec((RET_HEADS, 8, 128), lambda i, f: (0, 0, 0)),
                  row_spec(QK_W), row_spec(REST_W), row_spec(la_s.shape[1]),
                  ret_spec, gla_spec,
                  pl.BlockSpec((1, RET_W), lambda i, f: (0, 0)),
                  pl.BlockSpec((1, GLA_W), lambda i, f: (0, 0)),
                  cast_spec],
        out_specs=[pl.BlockSpec((tm, tf), lambda i, f: (i, f)),
                   pl.BlockSpec((1, 1, CONV_W - 1, 2, tf), lambda i, f: (i // tps, i % tps, 0, 0, f)),
                   row_spec(mix), ret_spec, gla_spec, cast_spec],
        out_shape=[jax.ShapeDtypeStruct((m, ff), BF16),
                   jax.ShapeDtypeStruct((batch, tps, CONV_W - 1, 2, ff), F32),
                   jax.ShapeDtypeStruct((ms, 1, mix), F32),
                   jax.ShapeDtypeStruct(state_ret.shape, F32),
                   jax.ShapeDtypeStruct(state_gla.shape, F32),
                   jax.ShapeDtypeStruct(w_down.shape, BF16)],
        scratch_shapes=[pltpu.VMEM((2, tf // LANES, HALO + tm, LANES), F32), pltpu.VMEM((2, nf, HALO, tf), F32)],
        compiler_params=_params(("arbitrary", "arbitrary")),
        name="ffn_up_prompt",
    )(h, w_up, w_up, conv_w, conv_w, cb, cb,
      lg, qk_s.astype(F32).reshape(ms, 1, QK_W), rest_s.reshape(ms, 1, REST_W), la_s.reshape(ms, 1, la_s.shape[1]),
      state_ret, state_gla, g_ret_out.reshape(1, RET_W), g_gla_out.reshape(1, GLA_W), w_down)
    return gate, tails[:, tps - 1], o_s.reshape(ms, mix), s_ret, s_gla, w_down_b


def _ffn_up_step_body(h_ref, w_ref, cw_ref, cb_ref, st_ref, g_ref, cs_ref, uca_ref, *, nf):
    j = pl.program_id(0)
    u = _dot(h_ref[...], w_ref[...])
    s1 = st_ref[0, :, 1, :]
    cw = cw_ref[...]
    uc = cb_ref[...] + cw[0:1] * st_ref[0, :, 0, :] + cw[1:2] * s1 + cw[2:3] * u
    cs_ref[0, :, 0, :] = s1
    cs_ref[0, :, 1, :] = u

    @pl.when(j < nf)
    def _():
        uca_ref[j] = uc

    @pl.when(j >= nf)
    def _():
        g_ref[...] = (_silu(uca_ref[j - nf]) * uc).astype(g_ref.dtype)


def _ffn_up_step(h, w_up, conv_w, conv_b, state_conv, *, tf):
    m, d = h.shape
    ff = w_up.shape[1] // 2
    nf = ff // tf
    st_spec = pl.BlockSpec((1, m, CONV_W - 1, tf), lambda j: (0, 0, 0, j))
    return pl.pallas_call(
        functools.partial(_ffn_up_step_body, nf=nf),
        grid=(2 * nf,),
        in_specs=[pl.BlockSpec((m, d), lambda j: (0, 0)),
                  pl.BlockSpec((d, tf), lambda j: (0, j)),
                  pl.BlockSpec((CONV_W, tf), lambda j: (0, j)),
                  pl.BlockSpec((1, tf), lambda j: (0, j)),
                  st_spec],
        out_specs=[pl.BlockSpec((m, tf), lambda j: (0, jnp.maximum(j - nf, 0))), st_spec],
        out_shape=[jax.ShapeDtypeStruct((m, ff), BF16),
                   jax.ShapeDtypeStruct(state_conv.shape, F32)],
        scratch_shapes=[pltpu.VMEM((nf, m, tf), F32)],
        compiler_params=_params(("arbitrary",)),
        name="ffn_up_step",
    )(h, w_up, conv_w, conv_b.reshape(1, 2 * ff), state_conv)


def _ffn_down_body(g_ref, w_ref, x1_ref, gt_ref, gf_ref, y_ref):
    k = pl.program_id(1)
    last = pl.num_programs(1) - 1
    d = y_ref.shape[1]

    @pl.when(k == 0)
    def _():
        y_ref[...] = _dot(g_ref[...], w_ref[...])

    @pl.when((k > 0) & (k < last))
    def _():
        y_ref[...] += _dot(g_ref[...], w_ref[...])

    @pl.when(k == last)
    def _():
        g = g_ref[...]
        ssq = jnp.zeros((y_ref.shape[0], 1), F32)
        for cb in range(d // PROJ_BLK):
            cs = slice(cb * PROJ_BLK, (cb + 1) * PROJ_BLK)
            x2 = x1_ref[:, cs] + gt_ref[0][:, cs] * (y_ref[:, cs] + _dot(g, w_ref[:, cs]))
            y_ref[:, cs] = x2
            ssq = ssq + jnp.sum(x2 * x2, axis=-1, keepdims=True)
        inv = lax.rsqrt(ssq * (1.0 / d) + EPS)
        for cb in range(d // PROJ_BLK):
            cs = slice(cb * PROJ_BLK, (cb + 1) * PROJ_BLK)
            y_ref[:, cs] = y_ref[:, cs] * inv * gf_ref[:, cs]


def _ffn_down(g, w_down, x1, gt, g_final, *, tm, tk, rows_per_mod):
    m, d = x1.shape
    ff = g.shape[1]
    assert ff // tk >= 2
    return pl.pallas_call(
        _ffn_down_body,
        grid=(m // tm, ff // tk),
        in_specs=[pl.BlockSpec((tm, tk), lambda i, k: (i, k)),
                  pl.BlockSpec((tk, d), lambda i, k: (k, 0)),
                  pl.BlockSpec((tm, d), lambda i, k: (i, 0)),
                  pl.BlockSpec((1, gt.shape[1], d), lambda i, k: ((i * tm) // rows_per_mod, 0, 0)),
                  pl.BlockSpec((1, d), lambda i, k: (0, 0))],
        out_specs=pl.BlockSpec((tm, d), lambda i, k: (i, 0)),
        out_shape=jax.ShapeDtypeStruct((m, d), F32),
        compiler_params=_params(("arbitrary", "arbitrary")),
        name="ffn_down",
    )(g, w_down, x1, gt, g_final.reshape(1, d))


def _rope_tables(pos):
    half = RET_DK // 2
    inv = ROPE_THETA ** (-jnp.arange(half, dtype=F32) / half)
    ang = pos.astype(F32)[:, None] * inv[None, :]
    return jnp.cos(ang), jnp.sin(ang)


def kernel(x_prompt, x_sample, c_prompt, c_sample, state_ret, state_gla, state_conv, w_ada, b_ada, g_attn, w_in, w_a2, b_a2, g_ret_out, g_gla_out, w_out, g_ffn, w_up, conv_w, conv_b, w_down, g_final):
    bp, t_p, d = x_prompt.shape
    bs, t_s, _ = x_sample.shape
    assert t_s == 1 and w_ada.shape[0] == 1
    mp = bp * t_p
    w_ada, b_ada, g_attn, w_in, w_a2, b_a2, g_ret_out, g_gla_out, w_out, g_ffn, w_up, conv_w, conv_b, w_down = (
        a[0] for a in (w_ada, b_ada, g_attn, w_in, w_a2, b_a2, g_ret_out, g_gla_out, w_out, g_ffn, w_up, conv_w, conv_b, w_down))

    mod, w_up_b = _ada(jnp.concatenate([c_prompt, c_sample], axis=0), w_ada, b_ada, w_up)
    sh1p, sc1p, gt1p, sh2p, sc2p, gt2p = (mod[:bp, i * d:(i + 1) * d].reshape(bp, 1, d) for i in range(6))
    sh1s, sc1s, gt1s, sh2s, sc2s, gt2s = (mod[bp:, i * d:(i + 1) * d].reshape(1, bs, d) for i in range(6))

    cos_p, sin_p = _rope_tables(jnp.arange(t_p, dtype=jnp.int32))
    cos_s, sin_s = (jnp.broadcast_to(t, (bs, RET_DK // 2)) for t in _rope_tables(PAST_LEN + jnp.arange(t_s, dtype=jnp.int32)))
    w_out_b = w_out.astype(BF16)
    w_in_t = w_in.T
    w_gate_t = w_in_t[IN_MAIN:]

    xp = x_prompt.reshape(mp, d)
    xs = x_sample.reshape(bs, d)
    qk_s, rest_s, la_s, w_in_b = _inproj(xs, sc1s, sh1s, g_attn, w_in_t, w_gate_t, w_a2, b_a2, cos_s, sin_s,
                                         tm=bs, rows_per_mod=bs, cast_w=True)
    qk_p, rest_p, la_p = _inproj(xp, sc1p, sh1p, g_attn, w_in_b, w_gate_t, w_a2, b_a2, cos_p, sin_p,
                                 tm=1024, rows_per_mod=t_p)
    x1_p, h2_p, s_ret_p, s_gla_p = _mixer_prompt(qk_p, rest_p, la_p, xp, gt1p, sc2p, sh2p, g_ret_out, g_gla_out, g_ffn,
                                                 w_out_b, batch=bp, seq=t_p, chunk=256)
    g_p, cs_p, o_s, s_ret_s, s_gla_s, w_down_b = _ffn_up_prompt(
        h2_p, w_up_b, conv_w, conv_b, qk_s, rest_s, la_s, state_ret, state_gla, g_ret_out, g_gla_out, w_down,
        batch=bp, seq=t_p, tm=1024, tf=512, nb=2, cast_rows=256)
    y_p = _ffn_down(g_p, w_down_b, x1_p, gt2p, g_final, tm=1024, tk=512, rows_per_mod=t_p)

    x1_s, h2_s = _outproj(o_s[:, :RET_W], o_s[:, RET_W:], xs, gt1s, sc2s, sh2s, g_ffn, w_out_b, tm=bs, rows_per_mod=bs)
    g_s, cs_s = _ffn_up_step(h2_s, w_up_b, conv_w, conv_b, state_conv, tf=512)
    y_s = _ffn_down(g_s, w_down_b, x1_s, gt2s, g_final, tm=bs, tk=1408, rows_per_mod=bs)

    return (y_p.reshape(bp, t_p, d), y_s.reshape(bs, t_s, d),
            s_ret_p[None], s_ret_s, s_gla_p[None], s_gla_s,
            cs_p.reshape(1, bp, CONV_W - 1, -1), cs_s)
```

```python
import functools

import numpy as np
import jax
import jax.numpy as jnp
from jax import lax
from jax.experimental import pallas as pl
from jax.experimental.pallas import tpu as pltpu

F32 = jnp.float32
BF16 = jnp.bfloat16

RET_HEADS = 4
RET_DK = 256
RET_DV = 256
GLA_HEADS = 4
GLA_DK = 128
GLA_DV = 256
GLA_RANK = 16
GLA_TAU = 16.0
ROPE_THETA = 10000.0
PAST_LEN = 16384
CONV_W = 3
EPS = 1e-6

RET_W = RET_HEADS * RET_DV
GLA_W = GLA_HEADS * GLA_DV
QK_W = 2 * RET_HEADS * RET_DK
R_RV = 0
R_RG = R_RV + RET_W
R_GQ = R_RG + RET_W
R_GK = R_GQ + GLA_HEADS * GLA_DK
R_GV = R_GK + GLA_HEADS * GLA_DK
R_GR = R_GV + GLA_W
REST_W = R_GR + GLA_W
IN_MAIN = QK_W + REST_W

V7X_VMEM_BYTES = 64 * 1024 * 1024
V7X_MXU_WIDTH = 256
LANES = 128
SUBLANES = 8
VMEM_LIMIT_BYTES = V7X_VMEM_BYTES - 8 * 1024 * 1024
PROJ_BLK = V7X_MXU_WIDTH

ROW_TILE = 1024
IN_COL_TILE = 1024
FF_COL_TILE = 512
DOWN_K_TILE = 512
CHUNK = 256
STATE_SEQS = 2
CAST_ROWS = 256


def _params(semantics):
    return pltpu.CompilerParams(dimension_semantics=semantics, vmem_limit_bytes=VMEM_LIMIT_BYTES)


def _dot(a, b):
    return jnp.dot(a, b, preferred_element_type=F32)


def _dot_nt(a, b):
    return lax.dot_general(a, b, (((1,), (1,)), ((), ())), preferred_element_type=F32)


def _dot_tn(a, b):
    return lax.dot_general(a, b, (((0,), (0,)), ((), ())), preferred_element_type=F32)


def _silu(x):
    return x * jax.nn.sigmoid(x)


def _rms(x):
    return x * lax.rsqrt(jnp.mean(x * x, axis=-1, keepdims=True) + EPS)


def _ada_body(c_ref, w_ref, b_ref, wu_ref, o_ref, wub_ref):
    s = _silu(c_ref[...]).astype(BF16)
    o_ref[...] = _dot(s, w_ref[...].astype(BF16)) + b_ref[...]
    wub_ref[...] = wu_ref[...].astype(wub_ref.dtype)


def _ada(c_all, w_ada, b_ada, w_up, tn=FF_COL_TILE):
    r, d = c_all.shape
    n = w_ada.shape[1]
    ncast = w_up.shape[1] // tn
    assert n // tn >= ncast
    cast_spec = pl.BlockSpec((w_up.shape[0], tn), lambda j: (0, jnp.minimum(j, ncast - 1)))
    return pl.pallas_call(
        _ada_body,
        grid=(n // tn,),
        in_specs=[pl.BlockSpec((r, d), lambda j: (0, 0)),
                  pl.BlockSpec((d, tn), lambda j: (0, j)),
                  pl.BlockSpec((1, tn), lambda j: (0, j)),
                  cast_spec],
        out_specs=[pl.BlockSpec((r, tn), lambda j: (0, j)), cast_spec],
        out_shape=[jax.ShapeDtypeStruct((r, n), F32), jax.ShapeDtypeStruct(w_up.shape, BF16)],
        compiler_params=_params(("arbitrary",)),
        name="ada_mod",
    )(c_all, w_ada, b_ada.reshape(1, n), w_up)


def _inproj_body(x_ref, sc_ref, sh_ref, g_ref, w_ref, wga_ref, wa2_ref, ba2_ref, cos_ref, sin_ref,
                 qk_ref, rest_ref, la_ref, *more, n_qk, cast_w):
    h_ref = more[-1]
    j = pl.program_id(1)

    @pl.when(j == 0)
    def _():
        gain = g_ref[...] * (1.0 + sc_ref[0])
        hb = (_rms(x_ref[...]) * gain + sh_ref[0]).astype(BF16)
        h_ref[...] = hb
        ga = _dot_nt(hb, wga_ref[...].astype(BF16))
        z = _dot(ga.astype(BF16), wa2_ref[...].astype(BF16)) + ba2_ref[...]
        la_ref[...] = (jnp.minimum(z, 0.0) - jnp.log1p(jnp.exp(-jnp.abs(z)))) * (1.0 / GLA_TAU)

    if cast_w:
        w_tile = w_ref[...].T.astype(BF16)
        more[0][...] = w_tile
        weight = lambda: w_tile
    else:
        weight = lambda: w_ref[...]

    @pl.when(j < n_qk)
    def _():
        acc = _dot(h_ref[...], weight())
        cos, sin = cos_ref[...], sin_ref[...]
        half = RET_DK // 2
        scale = jnp.where(j >= n_qk // 2, RET_DK ** -0.5, 1.0)
        outs = []
        for hh in range(acc.shape[1] // RET_DK):
            x1 = acc[:, hh * RET_DK: hh * RET_DK + half]
            x2 = acc[:, hh * RET_DK + half: (hh + 1) * RET_DK]
            outs += [x1 * cos - x2 * sin, x1 * sin + x2 * cos]
        qk_ref[...] = (jnp.concatenate(outs, axis=-1) * scale).astype(qk_ref.dtype)

    @pl.when(j >= n_qk)
    def _():
        rest_ref[...] = _dot(h_ref[...], weight())


def _inproj(x, sc, sh, g_attn, w, w_gate_t, w_a2, b_a2, cos, sin, *, tm, rows_per_mod, cast_w=False, tn=IN_COL_TILE):
    m, d = x.shape
    nq = w_a2.shape[1]
    n_qk = QK_W // tn
    tab_tiles = cos.shape[0] // tm
    mod_spec = pl.BlockSpec((1, sc.shape[1], d), lambda i, j: ((i * tm) // rows_per_mod, 0, 0))
    tab_spec = pl.BlockSpec((tm, RET_DK // 2), lambda i, j: (i % tab_tiles, 0))
    out_specs = [pl.BlockSpec((tm, tn), lambda i, j: (i, jnp.minimum(j, n_qk - 1))),
                 pl.BlockSpec((tm, tn), lambda i, j: (i, jnp.maximum(j - n_qk, 0))),
                 pl.BlockSpec((tm, nq), lambda i, j: (i, 0))]
    out_shape = [jax.ShapeDtypeStruct((m, QK_W), BF16),
                 jax.ShapeDtypeStruct((m, REST_W), F32),
                 jax.ShapeDtypeStruct((m, nq), F32)]
    if cast_w:
        assert m == tm
        w_spec = pl.BlockSpec((tn, d), lambda i, j: (j, 0))
        out_specs.append(pl.BlockSpec((d, tn), lambda i, j: (0, j)))
        out_shape.append(jax.ShapeDtypeStruct((d, IN_MAIN), BF16))
    else:
        w_spec = pl.BlockSpec((d, tn), lambda i, j: (0, j))
    return pl.pallas_call(
        functools.partial(_inproj_body, n_qk=n_qk, cast_w=cast_w),
        grid=(m // tm, IN_MAIN // tn),
        in_specs=[pl.BlockSpec((tm, d), lambda i, j: (i, 0)),
                  mod_spec, mod_spec,
                  pl.BlockSpec((1, d), lambda i, j: (0, 0)),
                  w_spec,
                  pl.BlockSpec((GLA_RANK, d), lambda i, j: (0, 0)),
                  pl.BlockSpec((GLA_RANK, nq), lambda i, j: (0, 0)),
                  pl.BlockSpec((1, nq), lambda i, j: (0, 0)),
                  tab_spec, tab_spec],
        out_specs=out_specs,
        out_shape=out_shape,
        scratch_shapes=[pltpu.VMEM((tm, d), BF16)],
        compiler_params=_params(("arbitrary", "arbitrary")),
        name="in_proj",
    )(x, sc, sh, g_attn.reshape(1, d), w, w_gate_t, w_a2, b_a2.reshape(1, nq), cos, sin)


def _ret_log_gamma():
    return jnp.log1p(-jnp.exp2(-5.0 - jnp.arange(RET_HEADS, dtype=F32)))


def _ret_decay_matrix(chunk):
    idx = jnp.arange(chunk, dtype=F32)
    rel = idx[:, None] - idx[None, :]
    lg = _ret_log_gamma()
    return jnp.where(rel[None] >= 0, jnp.exp(jnp.maximum(rel, 0.0)[None] * lg[:, None, None]), 0.0)


def _ret_heads(dmat_ref, lg_ref, q_ref, k_ref, v_ref, g_ref, gout_ref, s_ref, new_ref, fresh, valid):
    cl = q_ref.shape[0]
    idx = lax.broadcasted_iota(jnp.int32, (cl, 1), 0).astype(F32)
    for h in range(RET_HEADS):
        sl = slice(h * RET_DV, (h + 1) * RET_DV)
        lg = lg_ref[h][:1, :1]
        q_dec = jnp.exp((idx + 1.0) * lg)
        k_dec = jnp.exp((cl - 1.0 - idx) * lg)
        c_dec = jnp.exp(cl * lg)
        qb, kb = q_ref[:, sl], k_ref[:, sl]
        v = v_ref[:, sl]
        vb = v.astype(BF16)
        s_old = jnp.where(fresh, 0.0, s_ref[h])
        scores = _dot_nt(qb, kb) * dmat_ref[h]
        o = _dot(scores.astype(BF16), vb) + _dot(qb, s_old.astype(BF16)) * q_dec
        s_new = s_old * c_dec + _dot_tn(kb, (v * k_dec).astype(BF16))
        s_ref[h] = jnp.where(valid, s_new, s_old)
        new_ref[:, sl] = (_rms(o) * gout_ref[:, sl] * _silu(g_ref[:, sl])).astype(new_ref.dtype)


def _gla_level_map(c):
    t = np.arange(c)[:, None]
    s = np.arange(c)[None, :]
    x = np.bitwise_xor(t, s)
    lev = np.floor(np.log2(np.maximum(x, 1))).astype(np.int32)
    lev = np.where(t == s, -1, lev)
    lev = np.where(t < s, -2, lev)
    return lev.astype(np.int32)


SCORE_BLK = 128
LOG2E = 1.4426950408889634


def _gla_chunk(q, k, la2, lev, uppers, at_level=None):
    cl = q.shape[0]
    nblk = cl // SCORE_BLK
    blk = lambda x, i: x[i * SCORE_BLK:(i + 1) * SCORE_BLK]
    p = la2
    tot = la2
    diag = [jnp.zeros((SCORE_BLK, SCORE_BLK), F32) for _ in range(nblk)]
    off = {}
    level = 0
    half = 1
    while half < cl:
        if at_level is not None:
            at_level(level)
        upper = uppers[level]
        z = (jnp.where(upper, q, k) * jnp.exp2(jnp.where(upper, p, tot - p))).astype(BF16)
        if half < SCORE_BLK:
            for i in range(nblk):
                diag[i] = jnp.where(lev == level, _dot_nt(blk(z, i), blk(z, i)), diag[i])
        else:
            hb = half // SCORE_BLK
            for i in range(nblk):
                if (i // hb) % 2 == 1:
                    base = (i // (2 * hb)) * 2 * hb
                    for j in range(base, base + hb):
                        off[(i, j)] = _dot_nt(blk(z, i), blk(z, j))
        partner = jnp.where(upper, pltpu.roll(tot, half, 0), pltpu.roll(tot, cl - half, 0))
        p = p + jnp.where(upper, partner, 0.0)
        tot = tot + partner
        half *= 2
        level += 1
    dg = jnp.sum(q * k, axis=-1, keepdims=True)
    rows = []
    for i in range(nblk):
        d_i = jnp.where(lev == -1, blk(dg, i), diag[i])
        rows.append(jnp.concatenate([off[(i, j)] for j in range(i)] + [d_i], axis=-1))
    return rows, p, tot


def _gla_heads(q_ref, k_ref, v_ref, g_ref, la_ref, lev_ref, gout_ref, st_ref, new_ref, fresh, valid, at_slot=None):
    cl = q_ref.shape[0]
    lev = lev_ref[...]
    row = lax.broadcasted_iota(jnp.int32, (cl, GLA_DK), 0)
    uppers = []
    half = 1
    while half < cl:
        uppers.append((row & half) != 0)
        half *= 2
    nlev = len(uppers)
    for h in range(GLA_HEADS):
        ks = slice(h * GLA_DK, (h + 1) * GLA_DK)
        vs = slice(h * GLA_DV, (h + 1) * GLA_DV)
        q = q_ref[:, ks] * (GLA_DK ** -0.5)
        k = k_ref[:, ks]
        hook = None if at_slot is None else (lambda level, h=h: at_slot(h * nlev + level))
        rows, p, tot = _gla_chunk(q, k, la_ref[:, ks] * LOG2E, lev, uppers, hook)
        vb = v_ref[:, vs].astype(BF16)
        st_old = jnp.where(fresh, 0.0, st_ref[h])
        qt = (q * jnp.exp2(p)).astype(BF16)
        intra = jnp.concatenate([_dot(r.astype(BF16), vb[:r.shape[1]]) for r in rows], axis=0)
        o = intra + _dot_nt(qt, st_old.astype(BF16))
        kt = (k * jnp.exp2(tot - p)).astype(BF16)
        st_new = st_old * jnp.exp2(tot[0:1, :]) + _dot_tn(vb, kt)
        st_ref[h] = jnp.where(valid, st_new, st_old)
        new_ref[:, RET_W + h * GLA_DV: RET_W + (h + 1) * GLA_DV] = (
            _rms(o) * gout_ref[:, vs] * _silu(g_ref[:, vs])).astype(new_ref.dtype)


def _mixer_body(dmat_ref, lg_ref, q_ref, k_ref, rv_ref, rg_ref, gq_ref, gk_ref, gv_ref, gr_ref, la_ref, lev_ref,
                gro_ref, ggo_ref, x_ref, gt_ref, sc_ref, sh_ref, gf_ref, w_ref,
                x1_ref, h_ref, sr_out, sg_out, s_ref, st_ref, mixa_ref, mixb_ref, *, nc, n_chunks):
    s = pl.program_id(0)
    valid = s < n_chunks
    c = jnp.minimum(s, n_chunks - 1) % nc
    fresh = c == 0

    @pl.when(s == 0)
    def _():
        mixb_ref[...] = jnp.zeros_like(mixb_ref)

    def step(old_ref, new_ref):
        old = old_ref[...]
        d = x_ref.shape[1]
        nblk = d // PROJ_BLK
        ssq = [jnp.zeros((x_ref.shape[0], 1), F32)]

        def project(blocks):
            for cb in blocks:
                cs = slice(cb * PROJ_BLK, (cb + 1) * PROJ_BLK)
                x1 = x_ref[:, cs] + gt_ref[0][:, cs] * _dot(old, w_ref[:, cs])
                x1_ref[:, cs] = x1
                ssq[0] = ssq[0] + jnp.sum(x1 * x1, axis=-1, keepdims=True)

        nslots = GLA_HEADS * (x_ref.shape[0].bit_length() - 1)
        due = {(cb * nslots) // nblk: cb for cb in range(nblk)}
        _gla_heads(gq_ref, gk_ref, gv_ref, gr_ref, la_ref, lev_ref, ggo_ref, st_ref, new_ref, fresh, valid,
                   at_slot=lambda slot: project([due[slot]]) if slot in due else None)
        _ret_heads(dmat_ref, lg_ref, q_ref, k_ref, rv_ref, rg_ref, gro_ref, s_ref, new_ref, fresh, valid)
        inv = lax.rsqrt(ssq[0] * (1.0 / d) + EPS)
        gain = gf_ref[...] * (1.0 + sc_ref[0])
        for cb in range(nblk):
            cs = slice(cb * PROJ_BLK, (cb + 1) * PROJ_BLK)
            h_ref[:, cs] = (x1_ref[:, cs] * inv * gain[:, cs] + sh_ref[0][:, cs]).astype(h_ref.dtype)

    @pl.when(s % 2 == 0)
    def _():
        step(mixb_ref, mixa_ref)

    @pl.when(s % 2 == 1)
    def _():
        step(mixa_ref, mixb_ref)

    @pl.when(valid & (c == nc - 1))
    def _():
        sr_out[0] = s_ref[...]
        for h in range(GLA_HEADS):
            sg_out[0, h] = st_ref[h].T


def _mixer_prompt(qk, rest, la, x, gt, sc, sh, g_ret_out, g_gla_out, g_ffn, w_out_b, *, batch, seq, chunk):
    m, d = x.shape
    nc = seq // chunk
    n_chunks = batch * nc
    lg = jnp.broadcast_to(_ret_log_gamma()[:, None, None], (RET_HEADS, SUBLANES, LANES))
    lev = jnp.asarray(_gla_level_map(SCORE_BLK))
    gqk_w = GLA_HEADS * GLA_DK
    cur = lambda s: jnp.minimum(s, n_chunks - 1)
    prv = lambda s: jnp.maximum(s - 1, 0)
    rowc = lambda w, blk: pl.BlockSpec((chunk, w), lambda s: (cur(s), blk))
    const = lambda shape: pl.BlockSpec(shape, lambda s: (0,) * len(shape))
    mod_spec = pl.BlockSpec((1, 1, d), lambda s: (prv(s) // nc, 0, 0))
    return pl.pallas_call(
        functools.partial(_mixer_body, nc=nc, n_chunks=n_chunks),
        grid=(n_chunks + 1,),
        in_specs=[const((RET_HEADS, chunk, chunk)), const((RET_HEADS, SUBLANES, LANES)),
                  rowc(RET_W, 0), rowc(RET_W, 1),
                  rowc(RET_W, R_RV // RET_W), rowc(RET_W, R_RG // RET_W),
                  rowc(gqk_w, R_GQ // gqk_w), rowc(gqk_w, R_GK // gqk_w),
                  rowc(GLA_W, R_GV // GLA_W), rowc(GLA_W, R_GR // GLA_W),
                  rowc(gqk_w, 0),
                  const((SCORE_BLK, SCORE_BLK)), const((1, RET_W)), const((1, GLA_W)),
                  pl.BlockSpec((chunk, d), lambda s: (prv(s), 0)),
                  mod_spec, mod_spec, mod_spec,
                  const((1, d)),
                  pl.BlockSpec((RET_W + GLA_W, d), lambda s: (0, 0), pipeline_mode=pl.Buffered(1))],
        out_specs=[pl.BlockSpec((chunk, d), lambda s: (prv(s), 0)),
                   pl.BlockSpec((chunk, d), lambda s: (prv(s), 0)),
                   pl.BlockSpec((1, RET_HEADS, RET_DK, RET_DV), lambda s: (cur(s) // nc, 0, 0, 0)),
                   pl.BlockSpec((1, GLA_HEADS, GLA_DK, GLA_DV), lambda s: (cur(s) // nc, 0, 0, 0))],
        out_shape=[jax.ShapeDtypeStruct((m, d), F32),
                   jax.ShapeDtypeStruct((m, d), BF16),
                   jax.ShapeDtypeStruct((batch, RET_HEADS, RET_DK, RET_DV), F32),
                   jax.ShapeDtypeStruct((batch, GLA_HEADS, GLA_DK, GLA_DV), F32)],
        scratch_shapes=[pltpu.VMEM((RET_HEADS, RET_DK, RET_DV), F32),
                        pltpu.VMEM((GLA_HEADS, GLA_DV, GLA_DK), F32),
                        pltpu.VMEM((chunk, RET_W + GLA_W), BF16),
                        pltpu.VMEM((chunk, RET_W + GLA_W), BF16)],
        compiler_params=_params(("arbitrary",)),
        name="mixer_prompt",
    )(_ret_decay_matrix(chunk), lg, qk, qk, rest, rest, rest, rest, rest, rest, la, lev,
      g_ret_out.reshape(1, RET_W), g_gla_out.reshape(1, GLA_W), x, gt, sc, sh, g_ffn.reshape(1, d), w_out_b)


def _columns(rows):
    n = rows[0].shape[1]
    pad = (-len(rows)) % 8
    stack = jnp.concatenate(rows + [jnp.zeros((pad, n), F32)] if pad else rows, axis=0)
    return stack.T


def _state_body(lg_ref, qk_ref, rest_ref, la_ref, sr_ref, sg_ref, gro_ref, ggo_ref, o_ref, sr_out, sg_out):
    nb = qk_ref.shape[0]
    for i in range(nb):
        qkrow = qk_ref[i]
        rrow = rest_ref[i]
        larow = la_ref[i]
        rcols = _columns([qkrow[:, j * RET_DK:(j + 1) * RET_DK] for j in range(2 * RET_HEADS)])
        gq = [rrow[:, R_GQ + h * GLA_DK: R_GQ + (h + 1) * GLA_DK] * (GLA_DK ** -0.5) for h in range(GLA_HEADS)]
        gk = [rrow[:, R_GK + h * GLA_DK: R_GK + (h + 1) * GLA_DK] for h in range(GLA_HEADS)]
        ga = [jnp.exp(larow[:, h * GLA_DK:(h + 1) * GLA_DK]) for h in range(GLA_HEADS)]
        gcols = _columns(gq + gk + ga)
        outs = []
        for h in range(RET_HEADS):
            v = rrow[:, R_RV + h * RET_DV: R_RV + (h + 1) * RET_DV]
            gate = rrow[:, R_RG + h * RET_DV: R_RG + (h + 1) * RET_DV]
            gamma = jnp.exp(lg_ref[h][:1, :1])
            qc = rcols[:, h:h + 1]
            kc = rcols[:, RET_HEADS + h:RET_HEADS + h + 1]
            s_new = sr_ref[0, i, h] * gamma + kc * v
            sr_out[0, i, h] = s_new
            o = jnp.sum(qc * s_new, axis=0, keepdims=True)
            outs.append(_rms(o) * gro_ref[:, h * RET_DV:(h + 1) * RET_DV] * _silu(gate))
        for h in range(GLA_HEADS):
            v = rrow[:, R_GV + h * GLA_DV: R_GV + (h + 1) * GLA_DV]
            gate = rrow[:, R_GR + h * GLA_DV: R_GR + (h + 1) * GLA_DV]
            qc = gcols[:, h:h + 1]
            kc = gcols[:, GLA_HEADS + h:GLA_HEADS + h + 1]
            ac = gcols[:, 2 * GLA_HEADS + h:2 * GLA_HEADS + h + 1]
            s_new = sg_ref[0, i, h] * ac + kc * v
            sg_out[0, i, h] = s_new
            o = jnp.sum(qc * s_new, axis=0, keepdims=True)
            outs.append(_rms(o) * ggo_ref[:, h * GLA_DV:(h + 1) * GLA_DV] * _silu(gate))
        o_ref[i] = jnp.concatenate(outs, axis=-1)


def _outproj_body(oa_ref, ob_ref, x_ref, gt_ref, sc_ref, sh_ref, g_ref, w_ref, x1_ref, h_ref):
    ka = oa_ref.shape[1]
    mix = _dot(oa_ref[...].astype(BF16), w_ref[:ka, :]) + _dot(ob_ref[...].astype(BF16), w_ref[ka:, :])
    x1 = x_ref[...] + gt_ref[0] * mix
    x1_ref[...] = x1
    h = _rms(x1) * g_ref[...]
    h_ref[...] = (h * (1.0 + sc_ref[0]) + sh_ref[0]).astype(BF16)


def _outproj(o_a, o_b, x, gt, sc, sh, g_ffn, w_out_bf16, *, tm, rows_per_mod):
    m, d = x.shape
    ka, kb = o_a.shape[1], o_b.shape[1]
    mod_spec = pl.BlockSpec((1, gt.shape[1], d), lambda i: ((i * tm) // rows_per_mod, 0, 0))
    return pl.pallas_call(
        _outproj_body,
        grid=(m // tm,),
        in_specs=[pl.BlockSpec((tm, ka), lambda i: (i, 0)),
                  pl.BlockSpec((tm, kb), lambda i: (i, 0)),
                  pl.BlockSpec((tm, d), lambda i: (i, 0)),
                  mod_spec, mod_spec, mod_spec,
                  pl.BlockSpec((1, d), lambda i: (0, 0)),
                  pl.BlockSpec((ka + kb, d), lambda i: (0, 0), pipeline_mode=pl.Buffered(1))],
        out_specs=[pl.BlockSpec((tm, d), lambda i: (i, 0)),
                   pl.BlockSpec((tm, d), lambda i: (i, 0))],
        out_shape=[jax.ShapeDtypeStruct((m, d), F32),
                   jax.ShapeDtypeStruct((m, d), BF16)],
        compiler_params=_params(("arbitrary",)),
        name="out_proj",
    )(o_a, o_b, x, gt, sc, sh, g_ffn.reshape(1, d), w_out_bf16)


HALO = SUBLANES


def _ffn_up_body(h_ref, wa_ref, wb_ref, cwa_ref, cwb_ref, cba_ref, cbb_ref, g_ref, cs_ref, win_ref, tail_ref, *, tiles_per_seq):
    i, f = pl.program_id(0), pl.program_id(1)
    hb = h_ref[...]
    tm = hb.shape[0]
    first = (i % tiles_per_seq) == 0
    ucs = []
    for part, (w_ref, cw_ref, cb_ref) in enumerate(((wa_ref, cwa_ref, cba_ref), (wb_ref, cwb_ref, cbb_ref))):
        u = _dot(hb, w_ref[...])
        prev = jnp.where(first, 0.0, tail_ref[part, f])
        pieces = []
        for c in range(u.shape[1] // LANES):
            lc = slice(c * LANES, (c + 1) * LANES)
            win_ref[part, c, 0:HALO, :] = prev[:, lc]
            win_ref[part, c, HALO:HALO + tm, :] = u[:, lc]
            cw, cb = cw_ref[:, lc], cb_ref[:, lc]
            pieces.append(cb + cw[0:1] * win_ref[part, c, HALO - 2:HALO - 2 + tm, :]
                          + cw[1:2] * win_ref[part, c, HALO - 1:HALO - 1 + tm, :] + cw[2:3] * u[:, lc])
        ucs.append(jnp.concatenate(pieces, axis=-1))
        tail_ref[part, f] = u[tm - HALO:]
        for r in range(CONV_W - 1):
            cs_ref[0, 0, r, part:part + 1, :] = u[tm - (CONV_W - 1) + r: tm - (CONV_W - 1) + r + 1]
    g_ref[...] = (_silu(ucs[0]) * ucs[1]).astype(g_ref.dtype)


def _ffn_up_state_body(h_ref, wa_ref, wb_ref, cwa_ref, cwb_ref, cba_ref, cbb_ref,
                       lg_ref, qk_ref, rest_ref, la_ref, sr_ref, sg_ref, gro_ref, ggo_ref, wd_ref,
                       g_ref, cs_ref, o_ref, sr_out, sg_out, wdb_ref, win_ref, tail_ref, *, tiles_per_seq):
    _ffn_up_body(h_ref, wa_ref, wb_ref, cwa_ref, cwb_ref, cba_ref, cbb_ref, g_ref, cs_ref, win_ref, tail_ref,
                 tiles_per_seq=tiles_per_seq)
    _state_body(lg_ref, qk_ref, rest_ref, la_ref, sr_ref, sg_ref, gro_ref, ggo_ref, o_ref, sr_out, sg_out)
    wdb_ref[...] = wd_ref[...].astype(wdb_ref.dtype)


def _ffn_up_prompt(h, w_up, conv_w, conv_b, qk_s, rest_s, la_s, state_ret, state_gla, g_ret_out, g_gla_out, w_down,
                   *, batch, seq, tm, tf, nb, cast_rows):
    m, d = h.shape
    ff = w_up.shape[1] // 2
    nf = ff // tf
    cb = conv_b.reshape(1, 2 * ff)
    tps = seq // tm
    ms = qk_s.shape[0]
    nblk = ms // nb
    ncast = ff // cast_rows
    assert (m // tm) * nf >= nblk + ncast
    mix = RET_W + GLA_W
    lg = jnp.broadcast_to(_ret_log_gamma()[:, None, None], (RET_HEADS, SUBLANES, LANES))
    sblk = lambda i, f: jnp.minimum(i * nf + f, nblk - 1)
    ret_spec = pl.BlockSpec((1, nb, RET_HEADS, RET_DK, RET_DV), lambda i, f: (0, sblk(i, f), 0, 0, 0))
    gla_spec = pl.BlockSpec((1, nb, GLA_HEADS, GLA_DK, GLA_DV), lambda i, f: (0, sblk(i, f), 0, 0, 0))
    row_spec = lambda w: pl.BlockSpec((nb, 1, w), lambda i, f: (sblk(i, f), 0, 0))
    cast_spec = pl.BlockSpec((cast_rows, d), lambda i, f: (jnp.clip(i * nf + f - nblk, 0, ncast - 1), 0))
    body = functools.partial(_ffn_up_state_body, tiles_per_seq=tps)
    gate, tails, o_s, s_ret, s_gla, w_down_b = pl.pallas_call(
        body,
        grid=(m // tm, nf),
        in_specs=[pl.BlockSpec((tm, d), lambda i, f: (i, 0)),
                  pl.BlockSpec((d, tf), lambda i, f: (0, f)),
                  pl.BlockSpec((d, tf), lambda i, f: (0, nf + f)),
                  pl.BlockSpec((CONV_W, tf), lambda i, f: (0, f)),
                  pl.BlockSpec((CONV_W, tf), lambda i, f: (0, nf + f)),
                  pl.BlockSpec((1, tf), lambda i, f: (0, f)),
                  pl.BlockSpec((1, tf), lambda i, f: (0, nf + f)),
                  pl.BlockSpec((RET_HEADS, SUBLANES, LANES), lambda i, f: (0, 0, 0)),
                  row_spec(QK_W), row_spec(REST_W), row_spec(la_s.shape[1]),
                  ret_spec, gla_spec,
                  pl.BlockSpec((1, RET_W), lambda i, f: (0, 0)),
                  pl.BlockSpec((1, GLA_W), lambda i, f: (0, 0)),
                  cast_spec],
        out_specs=[pl.BlockSpec((tm, tf), lambda i, f: (i, f)),
                   pl.BlockSpec((1, 1, CONV_W - 1, 2, tf), lambda i, f: (i // tps, i % tps, 0, 0, f)),
                   row_spec(mix), ret_spec, gla_spec, cast_spec],
        out_shape=[jax.ShapeDtypeStruct((m, ff), BF16),
                   jax.ShapeDtypeStruct((batch, tps, CONV_W - 1, 2, ff), F32),
                   jax.ShapeDtypeStruct((ms, 1, mix), F32),
                   jax.ShapeDtypeStruct(state_ret.shape, F32),
                   jax.ShapeDtypeStruct(state_gla.shape, F32),
                   jax.ShapeDtypeStruct(w_down.shape, BF16)],
        scratch_shapes=[pltpu.VMEM((2, tf // LANES, HALO + tm, LANES), F32), pltpu.VMEM((2, nf, HALO, tf), F32)],
        compiler_params=_params(("arbitrary", "arbitrary")),
        name="ffn_up_prompt",
    )(h, w_up, w_up, conv_w, conv_w, cb, cb,
      lg, qk_s.astype(F32).reshape(ms, 1, QK_W), rest_s.reshape(ms, 1, REST_W), la_s.reshape(ms, 1, la_s.shape[1]),
      state_ret, state_gla, g_ret_out.reshape(1, RET_W), g_gla_out.reshape(1, GLA_W), w_down)
    return gate, tails[:, tps - 1], o_s.reshape(ms, mix), s_ret, s_gla, w_down_b


def _ffn_up_step_body(h_ref, w_ref, cw_ref, cb_ref, st_ref, g_ref, cs_ref, uca_ref, *, nf):
    j = pl.program_id(0)
    u = _dot(h_ref[...], w_ref[...])
    s1 = st_ref[0, :, 1, :]
    cw = cw_ref[...]
    uc = cb_ref[...] + cw[0:1] * st_ref[0, :, 0, :] + cw[1:2] * s1 + cw[2:3] * u
    cs_ref[0, :, 0, :] = s1
    cs_ref[0, :, 1, :] = u

    @pl.when(j < nf)
    def _():
        uca_ref[j] = uc

    @pl.when(j >= nf)
    def _():
        g_ref[...] = (_silu(uca_ref[j - nf]) * uc).astype(g_ref.dtype)


def _ffn_up_step(h, w_up, conv_w, conv_b, state_conv, *, tf):
    m, d = h.shape
    ff = w_up.shape[1] // 2
    nf = ff // tf
    st_spec = pl.BlockSpec((1, m, CONV_W - 1, tf), lambda j: (0, 0, 0, j))
    return pl.pallas_call(
        functools.partial(_ffn_up_step_body, nf=nf),
        grid=(2 * nf,),
        in_specs=[pl.BlockSpec((m, d), lambda j: (0, 0)),
                  pl.BlockSpec((d, tf), lambda j: (0, j)),
                  pl.BlockSpec((CONV_W, tf), lambda j: (0, j)),
                  pl.BlockSpec((1, tf), lambda j: (0, j)),
                  st_spec],
        out_specs=[pl.BlockSpec((m, tf), lambda j: (0, jnp.maximum(j - nf, 0))), st_spec],
        out_shape=[jax.ShapeDtypeStruct((m, ff), BF16),
                   jax.ShapeDtypeStruct(state_conv.shape, F32)],
        scratch_shapes=[pltpu.VMEM((nf, m, tf), F32)],
        compiler_params=_params(("arbitrary",)),
        name="ffn_up_step",
    )(h, w_up, conv_w, conv_b.reshape(1, 2 * ff), state_conv)


def _ffn_down_body(g_ref, w_ref, x1_ref, gt_ref, gf_ref, y_ref):
    k = pl.program_id(1)
    last = pl.num_programs(1) - 1
    d = y_ref.shape[1]

    @pl.when(k == 0)
    def _():
        y_ref[...] = _dot(g_ref[...], w_ref[...])

    @pl.when((k > 0) & (k < last))
    def _():
        y_ref[...] += _dot(g_ref[...], w_ref[...])

    @pl.when(k == last)
    def _():
        g = g_ref[...]
        ssq = jnp.zeros((y_ref.shape[0], 1), F32)
        for cb in range(d // PROJ_BLK):
            cs = slice(cb * PROJ_BLK, (cb + 1) * PROJ_BLK)
            x2 = x1_ref[:, cs] + gt_ref[0][:, cs] * (y_ref[:, cs] + _dot(g, w_ref[:, cs]))
            y_ref[:, cs] = x2
            ssq = ssq + jnp.sum(x2 * x2, axis=-1, keepdims=True)
        inv = lax.rsqrt(ssq * (1.0 / d) + EPS)
        for cb in range(d // PROJ_BLK):
            cs = slice(cb * PROJ_BLK, (cb + 1) * PROJ_BLK)
            y_ref[:, cs] = y_ref[:, cs] * inv * gf_ref[:, cs]


def _ffn_down(g, w_down, x1, gt, g_final, *, tm, tk, rows_per_mod):
    m, d = x1.shape
    ff = g.shape[1]
    assert ff // tk >= 2
    return pl.pallas_call(
        _ffn_down_body,
        grid=(m // tm, ff // tk),
        in_specs=[pl.BlockSpec((tm, tk), lambda i, k: (i, k)),
                  pl.BlockSpec((tk, d), lambda i, k: (k, 0)),
                  pl.BlockSpec((tm, d), lambda i, k: (i, 0)),
                  pl.BlockSpec((1, gt.shape[1], d), lambda i, k: ((i * tm) // rows_per_mod, 0, 0)),
                  pl.BlockSpec((1, d), lambda i, k: (0, 0))],
        out_specs=pl.BlockSpec((tm, d), lambda i, k: (i, 0)),
        out_shape=jax.ShapeDtypeStruct((m, d), F32),
        compiler_params=_params(("arbitrary", "arbitrary")),
        name="ffn_down",
    )(g, w_down, x1, gt, g_final.reshape(1, d))


def _rope_tables(pos):
    half = RET_DK // 2
    inv = ROPE_THETA ** (-jnp.arange(half, dtype=F32) / half)
    ang = pos.astype(F32)[:, None] * inv[None, :]
    return jnp.cos(ang), jnp.sin(ang)


def kernel(x_prompt, x_sample, c_prompt, c_sample, state_ret, state_gla, state_conv, w_ada, b_ada, g_attn, w_in, w_a2, b_a2, g_ret_out, g_gla_out, w_out, g_ffn, w_up, conv_w, conv_b, w_down, g_final):
    bp, t_p, d = x_prompt.shape
    bs, t_s, _ = x_sample.shape
    assert t_s == 1 and w_ada.shape[0] == 1
    mp = bp * t_p
    w_ada, b_ada, g_attn, w_in, w_a2, b_a2, g_ret_out, g_gla_out, w_out, g_ffn, w_up, conv_w, conv_b, w_down = (
        a[0] for a in (w_ada, b_ada, g_attn, w_in, w_a2, b_a2, g_ret_out, g_gla_out, w_out, g_ffn, w_up, conv_w, conv_b, w_down))

    mod, w_up_b = _ada(jnp.concatenate([c_prompt, c_sample], axis=0), w_ada, b_ada, w_up)
    sh1p, sc1p, gt1p, sh2p, sc2p, gt2p = (mod[:bp, i * d:(i + 1) * d].reshape(bp, 1, d) for i in range(6))
    sh1s, sc1s, gt1s, sh2s, sc2s, gt2s = (mod[bp:, i * d:(i + 1) * d].reshape(1, bs, d) for i in range(6))

    cos_p, sin_p = _rope_tables(jnp.arange(t_p, dtype=jnp.int32))
    cos_s, sin_s = (jnp.broadcast_to(t, (bs, RET_DK // 2)) for t in _rope_tables(PAST_LEN + jnp.arange(t_s, dtype=jnp.int32)))
    w_out_b = w_out.astype(BF16)
    w_in_t = w_in.T
    w_gate_t = w_in_t[IN_MAIN:]

    xp = x_prompt.reshape(mp, d)
    xs = x_sample.reshape(bs, d)
    qk_s, rest_s, la_s, w_in_b = _inproj(xs, sc1s, sh1s, g_attn, w_in_t, w_gate_t, w_a2, b_a2, cos_s, sin_s,
                                         tm=bs, rows_per_mod=bs, cast_w=True)
    qk_p, rest_p, la_p = _inproj(xp, sc1p, sh1p, g_attn, w_in_b, w_gate_t, w_a2, b_a2, cos_p, sin_p,
                                 tm=ROW_TILE, rows_per_mod=t_p)
    x1_p, h2_p, s_ret_p, s_gla_p = _mixer_prompt(qk_p, rest_p, la_p, xp, gt1p, sc2p, sh2p, g_ret_out, g_gla_out, g_ffn,
                                                 w_out_b, batch=bp, seq=t_p, chunk=CHUNK)
    g_p, cs_p, o_s, s_ret_s, s_gla_s, w_down_b = _ffn_up_prompt(
        h2_p, w_up_b, conv_w, conv_b, qk_s, rest_s, la_s, state_ret, state_gla, g_ret_out, g_gla_out, w_down,
        batch=bp, seq=t_p, tm=ROW_TILE, tf=FF_COL_TILE, nb=STATE_SEQS, cast_rows=CAST_ROWS)
    y_p = _ffn_down(g_p, w_down_b, x1_p, gt2p, g_final, tm=ROW_TILE, tk=DOWN_K_TILE, rows_per_mod=t_p)

    x1_s, h2_s = _outproj(o_s[:, :RET_W], o_s[:, RET_W:], xs, gt1s, sc2s, sh2s, g_ffn, w_out_b, tm=bs, rows_per_mod=bs)
    g_s, cs_s = _ffn_up_step(h2_s, w_up_b, conv_w, conv_b, state_conv, tf=FF_COL_TILE)
    y_s = _ffn_down(g_s, w_down_b, x1_s, gt2s, g_final, tm=bs, tk=DOWN_K_TILE, rows_per_mod=bs)

    return (y_p.reshape(bp, t_p, d), y_s.reshape(bs, t_s, d),
            s_ret_p[None], s_ret_s, s_gla_p[None], s_gla_s,
            cs_p.reshape(1, bp, CONV_W - 1, -1), cs_s)
```

```python
import functools

import numpy as np
import jax
import jax.numpy as jnp
from jax import lax
from jax.experimental import pallas as pl
from jax.experimental.pallas import tpu as pltpu

F32 = jnp.float32
BF16 = jnp.bfloat16

RET_HEADS = 4
RET_DK = 256
RET_DV = 256
GLA_HEADS = 4
GLA_DK = 128
GLA_DV = 256
GLA_RANK = 16
GLA_TAU = 16.0
ROPE_THETA = 10000.0
PAST_LEN = 16384
CONV_W = 3
EPS = 1e-6

RET_W = RET_HEADS * RET_DV
GLA_W = GLA_HEADS * GLA_DV
QK_W = 2 * RET_HEADS * RET_DK
R_RV = 0
R_RG = R_RV + RET_W
R_GQ = R_RG + RET_W
R_GK = R_GQ + GLA_HEADS * GLA_DK
R_GV = R_GK + GLA_HEADS * GLA_DK
R_GR = R_GV + GLA_W
REST_W = R_GR + GLA_W
IN_MAIN = QK_W + REST_W

V7X_VMEM_BYTES = 64 * 1024 * 1024
V7X_MXU_WIDTH = 256
LANES = 128
SUBLANES = 8
VMEM_LIMIT_BYTES = V7X_VMEM_BYTES - 8 * 1024 * 1024
PROJ_BLK = V7X_MXU_WIDTH

ROW_TILE = 1024
IN_COL_TILE = 1024
FF_COL_TILE = 512
DOWN_K_TILE = 512
CHUNK = 256
STATE_SEQS = 2
CAST_ROWS = 256


def _params(semantics):
    return pltpu.CompilerParams(dimension_semantics=semantics, vmem_limit_bytes=VMEM_LIMIT_BYTES)


def _dot(a, b):
    return jnp.dot(a, b, preferred_element_type=F32)


def _dot_nt(a, b):
    return lax.dot_general(a, b, (((1,), (1,)), ((), ())), preferred_element_type=F32)


def _dot_tn(a, b):
    return lax.dot_general(a, b, (((0,), (0,)), ((), ())), preferred_element_type=F32)


def _silu(x):
    return x * jax.nn.sigmoid(x)


def _rms(x):
    return x * lax.rsqrt(jnp.mean(x * x, axis=-1, keepdims=True) + EPS)


def _ada_body(c_ref, w_ref, b_ref, wu_ref, o_ref, wub_ref):
    s = _silu(c_ref[...]).astype(BF16)
    o_ref[...] = _dot(s, w_ref[...].astype(BF16)) + b_ref[...]
    wub_ref[...] = wu_ref[...].astype(wub_ref.dtype)


def _ada(c_all, w_ada, b_ada, w_up, tn=FF_COL_TILE):
    r, d = c_all.shape
    n = w_ada.shape[1]
    ncast = w_up.shape[1] // tn
    assert n // tn >= ncast
    cast_spec = pl.BlockSpec((w_up.shape[0], tn), lambda j: (0, jnp.minimum(j, ncast - 1)))
    return pl.pallas_call(
        _ada_body,
        grid=(n // tn,),
        in_specs=[pl.BlockSpec((r, d), lambda j: (0, 0)),
                  pl.BlockSpec((d, tn), lambda j: (0, j)),
                  pl.BlockSpec((1, tn), lambda j: (0, j)),
                  cast_spec],
        out_specs=[pl.BlockSpec((r, tn), lambda j: (0, j)), cast_spec],
        out_shape=[jax.ShapeDtypeStruct((r, n), F32), jax.ShapeDtypeStruct(w_up.shape, BF16)],
        compiler_params=_params(("arbitrary",)),
        name="ada_mod",
    )(c_all, w_ada, b_ada.reshape(1, n), w_up)


def _inproj_body(x_ref, sc_ref, sh_ref, g_ref, w_ref, wga_ref, wa2_ref, ba2_ref, cos_ref, sin_ref,
                 qk_ref, rest_ref, la_ref, *more, n_qk, cast_w):
    h_ref = more[-1]
    j = pl.program_id(1)

    @pl.when(j == 0)
    def _():
        gain = g_ref[...] * (1.0 + sc_ref[0])
        hb = (_rms(x_ref[...]) * gain + sh_ref[0]).astype(BF16)
        h_ref[...] = hb
        ga = _dot_nt(hb, wga_ref[...].astype(BF16))
        z = _dot(ga.astype(BF16), wa2_ref[...].astype(BF16)) + ba2_ref[...]
        la_ref[...] = (jnp.minimum(z, 0.0) - jnp.log1p(jnp.exp(-jnp.abs(z)))) * (1.0 / GLA_TAU)

    if cast_w:
        w_tile = w_ref[...].T.astype(BF16)
        more[0][...] = w_tile
        weight = lambda: w_tile
    else:
        weight = lambda: w_ref[...]

    @pl.when(j < n_qk)
    def _():
        acc = _dot(h_ref[...], weight())
        cos, sin = cos_ref[...], sin_ref[...]
        half = RET_DK // 2
        scale = jnp.where(j >= n_qk // 2, RET_DK ** -0.5, 1.0)
        outs = []
        for hh in range(acc.shape[1] // RET_DK):
            x1 = acc[:, hh * RET_DK: hh * RET_DK + half]
            x2 = acc[:, hh * RET_DK + half: (hh + 1) * RET_DK]
            outs += [x1 * cos - x2 * sin, x1 * sin + x2 * cos]
        qk_ref[...] = (jnp.concatenate(outs, axis=-1) * scale).astype(qk_ref.dtype)

    @pl.when(j >= n_qk)
    def _():
        rest_ref[...] = _dot(h_ref[...], weight())


def _inproj(x, sc, sh, g_attn, w, w_gate_t, w_a2, b_a2, cos, sin, *, tm, rows_per_mod, cast_w=False, tn=IN_COL_TILE):
    m, d = x.shape
    nq = w_a2.shape[1]
    n_qk = QK_W // tn
    tab_tiles = cos.shape[0] // tm
    mod_spec = pl.BlockSpec((1, sc.shape[1], d), lambda i, j: ((i * tm) // rows_per_mod, 0, 0))
    tab_spec = pl.BlockSpec((tm, RET_DK // 2), lambda i, j: (i % tab_tiles, 0))
    out_specs = [pl.BlockSpec((tm, tn), lambda i, j: (i, jnp.minimum(j, n_qk - 1))),
                 pl.BlockSpec((tm, tn), lambda i, j: (i, jnp.maximum(j - n_qk, 0))),
                 pl.BlockSpec((tm, nq), lambda i, j: (i, 0))]
    out_shape = [jax.ShapeDtypeStruct((m, QK_W), BF16),
                 jax.ShapeDtypeStruct((m, REST_W), F32),
                 jax.ShapeDtypeStruct((m, nq), F32)]
    if cast_w:
        assert m == tm
        w_spec = pl.BlockSpec((tn, d), lambda i, j: (j, 0))
        out_specs.append(pl.BlockSpec((d, tn), lambda i, j: (0, j)))
        out_shape.append(jax.ShapeDtypeStruct((d, IN_MAIN), BF16))
    else:
        w_spec = pl.BlockSpec((d, tn), lambda i, j: (0, j))
    return pl.pallas_call(
        functools.partial(_inproj_body, n_qk=n_qk, cast_w=cast_w),
        grid=(m // tm, IN_MAIN // tn),
        in_specs=[pl.BlockSpec((tm, d), lambda i, j: (i, 0)),
                  mod_spec, mod_spec,
                  pl.BlockSpec((1, d), lambda i, j: (0, 0)),
                  w_spec,
                  pl.BlockSpec((GLA_RANK, d), lambda i, j: (0, 0)),
                  pl.BlockSpec((GLA_RANK, nq), lambda i, j: (0, 0)),
                  pl.BlockSpec((1, nq), lambda i, j: (0, 0)),
                  tab_spec, tab_spec],
        out_specs=out_specs,
        out_shape=out_shape,
        scratch_shapes=[pltpu.VMEM((tm, d), BF16)],
        compiler_params=_params(("arbitrary", "arbitrary")),
        name="in_proj",
    )(x, sc, sh, g_attn.reshape(1, d), w, w_gate_t, w_a2, b_a2.reshape(1, nq), cos, sin)


def _ret_log_gamma():
    return jnp.log1p(-jnp.exp2(-5.0 - jnp.arange(RET_HEADS, dtype=F32)))


def _ret_decay_matrix(chunk):
    idx = jnp.arange(chunk, dtype=F32)
    rel = idx[:, None] - idx[None, :]
    lg = _ret_log_gamma()
    return jnp.where(rel[None] >= 0, jnp.exp(jnp.maximum(rel, 0.0)[None] * lg[:, None, None]), 0.0)


def _ret_heads(dmat_ref, lg_ref, q_ref, k_ref, v_ref, g_ref, gout_ref, s_ref, new_ref, fresh, valid):
    cl = q_ref.shape[0]
    idx = lax.broadcasted_iota(jnp.int32, (cl, 1), 0).astype(F32)
    for h in range(RET_HEADS):
        sl = slice(h * RET_DV, (h + 1) * RET_DV)
        lg = lg_ref[h][:1, :1]
        q_dec = jnp.exp((idx + 1.0) * lg)
        k_dec = jnp.exp((cl - 1.0 - idx) * lg)
        c_dec = jnp.exp(cl * lg)
        qb, kb = q_ref[:, sl], k_ref[:, sl]
        v = v_ref[:, sl]
        vb = v.astype(BF16)
        s_old = jnp.where(fresh, 0.0, s_ref[h])
        scores = _dot_nt(qb, kb) * dmat_ref[h]
        o = _dot(scores.astype(BF16), vb) + _dot(qb, s_old.astype(BF16)) * q_dec
        s_new = s_old * c_dec + _dot_tn(kb, (v * k_dec).astype(BF16))
        s_ref[h] = jnp.where(valid, s_new, s_old)
        new_ref[:, sl] = (_rms(o) * gout_ref[:, sl] * _silu(g_ref[:, sl])).astype(new_ref.dtype)


def _gla_level_map(c):
    t = np.arange(c)[:, None]
    s = np.arange(c)[None, :]
    x = np.bitwise_xor(t, s)
    lev = np.floor(np.log2(np.maximum(x, 1))).astype(np.int32)
    lev = np.where(t == s, -1, lev)
    lev = np.where(t < s, -2, lev)
    return lev.astype(np.int32)


SCORE_BLK = 128
LOG2E = 1.4426950408889634


def _gla_chunk(q, k, la2, lev, uppers, at_level=None):
    cl = q.shape[0]
    nblk = cl // SCORE_BLK
    blk = lambda x, i: x[i * SCORE_BLK:(i + 1) * SCORE_BLK]
    p = la2
    tot = la2
    diag = [jnp.zeros((SCORE_BLK, SCORE_BLK), F32) for _ in range(nblk)]
    off = {}
    level = 0
    half = 1
    while half < cl:
        if at_level is not None:
            at_level(level)
        upper = uppers[level]
        z = (jnp.where(upper, q, k) * jnp.exp2(jnp.where(upper, p, tot - p))).astype(BF16)
        if half < SCORE_BLK:
            for i in range(nblk):
                diag[i] = jnp.where(lev == level, _dot_nt(blk(z, i), blk(z, i)), diag[i])
        else:
            hb = half // SCORE_BLK
            for i in range(nblk):
                if (i // hb) % 2 == 1:
                    base = (i // (2 * hb)) * 2 * hb
                    for j in range(base, base + hb):
                        off[(i, j)] = _dot_nt(blk(z, i), blk(z, j))
        partner = jnp.where(upper, pltpu.roll(tot, half, 0), pltpu.roll(tot, cl - half, 0))
        p = p + jnp.where(upper, partner, 0.0)
        tot = tot + partner
        half *= 2
        level += 1
    dg = jnp.sum(q * k, axis=-1, keepdims=True)
    rows = []
    for i in range(nblk):
        d_i = jnp.where(lev == -1, blk(dg, i), diag[i])
        rows.append(jnp.concatenate([off[(i, j)] for j in range(i)] + [d_i], axis=-1))
    return rows, p, tot


def _gla_heads(q_ref, k_ref, v_ref, g_ref, la_ref, lev_ref, gout_ref, st_ref, new_ref, fresh, valid, at_slot=None):
    cl = q_ref.shape[0]
    lev = lev_ref[...]
    row = lax.broadcasted_iota(jnp.int32, (cl, GLA_DK), 0)
    uppers = []
    half = 1
    while half < cl:
        uppers.append((row & half) != 0)
        half *= 2
    nlev = len(uppers)
    for h in range(GLA_HEADS):
        ks = slice(h * GLA_DK, (h + 1) * GLA_DK)
        vs = slice(h * GLA_DV, (h + 1) * GLA_DV)
        q = q_ref[:, ks] * (GLA_DK ** -0.5)
        k = k_ref[:, ks]
        hook = None if at_slot is None else (lambda level, h=h: at_slot(h * nlev + level))
        rows, p, tot = _gla_chunk(q, k, la_ref[:, ks] * LOG2E, lev, uppers, hook)
        vb = v_ref[:, vs].astype(BF16)
        st_old = jnp.where(fresh, 0.0, st_ref[h])
        qt = (q * jnp.exp2(p)).astype(BF16)
        intra = jnp.concatenate([_dot(r.astype(BF16), vb[:r.shape[1]]) for r in rows], axis=0)
        o = intra + _dot_nt(qt, st_old.astype(BF16))
        kt = (k * jnp.exp2(tot - p)).astype(BF16)
        st_new = st_old * jnp.exp2(tot[0:1, :]) + _dot_tn(vb, kt)
        st_ref[h] = jnp.where(valid, st_new, st_old)
        new_ref[:, RET_W + h * GLA_DV: RET_W + (h + 1) * GLA_DV] = (
            _rms(o) * gout_ref[:, vs] * _silu(g_ref[:, vs])).astype(new_ref.dtype)


def _mixer_body(dmat_ref, lg_ref, q_ref, k_ref, rv_ref, rg_ref, gq_ref, gk_ref, gv_ref, gr_ref, la_ref, lev_ref,
                gro_ref, ggo_ref, x_ref, gt_ref, sc_ref, sh_ref, gf_ref, w_ref,
                x1_ref, h_ref, sr_out, sg_out, s_ref, st_ref, mixa_ref, mixb_ref, *, nc, n_chunks):
    s = pl.program_id(0)
    valid = s < n_chunks
    c = jnp.minimum(s, n_chunks - 1) % nc
    fresh = c == 0

    @pl.when(s == 0)
    def _():
        mixb_ref[...] = jnp.zeros_like(mixb_ref)

    def step(old_ref, new_ref):
        old = old_ref[...]
        d = x_ref.shape[1]
        nblk = d // PROJ_BLK
        ssq = [jnp.zeros((x_ref.shape[0], 1), F32)]

        def project(blocks):
            for cb in blocks:
                cs = slice(cb * PROJ_BLK, (cb + 1) * PROJ_BLK)
                x1 = x_ref[:, cs] + gt_ref[0][:, cs] * _dot(old, w_ref[:, cs])
                x1_ref[:, cs] = x1
                ssq[0] = ssq[0] + jnp.sum(x1 * x1, axis=-1, keepdims=True)

        nslots = GLA_HEADS * (x_ref.shape[0].bit_length() - 1)
        due = {(cb * nslots) // nblk: cb for cb in range(nblk)}
        _gla_heads(gq_ref, gk_ref, gv_ref, gr_ref, la_ref, lev_ref, ggo_ref, st_ref, new_ref, fresh, valid,
                   at_slot=lambda slot: project([due[slot]]) if slot in due else None)
        _ret_heads(dmat_ref, lg_ref, q_ref, k_ref, rv_ref, rg_ref, gro_ref, s_ref, new_ref, fresh, valid)
        inv = lax.rsqrt(ssq[0] * (1.0 / d) + EPS)
        gain = gf_ref[...] * (1.0 + sc_ref[0])
        for cb in range(nblk):
            cs = slice(cb * PROJ_BLK, (cb + 1) * PROJ_BLK)
            h_ref[:, cs] = (x1_ref[:, cs] * inv * gain[:, cs] + sh_ref[0][:, cs]).astype(h_ref.dtype)

    @pl.when(s % 2 == 0)
    def _():
        step(mixb_ref, mixa_ref)

    @pl.when(s % 2 == 1)
    def _():
        step(mixa_ref, mixb_ref)

    @pl.when(valid & (c == nc - 1))
    def _():
        sr_out[0] = s_ref[...]
        for h in range(GLA_HEADS):
            sg_out[0, h] = st_ref[h].T


def _mixer_prompt(qk, rest, la, x, gt, sc, sh, g_ret_out, g_gla_out, g_ffn, w_out_b, *, batch, seq, chunk):
    m, d = x.shape
    nc = seq // chunk
    n_chunks = batch * nc
    lg = jnp.broadcast_to(_ret_log_gamma()[:, None, None], (RET_HEADS, SUBLANES, LANES))
    lev = jnp.asarray(_gla_level_map(SCORE_BLK))
    gqk_w = GLA_HEADS * GLA_DK
    cur = lambda s: jnp.minimum(s, n_chunks - 1)
    prv = lambda s: jnp.maximum(s - 1, 0)
    rowc = lambda w, blk: pl.BlockSpec((chunk, w), lambda s: (cur(s), blk))
    const = lambda shape: pl.BlockSpec(shape, lambda s: (0,) * len(shape))
    mod_spec = pl.BlockSpec((1, 1, d), lambda s: (prv(s) // nc, 0, 0))
    return pl.pallas_call(
        functools.partial(_mixer_body, nc=nc, n_chunks=n_chunks),
        grid=(n_chunks + 1,),
        in_specs=[const((RET_HEADS, chunk, chunk)), const((RET_HEADS, SUBLANES, LANES)),
                  rowc(RET_W, 0), rowc(RET_W, 1),
                  rowc(RET_W, R_RV // RET_W), rowc(RET_W, R_RG // RET_W),
                  rowc(gqk_w, R_GQ // gqk_w), rowc(gqk_w, R_GK // gqk_w),
                  rowc(GLA_W, R_GV // GLA_W), rowc(GLA_W, R_GR // GLA_W),
                  rowc(gqk_w, 0),
                  const((SCORE_BLK, SCORE_BLK)), const((1, RET_W)), const((1, GLA_W)),
                  pl.BlockSpec((chunk, d), lambda s: (prv(s), 0)),
                  mod_spec, mod_spec, mod_spec,
                  const((1, d)),
                  pl.BlockSpec((RET_W + GLA_W, d), lambda s: (0, 0), pipeline_mode=pl.Buffered(1))],
        out_specs=[pl.BlockSpec((chunk, d), lambda s: (prv(s), 0)),
                   pl.BlockSpec((chunk, d), lambda s: (prv(s), 0)),
                   pl.BlockSpec((1, RET_HEADS, RET_DK, RET_DV), lambda s: (cur(s) // nc, 0, 0, 0)),
                   pl.BlockSpec((1, GLA_HEADS, GLA_DK, GLA_DV), lambda s: (cur(s) // nc, 0, 0, 0))],
        out_shape=[jax.ShapeDtypeStruct((m, d), F32),
                   jax.ShapeDtypeStruct((m, d), BF16),
                   jax.ShapeDtypeStruct((batch, RET_HEADS, RET_DK, RET_DV), F32),
                   jax.ShapeDtypeStruct((batch, GLA_HEADS, GLA_DK, GLA_DV), F32)],
        scratch_shapes=[pltpu.VMEM((RET_HEADS, RET_DK, RET_DV), F32),
                        pltpu.VMEM((GLA_HEADS, GLA_DV, GLA_DK), F32),
                        pltpu.VMEM((chunk, RET_W + GLA_W), BF16),
                        pltpu.VMEM((chunk, RET_W + GLA_W), BF16)],
        compiler_params=_params(("arbitrary",)),
        name="mixer_prompt",
    )(_ret_decay_matrix(chunk), lg, qk, qk, rest, rest, rest, rest, rest, rest, la, lev,
      g_ret_out.reshape(1, RET_W), g_gla_out.reshape(1, GLA_W), x, gt, sc, sh, g_ffn.reshape(1, d), w_out_b)


def _columns(rows):
    n = rows[0].shape[1]
    pad = (-len(rows)) % 8
    stack = jnp.concatenate(rows + [jnp.zeros((pad, n), F32)] if pad else rows, axis=0)
    return stack.T


def _state_body(lg_ref, qk_ref, rest_ref, la_ref, sr_ref, sg_ref, gro_ref, ggo_ref, o_ref, sr_out, sg_out):
    nb = qk_ref.shape[0]
    for i in range(nb):
        qkrow = qk_ref[i]
        rrow = rest_ref[i]
        larow = la_ref[i]
        rcols = _columns([qkrow[:, j * RET_DK:(j + 1) * RET_DK] for j in range(2 * RET_HEADS)])
        gq = [rrow[:, R_GQ + h * GLA_DK: R_GQ + (h + 1) * GLA_DK] * (GLA_DK ** -0.5) for h in range(GLA_HEADS)]
        gk = [rrow[:, R_GK + h * GLA_DK: R_GK + (h + 1) * GLA_DK] for h in range(GLA_HEADS)]
        ga = [jnp.exp(larow[:, h * GLA_DK:(h + 1) * GLA_DK]) for h in range(GLA_HEADS)]
        gcols = _columns(gq + gk + ga)
        outs = []
        for h in range(RET_HEADS):
            v = rrow[:, R_RV + h * RET_DV: R_RV + (h + 1) * RET_DV]
            gate = rrow[:, R_RG + h * RET_DV: R_RG + (h + 1) * RET_DV]
            gamma = jnp.exp(lg_ref[h][:1, :1])
            qc = rcols[:, h:h + 1]
            kc = rcols[:, RET_HEADS + h:RET_HEADS + h + 1]
            s_new = sr_ref[0, i, h] * gamma + kc * v
            sr_out[0, i, h] = s_new
            o = jnp.sum(qc * s_new, axis=0, keepdims=True)
            outs.append(_rms(o) * gro_ref[:, h * RET_DV:(h + 1) * RET_DV] * _silu(gate))
        for h in range(GLA_HEADS):
            v = rrow[:, R_GV + h * GLA_DV: R_GV + (h + 1) * GLA_DV]
            gate = rrow[:, R_GR + h * GLA_DV: R_GR + (h + 1) * GLA_DV]
            qc = gcols[:, h:h + 1]
            kc = gcols[:, GLA_HEADS + h:GLA_HEADS + h + 1]
            ac = gcols[:, 2 * GLA_HEADS + h:2 * GLA_HEADS + h + 1]
            s_new = sg_ref[0, i, h] * ac + kc * v
            sg_out[0, i, h] = s_new
            o = jnp.sum(qc * s_new, axis=0, keepdims=True)
            outs.append(_rms(o) * ggo_ref[:, h * GLA_DV:(h + 1) * GLA_DV] * _silu(gate))
        o_ref[i] = jnp.concatenate(outs, axis=-1)


def _outproj_body(oa_ref, ob_ref, x_ref, gt_ref, sc_ref, sh_ref, g_ref, w_ref, x1_ref, h_ref):
    ka = oa_ref.shape[1]
    mix = _dot(oa_ref[...].astype(BF16), w_ref[:ka, :]) + _dot(ob_ref[...].astype(BF16), w_ref[ka:, :])
    x1 = x_ref[...] + gt_ref[0] * mix
    x1_ref[...] = x1
    h = _rms(x1) * g_ref[...]
    h_ref[...] = (h * (1.0 + sc_ref[0]) + sh_ref[0]).astype(BF16)


def _outproj(o_a, o_b, x, gt, sc, sh, g_ffn, w_out_bf16, *, tm, rows_per_mod):
    m, d = x.shape
    ka, kb = o_a.shape[1], o_b.shape[1]
    mod_spec = pl.BlockSpec((1, gt.shape[1], d), lambda i: ((i * tm) // rows_per_mod, 0, 0))
    return pl.pallas_call(
        _outproj_body,
        grid=(m // tm,),
        in_specs=[pl.BlockSpec((tm, ka), lambda i: (i, 0)),
                  pl.BlockSpec((tm, kb), lambda i: (i, 0)),
                  pl.BlockSpec((tm, d), lambda i: (i, 0)),
                  mod_spec, mod_spec, mod_spec,
                  pl.BlockSpec((1, d), lambda i: (0, 0)),
                  pl.BlockSpec((ka + kb, d), lambda i: (0, 0), pipeline_mode=pl.Buffered(1))],
        out_specs=[pl.BlockSpec((tm, d), lambda i: (i, 0)),
                   pl.BlockSpec((tm, d), lambda i: (i, 0))],
        out_shape=[jax.ShapeDtypeStruct((m, d), F32),
                   jax.ShapeDtypeStruct((m, d), BF16)],
        compiler_params=_params(("arbitrary",)),
        name="out_proj",
    )(o_a, o_b, x, gt, sc, sh, g_ffn.reshape(1, d), w_out_bf16)


HALO = SUBLANES


def _ffn_up_body(h_ref, wa_ref, wb_ref, cwa_ref, cwb_ref, cba_ref, cbb_ref, g_ref, cs_ref, win_ref, tail_ref, *, tiles_per_seq):
    i, f = pl.program_id(0), pl.program_id(1)
    hb = h_ref[...]
    tm = hb.shape[0]
    first = (i % tiles_per_seq) == 0
    ucs = []
    for part, (w_ref, cw_ref, cb_ref) in enumerate(((wa_ref, cwa_ref, cba_ref), (wb_ref, cwb_ref, cbb_ref))):
        u = _dot(hb, w_ref[...])
        prev = jnp.where(first, 0.0, tail_ref[part, f])
        pieces = []
        for c in range(u.shape[1] // LANES):
            lc = slice(c * LANES, (c + 1) * LANES)
            win_ref[part, c, 0:HALO, :] = prev[:, lc]
            win_ref[part, c, HALO:HALO + tm, :] = u[:, lc]
            cw, cb = cw_ref[:, lc], cb_ref[:, lc]
            pieces.append(cb + cw[0:1] * win_ref[part, c, HALO - 2:HALO - 2 + tm, :]
                          + cw[1:2] * win_ref[part, c, HALO - 1:HALO - 1 + tm, :] + cw[2:3] * u[:, lc])
        ucs.append(jnp.concatenate(pieces, axis=-1))
        tail_ref[part, f] = u[tm - HALO:]
        for r in range(CONV_W - 1):
            cs_ref[0, 0, r, part:part + 1, :] = u[tm - (CONV_W - 1) + r: tm - (CONV_W - 1) + r + 1]
    g_ref[...] = (_silu(ucs[0]) * ucs[1]).astype(g_ref.dtype)


def _ffn_up_state_body(h_ref, wa_ref, wb_ref, cwa_ref, cwb_ref, cba_ref, cbb_ref,
                       lg_ref, qk_ref, rest_ref, la_ref, sr_ref, sg_ref, gro_ref, ggo_ref, wd_ref,
                       g_ref, cs_ref, o_ref, sr_out, sg_out, wdb_ref, win_ref, tail_ref, *, tiles_per_seq):
    _ffn_up_body(h_ref, wa_ref, wb_ref, cwa_ref, cwb_ref, cba_ref, cbb_ref, g_ref, cs_ref, win_ref, tail_ref,
                 tiles_per_seq=tiles_per_seq)
    _state_body(lg_ref, qk_ref, rest_ref, la_ref, sr_ref, sg_ref, gro_ref, ggo_ref, o_ref, sr_out, sg_out)
    wdb_ref[...] = wd_ref[...].astype(wdb_ref.dtype)


def _ffn_up_prompt(h, w_up, conv_w, conv_b, qk_s, rest_s, la_s, state_ret, state_gla, g_ret_out, g_gla_out, w_down,
                   *, batch, seq, tm, tf, nb, cast_rows):
    m, d = h.shape
    ff = w_up.shape[1] // 2
    nf = ff // tf
    cb = conv_b.reshape(1, 2 * ff)
    tps = seq // tm
    ms = qk_s.shape[0]
    nblk = ms // nb
    ncast = ff // cast_rows
    assert (m // tm) * nf >= nblk + ncast
    mix = RET_W + GLA_W
    lg = jnp.broadcast_to(_ret_log_gamma()[:, None, None], (RET_HEADS, SUBLANES, LANES))
    sblk = lambda i, f: jnp.minimum(i * nf + f, nblk - 1)
    ret_spec = pl.BlockSpec((1, nb, RET_HEADS, RET_DK, RET_DV), lambda i, f: (0, sblk(i, f), 0, 0, 0))
    gla_spec = pl.BlockSpec((1, nb, GLA_HEADS, GLA_DK, GLA_DV), lambda i, f: (0, sblk(i, f), 0, 0, 0))
    row_spec = lambda w: pl.BlockSpec((nb, 1, w), lambda i, f: (sblk(i, f), 0, 0))
    cast_spec = pl.BlockSpec((cast_rows, d), lambda i, f: (jnp.clip(i * nf + f - nblk, 0, ncast - 1), 0))
    body = functools.partial(_ffn_up_state_body, tiles_per_seq=tps)
    gate, tails, o_s, s_ret, s_gla, w_down_b = pl.pallas_call(
        body,
        grid=(m // tm, nf),
        in_specs=[pl.BlockSpec((tm, d), lambda i, f: (i, 0)),
                  pl.BlockSpec((d, tf), lambda i, f: (0, f)),
                  pl.BlockSpec((d, tf), lambda i, f: (0, nf + f)),
                  pl.BlockSpec((CONV_W, tf), lambda i, f: (0, f)),
                  pl.BlockSpec((CONV_W, tf), lambda i, f: (0, nf + f)),
                  pl.BlockSpec((1, tf), lambda i, f: (0, f)),
                  pl.BlockSpec((1, tf), lambda i, f: (0, nf + f)),
                  pl.BlockSpec((RET_HEADS, SUBLANES, LANES), lambda i, f: (0, 0, 0)),
                  row_spec(QK_W), row_spec(REST_W), row_spec(la_s.shape[1]),
                  ret_spec, gla_spec,
                  pl.BlockSpec((1, RET_W), lambda i, f: (0, 0)),
                  pl.BlockSpec((1, GLA_W), lambda i, f: (0, 0)),
                  cast_spec],
        out_specs=[pl.BlockSpec((tm, tf), lambda i, f: (i, f)),
                   pl.BlockSpec((1, 1, CONV_W - 1, 2, tf), lambda i, f: (i // tps, i % tps, 0, 0, f)),
                   row_spec(mix), ret_spec, gla_spec, cast_spec],
        out_shape=[jax.ShapeDtypeStruct((m, ff), BF16),
                   jax.ShapeDtypeStruct((batch, tps, CONV_W - 1, 2, ff), F32),
                   jax.ShapeDtypeStruct((ms, 1, mix), F32),
                   jax.ShapeDtypeStruct(state_ret.shape, F32),
                   jax.ShapeDtypeStruct(state_gla.shape, F32),
                   jax.ShapeDtypeStruct(w_down.shape, BF16)],
        scratch_shapes=[pltpu.VMEM((2, tf // LANES, HALO + tm, LANES), F32), pltpu.VMEM((2, nf, HALO, tf), F32)],
        compiler_params=_params(("arbitrary", "arbitrary")),
        name="ffn_up_prompt",
    )(h, w_up, w_up, conv_w, conv_w, cb, cb,
      lg, qk_s.astype(F32).reshape(ms, 1, QK_W), rest_s.reshape(ms, 1, REST_W), la_s.reshape(ms, 1, la_s.shape[1]),
      state_ret, state_gla, g_ret_out.reshape(1, RET_W), g_gla_out.reshape(1, GLA_W), w_down)
    return gate, tails[:, tps - 1], o_s.reshape(ms, mix), s_ret, s_gla, w_down_b


def _ffn_up_step_body(h_ref, w_ref, cw_ref, cb_ref, st_ref, g_ref, cs_ref, uca_ref, *, nf):
    j = pl.program_id(0)
    u = _dot(h_ref[...], w_ref[...])
    s1 = st_ref[0, :, 1, :]
    cw = cw_ref[...]
    uc = cb_ref[...] + cw[0:1] * st_ref[0, :, 0, :] + cw[1:2] * s1 + cw[2:3] * u
    cs_ref[0, :, 0, :] = s1
    cs_ref[0, :, 1, :] = u

    @pl.when(j < nf)
    def _():
        uca_ref[j] = uc

    @pl.when(j >= nf)
    def _():
        g_ref[...] = (_silu(uca_ref[j - nf]) * uc).astype(g_ref.dtype)


def _ffn_up_step(h, w_up, conv_w, conv_b, state_conv, *, tf):
    m, d = h.shape
    ff = w_up.shape[1] // 2
    nf = ff // tf
    st_spec = pl.BlockSpec((1, m, CONV_W - 1, tf), lambda j: (0, 0, 0, j))
    return pl.pallas_call(
        functools.partial(_ffn_up_step_body, nf=nf),
        grid=(2 * nf,),
        in_specs=[pl.BlockSpec((m, d), lambda j: (0, 0)),
                  pl.BlockSpec((d, tf), lambda j: (0, j)),
                  pl.BlockSpec((CONV_W, tf), lambda j: (0, j)),
                  pl.BlockSpec((1, tf), lambda j: (0, j)),
                  st_spec],
        out_specs=[pl.BlockSpec((m, tf), lambda j: (0, jnp.maximum(j - nf, 0))), st_spec],
        out_shape=[jax.ShapeDtypeStruct((m, ff), BF16),
                   jax.ShapeDtypeStruct(state_conv.shape, F32)],
        scratch_shapes=[pltpu.VMEM((nf, m, tf), F32)],
        compiler_params=_params(("arbitrary",)),
        name="ffn_up_step",
    )(h, w_up, conv_w, conv_b.reshape(1, 2 * ff), state_conv)


def _ffn_down_body(g_ref, w_ref, x1_ref, gt_ref, gf_ref, y_ref):
    k = pl.program_id(1)
    last = pl.num_programs(1) - 1
    d = y_ref.shape[1]

    @pl.when(k == 0)
    def _():
        y_ref[...] = _dot(g_ref[...], w_ref[...])

    @pl.when((k > 0) & (k < last))
    def _():
        y_ref[...] += _dot(g_ref[...], w_ref[...])

    @pl.when(k == last)
    def _():
        g = g_ref[...]
        ssq = jnp.zeros((y_ref.shape[0], 1), F32)
        for cb in range(d // PROJ_BLK):
            cs = slice(cb * PROJ_BLK, (cb + 1) * PROJ_BLK)
            x2 = x1_ref[:, cs] + gt_ref[0][:, cs] * (y_ref[:, cs] + _dot(g, w_ref[:, cs]))
            y_ref[:, cs] = x2
            ssq = ssq + jnp.sum(x2 * x2, axis=-1, keepdims=True)
        inv = lax.rsqrt(ssq * (1.0 / d) + EPS)
        for cb in range(d // PROJ_BLK):
            cs = slice(cb * PROJ_BLK, (cb + 1) * PROJ_BLK)
            y_ref[:, cs] = y_ref[:, cs] * inv * gf_ref[:, cs]


def _ffn_down(g, w_down, x1, gt, g_final, *, tm, tk, rows_per_mod):
    m, d = x1.shape
    ff = g.shape[1]
    assert ff // tk >= 2
    return pl.pallas_call(
        _ffn_down_body,
        grid=(m // tm, ff // tk),
        in_specs=[pl.BlockSpec((tm, tk), lambda i, k: (i, k)),
                  pl.BlockSpec((tk, d), lambda i, k: (k, 0)),
                  pl.BlockSpec((tm, d), lambda i, k: (i, 0)),
                  pl.BlockSpec((1, gt.shape[1], d), lambda i, k: ((i * tm) // rows_per_mod, 0, 0)),
                  pl.BlockSpec((1, d), lambda i, k: (0, 0))],
        out_specs=pl.BlockSpec((tm, d), lambda i, k: (i, 0)),
        out_shape=jax.ShapeDtypeStruct((m, d), F32),
        compiler_params=_params(("arbitrary", "arbitrary")),
        name="ffn_down",
    )(g, w_down, x1, gt, g_final.reshape(1, d))


def _rope_tables(pos):
    half = RET_DK // 2
    inv = ROPE_THETA ** (-jnp.arange(half, dtype=F32) / half)
    ang = pos.astype(F32)[:, None] * inv[None, :]
    return jnp.cos(ang), jnp.sin(ang)


def kernel(x_prompt, x_sample, c_prompt, c_sample, state_ret, state_gla, state_conv, w_ada, b_ada, g_attn, w_in, w_a2, b_a2, g_ret_out, g_gla_out, w_out, g_ffn, w_up, conv_w, conv_b, w_down, g_final):
    bp, t_p, d = x_prompt.shape
    bs, t_s, _ = x_sample.shape
    assert t_s == 1 and w_ada.shape[0] == 1
    mp = bp * t_p
    w_ada, b_ada, g_attn, w_in, w_a2, b_a2, g_ret_out, g_gla_out, w_out, g_ffn, w_up, conv_w, conv_b, w_down = (
        a[0] for a in (w_ada, b_ada, g_attn, w_in, w_a2, b_a2, g_ret_out, g_gla_out, w_out, g_ffn, w_up, conv_w, conv_b, w_down))

    mod, w_up_b = _ada(jnp.concatenate([c_prompt, c_sample], axis=0), w_ada, b_ada, w_up)
    sh1p, sc1p, gt1p, sh2p, sc2p, gt2p = (mod[:bp, i * d:(i + 1) * d].reshape(bp, 1, d) for i in range(6))
    sh1s, sc1s, gt1s, sh2s, sc2s, gt2s = (mod[bp:, i * d:(i + 1) * d].reshape(1, bs, d) for i in range(6))

    cos_p, sin_p = _rope_tables(jnp.arange(t_p, dtype=jnp.int32))
    cos_s, sin_s = (jnp.broadcast_to(t, (bs, RET_DK // 2)) for t in _rope_tables(PAST_LEN + jnp.arange(t_s, dtype=jnp.int32)))
    w_out_b = w_out.astype(BF16)
    w_in_t = w_in.T
    w_gate_t = w_in_t[IN_MAIN:]

    xp = x_prompt.reshape(mp, d)
    xs = x_sample.reshape(bs, d)
    qk_s, rest_s, la_s, w_in_b = _inproj(xs, sc1s, sh1s, g_attn, w_in_t, w_gate_t, w_a2, b_a2, cos_s, sin_s,
                                         tm=bs, rows_per_mod=bs, cast_w=True)
    qk_p, rest_p, la_p = _inproj(xp, sc1p, sh1p, g_attn, w_in_b, w_gate_t, w_a2, b_a2, cos_p, sin_p,
                                 tm=ROW_TILE, rows_per_mod=t_p)
    x1_p, h2_p, s_ret_p, s_gla_p = _mixer_prompt(qk_p, rest_p, la_p, xp, gt1p, sc2p, sh2p, g_ret_out, g_gla_out, g_ffn,
                                                 w_out_b, batch=bp, seq=t_p, chunk=CHUNK)
    g_p, cs_p, o_s, s_ret_s, s_gla_s, w_down_b = _ffn_up_prompt(
        h2_p, w_up_b, conv_w, conv_b, qk_s, rest_s, la_s, state_ret, state_gla, g_ret_out, g_gla_out, w_down,
        batch=bp, seq=t_p, tm=ROW_TILE, tf=FF_COL_TILE, nb=STATE_SEQS, cast_rows=CAST_ROWS)
    y_p = _ffn_down(g_p, w_down_b, x1_p, gt2p, g_final, tm=ROW_TILE, tk=DOWN_K_TILE, rows_per_mod=t_p)

    x1_s, h2_s = _outproj(o_s[:, :RET_W], o_s[:, RET_W:], xs, gt1s, sc2s, sh2s, g_ffn, w_out_b, tm=bs, rows_per_mod=bs)
    ff = w_down.shape[0]
    g_s, cs_s = _ffn_up_step(h2_s, w_up_b, conv_w, conv_b, state_conv, tf=ff // 4)
    y_s = _ffn_down(g_s, w_down_b, x1_s, gt2s, g_final, tm=bs, tk=ff // 2, rows_per_mod=bs)

    return (y_p.reshape(bp, t_p, d), y_s.reshape(bs, t_s, d),
            s_ret_p[None], s_ret_s, s_gla_p[None], s_gla_s,
            cs_p.reshape(1, bp, CONV_W - 1, -1), cs_s)
```

```python
import functools

import numpy as np
import jax
import jax.numpy as jnp
from jax import lax
from jax.experimental import pallas as pl
from jax.experimental.pallas import tpu as pltpu

F32 = jnp.float32
BF16 = jnp.bfloat16

RET_HEADS = 4
RET_DK = 256
RET_DV = 256
GLA_HEADS = 4
GLA_DK = 128
GLA_DV = 256
GLA_RANK = 16
GLA_TAU = 16.0
ROPE_THETA = 10000.0
PAST_LEN = 16384
CONV_W = 3
EPS = 1e-6

RET_W = RET_HEADS * RET_DV
GLA_W = GLA_HEADS * GLA_DV
QK_W = 2 * RET_HEADS * RET_DK
R_RV = 0
R_RG = R_RV + RET_W
R_GQ = R_RG + RET_W
R_GK = R_GQ + GLA_HEADS * GLA_DK
R_GV = R_GK + GLA_HEADS * GLA_DK
R_GR = R_GV + GLA_W
REST_W = R_GR + GLA_W
IN_MAIN = QK_W + REST_W

V7X_VMEM_BYTES = 64 * 1024 * 1024
V7X_MXU_WIDTH = 256
LANES = 128
SUBLANES = 8
VMEM_LIMIT_BYTES = V7X_VMEM_BYTES - 8 * 1024 * 1024
PROJ_BLK = V7X_MXU_WIDTH

ROW_TILE = 1024
IN_COL_TILE = 1024
FF_COL_TILE = 512
DOWN_K_TILE = 512
CHUNK = 256
STATE_SEQS = 2
CAST_ROWS = 256
CAST_COLS = 256


def _params(semantics):
    return pltpu.CompilerParams(dimension_semantics=semantics, vmem_limit_bytes=VMEM_LIMIT_BYTES)


def _dot(a, b):
    return jnp.dot(a, b, preferred_element_type=F32)


def _dot_nt(a, b):
    return lax.dot_general(a, b, (((1,), (1,)), ((), ())), preferred_element_type=F32)


def _dot_tn(a, b):
    return lax.dot_general(a, b, (((0,), (0,)), ((), ())), preferred_element_type=F32)


def _silu(x):
    return x * jax.nn.sigmoid(x)


def _rms(x):
    return x * lax.rsqrt(jnp.mean(x * x, axis=-1, keepdims=True) + EPS)


def _ada_body(c_ref, w_ref, b_ref, o_ref):
    s = _silu(c_ref[...]).astype(BF16)
    o_ref[...] = _dot(s, w_ref[...].astype(BF16)) + b_ref[...]


def _ada(c_all, w_ada, b_ada, tn=FF_COL_TILE):
    r, d = c_all.shape
    n = w_ada.shape[1]
    return pl.pallas_call(
        _ada_body,
        grid=(n // tn,),
        in_specs=[pl.BlockSpec((r, d), lambda j: (0, 0)),
                  pl.BlockSpec((d, tn), lambda j: (0, j)),
                  pl.BlockSpec((1, tn), lambda j: (0, j))],
        out_specs=pl.BlockSpec((r, tn), lambda j: (0, j)),
        out_shape=jax.ShapeDtypeStruct((r, n), F32),
        compiler_params=_params(("arbitrary",)),
        name="ada_mod",
    )(c_all, w_ada, b_ada.reshape(1, n))


def _inproj_body(*refs, n_qk, cast_w, rider):
    refs = list(refs)
    x_ref, sc_ref, sh_ref, g_ref, w_ref, wga_ref, wa2_ref, ba2_ref, cos_ref, sin_ref = refs[:10]
    del refs[:10]
    src_ref = refs.pop(0) if rider else None
    qk_ref, rest_ref, la_ref = refs[:3]
    del refs[:3]
    wcast_ref = refs.pop(0) if cast_w else None
    dst_ref = refs.pop(0) if rider else None
    (h_ref,) = refs
    j = pl.program_id(1)

    def ride():
        if rider:
            dst_ref[...] = src_ref[...].astype(dst_ref.dtype)

    @pl.when(j == 0)
    def _():
        gain = g_ref[...] * (1.0 + sc_ref[0])
        hb = (_rms(x_ref[...]) * gain + sh_ref[0]).astype(BF16)
        h_ref[...] = hb
        ga = _dot_nt(hb, wga_ref[...].astype(BF16))
        z = _dot(ga.astype(BF16), wa2_ref[...].astype(BF16)) + ba2_ref[...]
        la_ref[...] = (jnp.minimum(z, 0.0) - jnp.log1p(jnp.exp(-jnp.abs(z)))) * (1.0 / GLA_TAU)

    if cast_w:
        w_tile = w_ref[...].T.astype(BF16)
        wcast_ref[...] = w_tile
        weight = lambda: w_tile
    else:
        weight = lambda: w_ref[...]

    @pl.when(j < n_qk)
    def _():
        ride()
        acc = _dot(h_ref[...], weight())
        cos, sin = cos_ref[...], sin_ref[...]
        half = RET_DK // 2
        scale = jnp.where(j >= n_qk // 2, RET_DK ** -0.5, 1.0)
        outs = []
        for hh in range(acc.shape[1] // RET_DK):
            x1 = acc[:, hh * RET_DK: hh * RET_DK + half]
            x2 = acc[:, hh * RET_DK + half: (hh + 1) * RET_DK]
            outs += [x1 * cos - x2 * sin, x1 * sin + x2 * cos]
        qk_ref[...] = (jnp.concatenate(outs, axis=-1) * scale).astype(qk_ref.dtype)

    @pl.when(j >= n_qk)
    def _():
        ride()
        rest_ref[...] = _dot(h_ref[...], weight())


def _inproj(x, sc, sh, g_attn, w, w_gate_t, w_a2, b_a2, cos, sin, *, tm, rows_per_mod, cast_w=False, cast_src=None,
            tn=IN_COL_TILE):
    m, d = x.shape
    nq = w_a2.shape[1]
    n_qk = QK_W // tn
    tab_tiles = cos.shape[0] // tm
    n_j = IN_MAIN // tn
    once = pl.Buffered(1)
    mod_spec = pl.BlockSpec((1, sc.shape[1], d), lambda i, j: ((i * tm) // rows_per_mod, 0, 0))
    tab_spec = pl.BlockSpec((tm, RET_DK // 2), lambda i, j: (i % tab_tiles, 0), pipeline_mode=once)
    out_specs = [pl.BlockSpec((tm, tn), lambda i, j: (i, jnp.minimum(j, n_qk - 1))),
                 pl.BlockSpec((tm, tn), lambda i, j: (i, jnp.maximum(j - n_qk, 0))),
                 pl.BlockSpec((tm, nq), lambda i, j: (i, 0), pipeline_mode=once)]
    out_shape = [jax.ShapeDtypeStruct((m, QK_W), BF16),
                 jax.ShapeDtypeStruct((m, REST_W), F32),
                 jax.ShapeDtypeStruct((m, nq), F32)]
    if cast_w:
        assert m == tm
        w_spec = pl.BlockSpec((tn, d), lambda i, j: (j, 0))
        out_specs.append(pl.BlockSpec((d, tn), lambda i, j: (0, j)))
        out_shape.append(jax.ShapeDtypeStruct((d, IN_MAIN), BF16))
    else:
        w_spec = pl.BlockSpec((d, tn), lambda i, j: (0, j))
    in_specs = [pl.BlockSpec((tm, d), lambda i, j: (i, 0)),
                mod_spec, mod_spec,
                pl.BlockSpec((1, d), lambda i, j: (0, 0)),
                w_spec,
                pl.BlockSpec((GLA_RANK, d), lambda i, j: (0, 0)),
                pl.BlockSpec((GLA_RANK, nq), lambda i, j: (0, 0)),
                pl.BlockSpec((1, nq), lambda i, j: (0, 0)),
                tab_spec, tab_spec]
    args = [x, sc, sh, g_attn.reshape(1, d), w, w_gate_t, w_a2, b_a2.reshape(1, nq), cos, sin]
    if cast_src is not None:
        ncast = cast_src.shape[1] // CAST_COLS
        assert (m // tm) * n_j >= ncast
        ride_spec = pl.BlockSpec((cast_src.shape[0], CAST_COLS), lambda i, j: (0, jnp.minimum(i * n_j + j, ncast - 1)))
        in_specs.append(ride_spec)
        args.append(cast_src)
        out_specs.append(ride_spec)
        out_shape.append(jax.ShapeDtypeStruct(cast_src.shape, BF16))
    return pl.pallas_call(
        functools.partial(_inproj_body, n_qk=n_qk, cast_w=cast_w, rider=cast_src is not None),
        grid=(m // tm, n_j),
        in_specs=in_specs,
        out_specs=out_specs,
        out_shape=out_shape,
        scratch_shapes=[pltpu.VMEM((tm, d), BF16)],
        compiler_params=_params(("arbitrary", "arbitrary")),
        name="in_proj",
    )(*args)


def _ret_log_gamma():
    return jnp.log1p(-jnp.exp2(-5.0 - jnp.arange(RET_HEADS, dtype=F32)))


def _ret_decay_matrix(chunk):
    idx = jnp.arange(chunk, dtype=F32)
    rel = idx[:, None] - idx[None, :]
    lg = _ret_log_gamma()
    return jnp.where(rel[None] >= 0, jnp.exp(jnp.maximum(rel, 0.0)[None] * lg[:, None, None]), 0.0)


def _ret_heads(dmat_ref, lg_ref, q_ref, k_ref, v_ref, g_ref, gout_ref, s_ref, new_ref, fresh, valid):
    cl = q_ref.shape[0]
    idx = lax.broadcasted_iota(jnp.int32, (cl, 1), 0).astype(F32)
    for h in range(RET_HEADS):
        sl = slice(h * RET_DV, (h + 1) * RET_DV)
        lg = lg_ref[h][:1, :1]
        q_dec = jnp.exp((idx + 1.0) * lg)
        k_dec = jnp.exp((cl - 1.0 - idx) * lg)
        c_dec = jnp.exp(cl * lg)
        qb, kb = q_ref[:, sl], k_ref[:, sl]
        v = v_ref[:, sl]
        vb = v.astype(BF16)
        s_old = jnp.where(fresh, 0.0, s_ref[h])
        scores = _dot_nt(qb, kb) * dmat_ref[h]
        o = _dot(scores.astype(BF16), vb) + _dot(qb, s_old.astype(BF16)) * q_dec
        s_new = s_old * c_dec + _dot_tn(kb, (v * k_dec).astype(BF16))
        s_ref[h] = jnp.where(valid, s_new, s_old)
        new_ref[:, sl] = (_rms(o) * gout_ref[:, sl] * _silu(g_ref[:, sl])).astype(new_ref.dtype)


def _gla_level_map(c):
    t = np.arange(c)[:, None]
    s = np.arange(c)[None, :]
    x = np.bitwise_xor(t, s)
    lev = np.floor(np.log2(np.maximum(x, 1))).astype(np.int32)
    lev = np.where(t == s, -1, lev)
    lev = np.where(t < s, -2, lev)
    return lev.astype(np.int32)


SCORE_BLK = 128
LOG2E = 1.4426950408889634


def _gla_chunk(q, k, la2, lev, uppers, at_level=None):
    cl = q.shape[0]
    nblk = cl // SCORE_BLK
    blk = lambda x, i: x[i * SCORE_BLK:(i + 1) * SCORE_BLK]
    p = la2
    tot = la2
    diag = [jnp.zeros((SCORE_BLK, SCORE_BLK), F32) for _ in range(nblk)]
    off = {}
    level = 0
    half = 1
    while half < cl:
        if at_level is not None:
            at_level(level)
        upper = uppers[level]
        z = (jnp.where(upper, q, k) * jnp.exp2(jnp.where(upper, p, tot - p))).astype(BF16)
        if half < SCORE_BLK:
            for i in range(nblk):
                diag[i] = jnp.where(lev == level, _dot_nt(blk(z, i), blk(z, i)), diag[i])
        else:
            hb = half // SCORE_BLK
            for i in range(nblk):
                if (i // hb) % 2 == 1:
                    base = (i // (2 * hb)) * 2 * hb
                    for j in range(base, base + hb):
                        off[(i, j)] = _dot_nt(blk(z, i), blk(z, j))
        partner = jnp.where(upper, pltpu.roll(tot, half, 0), pltpu.roll(tot, cl - half, 0))
        p = p + jnp.where(upper, partner, 0.0)
        tot = tot + partner
        half *= 2
        level += 1
    dg = jnp.sum(q * k, axis=-1, keepdims=True)
    rows = []
    for i in range(nblk):
        d_i = jnp.where(lev == -1, blk(dg, i), diag[i])
        rows.append(jnp.concatenate([off[(i, j)] for j in range(i)] + [d_i], axis=-1))
    return rows, p, tot


def _gla_heads(q_ref, k_ref, v_ref, g_ref, la_ref, lev_ref, gout_ref, st_ref, new_ref, fresh, valid, at_slot=None):
    cl = q_ref.shape[0]
    lev = lev_ref[...]
    row = lax.broadcasted_iota(jnp.int32, (cl, GLA_DK), 0)
    uppers = []
    half = 1
    while half < cl:
        uppers.append((row & half) != 0)
        half *= 2
    nlev = len(uppers)
    for h in range(GLA_HEADS):
        ks = slice(h * GLA_DK, (h + 1) * GLA_DK)
        vs = slice(h * GLA_DV, (h + 1) * GLA_DV)
        q = q_ref[:, ks] * (GLA_DK ** -0.5)
        k = k_ref[:, ks]
        hook = None if at_slot is None else (lambda level, h=h: at_slot(h * nlev + level))
        rows, p, tot = _gla_chunk(q, k, la_ref[:, ks] * LOG2E, lev, uppers, hook)
        vb = v_ref[:, vs].astype(BF16)
        st_old = jnp.where(fresh, 0.0, st_ref[h])
        qt = (q * jnp.exp2(p)).astype(BF16)
        intra = jnp.concatenate([_dot(r.astype(BF16), vb[:r.shape[1]]) for r in rows], axis=0)
        o = intra + _dot_nt(qt, st_old.astype(BF16))
        kt = (k * jnp.exp2(tot - p)).astype(BF16)
        st_new = st_old * jnp.exp2(tot[0:1, :]) + _dot_tn(vb, kt)
        st_ref[h] = jnp.where(valid, st_new, st_old)
        new_ref[:, RET_W + h * GLA_DV: RET_W + (h + 1) * GLA_DV] = (
            _rms(o) * gout_ref[:, vs] * _silu(g_ref[:, vs])).astype(new_ref.dtype)


def _mixer_body(dmat_ref, lg_ref, q_ref, k_ref, rv_ref, rg_ref, gq_ref, gk_ref, gv_ref, gr_ref, la_ref, lev_ref,
                gro_ref, ggo_ref, x_ref, gt_ref, sc_ref, sh_ref, gf_ref, w_ref,
                x1_ref, h_ref, sr_out, sg_out, s_ref, st_ref, mixa_ref, mixb_ref, *, nc, n_chunks):
    s = pl.program_id(0)
    valid = s < n_chunks
    c = jnp.minimum(s, n_chunks - 1) % nc
    fresh = c == 0

    @pl.when(s == 0)
    def _():
        mixb_ref[...] = jnp.zeros_like(mixb_ref)

    def step(old_ref, new_ref):
        old = old_ref[...]
        d = x_ref.shape[1]
        nblk = d // PROJ_BLK
        ssq = [jnp.zeros((x_ref.shape[0], 1), F32)]

        def project(blocks):
            for cb in blocks:
                cs = slice(cb * PROJ_BLK, (cb + 1) * PROJ_BLK)
                x1 = x_ref[:, cs] + gt_ref[0][:, cs] * _dot(old, w_ref[:, cs])
                x1_ref[:, cs] = x1
                ssq[0] = ssq[0] + jnp.sum(x1 * x1, axis=-1, keepdims=True)

        nslots = GLA_HEADS * (x_ref.shape[0].bit_length() - 1)
        due = {(cb * nslots) // nblk: cb for cb in range(nblk)}
        _gla_heads(gq_ref, gk_ref, gv_ref, gr_ref, la_ref, lev_ref, ggo_ref, st_ref, new_ref, fresh, valid,
                   at_slot=lambda slot: project([due[slot]]) if slot in due else None)
        _ret_heads(dmat_ref, lg_ref, q_ref, k_ref, rv_ref, rg_ref, gro_ref, s_ref, new_ref, fresh, valid)
        inv = lax.rsqrt(ssq[0] * (1.0 / d) + EPS)
        gain = gf_ref[...] * (1.0 + sc_ref[0])
        for cb in range(nblk):
            cs = slice(cb * PROJ_BLK, (cb + 1) * PROJ_BLK)
            h_ref[:, cs] = (x1_ref[:, cs] * inv * gain[:, cs] + sh_ref[0][:, cs]).astype(h_ref.dtype)

    @pl.when(s % 2 == 0)
    def _():
        step(mixb_ref, mixa_ref)

    @pl.when(s % 2 == 1)
    def _():
        step(mixa_ref, mixb_ref)

    @pl.when(valid & (c == nc - 1))
    def _():
        sr_out[0] = s_ref[...]
        for h in range(GLA_HEADS):
            sg_out[0, h] = st_ref[h].T


def _mixer_prompt(qk, rest, la, x, gt, sc, sh, g_ret_out, g_gla_out, g_ffn, w_out_b, *, batch, seq, chunk):
    m, d = x.shape
    nc = seq // chunk
    n_chunks = batch * nc
    lg = jnp.broadcast_to(_ret_log_gamma()[:, None, None], (RET_HEADS, SUBLANES, LANES))
    lev = jnp.asarray(_gla_level_map(SCORE_BLK))
    gqk_w = GLA_HEADS * GLA_DK
    cur = lambda s: jnp.minimum(s, n_chunks - 1)
    prv = lambda s: jnp.maximum(s - 1, 0)
    rowc = lambda w, blk: pl.BlockSpec((chunk, w), lambda s: (cur(s), blk))
    const = lambda shape: pl.BlockSpec(shape, lambda s: (0,) * len(shape))
    mod_spec = pl.BlockSpec((1, 1, d), lambda s: (prv(s) // nc, 0, 0))
    return pl.pallas_call(
        functools.partial(_mixer_body, nc=nc, n_chunks=n_chunks),
        grid=(n_chunks + 1,),
        in_specs=[const((RET_HEADS, chunk, chunk)), const((RET_HEADS, SUBLANES, LANES)),
                  rowc(RET_W, 0), rowc(RET_W, 1),
                  rowc(RET_W, R_RV // RET_W), rowc(RET_W, R_RG // RET_W),
                  rowc(gqk_w, R_GQ // gqk_w), rowc(gqk_w, R_GK // gqk_w),
                  rowc(GLA_W, R_GV // GLA_W), rowc(GLA_W, R_GR // GLA_W),
                  rowc(gqk_w, 0),
                  const((SCORE_BLK, SCORE_BLK)), const((1, RET_W)), const((1, GLA_W)),
                  pl.BlockSpec((chunk, d), lambda s: (prv(s), 0)),
                  mod_spec, mod_spec, mod_spec,
                  const((1, d)),
                  pl.BlockSpec((RET_W + GLA_W, d), lambda s: (0, 0), pipeline_mode=pl.Buffered(1))],
        out_specs=[pl.BlockSpec((chunk, d), lambda s: (prv(s), 0)),
                   pl.BlockSpec((chunk, d), lambda s: (prv(s), 0)),
                   pl.BlockSpec((1, RET_HEADS, RET_DK, RET_DV), lambda s: (cur(s) // nc, 0, 0, 0)),
                   pl.BlockSpec((1, GLA_HEADS, GLA_DK, GLA_DV), lambda s: (cur(s) // nc, 0, 0, 0))],
        out_shape=[jax.ShapeDtypeStruct((m, d), F32),
                   jax.ShapeDtypeStruct((m, d), BF16),
                   jax.ShapeDtypeStruct((batch, RET_HEADS, RET_DK, RET_DV), F32),
                   jax.ShapeDtypeStruct((batch, GLA_HEADS, GLA_DK, GLA_DV), F32)],
        scratch_shapes=[pltpu.VMEM((RET_HEADS, RET_DK, RET_DV), F32),
                        pltpu.VMEM((GLA_HEADS, GLA_DV, GLA_DK), F32),
                        pltpu.VMEM((chunk, RET_W + GLA_W), BF16),
                        pltpu.VMEM((chunk, RET_W + GLA_W), BF16)],
        compiler_params=_params(("arbitrary",)),
        name="mixer_prompt",
    )(_ret_decay_matrix(chunk), lg, qk, qk, rest, rest, rest, rest, rest, rest, la, lev,
      g_ret_out.reshape(1, RET_W), g_gla_out.reshape(1, GLA_W), x, gt, sc, sh, g_ffn.reshape(1, d), w_out_b)


def _columns(rows):
    n = rows[0].shape[1]
    pad = (-len(rows)) % 8
    stack = jnp.concatenate(rows + [jnp.zeros((pad, n), F32)] if pad else rows, axis=0)
    return stack.T


def _state_body(lg_ref, qk_ref, rest_ref, la_ref, sr_ref, sg_ref, gro_ref, ggo_ref, o_ref, sr_out, sg_out):
    nb = qk_ref.shape[0]
    for i in range(nb):
        qkrow = qk_ref[i]
        rrow = rest_ref[i]
        larow = la_ref[i]
        rcols = _columns([qkrow[:, j * RET_DK:(j + 1) * RET_DK] for j in range(2 * RET_HEADS)])
        gq = [rrow[:, R_GQ + h * GLA_DK: R_GQ + (h + 1) * GLA_DK] * (GLA_DK ** -0.5) for h in range(GLA_HEADS)]
        gk = [rrow[:, R_GK + h * GLA_DK: R_GK + (h + 1) * GLA_DK] for h in range(GLA_HEADS)]
        ga = [jnp.exp(larow[:, h * GLA_DK:(h + 1) * GLA_DK]) for h in range(GLA_HEADS)]
        gcols = _columns(gq + gk + ga)
        outs = []
        for h in range(RET_HEADS):
            v = rrow[:, R_RV + h * RET_DV: R_RV + (h + 1) * RET_DV]
            gate = rrow[:, R_RG + h * RET_DV: R_RG + (h + 1) * RET_DV]
            gamma = jnp.exp(lg_ref[h][:1, :1])
            qc = rcols[:, h:h + 1]
            kc = rcols[:, RET_HEADS + h:RET_HEADS + h + 1]
            s_new = sr_ref[0, i, h] * gamma + kc * v
            sr_out[0, i, h] = s_new
            o = jnp.sum(qc * s_new, axis=0, keepdims=True)
            outs.append(_rms(o) * gro_ref[:, h * RET_DV:(h + 1) * RET_DV] * _silu(gate))
        for h in range(GLA_HEADS):
            v = rrow[:, R_GV + h * GLA_DV: R_GV + (h + 1) * GLA_DV]
            gate = rrow[:, R_GR + h * GLA_DV: R_GR + (h + 1) * GLA_DV]
            qc = gcols[:, h:h + 1]
            kc = gcols[:, GLA_HEADS + h:GLA_HEADS + h + 1]
            ac = gcols[:, 2 * GLA_HEADS + h:2 * GLA_HEADS + h + 1]
            s_new = sg_ref[0, i, h] * ac + kc * v
            sg_out[0, i, h] = s_new
            o = jnp.sum(qc * s_new, axis=0, keepdims=True)
            outs.append(_rms(o) * ggo_ref[:, h * GLA_DV:(h + 1) * GLA_DV] * _silu(gate))
        o_ref[i] = jnp.concatenate(outs, axis=-1)


def _outproj_body(oa_ref, ob_ref, x_ref, gt_ref, sc_ref, sh_ref, g_ref, w_ref, x1_ref, h_ref):
    ka = oa_ref.shape[1]
    mix = _dot(oa_ref[...].astype(BF16), w_ref[:ka, :]) + _dot(ob_ref[...].astype(BF16), w_ref[ka:, :])
    x1 = x_ref[...] + gt_ref[0] * mix
    x1_ref[...] = x1
    h = _rms(x1) * g_ref[...]
    h_ref[...] = (h * (1.0 + sc_ref[0]) + sh_ref[0]).astype(BF16)


def _outproj(o_a, o_b, x, gt, sc, sh, g_ffn, w_out_bf16, *, tm, rows_per_mod):
    m, d = x.shape
    ka, kb = o_a.shape[1], o_b.shape[1]
    mod_spec = pl.BlockSpec((1, gt.shape[1], d), lambda i: ((i * tm) // rows_per_mod, 0, 0))
    return pl.pallas_call(
        _outproj_body,
        grid=(m // tm,),
        in_specs=[pl.BlockSpec((tm, ka), lambda i: (i, 0)),
                  pl.BlockSpec((tm, kb), lambda i: (i, 0)),
                  pl.BlockSpec((tm, d), lambda i: (i, 0)),
                  mod_spec, mod_spec, mod_spec,
                  pl.BlockSpec((1, d), lambda i: (0, 0)),
                  pl.BlockSpec((ka + kb, d), lambda i: (0, 0), pipeline_mode=pl.Buffered(1))],
        out_specs=[pl.BlockSpec((tm, d), lambda i: (i, 0)),
                   pl.BlockSpec((tm, d), lambda i: (i, 0))],
        out_shape=[jax.ShapeDtypeStruct((m, d), F32),
                   jax.ShapeDtypeStruct((m, d), BF16)],
        compiler_params=_params(("arbitrary",)),
        name="out_proj",
    )(o_a, o_b, x, gt, sc, sh, g_ffn.reshape(1, d), w_out_bf16)


HALO = SUBLANES


def _ffn_up_body(h_ref, wa_ref, wb_ref, cwa_ref, cwb_ref, cba_ref, cbb_ref, g_ref, cs_ref, win_ref, tail_ref, *, tiles_per_seq):
    i, f = pl.program_id(0), pl.program_id(1)
    hb = h_ref[...]
    tm = hb.shape[0]
    first = (i % tiles_per_seq) == 0
    ucs = []
    for part, (w_ref, cw_ref, cb_ref) in enumerate(((wa_ref, cwa_ref, cba_ref), (wb_ref, cwb_ref, cbb_ref))):
        u = _dot(hb, w_ref[...])
        prev = jnp.where(first, 0.0, tail_ref[part, f])
        pieces = []
        for c in range(u.shape[1] // LANES):
            lc = slice(c * LANES, (c + 1) * LANES)
            win_ref[part, c, 0:HALO, :] = prev[:, lc]
            win_ref[part, c, HALO:HALO + tm, :] = u[:, lc]
            cw, cb = cw_ref[:, lc], cb_ref[:, lc]
            pieces.append(cb + cw[0:1] * win_ref[part, c, HALO - 2:HALO - 2 + tm, :]
                          + cw[1:2] * win_ref[part, c, HALO - 1:HALO - 1 + tm, :] + cw[2:3] * u[:, lc])
        ucs.append(jnp.concatenate(pieces, axis=-1))
        tail_ref[part, f] = u[tm - HALO:]
        for r in range(CONV_W - 1):
            cs_ref[0, 0, r, part:part + 1, :] = u[tm - (CONV_W - 1) + r: tm - (CONV_W - 1) + r + 1]
    g_ref[...] = (_silu(ucs[0]) * ucs[1]).astype(g_ref.dtype)


def _ffn_up_state_body(h_ref, wa_ref, wb_ref, cwa_ref, cwb_ref, cba_ref, cbb_ref,
                       lg_ref, qk_ref, rest_ref, la_ref, sr_ref, sg_ref, gro_ref, ggo_ref, wd_ref,
                       g_ref, cs_ref, o_ref, sr_out, sg_out, wdb_ref, win_ref, tail_ref, *, tiles_per_seq):
    _ffn_up_body(h_ref, wa_ref, wb_ref, cwa_ref, cwb_ref, cba_ref, cbb_ref, g_ref, cs_ref, win_ref, tail_ref,
                 tiles_per_seq=tiles_per_seq)
    _state_body(lg_ref, qk_ref, rest_ref, la_ref, sr_ref, sg_ref, gro_ref, ggo_ref, o_ref, sr_out, sg_out)
    wdb_ref[...] = wd_ref[...].astype(wdb_ref.dtype)


def _ffn_up_prompt(h, w_up, conv_w, conv_b, qk_s, rest_s, la_s, state_ret, state_gla, g_ret_out, g_gla_out, w_down,
                   *, batch, seq, tm, tf, nb, cast_rows):
    m, d = h.shape
    ff = w_up.shape[1] // 2
    nf = ff // tf
    cb = conv_b.reshape(1, 2 * ff)
    tps = seq // tm
    ms = qk_s.shape[0]
    nblk = ms // nb
    ncast = ff // cast_rows
    assert (m // tm) * nf >= nblk + ncast
    mix = RET_W + GLA_W
    lg = jnp.broadcast_to(_ret_log_gamma()[:, None, None], (RET_HEADS, SUBLANES, LANES))
    sblk = lambda i, f: jnp.minimum(i * nf + f, nblk - 1)
    ret_spec = pl.BlockSpec((1, nb, RET_HEADS, RET_DK, RET_DV), lambda i, f: (0, sblk(i, f), 0, 0, 0))
    gla_spec = pl.BlockSpec((1, nb, GLA_HEADS, GLA_DK, GLA_DV), lambda i, f: (0, sblk(i, f), 0, 0, 0))
    row_spec = lambda w: pl.BlockSpec((nb, 1, w), lambda i, f: (sblk(i, f), 0, 0))
    cast_spec = pl.BlockSpec((cast_rows, d), lambda i, f: (jnp.clip(i * nf + f - nblk, 0, ncast - 1), 0))
    body = functools.partial(_ffn_up_state_body, tiles_per_seq=tps)
    gate, tails, o_s, s_ret, s_gla, w_down_b = pl.pallas_call(
        body,
        grid=(m // tm, nf),
        in_specs=[pl.BlockSpec((tm, d), lambda i, f: (i, 0)),
                  pl.BlockSpec((d, tf), lambda i, f: (0, f)),
                  pl.BlockSpec((d, tf), lambda i, f: (0, nf + f)),
                  pl.BlockSpec((CONV_W, tf), lambda i, f: (0, f)),
                  pl.BlockSpec((CONV_W, tf), lambda i, f: (0, nf + f)),
                  pl.BlockSpec((1, tf), lambda i, f: (0, f)),
                  pl.BlockSpec((1, tf), lambda i, f: (0, nf + f)),
                  pl.BlockSpec((RET_HEADS, SUBLANES, LANES), lambda i, f: (0, 0, 0)),
                  row_spec(QK_W), row_spec(REST_W), row_spec(la_s.shape[1]),
                  ret_spec, gla_spec,
                  pl.BlockSpec((1, RET_W), lambda i, f: (0, 0)),
                  pl.BlockSpec((1, GLA_W), lambda i, f: (0, 0)),
                  cast_spec],
        out_specs=[pl.BlockSpec((tm, tf), lambda i, f: (i, f)),
                   pl.BlockSpec((1, 1, CONV_W - 1, 2, tf), lambda i, f: (i // tps, i % tps, 0, 0, f)),
                   row_spec(mix), ret_spec, gla_spec, cast_spec],
        out_shape=[jax.ShapeDtypeStruct((m, ff), BF16),
                   jax.ShapeDtypeStruct((batch, tps, CONV_W - 1, 2, ff), F32),
                   jax.ShapeDtypeStruct((ms, 1, mix), F32),
                   jax.ShapeDtypeStruct(state_ret.shape, F32),
                   jax.ShapeDtypeStruct(state_gla.shape, F32),
                   jax.ShapeDtypeStruct(w_down.shape, BF16)],
        scratch_shapes=[pltpu.VMEM((2, tf // LANES, HALO + tm, LANES), F32), pltpu.VMEM((2, nf, HALO, tf), F32)],
        compiler_params=_params(("arbitrary", "arbitrary")),
        name="ffn_up_prompt",
    )(h, w_up, w_up, conv_w, conv_w, cb, cb,
      lg, qk_s.astype(F32).reshape(ms, 1, QK_W), rest_s.reshape(ms, 1, REST_W), la_s.reshape(ms, 1, la_s.shape[1]),
      state_ret, state_gla, g_ret_out.reshape(1, RET_W), g_gla_out.reshape(1, GLA_W), w_down)
    return gate, tails[:, tps - 1], o_s.reshape(ms, mix), s_ret, s_gla, w_down_b


def _ffn_up_step_body(h_ref, w_ref, cw_ref, cb_ref, st_ref, g_ref, cs_ref, uca_ref, *, nf):
    j = pl.program_id(0)
    u = _dot(h_ref[...], w_ref[...])
    s1 = st_ref[0, :, 1, :]
    cw = cw_ref[...]
    uc = cb_ref[...] + cw[0:1] * st_ref[0, :, 0, :] + cw[1:2] * s1 + cw[2:3] * u
    cs_ref[0, :, 0, :] = s1
    cs_ref[0, :, 1, :] = u

    @pl.when(j < nf)
    def _():
        uca_ref[j] = uc

    @pl.when(j >= nf)
    def _():
        g_ref[...] = (_silu(uca_ref[j - nf]) * uc).astype(g_ref.dtype)


def _ffn_up_step(h, w_up, conv_w, conv_b, state_conv, *, tf):
    m, d = h.shape
    ff = w_up.shape[1] // 2
    nf = ff // tf
    st_spec = pl.BlockSpec((1, m, CONV_W - 1, tf), lambda j: (0, 0, 0, j))
    return pl.pallas_call(
        functools.partial(_ffn_up_step_body, nf=nf),
        grid=(2 * nf,),
        in_specs=[pl.BlockSpec((m, d), lambda j: (0, 0)),
                  pl.BlockSpec((d, tf), lambda j: (0, j)),
                  pl.BlockSpec((CONV_W, tf), lambda j: (0, j)),
                  pl.BlockSpec((1, tf), lambda j: (0, j)),
                  st_spec],
        out_specs=[pl.BlockSpec((m, tf), lambda j: (0, jnp.maximum(j - nf, 0))), st_spec],
        out_shape=[jax.ShapeDtypeStruct((m, ff), BF16),
                   jax.ShapeDtypeStruct(state_conv.shape, F32)],
        scratch_shapes=[pltpu.VMEM((nf, m, tf), F32)],
        compiler_params=_params(("arbitrary",)),
        name="ffn_up_step",
    )(h, w_up, conv_w, conv_b.reshape(1, 2 * ff), state_conv)


def _ffn_down_body(g_ref, w_ref, x1_ref, gt_ref, gf_ref, y_ref):
    k = pl.program_id(1)
    last = pl.num_programs(1) - 1
    d = y_ref.shape[1]

    @pl.when(k == 0)
    def _():
        y_ref[...] = _dot(g_ref[...], w_ref[...])

    @pl.when((k > 0) & (k < last))
    def _():
        y_ref[...] += _dot(g_ref[...], w_ref[...])

    @pl.when(k == last)
    def _():
        g = g_ref[...]
        ssq = jnp.zeros((y_ref.shape[0], 1), F32)
        for cb in range(d // PROJ_BLK):
            cs = slice(cb * PROJ_BLK, (cb + 1) * PROJ_BLK)
            x2 = x1_ref[:, cs] + gt_ref[0][:, cs] * (y_ref[:, cs] + _dot(g, w_ref[:, cs]))
            y_ref[:, cs] = x2
            ssq = ssq + jnp.sum(x2 * x2, axis=-1, keepdims=True)
        inv = lax.rsqrt(ssq * (1.0 / d) + EPS)
        for cb in range(d // PROJ_BLK):
            cs = slice(cb * PROJ_BLK, (cb + 1) * PROJ_BLK)
            y_ref[:, cs] = y_ref[:, cs] * inv * gf_ref[:, cs]


def _ffn_down(g, w_down, x1, gt, g_final, *, tm, tk, rows_per_mod):
    m, d = x1.shape
    ff = g.shape[1]
    assert ff // tk >= 2
    return pl.pallas_call(
        _ffn_down_body,
        grid=(m // tm, ff // tk),
        in_specs=[pl.BlockSpec((tm, tk), lambda i, k: (i, k)),
                  pl.BlockSpec((tk, d), lambda i, k: (k, 0)),
                  pl.BlockSpec((tm, d), lambda i, k: (i, 0)),
                  pl.BlockSpec((1, gt.shape[1], d), lambda i, k: ((i * tm) // rows_per_mod, 0, 0)),
                  pl.BlockSpec((1, d), lambda i, k: (0, 0))],
        out_specs=pl.BlockSpec((tm, d), lambda i, k: (i, 0)),
        out_shape=jax.ShapeDtypeStruct((m, d), F32),
        compiler_params=_params(("arbitrary", "arbitrary")),
        name="ffn_down",
    )(g, w_down, x1, gt, g_final.reshape(1, d))


def _rope_tables(pos):
    half = RET_DK // 2
    inv = ROPE_THETA ** (-jnp.arange(half, dtype=F32) / half)
    ang = pos.astype(F32)[:, None] * inv[None, :]
    return jnp.cos(ang), jnp.sin(ang)


def kernel(x_prompt, x_sample, c_prompt, c_sample, state_ret, state_gla, state_conv, w_ada, b_ada, g_attn, w_in, w_a2, b_a2, g_ret_out, g_gla_out, w_out, g_ffn, w_up, conv_w, conv_b, w_down, g_final):
    bp, t_p, d = x_prompt.shape
    bs, t_s, _ = x_sample.shape
    assert t_s == 1 and w_ada.shape[0] == 1
    mp = bp * t_p
    w_ada, b_ada, g_attn, w_in, w_a2, b_a2, g_ret_out, g_gla_out, w_out, g_ffn, w_up, conv_w, conv_b, w_down = (
        a[0] for a in (w_ada, b_ada, g_attn, w_in, w_a2, b_a2, g_ret_out, g_gla_out, w_out, g_ffn, w_up, conv_w, conv_b, w_down))

    mod = _ada(jnp.concatenate([c_prompt, c_sample], axis=0), w_ada, b_ada)
    sh1p, sc1p, gt1p, sh2p, sc2p, gt2p = (mod[:bp, i * d:(i + 1) * d].reshape(bp, 1, d) for i in range(6))
    sh1s, sc1s, gt1s, sh2s, sc2s, gt2s = (mod[bp:, i * d:(i + 1) * d].reshape(1, bs, d) for i in range(6))

    cos_p, sin_p = _rope_tables(jnp.arange(t_p, dtype=jnp.int32))
    cos_s, sin_s = (jnp.broadcast_to(t, (bs, RET_DK // 2)) for t in _rope_tables(PAST_LEN + jnp.arange(t_s, dtype=jnp.int32)))
    w_out_b = w_out.astype(BF16)
    w_in_t = w_in.T
    w_gate_t = w_in_t[IN_MAIN:]

    xp = x_prompt.reshape(mp, d)
    xs = x_sample.reshape(bs, d)
    qk_s, rest_s, la_s, w_in_b = _inproj(xs, sc1s, sh1s, g_attn, w_in_t, w_gate_t, w_a2, b_a2, cos_s, sin_s,
                                         tm=bs, rows_per_mod=bs, cast_w=True)
    qk_p, rest_p, la_p, w_up_b = _inproj(xp, sc1p, sh1p, g_attn, w_in_b, w_gate_t, w_a2, b_a2, cos_p, sin_p,
                                         tm=ROW_TILE, rows_per_mod=t_p, cast_src=w_up)
    x1_p, h2_p, s_ret_p, s_gla_p = _mixer_prompt(qk_p, rest_p, la_p, xp, gt1p, sc2p, sh2p, g_ret_out, g_gla_out, g_ffn,
                                                 w_out_b, batch=bp, seq=t_p, chunk=CHUNK)
    g_p, cs_p, o_s, s_ret_s, s_gla_s, w_down_b = _ffn_up_prompt(
        h2_p, w_up_b, conv_w, conv_b, qk_s, rest_s, la_s, state_ret, state_gla, g_ret_out, g_gla_out, w_down,
        batch=bp, seq=t_p, tm=ROW_TILE, tf=FF_COL_TILE, nb=STATE_SEQS, cast_rows=CAST_ROWS)
    y_p = _ffn_down(g_p, w_down_b, x1_p, gt2p, g_final, tm=ROW_TILE, tk=DOWN_K_TILE, rows_per_mod=t_p)

    x1_s, h2_s = _outproj(o_s[:, :RET_W], o_s[:, RET_W:], xs, gt1s, sc2s, sh2s, g_ffn, w_out_b, tm=bs, rows_per_mod=bs)
    ff = w_down.shape[0]
    g_s, cs_s = _ffn_up_step(h2_s, w_up_b, conv_w, conv_b, state_conv, tf=ff // 4)
    y_s = _ffn_down(g_s, w_down_b, x1_s, gt2s, g_final, tm=bs, tk=ff // 2, rows_per_mod=bs)

    return (y_p.reshape(bp, t_p, d), y_s.reshape(bs, t_s, d),
            s_ret_p[None], s_ret_s, s_gla_p[None], s_gla_s,
            cs_p.reshape(1, bp, CONV_W - 1, -1), cs_s)
```

```python
import functools

import numpy as np
import jax
import jax.numpy as jnp
from jax import lax
from jax.experimental import pallas as pl
from jax.experimental.pallas import tpu as pltpu

F32 = jnp.float32
BF16 = jnp.bfloat16

RET_HEADS = 4
RET_DK = 256
RET_DV = 256
GLA_HEADS = 4
GLA_DK = 128
GLA_DV = 256
GLA_RANK = 16
GLA_TAU = 16.0
ROPE_THETA = 10000.0
PAST_LEN = 16384
CONV_W = 3
EPS = 1e-6

RET_W = RET_HEADS * RET_DV
GLA_W = GLA_HEADS * GLA_DV
QK_W = 2 * RET_HEADS * RET_DK
R_RV = 0
R_RG = R_RV + RET_W
R_GQ = R_RG + RET_W
R_GK = R_GQ + GLA_HEADS * GLA_DK
R_GV = R_GK + GLA_HEADS * GLA_DK
R_GR = R_GV + GLA_W
REST_W = R_GR + GLA_W
IN_MAIN = QK_W + REST_W
P_RV, P_GV, VAL_W = 0, RET_W, RET_W + GLA_W
P_RG, P_GQ, P_GK, P_GR = 0, RET_W, RET_W + GLA_HEADS * GLA_DK, RET_W + 2 * GLA_HEADS * GLA_DK
F32_W = REST_W - VAL_W


def _is_value_col(c):
    return R_RV <= c < R_RG or R_GV <= c < R_GR

V7X_VMEM_BYTES = 64 * 1024 * 1024
V7X_MXU_WIDTH = 256
LANES = 128
SUBLANES = 8
VMEM_LIMIT_BYTES = V7X_VMEM_BYTES - 8 * 1024 * 1024
PROJ_BLK = V7X_MXU_WIDTH

ROW_TILE = 1024
IN_COL_TILE = 1024
FF_COL_TILE = 512
DOWN_K_TILE = 512
CHUNK = 256
STATE_SEQS = 2
CAST_ROWS = 256


def _params(semantics):
    return pltpu.CompilerParams(dimension_semantics=semantics, vmem_limit_bytes=VMEM_LIMIT_BYTES)


def _dot(a, b):
    return jnp.dot(a, b, preferred_element_type=F32)


def _dot_nt(a, b):
    return lax.dot_general(a, b, (((1,), (1,)), ((), ())), preferred_element_type=F32)


def _dot_tn(a, b):
    return lax.dot_general(a, b, (((0,), (0,)), ((), ())), preferred_element_type=F32)


def _silu(x):
    return x * jax.nn.sigmoid(x)


def _rms(x):
    return x * lax.rsqrt(jnp.mean(x * x, axis=-1, keepdims=True) + EPS)


def _ada_body(c_ref, w_ref, b_ref, wu_ref, o_ref, wub_ref):
    s = _silu(c_ref[...]).astype(BF16)
    o_ref[...] = _dot(s, w_ref[...].astype(BF16)) + b_ref[...]
    wub_ref[...] = wu_ref[...].astype(wub_ref.dtype)


def _ada(c_all, w_ada, b_ada, w_up, tn=FF_COL_TILE):
    r, d = c_all.shape
    n = w_ada.shape[1]
    ncast = w_up.shape[1] // tn
    assert n // tn >= ncast
    cast_spec = pl.BlockSpec((w_up.shape[0], tn), lambda j: (0, jnp.minimum(j, ncast - 1)))
    return pl.pallas_call(
        _ada_body,
        grid=(n // tn,),
        in_specs=[pl.BlockSpec((r, d), lambda j: (0, 0)),
                  pl.BlockSpec((d, tn), lambda j: (0, j)),
                  pl.BlockSpec((1, tn), lambda j: (0, j)),
                  cast_spec],
        out_specs=[pl.BlockSpec((r, tn), lambda j: (0, j)), cast_spec],
        out_shape=[jax.ShapeDtypeStruct((r, n), F32), jax.ShapeDtypeStruct(w_up.shape, BF16)],
        compiler_params=_params(("arbitrary",)),
        name="ada_mod",
    )(c_all, w_ada, b_ada.reshape(1, n), w_up)


def _inproj_body(x_ref, sc_ref, sh_ref, g_ref, w_ref, wga_ref, wa2_ref, ba2_ref, cos_ref, sin_ref,
                 qk_ref, val_ref, rest_ref, la_ref, *more, n_qk, value_tiles, cast_w):
    h_ref = more[-1]
    j = pl.program_id(1)

    @pl.when(j == 0)
    def _():
        gain = g_ref[...] * (1.0 + sc_ref[0])
        hb = (_rms(x_ref[...]) * gain + sh_ref[0]).astype(BF16)
        h_ref[...] = hb
        ga = _dot_nt(hb, wga_ref[...].astype(BF16))
        z = _dot(ga.astype(BF16), wa2_ref[...].astype(BF16)) + ba2_ref[...]
        la_ref[...] = (jnp.minimum(z, 0.0) - jnp.log1p(jnp.exp(-jnp.abs(z)))) * (1.0 / GLA_TAU)

    if cast_w:
        w_tile = w_ref[...].T.astype(BF16)
        more[0][...] = w_tile
        weight = lambda: w_tile
    else:
        weight = lambda: w_ref[...]

    @pl.when(j < n_qk)
    def _():
        acc = _dot(h_ref[...], weight())
        cos, sin = cos_ref[...], sin_ref[...]
        half = RET_DK // 2
        scale = jnp.where(j >= n_qk // 2, RET_DK ** -0.5, 1.0)
        outs = []
        for hh in range(acc.shape[1] // RET_DK):
            x1 = acc[:, hh * RET_DK: hh * RET_DK + half]
            x2 = acc[:, hh * RET_DK + half: (hh + 1) * RET_DK]
            outs += [x1 * cos - x2 * sin, x1 * sin + x2 * cos]
        qk_ref[...] = (jnp.concatenate(outs, axis=-1) * scale).astype(qk_ref.dtype)

    is_value = functools.reduce(jnp.logical_or, [j == t for t in value_tiles])

    @pl.when(is_value)
    def _():
        val_ref[...] = _dot(h_ref[...], weight()).astype(val_ref.dtype)

    @pl.when((j >= n_qk) & jnp.logical_not(is_value))
    def _():
        rest_ref[...] = _dot(h_ref[...], weight())


def _inproj(x, sc, sh, g_attn, w, w_gate_t, w_a2, b_a2, cos, sin, *, tm, rows_per_mod, cast_w=False, tn=IN_COL_TILE):
    m, d = x.shape
    nq = w_a2.shape[1]
    n_qk = QK_W // tn
    tab_tiles = cos.shape[0] // tm
    n_j = IN_MAIN // tn
    value_tiles = [t for t in range(n_qk, n_j) if _is_value_col(t * tn - QK_W)]
    f32_tiles = [t for t in range(n_qk, n_j) if t not in value_tiles]

    def packed(tiles):
        return lambda i, j: (i, jnp.minimum(sum((j > t).astype(jnp.int32) for t in tiles), len(tiles) - 1))

    once = pl.Buffered(1)
    mod_spec = pl.BlockSpec((1, sc.shape[1], d), lambda i, j: ((i * tm) // rows_per_mod, 0, 0))
    tab_spec = pl.BlockSpec((tm, RET_DK // 2), lambda i, j: (i % tab_tiles, 0), pipeline_mode=once)
    out_specs = [pl.BlockSpec((tm, tn), lambda i, j: (i, jnp.minimum(j, n_qk - 1))),
                 pl.BlockSpec((tm, tn), packed(value_tiles)),
                 pl.BlockSpec((tm, tn), packed(f32_tiles)),
                 pl.BlockSpec((tm, nq), lambda i, j: (i, 0), pipeline_mode=once)]
    out_shape = [jax.ShapeDtypeStruct((m, QK_W), BF16),
                 jax.ShapeDtypeStruct((m, VAL_W), BF16),
                 jax.ShapeDtypeStruct((m, F32_W), F32),
                 jax.ShapeDtypeStruct((m, nq), F32)]
    if cast_w:
        assert m == tm
        w_spec = pl.BlockSpec((tn, d), lambda i, j: (j, 0))
        out_specs.append(pl.BlockSpec((d, tn), lambda i, j: (0, j)))
        out_shape.append(jax.ShapeDtypeStruct((d, IN_MAIN), BF16))
    else:
        w_spec = pl.BlockSpec((d, tn), lambda i, j: (0, j))
    return pl.pallas_call(
        functools.partial(_inproj_body, n_qk=n_qk, value_tiles=value_tiles, cast_w=cast_w),
        grid=(m // tm, n_j),
        in_specs=[pl.BlockSpec((tm, d), lambda i, j: (i, 0)),
                  mod_spec, mod_spec,
                  pl.BlockSpec((1, d), lambda i, j: (0, 0)),
                  w_spec,
                  pl.BlockSpec((GLA_RANK, d), lambda i, j: (0, 0)),
                  pl.BlockSpec((GLA_RANK, nq), lambda i, j: (0, 0)),
                  pl.BlockSpec((1, nq), lambda i, j: (0, 0)),
                  tab_spec, tab_spec],
        out_specs=out_specs,
        out_shape=out_shape,
        scratch_shapes=[pltpu.VMEM((tm, d), BF16)],
        compiler_params=_params(("arbitrary", "arbitrary")),
        name="in_proj",
    )(x, sc, sh, g_attn.reshape(1, d), w, w_gate_t, w_a2, b_a2.reshape(1, nq), cos, sin)


def _ret_log_gamma():
    return jnp.log1p(-jnp.exp2(-5.0 - jnp.arange(RET_HEADS, dtype=F32)))


def _ret_decay_matrix(chunk):
    idx = jnp.arange(chunk, dtype=F32)
    rel = idx[:, None] - idx[None, :]
    lg = _ret_log_gamma()
    return jnp.where(rel[None] >= 0, jnp.exp(jnp.maximum(rel, 0.0)[None] * lg[:, None, None]), 0.0)


def _ret_heads(dmat_ref, lg_ref, q_ref, k_ref, v_ref, g_ref, gout_ref, s_ref, new_ref, fresh, valid):
    cl = q_ref.shape[0]
    idx = lax.broadcasted_iota(jnp.int32, (cl, 1), 0).astype(F32)
    for h in range(RET_HEADS):
        sl = slice(h * RET_DV, (h + 1) * RET_DV)
        lg = lg_ref[h][:1, :1]
        q_dec = jnp.exp((idx + 1.0) * lg)
        k_dec = jnp.exp((cl - 1.0 - idx) * lg)
        c_dec = jnp.exp(cl * lg)
        qb, kb = q_ref[:, sl], k_ref[:, sl]
        vb = v_ref[:, sl]
        s_old = jnp.where(fresh, 0.0, s_ref[h])
        scores = _dot_nt(qb, kb) * dmat_ref[h]
        o = _dot(scores.astype(BF16), vb) + _dot(qb, s_old.astype(BF16)) * q_dec
        s_new = s_old * c_dec + _dot_tn(kb, (vb.astype(F32) * k_dec).astype(BF16))
        s_ref[h] = jnp.where(valid, s_new, s_old)
        new_ref[:, sl] = (_rms(o) * gout_ref[:, sl] * _silu(g_ref[:, sl])).astype(new_ref.dtype)


def _gla_level_map(c):
    t = np.arange(c)[:, None]
    s = np.arange(c)[None, :]
    x = np.bitwise_xor(t, s)
    lev = np.floor(np.log2(np.maximum(x, 1))).astype(np.int32)
    lev = np.where(t == s, -1, lev)
    lev = np.where(t < s, -2, lev)
    return lev.astype(np.int32)


SCORE_BLK = 128
LOG2E = 1.4426950408889634


def _gla_chunk(q, k, la2, lev, uppers, at_level=None):
    cl = q.shape[0]
    nblk = cl // SCORE_BLK
    blk = lambda x, i: x[i * SCORE_BLK:(i + 1) * SCORE_BLK]
    p = la2
    tot = la2
    diag = [jnp.zeros((SCORE_BLK, SCORE_BLK), F32) for _ in range(nblk)]
    off = {}
    level = 0
    half = 1
    while half < cl:
        if at_level is not None:
            at_level(level)
        upper = uppers[level]
        z = (jnp.where(upper, q, k) * jnp.exp2(jnp.where(upper, p, tot - p))).astype(BF16)
        if half < SCORE_BLK:
            for i in range(nblk):
                diag[i] = jnp.where(lev == level, _dot_nt(blk(z, i), blk(z, i)), diag[i])
        else:
            hb = half // SCORE_BLK
            for i in range(nblk):
                if (i // hb) % 2 == 1:
                    base = (i // (2 * hb)) * 2 * hb
                    for j in range(base, base + hb):
                        off[(i, j)] = _dot_nt(blk(z, i), blk(z, j))
        partner = jnp.where(upper, pltpu.roll(tot, half, 0), pltpu.roll(tot, cl - half, 0))
        p = p + jnp.where(upper, partner, 0.0)
        tot = tot + partner
        half *= 2
        level += 1
    dg = jnp.sum(q * k, axis=-1, keepdims=True)
    rows = []
    for i in range(nblk):
        d_i = jnp.where(lev == -1, blk(dg, i), diag[i])
        rows.append(jnp.concatenate([off[(i, j)] for j in range(i)] + [d_i], axis=-1))
    return rows, p, tot


def _gla_heads(q_ref, k_ref, v_ref, g_ref, la_ref, lev_ref, gout_ref, st_ref, new_ref, fresh, valid, at_slot=None):
    cl = q_ref.shape[0]
    lev = lev_ref[...]
    row = lax.broadcasted_iota(jnp.int32, (cl, GLA_DK), 0)
    uppers = []
    half = 1
    while half < cl:
        uppers.append((row & half) != 0)
        half *= 2
    nlev = len(uppers)
    for h in range(GLA_HEADS):
        ks = slice(h * GLA_DK, (h + 1) * GLA_DK)
        vs = slice(h * GLA_DV, (h + 1) * GLA_DV)
        q = q_ref[:, ks] * (GLA_DK ** -0.5)
        k = k_ref[:, ks]
        hook = None if at_slot is None else (lambda level, h=h: at_slot(h * nlev + level))
        rows, p, tot = _gla_chunk(q, k, la_ref[:, ks] * LOG2E, lev, uppers, hook)
        vb = v_ref[:, vs]
        st_old = jnp.where(fresh, 0.0, st_ref[h])
        qt = (q * jnp.exp2(p)).astype(BF16)
        intra = jnp.concatenate([_dot(r.astype(BF16), vb[:r.shape[1]]) for r in rows], axis=0)
        o = intra + _dot_nt(qt, st_old.astype(BF16))
        kt = (k * jnp.exp2(tot - p)).astype(BF16)
        st_new = st_old * jnp.exp2(tot[0:1, :]) + _dot_tn(vb, kt)
        st_ref[h] = jnp.where(valid, st_new, st_old)
        new_ref[:, RET_W + h * GLA_DV: RET_W + (h + 1) * GLA_DV] = (
            _rms(o) * gout_ref[:, vs] * _silu(g_ref[:, vs])).astype(new_ref.dtype)


def _mixer_body(dmat_ref, lg_ref, q_ref, k_ref, rv_ref, rg_ref, gq_ref, gk_ref, gv_ref, gr_ref, la_ref, lev_ref,
                gro_ref, ggo_ref, x_ref, gt_ref, sc_ref, sh_ref, gf_ref, w_ref,
                x1_ref, h_ref, sr_out, sg_out, s_ref, st_ref, mixa_ref, mixb_ref, *, nc, n_chunks):
    s = pl.program_id(0)
    valid = s < n_chunks
    c = jnp.minimum(s, n_chunks - 1) % nc
    fresh = c == 0

    @pl.when(s == 0)
    def _():
        mixb_ref[...] = jnp.zeros_like(mixb_ref)

    def step(old_ref, new_ref):
        old = old_ref[...]
        d = x_ref.shape[1]
        nblk = d // PROJ_BLK
        ssq = [jnp.zeros((x_ref.shape[0], 1), F32)]

        def project(blocks):
            for cb in blocks:
                cs = slice(cb * PROJ_BLK, (cb + 1) * PROJ_BLK)
                x1 = x_ref[:, cs] + gt_ref[0][:, cs] * _dot(old, w_ref[:, cs])
                x1_ref[:, cs] = x1
                ssq[0] = ssq[0] + jnp.sum(x1 * x1, axis=-1, keepdims=True)

        nslots = GLA_HEADS * (x_ref.shape[0].bit_length() - 1)
        due = {(cb * nslots) // nblk: cb for cb in range(nblk)}
        _gla_heads(gq_ref, gk_ref, gv_ref, gr_ref, la_ref, lev_ref, ggo_ref, st_ref, new_ref, fresh, valid,
                   at_slot=lambda slot: project([due[slot]]) if slot in due else None)
        _ret_heads(dmat_ref, lg_ref, q_ref, k_ref, rv_ref, rg_ref, gro_ref, s_ref, new_ref, fresh, valid)
        inv = lax.rsqrt(ssq[0] * (1.0 / d) + EPS)
        gain = gf_ref[...] * (1.0 + sc_ref[0])
        for cb in range(nblk):
            cs = slice(cb * PROJ_BLK, (cb + 1) * PROJ_BLK)
            h_ref[:, cs] = (x1_ref[:, cs] * inv * gain[:, cs] + sh_ref[0][:, cs]).astype(h_ref.dtype)

    @pl.when(s % 2 == 0)
    def _():
        step(mixb_ref, mixa_ref)

    @pl.when(s % 2 == 1)
    def _():
        step(mixa_ref, mixb_ref)

    @pl.when(valid & (c == nc - 1))
    def _():
        sr_out[0] = s_ref[...]
        for h in range(GLA_HEADS):
            sg_out[0, h] = st_ref[h].T


def _mixer_prompt(qk, val, rest, la, x, gt, sc, sh, g_ret_out, g_gla_out, g_ffn, w_out_b, *, batch, seq, chunk):
    m, d = x.shape
    nc = seq // chunk
    n_chunks = batch * nc
    lg = jnp.broadcast_to(_ret_log_gamma()[:, None, None], (RET_HEADS, SUBLANES, LANES))
    lev = jnp.asarray(_gla_level_map(SCORE_BLK))
    gqk_w = GLA_HEADS * GLA_DK
    cur = lambda s: jnp.minimum(s, n_chunks - 1)
    prv = lambda s: jnp.maximum(s - 1, 0)
    rowc = lambda w, blk: pl.BlockSpec((chunk, w), lambda s: (cur(s), blk))
    const = lambda shape: pl.BlockSpec(shape, lambda s: (0,) * len(shape))
    mod_spec = pl.BlockSpec((1, 1, d), lambda s: (prv(s) // nc, 0, 0))
    return pl.pallas_call(
        functools.partial(_mixer_body, nc=nc, n_chunks=n_chunks),
        grid=(n_chunks + 1,),
        in_specs=[const((RET_HEADS, chunk, chunk)), const((RET_HEADS, SUBLANES, LANES)),
                  rowc(RET_W, 0), rowc(RET_W, 1),
                  rowc(RET_W, P_RV // RET_W), rowc(RET_W, P_RG // RET_W),
                  rowc(gqk_w, P_GQ // gqk_w), rowc(gqk_w, P_GK // gqk_w),
                  rowc(GLA_W, P_GV // GLA_W), rowc(GLA_W, P_GR // GLA_W),
                  rowc(gqk_w, 0),
                  const((SCORE_BLK, SCORE_BLK)), const((1, RET_W)), const((1, GLA_W)),
                  pl.BlockSpec((chunk, d), lambda s: (prv(s), 0)),
                  mod_spec, mod_spec, mod_spec,
                  const((1, d)),
                  pl.BlockSpec((RET_W + GLA_W, d), lambda s: (0, 0), pipeline_mode=pl.Buffered(1))],
        out_specs=[pl.BlockSpec((chunk, d), lambda s: (prv(s), 0)),
                   pl.BlockSpec((chunk, d), lambda s: (prv(s), 0)),
                   pl.BlockSpec((1, RET_HEADS, RET_DK, RET_DV), lambda s: (cur(s) // nc, 0, 0, 0)),
                   pl.BlockSpec((1, GLA_HEADS, GLA_DK, GLA_DV), lambda s: (cur(s) // nc, 0, 0, 0))],
        out_shape=[jax.ShapeDtypeStruct((m, d), F32),
                   jax.ShapeDtypeStruct((m, d), BF16),
                   jax.ShapeDtypeStruct((batch, RET_HEADS, RET_DK, RET_DV), F32),
                   jax.ShapeDtypeStruct((batch, GLA_HEADS, GLA_DK, GLA_DV), F32)],
        scratch_shapes=[pltpu.VMEM((RET_HEADS, RET_DK, RET_DV), F32),
                        pltpu.VMEM((GLA_HEADS, GLA_DV, GLA_DK), F32),
                        pltpu.VMEM((chunk, RET_W + GLA_W), BF16),
                        pltpu.VMEM((chunk, RET_W + GLA_W), BF16)],
        compiler_params=_params(("arbitrary",)),
        name="mixer_prompt",
    )(_ret_decay_matrix(chunk), lg, qk, qk, val, rest, rest, rest, val, rest, la, lev,
      g_ret_out.reshape(1, RET_W), g_gla_out.reshape(1, GLA_W), x, gt, sc, sh, g_ffn.reshape(1, d), w_out_b)


def _columns(rows):
    n = rows[0].shape[1]
    pad = (-len(rows)) % 8
    stack = jnp.concatenate(rows + [jnp.zeros((pad, n), F32)] if pad else rows, axis=0)
    return stack.T


def _state_body(lg_ref, qk_ref, rest_ref, la_ref, sr_ref, sg_ref, gro_ref, ggo_ref, o_ref, sr_out, sg_out):
    nb = qk_ref.shape[0]
    for i in range(nb):
        qkrow = qk_ref[i]
        rrow = rest_ref[i]
        larow = la_ref[i]
        rcols = _columns([qkrow[:, j * RET_DK:(j + 1) * RET_DK] for j in range(2 * RET_HEADS)])
        gq = [rrow[:, R_GQ + h * GLA_DK: R_GQ + (h + 1) * GLA_DK] * (GLA_DK ** -0.5) for h in range(GLA_HEADS)]
        gk = [rrow[:, R_GK + h * GLA_DK: R_GK + (h + 1) * GLA_DK] for h in range(GLA_HEADS)]
        ga = [jnp.exp(larow[:, h * GLA_DK:(h + 1) * GLA_DK]) for h in range(GLA_HEADS)]
        gcols = _columns(gq + gk + ga)
        outs = []
        for h in range(RET_HEADS):
            v = rrow[:, R_RV + h * RET_DV: R_RV + (h + 1) * RET_DV]
            gate = rrow[:, R_RG + h * RET_DV: R_RG + (h + 1) * RET_DV]
            gamma = jnp.exp(lg_ref[h][:1, :1])
            qc = rcols[:, h:h + 1]
            kc = rcols[:, RET_HEADS + h:RET_HEADS + h + 1]
            s_new = sr_ref[0, i, h] * gamma + kc * v
            sr_out[0, i, h] = s_new
            o = jnp.sum(qc * s_new, axis=0, keepdims=True)
            outs.append(_rms(o) * gro_ref[:, h * RET_DV:(h + 1) * RET_DV] * _silu(gate))
        for h in range(GLA_HEADS):
            v = rrow[:, R_GV + h * GLA_DV: R_GV + (h + 1) * GLA_DV]
            gate = rrow[:, R_GR + h * GLA_DV: R_GR + (h + 1) * GLA_DV]
            qc = gcols[:, h:h + 1]
            kc = gcols[:, GLA_HEADS + h:GLA_HEADS + h + 1]
            ac = gcols[:, 2 * GLA_HEADS + h:2 * GLA_HEADS + h + 1]
            s_new = sg_ref[0, i, h] * ac + kc * v
            sg_out[0, i, h] = s_new
            o = jnp.sum(qc * s_new, axis=0, keepdims=True)
            outs.append(_rms(o) * ggo_ref[:, h * GLA_DV:(h + 1) * GLA_DV] * _silu(gate))
        o_ref[i] = jnp.concatenate(outs, axis=-1)


def _outproj_body(oa_ref, ob_ref, x_ref, gt_ref, sc_ref, sh_ref, g_ref, w_ref, x1_ref, h_ref):
    ka = oa_ref.shape[1]
    mix = _dot(oa_ref[...].astype(BF16), w_ref[:ka, :]) + _dot(ob_ref[...].astype(BF16), w_ref[ka:, :])
    x1 = x_ref[...] + gt_ref[0] * mix
    x1_ref[...] = x1
    h = _rms(x1) * g_ref[...]
    h_ref[...] = (h * (1.0 + sc_ref[0]) + sh_ref[0]).astype(BF16)


def _outproj(o_a, o_b, x, gt, sc, sh, g_ffn, w_out_bf16, *, tm, rows_per_mod):
    m, d = x.shape
    ka, kb = o_a.shape[1], o_b.shape[1]
    mod_spec = pl.BlockSpec((1, gt.shape[1], d), lambda i: ((i * tm) // rows_per_mod, 0, 0))
    return pl.pallas_call(
        _outproj_body,
        grid=(m // tm,),
        in_specs=[pl.BlockSpec((tm, ka), lambda i: (i, 0)),
                  pl.BlockSpec((tm, kb), lambda i: (i, 0)),
                  pl.BlockSpec((tm, d), lambda i: (i, 0)),
                  mod_spec, mod_spec, mod_spec,
                  pl.BlockSpec((1, d), lambda i: (0, 0)),
                  pl.BlockSpec((ka + kb, d), lambda i: (0, 0), pipeline_mode=pl.Buffered(1))],
        out_specs=[pl.BlockSpec((tm, d), lambda i: (i, 0)),
                   pl.BlockSpec((tm, d), lambda i: (i, 0))],
        out_shape=[jax.ShapeDtypeStruct((m, d), F32),
                   jax.ShapeDtypeStruct((m, d), BF16)],
        compiler_params=_params(("arbitrary",)),
        name="out_proj",
    )(o_a, o_b, x, gt, sc, sh, g_ffn.reshape(1, d), w_out_bf16)


HALO = SUBLANES


def _ffn_up_body(h_ref, wa_ref, wb_ref, cwa_ref, cwb_ref, cba_ref, cbb_ref, g_ref, cs_ref, win_ref, tail_ref, *, tiles_per_seq):
    i, f = pl.program_id(0), pl.program_id(1)
    hb = h_ref[...]
    tm = hb.shape[0]
    first = (i % tiles_per_seq) == 0
    ucs = []
    for part, (w_ref, cw_ref, cb_ref) in enumerate(((wa_ref, cwa_ref, cba_ref), (wb_ref, cwb_ref, cbb_ref))):
        u = _dot(hb, w_ref[...])
        prev = jnp.where(first, 0.0, tail_ref[part, f])
        pieces = []
        for c in range(u.shape[1] // LANES):
            lc = slice(c * LANES, (c + 1) * LANES)
            win_ref[part, c, 0:HALO, :] = prev[:, lc]
            win_ref[part, c, HALO:HALO + tm, :] = u[:, lc]
            cw, cb = cw_ref[:, lc], cb_ref[:, lc]
            pieces.append(cb + cw[0:1] * win_ref[part, c, HALO - 2:HALO - 2 + tm, :]
                          + cw[1:2] * win_ref[part, c, HALO - 1:HALO - 1 + tm, :] + cw[2:3] * u[:, lc])
        ucs.append(jnp.concatenate(pieces, axis=-1))
        tail_ref[part, f] = u[tm - HALO:]
        for r in range(CONV_W - 1):
            cs_ref[0, 0, r, part:part + 1, :] = u[tm - (CONV_W - 1) + r: tm - (CONV_W - 1) + r + 1]
    g_ref[...] = (_silu(ucs[0]) * ucs[1]).astype(g_ref.dtype)


def _ffn_up_state_body(h_ref, wa_ref, wb_ref, cwa_ref, cwb_ref, cba_ref, cbb_ref,
                       lg_ref, qk_ref, rest_ref, la_ref, sr_ref, sg_ref, gro_ref, ggo_ref, wd_ref,
                       g_ref, cs_ref, o_ref, sr_out, sg_out, wdb_ref, win_ref, tail_ref, *, tiles_per_seq):
    _ffn_up_body(h_ref, wa_ref, wb_ref, cwa_ref, cwb_ref, cba_ref, cbb_ref, g_ref, cs_ref, win_ref, tail_ref,
                 tiles_per_seq=tiles_per_seq)
    _state_body(lg_ref, qk_ref, rest_ref, la_ref, sr_ref, sg_ref, gro_ref, ggo_ref, o_ref, sr_out, sg_out)
    wdb_ref[...] = wd_ref[...].astype(wdb_ref.dtype)


def _ffn_up_prompt(h, w_up, conv_w, conv_b, qk_s, rest_s, la_s, state_ret, state_gla, g_ret_out, g_gla_out, w_down,
                   *, batch, seq, tm, tf, nb, cast_rows):
    m, d = h.shape
    ff = w_up.shape[1] // 2
    nf = ff // tf
    cb = conv_b.reshape(1, 2 * ff)
    tps = seq // tm
    ms = qk_s.shape[0]
    nblk = ms // nb
    ncast = ff // cast_rows
    assert (m // tm) * nf >= nblk + ncast
    mix = RET_W + GLA_W
    lg = jnp.broadcast_to(_ret_log_gamma()[:, None, None], (RET_HEADS, SUBLANES, LANES))
    sblk = lambda i, f: jnp.minimum(i * nf + f, nblk - 1)
    ret_spec = pl.BlockSpec((1, nb, RET_HEADS, RET_DK, RET_DV), lambda i, f: (0, sblk(i, f), 0, 0, 0))
    gla_spec = pl.BlockSpec((1, nb, GLA_HEADS, GLA_DK, GLA_DV), lambda i, f: (0, sblk(i, f), 0, 0, 0))
    row_spec = lambda w: pl.BlockSpec((nb, 1, w), lambda i, f: (sblk(i, f), 0, 0))
    cast_spec = pl.BlockSpec((cast_rows, d), lambda i, f: (jnp.clip(i * nf + f - nblk, 0, ncast - 1), 0))
    body = functools.partial(_ffn_up_state_body, tiles_per_seq=tps)
    gate, tails, o_s, s_ret, s_gla, w_down_b = pl.pallas_call(
        body,
        grid=(m // tm, nf),
        in_specs=[pl.BlockSpec((tm, d), lambda i, f: (i, 0)),
                  pl.BlockSpec((d, tf), lambda i, f: (0, f)),
                  pl.BlockSpec((d, tf), lambda i, f: (0, nf + f)),
                  pl.BlockSpec((CONV_W, tf), lambda i, f: (0, f)),
                  pl.BlockSpec((CONV_W, tf), lambda i, f: (0, nf + f)),
                  pl.BlockSpec((1, tf), lambda i, f: (0, f)),
                  pl.BlockSpec((1, tf), lambda i, f: (0, nf + f)),
                  pl.BlockSpec((RET_HEADS, SUBLANES, LANES), lambda i, f: (0, 0, 0)),
                  row_spec(QK_W), row_spec(REST_W), row_spec(la_s.shape[1]),
                  ret_spec, gla_spec,
                  pl.BlockSpec((1, RET_W), lambda i, f: (0, 0)),
                  pl.BlockSpec((1, GLA_W), lambda i, f: (0, 0)),
                  cast_spec],
        out_specs=[pl.BlockSpec((tm, tf), lambda i, f: (i, f)),
                   pl.BlockSpec((1, 1, CONV_W - 1, 2, tf), lambda i, f: (i // tps, i % tps, 0, 0, f)),
                   row_spec(mix), ret_spec, gla_spec, cast_spec],
        out_shape=[jax.ShapeDtypeStruct((m, ff), BF16),
                   jax.ShapeDtypeStruct((batch, tps, CONV_W - 1, 2, ff), F32),
                   jax.ShapeDtypeStruct((ms, 1, mix), F32),
                   jax.ShapeDtypeStruct(state_ret.shape, F32),
                   jax.ShapeDtypeStruct(state_gla.shape, F32),
                   jax.ShapeDtypeStruct(w_down.shape, BF16)],
        scratch_shapes=[pltpu.VMEM((2, tf // LANES, HALO + tm, LANES), F32), pltpu.VMEM((2, nf, HALO, tf), F32)],
        compiler_params=_params(("arbitrary", "arbitrary")),
        name="ffn_up_prompt",
    )(h, w_up, w_up, conv_w, conv_w, cb, cb,
      lg, qk_s.astype(F32).reshape(ms, 1, QK_W), rest_s.reshape(ms, 1, REST_W), la_s.reshape(ms, 1, la_s.shape[1]),
      state_ret, state_gla, g_ret_out.reshape(1, RET_W), g_gla_out.reshape(1, GLA_W), w_down)
    return gate, tails[:, tps - 1], o_s.reshape(ms, mix), s_ret, s_gla, w_down_b


def _ffn_up_step_body(h_ref, w_ref, cw_ref, cb_ref, st_ref, g_ref, cs_ref, uca_ref, *, nf):
    j = pl.program_id(0)
    u = _dot(h_ref[...], w_ref[...])
    s1 = st_ref[0, :, 1, :]
    cw = cw_ref[...]
    uc = cb_ref[...] + cw[0:1] * st_ref[0, :, 0, :] + cw[1:2] * s1 + cw[2:3] * u
    cs_ref[0, :, 0, :] = s1
    cs_ref[0, :, 1, :] = u

    @pl.when(j < nf)
    def _():
        uca_ref[j] = uc

    @pl.when(j >= nf)
    def _():
        g_ref[...] = (_silu(uca_ref[j - nf]) * uc).astype(g_ref.dtype)


def _ffn_up_step(h, w_up, conv_w, conv_b, state_conv, *, tf):
    m, d = h.shape
    ff = w_up.shape[1] // 2
    nf = ff // tf
    st_spec = pl.BlockSpec((1, m, CONV_W - 1, tf), lambda j: (0, 0, 0, j))
    return pl.pallas_call(
        functools.partial(_ffn_up_step_body, nf=nf),
        grid=(2 * nf,),
        in_specs=[pl.BlockSpec((m, d), lambda j: (0, 0)),
                  pl.BlockSpec((d, tf), lambda j: (0, j)),
                  pl.BlockSpec((CONV_W, tf), lambda j: (0, j)),
                  pl.BlockSpec((1, tf), lambda j: (0, j)),
                  st_spec],
        out_specs=[pl.BlockSpec((m, tf), lambda j: (0, jnp.maximum(j - nf, 0))), st_spec],
        out_shape=[jax.ShapeDtypeStruct((m, ff), BF16),
                   jax.ShapeDtypeStruct(state_conv.shape, F32)],
        scratch_shapes=[pltpu.VMEM((nf, m, tf), F32)],
        compiler_params=_params(("arbitrary",)),
        name="ffn_up_step",
    )(h, w_up, conv_w, conv_b.reshape(1, 2 * ff), state_conv)


def _ffn_down_body(g_ref, w_ref, x1_ref, gt_ref, gf_ref, y_ref):
    k = pl.program_id(1)
    last = pl.num_programs(1) - 1
    d = y_ref.shape[1]

    @pl.when(k == 0)
    def _():
        y_ref[...] = _dot(g_ref[...], w_ref[...])

    @pl.when((k > 0) & (k < last))
    def _():
        y_ref[...] += _dot(g_ref[...], w_ref[...])

    @pl.when(k == last)
    def _():
        g = g_ref[...]
        ssq = jnp.zeros((y_ref.shape[0], 1), F32)
        for cb in range(d // PROJ_BLK):
            cs = slice(cb * PROJ_BLK, (cb + 1) * PROJ_BLK)
            x2 = x1_ref[:, cs] + gt_ref[0][:, cs] * (y_ref[:, cs] + _dot(g, w_ref[:, cs]))
            y_ref[:, cs] = x2
            ssq = ssq + jnp.sum(x2 * x2, axis=-1, keepdims=True)
        inv = lax.rsqrt(ssq * (1.0 / d) + EPS)
        for cb in range(d // PROJ_BLK):
            cs = slice(cb * PROJ_BLK, (cb + 1) * PROJ_BLK)
            y_ref[:, cs] = y_ref[:, cs] * inv * gf_ref[:, cs]


def _ffn_down(g, w_down, x1, gt, g_final, *, tm, tk, rows_per_mod):
    m, d = x1.shape
    ff = g.shape[1]
    assert ff // tk >= 2
    return pl.pallas_call(
        _ffn_down_body,
        grid=(m // tm, ff // tk),
        in_specs=[pl.BlockSpec((tm, tk), lambda i, k: (i, k)),
                  pl.BlockSpec((tk, d), lambda i, k: (k, 0)),
                  pl.BlockSpec((tm, d), lambda i, k: (i, 0)),
                  pl.BlockSpec((1, gt.shape[1], d), lambda i, k: ((i * tm) // rows_per_mod, 0, 0)),
                  pl.BlockSpec((1, d), lambda i, k: (0, 0))],
        out_specs=pl.BlockSpec((tm, d), lambda i, k: (i, 0)),
        out_shape=jax.ShapeDtypeStruct((m, d), F32),
        compiler_params=_params(("arbitrary", "arbitrary")),
        name="ffn_down",
    )(g, w_down, x1, gt, g_final.reshape(1, d))


def _rope_tables(pos):
    half = RET_DK // 2
    inv = ROPE_THETA ** (-jnp.arange(half, dtype=F32) / half)
    ang = pos.astype(F32)[:, None] * inv[None, :]
    return jnp.cos(ang), jnp.sin(ang)


def kernel(x_prompt, x_sample, c_prompt, c_sample, state_ret, state_gla, state_conv, w_ada, b_ada, g_attn, w_in, w_a2, b_a2, g_ret_out, g_gla_out, w_out, g_ffn, w_up, conv_w, conv_b, w_down, g_final):
    bp, t_p, d = x_prompt.shape
    bs, t_s, _ = x_sample.shape
    assert t_s == 1 and w_ada.shape[0] == 1
    mp = bp * t_p
    w_ada, b_ada, g_attn, w_in, w_a2, b_a2, g_ret_out, g_gla_out, w_out, g_ffn, w_up, conv_w, conv_b, w_down = (
        a[0] for a in (w_ada, b_ada, g_attn, w_in, w_a2, b_a2, g_ret_out, g_gla_out, w_out, g_ffn, w_up, conv_w, conv_b, w_down))

    mod, w_up_b = _ada(jnp.concatenate([c_prompt, c_sample], axis=0), w_ada, b_ada, w_up)
    sh1p, sc1p, gt1p, sh2p, sc2p, gt2p = (mod[:bp, i * d:(i + 1) * d].reshape(bp, 1, d) for i in range(6))
    sh1s, sc1s, gt1s, sh2s, sc2s, gt2s = (mod[bp:, i * d:(i + 1) * d].reshape(1, bs, d) for i in range(6))

    cos_p, sin_p = _rope_tables(jnp.arange(t_p, dtype=jnp.int32))
    cos_s, sin_s = (jnp.broadcast_to(t, (bs, RET_DK // 2)) for t in _rope_tables(PAST_LEN + jnp.arange(t_s, dtype=jnp.int32)))
    w_out_b = w_out.astype(BF16)
    w_in_t = w_in.T
    w_gate_t = w_in_t[IN_MAIN:]

    xp = x_prompt.reshape(mp, d)
    xs = x_sample.reshape(bs, d)
    qk_s, val_s, f32_s, la_s, w_in_b = _inproj(xs, sc1s, sh1s, g_attn, w_in_t, w_gate_t, w_a2, b_a2, cos_s, sin_s,
                                               tm=bs, rows_per_mod=bs, cast_w=True)
    val_s = val_s.astype(F32)
    rest_s = jnp.concatenate([val_s[:, :P_GV], f32_s[:, :P_GR], val_s[:, P_GV:], f32_s[:, P_GR:]], axis=-1)
    qk_p, val_p, rest_p, la_p = _inproj(xp, sc1p, sh1p, g_attn, w_in_b, w_gate_t, w_a2, b_a2, cos_p, sin_p,
                                        tm=ROW_TILE, rows_per_mod=t_p)
    x1_p, h2_p, s_ret_p, s_gla_p = _mixer_prompt(qk_p, val_p, rest_p, la_p, xp, gt1p, sc2p, sh2p, g_ret_out, g_gla_out,
                                                 g_ffn, w_out_b, batch=bp, seq=t_p, chunk=CHUNK)
    g_p, cs_p, o_s, s_ret_s, s_gla_s, w_down_b = _ffn_up_prompt(
        h2_p, w_up_b, conv_w, conv_b, qk_s, rest_s, la_s, state_ret, state_gla, g_ret_out, g_gla_out, w_down,
        batch=bp, seq=t_p, tm=ROW_TILE, tf=FF_COL_TILE, nb=STATE_SEQS, cast_rows=CAST_ROWS)
    y_p = _ffn_down(g_p, w_down_b, x1_p, gt2p, g_final, tm=ROW_TILE, tk=DOWN_K_TILE, rows_per_mod=t_p)

    x1_s, h2_s = _outproj(o_s[:, :RET_W], o_s[:, RET_W:], xs, gt1s, sc2s, sh2s, g_ffn, w_out_b, tm=bs, rows_per_mod=bs)
    ff = w_down.shape[0]
    g_s, cs_s = _ffn_up_step(h2_s, w_up_b, conv_w, conv_b, state_conv, tf=ff // 4)
    y_s = _ffn_down(g_s, w_down_b, x1_s, gt2s, g_final, tm=bs, tk=ff // 2, rows_per_mod=bs)

    return (y_p.reshape(bp, t_p, d), y_s.reshape(bs, t_s, d),
            s_ret_p[None], s_ret_s, s_gla_p[None], s_gla_s,
            cs_p.reshape(1, bp, CONV_W - 1, -1), cs_s)
```

```python
import functools

import numpy as np
import jax
import jax.numpy as jnp
from jax import lax
from jax.experimental import pallas as pl
from jax.experimental.pallas import tpu as pltpu

F32 = jnp.float32
BF16 = jnp.bfloat16

RET_HEADS = 4
RET_DK = 256
RET_DV = 256
GLA_HEADS = 4
GLA_DK = 128
GLA_DV = 256
GLA_RANK = 16
GLA_TAU = 16.0
ROPE_THETA = 10000.0
PAST_LEN = 16384
CONV_W = 3
EPS = 1e-6

RET_W = RET_HEADS * RET_DV
GLA_W = GLA_HEADS * GLA_DV
QK_W = 2 * RET_HEADS * RET_DK
R_RV = 0
R_RG = R_RV + RET_W
R_GQ = R_RG + RET_W
R_GK = R_GQ + GLA_HEADS * GLA_DK
R_GV = R_GK + GLA_HEADS * GLA_DK
R_GR = R_GV + GLA_W
REST_W = R_GR + GLA_W
IN_MAIN = QK_W + REST_W

V7X_VMEM_BYTES = 64 * 1024 * 1024
V7X_MXU_WIDTH = 256
LANES = 128
SUBLANES = 8
VMEM_LIMIT_BYTES = V7X_VMEM_BYTES - 8 * 1024 * 1024
PROJ_BLK = V7X_MXU_WIDTH

ROW_TILE = 1024
IN_COL_TILE = 1024
FF_COL_TILE = 512
DOWN_K_TILE = 1408
CHUNK = 256
STATE_SEQS = 2
CAST_ROWS = 256
CAST_COLS = 256
NORM_ROWS = 128


def _params(semantics):
    return pltpu.CompilerParams(dimension_semantics=semantics, vmem_limit_bytes=VMEM_LIMIT_BYTES)


def _dot(a, b):
    return jnp.dot(a, b, preferred_element_type=F32)


def _dot_nt(a, b):
    return lax.dot_general(a, b, (((1,), (1,)), ((), ())), preferred_element_type=F32)


def _dot_tn(a, b):
    return lax.dot_general(a, b, (((0,), (0,)), ((), ())), preferred_element_type=F32)


def _silu(x):
    return x * jax.nn.sigmoid(x)


def _rms(x):
    return x * lax.rsqrt(jnp.mean(x * x, axis=-1, keepdims=True) + EPS)


def _ada_body(c_ref, w_ref, b_ref, o_ref):
    s = _silu(c_ref[...]).astype(BF16)
    o_ref[...] = _dot(s, w_ref[...].astype(BF16)) + b_ref[...]


def _ada(c_all, w_ada, b_ada, tn=FF_COL_TILE):
    r, d = c_all.shape
    n = w_ada.shape[1]
    return pl.pallas_call(
        _ada_body,
        grid=(n // tn,),
        in_specs=[pl.BlockSpec((r, d), lambda j: (0, 0)),
                  pl.BlockSpec((d, tn), lambda j: (0, j)),
                  pl.BlockSpec((1, tn), lambda j: (0, j))],
        out_specs=pl.BlockSpec((r, tn), lambda j: (0, j)),
        out_shape=jax.ShapeDtypeStruct((r, n), F32),
        compiler_params=_params(("arbitrary",)),
        name="ada_mod",
    )(c_all, w_ada, b_ada.reshape(1, n))


def _inproj_body(*refs, n_qk, cast_w, rider):
    refs = list(refs)
    x_ref, sc_ref, sh_ref, g_ref, w_ref, wga_ref, wa2_ref, ba2_ref, cos_ref, sin_ref = refs[:10]
    del refs[:10]
    src_ref = refs.pop(0) if rider else None
    qk_ref, rest_ref, la_ref = refs[:3]
    del refs[:3]
    wcast_ref = refs.pop(0) if cast_w else None
    dst_ref = refs.pop(0) if rider else None
    (h_ref,) = refs
    j = pl.program_id(1)

    def ride():
        if rider:
            dst_ref[...] = src_ref[...].astype(dst_ref.dtype)

    @pl.when(j == 0)
    def _():
        gain = g_ref[...] * (1.0 + sc_ref[0])
        hb = (_rms(x_ref[...]) * gain + sh_ref[0]).astype(BF16)
        h_ref[...] = hb
        ga = _dot_nt(hb, wga_ref[...].astype(BF16))
        z = _dot(ga.astype(BF16), wa2_ref[...].astype(BF16)) + ba2_ref[...]
        la_ref[...] = (jnp.minimum(z, 0.0) - jnp.log1p(jnp.exp(-jnp.abs(z)))) * (1.0 / GLA_TAU)

    if cast_w:
        w_tile = w_ref[...].T.astype(BF16)
        wcast_ref[...] = w_tile
        weight = lambda: w_tile
    else:
        weight = lambda: w_ref[...]

    @pl.when(j < n_qk)
    def _():
        ride()
        acc = _dot(h_ref[...], weight())
        cos, sin = cos_ref[...], sin_ref[...]
        half = RET_DK // 2
        scale = jnp.where(j >= n_qk // 2, RET_DK ** -0.5, 1.0)
        outs = []
        for hh in range(acc.shape[1] // RET_DK):
            x1 = acc[:, hh * RET_DK: hh * RET_DK + half]
            x2 = acc[:, hh * RET_DK + half: (hh + 1) * RET_DK]
            outs += [x1 * cos - x2 * sin, x1 * sin + x2 * cos]
        qk_ref[...] = (jnp.concatenate(outs, axis=-1) * scale).astype(qk_ref.dtype)

    @pl.when(j >= n_qk)
    def _():
        ride()
        rest_ref[...] = _dot(h_ref[...], weight())


def _inproj(x, sc, sh, g_attn, w, w_gate_t, w_a2, b_a2, cos, sin, *, tm, rows_per_mod, cast_w=False, cast_src=None,
            tn=IN_COL_TILE):
    m, d = x.shape
    nq = w_a2.shape[1]
    n_qk = QK_W // tn
    tab_tiles = cos.shape[0] // tm
    n_j = IN_MAIN // tn
    once = pl.Buffered(1)
    mod_spec = pl.BlockSpec((1, sc.shape[1], d), lambda i, j: ((i * tm) // rows_per_mod, 0, 0))
    tab_spec = pl.BlockSpec((tm, RET_DK // 2), lambda i, j: (i % tab_tiles, 0), pipeline_mode=once)
    out_specs = [pl.BlockSpec((tm, tn), lambda i, j: (i, jnp.minimum(j, n_qk - 1))),
                 pl.BlockSpec((tm, tn), lambda i, j: (i, jnp.maximum(j - n_qk, 0))),
                 pl.BlockSpec((tm, nq), lambda i, j: (i, 0), pipeline_mode=once)]
    out_shape = [jax.ShapeDtypeStruct((m, QK_W), BF16),
                 jax.ShapeDtypeStruct((m, REST_W), F32),
                 jax.ShapeDtypeStruct((m, nq), F32)]
    if cast_w:
        assert m == tm
        w_spec = pl.BlockSpec((tn, d), lambda i, j: (j, 0))
        out_specs.append(pl.BlockSpec((d, tn), lambda i, j: (0, j)))
        out_shape.append(jax.ShapeDtypeStruct((d, IN_MAIN), BF16))
    else:
        w_spec = pl.BlockSpec((d, tn), lambda i, j: (0, j))
    in_specs = [pl.BlockSpec((tm, d), lambda i, j: (i, 0)),
                mod_spec, mod_spec,
                pl.BlockSpec((1, d), lambda i, j: (0, 0)),
                w_spec,
                pl.BlockSpec((GLA_RANK, d), lambda i, j: (0, 0)),
                pl.BlockSpec((GLA_RANK, nq), lambda i, j: (0, 0)),
                pl.BlockSpec((1, nq), lambda i, j: (0, 0)),
                tab_spec, tab_spec]
    args = [x, sc, sh, g_attn.reshape(1, d), w, w_gate_t, w_a2, b_a2.reshape(1, nq), cos, sin]
    if cast_src is not None:
        ncast = cast_src.shape[1] // CAST_COLS
        assert (m // tm) * n_j >= ncast
        ride_spec = pl.BlockSpec((cast_src.shape[0], CAST_COLS), lambda i, j: (0, jnp.minimum(i * n_j + j, ncast - 1)))
        in_specs.append(ride_spec)
        args.append(cast_src)
        out_specs.append(ride_spec)
        out_shape.append(jax.ShapeDtypeStruct(cast_src.shape, BF16))
    return pl.pallas_call(
        functools.partial(_inproj_body, n_qk=n_qk, cast_w=cast_w, rider=cast_src is not None),
        grid=(m // tm, n_j),
        in_specs=in_specs,
        out_specs=out_specs,
        out_shape=out_shape,
        scratch_shapes=[pltpu.VMEM((tm, d), BF16)],
        compiler_params=_params(("arbitrary", "arbitrary")),
        name="in_proj",
    )(*args)


def _ret_log_gamma():
    return jnp.log1p(-jnp.exp2(-5.0 - jnp.arange(RET_HEADS, dtype=F32)))


def _ret_decay_matrix(chunk):
    idx = jnp.arange(chunk, dtype=F32)
    rel = idx[:, None] - idx[None, :]
    lg = _ret_log_gamma()
    return jnp.where(rel[None] >= 0, jnp.exp(jnp.maximum(rel, 0.0)[None] * lg[:, None, None]), 0.0)


def _ret_heads(dmat_ref, lg_ref, q_ref, k_ref, v_ref, g_ref, gout_ref, s_ref, new_ref, fresh, valid):
    cl = q_ref.shape[0]
    idx = lax.broadcasted_iota(jnp.int32, (cl, 1), 0).astype(F32)
    for h in range(RET_HEADS):
        sl = slice(h * RET_DV, (h + 1) * RET_DV)
        lg = lg_ref[h][:1, :1]
        q_dec = jnp.exp((idx + 1.0) * lg)
        k_dec = jnp.exp((cl - 1.0 - idx) * lg)
        c_dec = jnp.exp(cl * lg)
        qb, kb = q_ref[:, sl], k_ref[:, sl]
        v = v_ref[:, sl]
        vb = v.astype(BF16)
        s_old = jnp.where(fresh, 0.0, s_ref[h])
        scores = _dot_nt(qb, kb) * dmat_ref[h]
        o = _dot(scores.astype(BF16), vb) + _dot(qb, s_old.astype(BF16)) * q_dec
        s_new = s_old * c_dec + _dot_tn(kb, (v * k_dec).astype(BF16))
        s_ref[h] = jnp.where(valid, s_new, s_old)
        new_ref[:, sl] = (_rms(o) * gout_ref[:, sl] * _silu(g_ref[:, sl])).astype(new_ref.dtype)


def _gla_level_map(c):
    t = np.arange(c)[:, None]
    s = np.arange(c)[None, :]
    x = np.bitwise_xor(t, s)
    lev = np.floor(np.log2(np.maximum(x, 1))).astype(np.int32)
    lev = np.where(t == s, -1, lev)
    lev = np.where(t < s, -2, lev)
    return lev.astype(np.int32)


SCORE_BLK = 128
LOG2E = 1.4426950408889634


def _gla_chunk(q, k, la2, lev, uppers, at_level=None):
    cl = q.shape[0]
    nblk = cl // SCORE_BLK
    blk = lambda x, i: x[i * SCORE_BLK:(i + 1) * SCORE_BLK]
    p = la2
    tot = la2
    diag = [jnp.zeros((SCORE_BLK, SCORE_BLK), F32) for _ in range(nblk)]
    off = {}
    level = 0
    half = 1
    while half < cl:
        if at_level is not None:
            at_level(level)
        upper = uppers[level]
        z = (jnp.where(upper, q, k) * jnp.exp2(jnp.where(upper, p, tot - p))).astype(BF16)
        if half < SCORE_BLK:
            for i in range(nblk):
                diag[i] = jnp.where(lev == level, _dot_nt(blk(z, i), blk(z, i)), diag[i])
        else:
            hb = half // SCORE_BLK
            for i in range(nblk):
                if (i // hb) % 2 == 1:
                    base = (i // (2 * hb)) * 2 * hb
                    for j in range(base, base + hb):
                        off[(i, j)] = _dot_nt(blk(z, i), blk(z, j))
        partner = jnp.where(upper, pltpu.roll(tot, half, 0), pltpu.roll(tot, cl - half, 0))
        p = p + jnp.where(upper, partner, 0.0)
        tot = tot + partner
        half *= 2
        level += 1
    dg = jnp.sum(q * k, axis=-1, keepdims=True)
    rows = []
    for i in range(nblk):
        d_i = jnp.where(lev == -1, blk(dg, i), diag[i])
        rows.append(jnp.concatenate([off[(i, j)] for j in range(i)] + [d_i], axis=-1))
    return rows, p, tot


def _gla_heads(q_ref, k_ref, v_ref, g_ref, la_ref, lev_ref, gout_ref, st_ref, new_ref, fresh, valid, at_slot=None):
    cl = q_ref.shape[0]
    lev = lev_ref[...]
    row = lax.broadcasted_iota(jnp.int32, (cl, GLA_DK), 0)
    uppers = []
    half = 1
    while half < cl:
        uppers.append((row & half) != 0)
        half *= 2
    nlev = len(uppers)
    for h in range(GLA_HEADS):
        ks = slice(h * GLA_DK, (h + 1) * GLA_DK)
        vs = slice(h * GLA_DV, (h + 1) * GLA_DV)
        q = q_ref[:, ks] * (GLA_DK ** -0.5)
        k = k_ref[:, ks]
        hook = None if at_slot is None else (lambda level, h=h: at_slot(h * nlev + level))
        rows, p, tot = _gla_chunk(q, k, la_ref[:, ks] * LOG2E, lev, uppers, hook)
        vb = v_ref[:, vs].astype(BF16)
        st_old = jnp.where(fresh, 0.0, st_ref[h])
        qt = (q * jnp.exp2(p)).astype(BF16)
        intra = jnp.concatenate([_dot(r.astype(BF16), vb[:r.shape[1]]) for r in rows], axis=0)
        o = intra + _dot_nt(qt, st_old.astype(BF16))
        kt = (k * jnp.exp2(tot - p)).astype(BF16)
        st_new = st_old * jnp.exp2(tot[0:1, :]) + _dot_tn(vb, kt)
        st_ref[h] = jnp.where(valid, st_new, st_old)
        new_ref[:, RET_W + h * GLA_DV: RET_W + (h + 1) * GLA_DV] = (
            _rms(o) * gout_ref[:, vs] * _silu(g_ref[:, vs])).astype(new_ref.dtype)


def _mixer_body(dmat_ref, lg_ref, q_ref, k_ref, rv_ref, rg_ref, gq_ref, gk_ref, gv_ref, gr_ref, la_ref, lev_ref,
                gro_ref, ggo_ref, x_ref, gt_ref, sc_ref, sh_ref, gf_ref, w_ref,
                x1_ref, h_ref, sr_out, sg_out, s_ref, st_ref, mixa_ref, mixb_ref, *, nc, n_chunks):
    s = pl.program_id(0)
    valid = s < n_chunks
    c = jnp.minimum(s, n_chunks - 1) % nc
    fresh = c == 0

    @pl.when(s == 0)
    def _():
        mixb_ref[...] = jnp.zeros_like(mixb_ref)

    def step(old_ref, new_ref):
        old = old_ref[...]
        d = x_ref.shape[1]
        nblk = d // PROJ_BLK
        ssq = [jnp.zeros((x_ref.shape[0], 1), F32)]

        def project(blocks):
            for cb in blocks:
                cs = slice(cb * PROJ_BLK, (cb + 1) * PROJ_BLK)
                x1 = x_ref[:, cs] + gt_ref[0][:, cs] * _dot(old, w_ref[:, cs])
                x1_ref[:, cs] = x1
                ssq[0] = ssq[0] + jnp.sum(x1 * x1, axis=-1, keepdims=True)

        nslots = GLA_HEADS * (x_ref.shape[0].bit_length() - 1)
        due = {(cb * nslots) // nblk: cb for cb in range(nblk)}
        _gla_heads(gq_ref, gk_ref, gv_ref, gr_ref, la_ref, lev_ref, ggo_ref, st_ref, new_ref, fresh, valid,
                   at_slot=lambda slot: project([due[slot]]) if slot in due else None)
        _ret_heads(dmat_ref, lg_ref, q_ref, k_ref, rv_ref, rg_ref, gro_ref, s_ref, new_ref, fresh, valid)
        inv = lax.rsqrt(ssq[0] * (1.0 / d) + EPS)
        gain = gf_ref[...] * (1.0 + sc_ref[0])
        for cb in range(nblk):
            cs = slice(cb * PROJ_BLK, (cb + 1) * PROJ_BLK)
            h_ref[:, cs] = (x1_ref[:, cs] * inv * gain[:, cs] + sh_ref[0][:, cs]).astype(h_ref.dtype)

    @pl.when(s % 2 == 0)
    def _():
        step(mixb_ref, mixa_ref)

    @pl.when(s % 2 == 1)
    def _():
        step(mixa_ref, mixb_ref)

    @pl.when(valid & (c == nc - 1))
    def _():
        sr_out[0] = s_ref[...]
        for h in range(GLA_HEADS):
            sg_out[0, h] = st_ref[h].T


def _mixer_prompt(qk, rest, la, x, gt, sc, sh, g_ret_out, g_gla_out, g_ffn, w_out_b, *, batch, seq, chunk):
    m, d = x.shape
    nc = seq // chunk
    n_chunks = batch * nc
    lg = jnp.broadcast_to(_ret_log_gamma()[:, None, None], (RET_HEADS, SUBLANES, LANES))
    lev = jnp.asarray(_gla_level_map(SCORE_BLK))
    gqk_w = GLA_HEADS * GLA_DK
    cur = lambda s: jnp.minimum(s, n_chunks - 1)
    prv = lambda s: jnp.maximum(s - 1, 0)
    rowc = lambda w, blk: pl.BlockSpec((chunk, w), lambda s: (cur(s), blk))
    const = lambda shape: pl.BlockSpec(shape, lambda s: (0,) * len(shape))
    mod_spec = pl.BlockSpec((1, 1, d), lambda s: (prv(s) // nc, 0, 0))
    return pl.pallas_call(
        functools.partial(_mixer_body, nc=nc, n_chunks=n_chunks),
        grid=(n_chunks + 1,),
        in_specs=[const((RET_HEADS, chunk, chunk)), const((RET_HEADS, SUBLANES, LANES)),
                  rowc(RET_W, 0), rowc(RET_W, 1),
                  rowc(RET_W, R_RV // RET_W), rowc(RET_W, R_RG // RET_W),
                  rowc(gqk_w, R_GQ // gqk_w), rowc(gqk_w, R_GK // gqk_w),
                  rowc(GLA_W, R_GV // GLA_W), rowc(GLA_W, R_GR // GLA_W),
                  rowc(gqk_w, 0),
                  const((SCORE_BLK, SCORE_BLK)), const((1, RET_W)), const((1, GLA_W)),
                  pl.BlockSpec((chunk, d), lambda s: (prv(s), 0)),
                  mod_spec, mod_spec, mod_spec,
                  const((1, d)),
                  pl.BlockSpec((RET_W + GLA_W, d), lambda s: (0, 0), pipeline_mode=pl.Buffered(1))],
        out_specs=[pl.BlockSpec((chunk, d), lambda s: (prv(s), 0)),
                   pl.BlockSpec((chunk, d), lambda s: (prv(s), 0)),
                   pl.BlockSpec((1, RET_HEADS, RET_DK, RET_DV), lambda s: (cur(s) // nc, 0, 0, 0)),
                   pl.BlockSpec((1, GLA_HEADS, GLA_DK, GLA_DV), lambda s: (cur(s) // nc, 0, 0, 0))],
        out_shape=[jax.ShapeDtypeStruct((m, d), F32),
                   jax.ShapeDtypeStruct((m, d), BF16),
                   jax.ShapeDtypeStruct((batch, RET_HEADS, RET_DK, RET_DV), F32),
                   jax.ShapeDtypeStruct((batch, GLA_HEADS, GLA_DK, GLA_DV), F32)],
        scratch_shapes=[pltpu.VMEM((RET_HEADS, RET_DK, RET_DV), F32),
                        pltpu.VMEM((GLA_HEADS, GLA_DV, GLA_DK), F32),
                        pltpu.VMEM((chunk, RET_W + GLA_W), BF16),
                        pltpu.VMEM((chunk, RET_W + GLA_W), BF16)],
        compiler_params=_params(("arbitrary",)),
        name="mixer_prompt",
    )(_ret_decay_matrix(chunk), lg, qk, qk, rest, rest, rest, rest, rest, rest, la, lev,
      g_ret_out.reshape(1, RET_W), g_gla_out.reshape(1, GLA_W), x, gt, sc, sh, g_ffn.reshape(1, d), w_out_b)


def _columns(rows):
    n = rows[0].shape[1]
    pad = (-len(rows)) % 8
    stack = jnp.concatenate(rows + [jnp.zeros((pad, n), F32)] if pad else rows, axis=0)
    return stack.T


def _state_body(lg_ref, qk_ref, rest_ref, la_ref, sr_ref, sg_ref, gro_ref, ggo_ref, o_ref, sr_out, sg_out):
    nb = qk_ref.shape[0]
    for i in range(nb):
        qkrow = qk_ref[i]
        rrow = rest_ref[i]
        larow = la_ref[i]
        rcols = _columns([qkrow[:, j * RET_DK:(j + 1) * RET_DK] for j in range(2 * RET_HEADS)])
        gq = [rrow[:, R_GQ + h * GLA_DK: R_GQ + (h + 1) * GLA_DK] * (GLA_DK ** -0.5) for h in range(GLA_HEADS)]
        gk = [rrow[:, R_GK + h * GLA_DK: R_GK + (h + 1) * GLA_DK] for h in range(GLA_HEADS)]
        ga = [jnp.exp(larow[:, h * GLA_DK:(h + 1) * GLA_DK]) for h in range(GLA_HEADS)]
        gcols = _columns(gq + gk + ga)
        outs = []
        for h in range(RET_HEADS):
            v = rrow[:, R_RV + h * RET_DV: R_RV + (h + 1) * RET_DV]
            gate = rrow[:, R_RG + h * RET_DV: R_RG + (h + 1) * RET_DV]
            gamma = jnp.exp(lg_ref[h][:1, :1])
            qc = rcols[:, h:h + 1]
            kc = rcols[:, RET_HEADS + h:RET_HEADS + h + 1]
            s_new = sr_ref[0, i, h] * gamma + kc * v
            sr_out[0, i, h] = s_new
            o = jnp.sum(qc * s_new, axis=0, keepdims=True)
            outs.append(_rms(o) * gro_ref[:, h * RET_DV:(h + 1) * RET_DV] * _silu(gate))
        for h in range(GLA_HEADS):
            v = rrow[:, R_GV + h * GLA_DV: R_GV + (h + 1) * GLA_DV]
            gate = rrow[:, R_GR + h * GLA_DV: R_GR + (h + 1) * GLA_DV]
            qc = gcols[:, h:h + 1]
            kc = gcols[:, GLA_HEADS + h:GLA_HEADS + h + 1]
            ac = gcols[:, 2 * GLA_HEADS + h:2 * GLA_HEADS + h + 1]
            s_new = sg_ref[0, i, h] * ac + kc * v
            sg_out[0, i, h] = s_new
            o = jnp.sum(qc * s_new, axis=0, keepdims=True)
            outs.append(_rms(o) * ggo_ref[:, h * GLA_DV:(h + 1) * GLA_DV] * _silu(gate))
        o_ref[i] = jnp.concatenate(outs, axis=-1)


def _outproj_body(oa_ref, ob_ref, x_ref, gt_ref, sc_ref, sh_ref, g_ref, w_ref, x1_ref, h_ref):
    ka = oa_ref.shape[1]
    mix = _dot(oa_ref[...].astype(BF16), w_ref[:ka, :]) + _dot(ob_ref[...].astype(BF16), w_ref[ka:, :])
    x1 = x_ref[...] + gt_ref[0] * mix
    x1_ref[...] = x1
    h = _rms(x1) * g_ref[...]
    h_ref[...] = (h * (1.0 + sc_ref[0]) + sh_ref[0]).astype(BF16)


def _outproj(o_a, o_b, x, gt, sc, sh, g_ffn, w_out_bf16, *, tm, rows_per_mod):
    m, d = x.shape
    ka, kb = o_a.shape[1], o_b.shape[1]
    mod_spec = pl.BlockSpec((1, gt.shape[1], d), lambda i: ((i * tm) // rows_per_mod, 0, 0))
    return pl.pallas_call(
        _outproj_body,
        grid=(m // tm,),
        in_specs=[pl.BlockSpec((tm, ka), lambda i: (i, 0)),
                  pl.BlockSpec((tm, kb), lambda i: (i, 0)),
                  pl.BlockSpec((tm, d), lambda i: (i, 0)),
                  mod_spec, mod_spec, mod_spec,
                  pl.BlockSpec((1, d), lambda i: (0, 0)),
                  pl.BlockSpec((ka + kb, d), lambda i: (0, 0), pipeline_mode=pl.Buffered(1))],
        out_specs=[pl.BlockSpec((tm, d), lambda i: (i, 0)),
                   pl.BlockSpec((tm, d), lambda i: (i, 0))],
        out_shape=[jax.ShapeDtypeStruct((m, d), F32),
                   jax.ShapeDtypeStruct((m, d), BF16)],
        compiler_params=_params(("arbitrary",)),
        name="out_proj",
    )(o_a, o_b, x, gt, sc, sh, g_ffn.reshape(1, d), w_out_bf16)


HALO = SUBLANES


def _ffn_up_body(h_ref, wa_ref, wb_ref, cwa_ref, cwb_ref, cba_ref, cbb_ref, g_ref, cs_ref, win_ref, tail_ref, *, tiles_per_seq):
    i, f = pl.program_id(0), pl.program_id(1)
    hb = h_ref[...]
    tm = hb.shape[0]
    first = (i % tiles_per_seq) == 0
    ucs = []
    for part, (w_ref, cw_ref, cb_ref) in enumerate(((wa_ref, cwa_ref, cba_ref), (wb_ref, cwb_ref, cbb_ref))):
        u = _dot(hb, w_ref[...])
        prev = jnp.where(first, 0.0, tail_ref[part, f])
        pieces = []
        for c in range(u.shape[1] // LANES):
            lc = slice(c * LANES, (c + 1) * LANES)
            win_ref[part, c, 0:HALO, :] = prev[:, lc]
            win_ref[part, c, HALO:HALO + tm, :] = u[:, lc]
            cw, cb = cw_ref[:, lc], cb_ref[:, lc]
            pieces.append(cb + cw[0:1] * win_ref[part, c, HALO - 2:HALO - 2 + tm, :]
                          + cw[1:2] * win_ref[part, c, HALO - 1:HALO - 1 + tm, :] + cw[2:3] * u[:, lc])
        ucs.append(jnp.concatenate(pieces, axis=-1))
        tail_ref[part, f] = u[tm - HALO:]
        for r in range(CONV_W - 1):
            cs_ref[0, 0, r, part:part + 1, :] = u[tm - (CONV_W - 1) + r: tm - (CONV_W - 1) + r + 1]
    g_ref[...] = (_silu(ucs[0]) * ucs[1]).astype(g_ref.dtype)


def _ffn_up_state_body(h_ref, wa_ref, wb_ref, cwa_ref, cwb_ref, cba_ref, cbb_ref,
                       lg_ref, qk_ref, rest_ref, la_ref, sr_ref, sg_ref, gro_ref, ggo_ref, wd_ref,
                       g_ref, cs_ref, o_ref, sr_out, sg_out, wdb_ref, win_ref, tail_ref, *, tiles_per_seq):
    _ffn_up_body(h_ref, wa_ref, wb_ref, cwa_ref, cwb_ref, cba_ref, cbb_ref, g_ref, cs_ref, win_ref, tail_ref,
                 tiles_per_seq=tiles_per_seq)
    _state_body(lg_ref, qk_ref, rest_ref, la_ref, sr_ref, sg_ref, gro_ref, ggo_ref, o_ref, sr_out, sg_out)
    wdb_ref[...] = wd_ref[...].astype(wdb_ref.dtype)


def _ffn_up_prompt(h, w_up, conv_w, conv_b, qk_s, rest_s, la_s, state_ret, state_gla, g_ret_out, g_gla_out, w_down,
                   *, batch, seq, tm, tf, nb, cast_rows):
    m, d = h.shape
    ff = w_up.shape[1] // 2
    nf = ff // tf
    cb = conv_b.reshape(1, 2 * ff)
    tps = seq // tm
    ms = qk_s.shape[0]
    nblk = ms // nb
    ncast = ff // cast_rows
    assert (m // tm) * nf >= nblk + ncast
    mix = RET_W + GLA_W
    lg = jnp.broadcast_to(_ret_log_gamma()[:, None, None], (RET_HEADS, SUBLANES, LANES))
    sblk = lambda i, f: jnp.minimum(i * nf + f, nblk - 1)
    ret_spec = pl.BlockSpec((1, nb, RET_HEADS, RET_DK, RET_DV), lambda i, f: (0, sblk(i, f), 0, 0, 0))
    gla_spec = pl.BlockSpec((1, nb, GLA_HEADS, GLA_DK, GLA_DV), lambda i, f: (0, sblk(i, f), 0, 0, 0))
    row_spec = lambda w: pl.BlockSpec((nb, 1, w), lambda i, f: (sblk(i, f), 0, 0))
    cast_spec = pl.BlockSpec((cast_rows, d), lambda i, f: (jnp.clip(i * nf + f - nblk, 0, ncast - 1), 0))
    body = functools.partial(_ffn_up_state_body, tiles_per_seq=tps)
    gate, tails, o_s, s_ret, s_gla, w_down_b = pl.pallas_call(
        body,
        grid=(m // tm, nf),
        in_specs=[pl.BlockSpec((tm, d), lambda i, f: (i, 0)),
                  pl.BlockSpec((d, tf), lambda i, f: (0, f)),
                  pl.BlockSpec((d, tf), lambda i, f: (0, nf + f)),
                  pl.BlockSpec((CONV_W, tf), lambda i, f: (0, f)),
                  pl.BlockSpec((CONV_W, tf), lambda i, f: (0, nf + f)),
                  pl.BlockSpec((1, tf), lambda i, f: (0, f)),
                  pl.BlockSpec((1, tf), lambda i, f: (0, nf + f)),
                  pl.BlockSpec((RET_HEADS, SUBLANES, LANES), lambda i, f: (0, 0, 0)),
                  row_spec(QK_W), row_spec(REST_W), row_spec(la_s.shape[1]),
                  ret_spec, gla_spec,
                  pl.BlockSpec((1, RET_W), lambda i, f: (0, 0)),
                  pl.BlockSpec((1, GLA_W), lambda i, f: (0, 0)),
                  cast_spec],
        out_specs=[pl.BlockSpec((tm, tf), lambda i, f: (i, f)),
                   pl.BlockSpec((1, 1, CONV_W - 1, 2, tf), lambda i, f: (i // tps, i % tps, 0, 0, f)),
                   row_spec(mix), ret_spec, gla_spec, cast_spec],
        out_shape=[jax.ShapeDtypeStruct((m, ff), BF16),
                   jax.ShapeDtypeStruct((batch, tps, CONV_W - 1, 2, ff), F32),
                   jax.ShapeDtypeStruct((ms, 1, mix), F32),
                   jax.ShapeDtypeStruct(state_ret.shape, F32),
                   jax.ShapeDtypeStruct(state_gla.shape, F32),
                   jax.ShapeDtypeStruct(w_down.shape, BF16)],
        scratch_shapes=[pltpu.VMEM((2, tf // LANES, HALO + tm, LANES), F32), pltpu.VMEM((2, nf, HALO, tf), F32)],
        compiler_params=_params(("arbitrary", "arbitrary")),
        name="ffn_up_prompt",
    )(h, w_up, w_up, conv_w, conv_w, cb, cb,
      lg, qk_s.astype(F32).reshape(ms, 1, QK_W), rest_s.reshape(ms, 1, REST_W), la_s.reshape(ms, 1, la_s.shape[1]),
      state_ret, state_gla, g_ret_out.reshape(1, RET_W), g_gla_out.reshape(1, GLA_W), w_down)
    return gate, tails[:, tps - 1], o_s.reshape(ms, mix), s_ret, s_gla, w_down_b


def _ffn_up_step_body(h_ref, w_ref, cw_ref, cb_ref, st_ref, g_ref, cs_ref, uca_ref, *, nf):
    j = pl.program_id(0)
    u = _dot(h_ref[...], w_ref[...])
    s1 = st_ref[0, :, 1, :]
    cw = cw_ref[...]
    uc = cb_ref[...] + cw[0:1] * st_ref[0, :, 0, :] + cw[1:2] * s1 + cw[2:3] * u
    cs_ref[0, :, 0, :] = s1
    cs_ref[0, :, 1, :] = u

    @pl.when(j < nf)
    def _():
        uca_ref[j] = uc

    @pl.when(j >= nf)
    def _():
        g_ref[...] = (_silu(uca_ref[j - nf]) * uc).astype(g_ref.dtype)


def _ffn_up_step(h, w_up, conv_w, conv_b, state_conv, *, tf):
    m, d = h.shape
    ff = w_up.shape[1] // 2
    nf = ff // tf
    st_spec = pl.BlockSpec((1, m, CONV_W - 1, tf), lambda j: (0, 0, 0, j))
    return pl.pallas_call(
        functools.partial(_ffn_up_step_body, nf=nf),
        grid=(2 * nf,),
        in_specs=[pl.BlockSpec((m, d), lambda j: (0, 0)),
                  pl.BlockSpec((d, tf), lambda j: (0, j)),
                  pl.BlockSpec((CONV_W, tf), lambda j: (0, j)),
                  pl.BlockSpec((1, tf), lambda j: (0, j)),
                  st_spec],
        out_specs=[pl.BlockSpec((m, tf), lambda j: (0, jnp.maximum(j - nf, 0))), st_spec],
        out_shape=[jax.ShapeDtypeStruct((m, ff), BF16),
                   jax.ShapeDtypeStruct(state_conv.shape, F32)],
        scratch_shapes=[pltpu.VMEM((nf, m, tf), F32)],
        compiler_params=_params(("arbitrary",)),
        name="ffn_up_step",
    )(h, w_up, conv_w, conv_b.reshape(1, 2 * ff), state_conv)


def _ffn_down_body(g_ref, w_ref, x1_ref, gt_ref, gf_ref, y_ref, xs_ref, inv_ref):
    k = pl.program_id(1)
    last = pl.num_programs(1) - 1
    d = y_ref.shape[1]
    xb = x1_ref.shape[1]
    xs_ref[k] = x1_ref[...]

    @pl.when(k == 0)
    def _():
        for cb in range(d // PROJ_BLK):
            cs = slice(cb * PROJ_BLK, (cb + 1) * PROJ_BLK)
            y_ref[:, cs] = _dot(g_ref[...], w_ref[:, cs])

    @pl.when((k > 0) & (k < last))
    def _():
        for cb in range(d // PROJ_BLK):
            cs = slice(cb * PROJ_BLK, (cb + 1) * PROJ_BLK)
            y_ref[:, cs] += _dot(g_ref[...], w_ref[:, cs])

    @pl.when(k == last)
    def _():
        ssq = jnp.zeros((y_ref.shape[0], 1), F32)
        for cb in range(d // PROJ_BLK):
            cs = slice(cb * PROJ_BLK, (cb + 1) * PROJ_BLK)
            x1 = xs_ref[(cb * PROJ_BLK) // xb, :, (cb * PROJ_BLK) % xb: (cb * PROJ_BLK) % xb + PROJ_BLK]
            x2 = x1 + gt_ref[0][:, cs] * (y_ref[:, cs] + _dot(g_ref[...], w_ref[:, cs]))
            y_ref[:, cs] = x2
            ssq = ssq + jnp.sum(x2 * x2, axis=-1, keepdims=True)
        inv_ref[...] = lax.rsqrt(ssq * (1.0 / d) + EPS)

        rg = min(y_ref.shape[0], NORM_ROWS)

        def norm_rows(r, carry):
            rows = pl.ds(pl.multiple_of(r * rg, rg), rg)
            y_ref[rows, :] = y_ref[rows, :] * inv_ref[rows, :] * gf_ref[...]
            return carry

        lax.fori_loop(0, y_ref.shape[0] // rg, norm_rows, 0)


def _ffn_down(g, w_down, x1, gt, g_final, *, tm, tk, rows_per_mod):
    m, d = x1.shape
    ff = g.shape[1]
    nk = ff // tk
    assert nk >= 2
    xb = d // nk
    assert d % nk == 0 and xb % PROJ_BLK == 0
    return pl.pallas_call(
        _ffn_down_body,
        grid=(m // tm, nk),
        in_specs=[pl.BlockSpec((tm, tk), lambda i, k: (i, k)),
                  pl.BlockSpec((tk, d), lambda i, k: (k, 0)),
                  pl.BlockSpec((tm, xb), lambda i, k: (i, k)),
                  pl.BlockSpec((1, gt.shape[1], d), lambda i, k: ((i * tm) // rows_per_mod, 0, 0)),
                  pl.BlockSpec((1, d), lambda i, k: (0, 0))],
        out_specs=pl.BlockSpec((tm, d), lambda i, k: (i, 0)),
        out_shape=jax.ShapeDtypeStruct((m, d), F32),
        scratch_shapes=[pltpu.VMEM((nk, tm, xb), F32), pltpu.VMEM((tm, 1), F32)],
        compiler_params=_params(("arbitrary", "arbitrary")),
        name="ffn_down",
    )(g, w_down, x1, gt, g_final.reshape(1, d))


def _rope_tables(pos):
    half = RET_DK // 2
    inv = ROPE_THETA ** (-jnp.arange(half, dtype=F32) / half)
    ang = pos.astype(F32)[:, None] * inv[None, :]
    return jnp.cos(ang), jnp.sin(ang)


def kernel(x_prompt, x_sample, c_prompt, c_sample, state_ret, state_gla, state_conv, w_ada, b_ada, g_attn, w_in, w_a2, b_a2, g_ret_out, g_gla_out, w_out, g_ffn, w_up, conv_w, conv_b, w_down, g_final):
    bp, t_p, d = x_prompt.shape
    bs, t_s, _ = x_sample.shape
    assert t_s == 1 and w_ada.shape[0] == 1
    mp = bp * t_p
    w_ada, b_ada, g_attn, w_in, w_a2, b_a2, g_ret_out, g_gla_out, w_out, g_ffn, w_up, conv_w, conv_b, w_down = (
        a[0] for a in (w_ada, b_ada, g_attn, w_in, w_a2, b_a2, g_ret_out, g_gla_out, w_out, g_ffn, w_up, conv_w, conv_b, w_down))

    mod = _ada(jnp.concatenate([c_prompt, c_sample], axis=0), w_ada, b_ada)
    sh1p, sc1p, gt1p, sh2p, sc2p, gt2p = (mod[:bp, i * d:(i + 1) * d].reshape(bp, 1, d) for i in range(6))
    sh1s, sc1s, gt1s, sh2s, sc2s, gt2s = (mod[bp:, i * d:(i + 1) * d].reshape(1, bs, d) for i in range(6))

    cos_p, sin_p = _rope_tables(jnp.arange(t_p, dtype=jnp.int32))
    cos_s, sin_s = (jnp.broadcast_to(t, (bs, RET_DK // 2)) for t in _rope_tables(PAST_LEN + jnp.arange(t_s, dtype=jnp.int32)))
    w_out_b = w_out.astype(BF16)
    w_in_t = w_in.T
    w_gate_t = w_in_t[IN_MAIN:]

    xp = x_prompt.reshape(mp, d)
    xs = x_sample.reshape(bs, d)
    qk_s, rest_s, la_s, w_in_b = _inproj(xs, sc1s, sh1s, g_attn, w_in_t, w_gate_t, w_a2, b_a2, cos_s, sin_s,
                                         tm=bs, rows_per_mod=bs, cast_w=True)
    qk_p, rest_p, la_p, w_up_b = _inproj(xp, sc1p, sh1p, g_attn, w_in_b, w_gate_t, w_a2, b_a2, cos_p, sin_p,
                                         tm=ROW_TILE, rows_per_mod=t_p, cast_src=w_up)
    x1_p, h2_p, s_ret_p, s_gla_p = _mixer_prompt(qk_p, rest_p, la_p, xp, gt1p, sc2p, sh2p, g_ret_out, g_gla_out, g_ffn,
                                                 w_out_b, batch=bp, seq=t_p, chunk=CHUNK)
    g_p, cs_p, o_s, s_ret_s, s_gla_s, w_down_b = _ffn_up_prompt(
        h2_p, w_up_b, conv_w, conv_b, qk_s, rest_s, la_s, state_ret, state_gla, g_ret_out, g_gla_out, w_down,
        batch=bp, seq=t_p, tm=ROW_TILE, tf=FF_COL_TILE, nb=STATE_SEQS, cast_rows=CAST_ROWS)
    y_p = _ffn_down(g_p, w_down_b, x1_p, gt2p, g_final, tm=ROW_TILE, tk=DOWN_K_TILE, rows_per_mod=t_p)

    x1_s, h2_s = _outproj(o_s[:, :RET_W], o_s[:, RET_W:], xs, gt1s, sc2s, sh2s, g_ffn, w_out_b, tm=bs, rows_per_mod=bs)
    ff = w_down.shape[0]
    g_s, cs_s = _ffn_up_step(h2_s, w_up_b, conv_w, conv_b, state_conv, tf=ff // 4)
    y_s = _ffn_down(g_s, w_down_b, x1_s, gt2s, g_final, tm=bs, tk=ff // 2, rows_per_mod=bs)

    return (y_p.reshape(bp, t_p, d), y_s.reshape(bs, t_s, d),
            s_ret_p[None], s_ret_s, s_gla_p[None], s_gla_s,
            cs_p.reshape(1, bp, CONV_W - 1, -1), cs_s)
```

```python
import functools

import numpy as np
import jax
import jax.numpy as jnp
from jax import lax
from jax.experimental import pallas as pl
from jax.experimental.pallas import tpu as pltpu

F32 = jnp.float32
BF16 = jnp.bfloat16

RET_HEADS = 4
RET_DK = 256
RET_DV = 256
GLA_HEADS = 4
GLA_DK = 128
GLA_DV = 256
GLA_RANK = 16
GLA_TAU = 16.0
ROPE_THETA = 10000.0
PAST_LEN = 16384
CONV_W = 3
EPS = 1e-6

RET_W = RET_HEADS * RET_DV
GLA_W = GLA_HEADS * GLA_DV
QK_W = 2 * RET_HEADS * RET_DK
R_RV = 0
R_RG = R_RV + RET_W
R_GQ = R_RG + RET_W
R_GK = R_GQ + GLA_HEADS * GLA_DK
R_GV = R_GK + GLA_HEADS * GLA_DK
R_GR = R_GV + GLA_W
REST_W = R_GR + GLA_W
IN_MAIN = QK_W + REST_W

V7X_VMEM_BYTES = 64 * 1024 * 1024
V7X_MXU_WIDTH = 256
LANES = 128
SUBLANES = 8
VMEM_LIMIT_BYTES = V7X_VMEM_BYTES - 8 * 1024 * 1024
PROJ_BLK = V7X_MXU_WIDTH

ROW_TILE = 1024
IN_COL_TILE = 1024
FF_COL_TILE = 512
DOWN_K_TILE = 1408
CHUNK = 256
STATE_SEQS = 2
CAST_ROWS = 256
CAST_COLS = 256
NORM_ROWS = 128


def _params(semantics):
    return pltpu.CompilerParams(dimension_semantics=semantics, vmem_limit_bytes=VMEM_LIMIT_BYTES)


def _dot(a, b):
    return jnp.dot(a, b, preferred_element_type=F32)


def _dot_nt(a, b):
    return lax.dot_general(a, b, (((1,), (1,)), ((), ())), preferred_element_type=F32)


def _dot_tn(a, b):
    return lax.dot_general(a, b, (((0,), (0,)), ((), ())), preferred_element_type=F32)


def _silu(x):
    return x * jax.nn.sigmoid(x)


def _rms(x):
    return x * lax.rsqrt(jnp.mean(x * x, axis=-1, keepdims=True) + EPS)


def _mod_spec(mod, d, tm, rows_per_mod):
    arr, col = mod
    rows = 1 if arr.shape[1] == 1 else tm
    return pl.BlockSpec((1, rows, d), lambda i, *_: ((i * tm) // rows_per_mod, 0, col))


def _ada_body(c_ref, w_ref, b_ref, o_ref):
    s = _silu(c_ref[...]).astype(BF16)
    o_ref[...] = _dot(s, w_ref[...].astype(BF16)) + b_ref[...]


def _ada(c_all, w_ada, b_ada, tn=FF_COL_TILE):
    r, d = c_all.shape
    n = w_ada.shape[1]
    return pl.pallas_call(
        _ada_body,
        grid=(n // tn,),
        in_specs=[pl.BlockSpec((r, d), lambda j: (0, 0)),
                  pl.BlockSpec((d, tn), lambda j: (0, j)),
                  pl.BlockSpec((1, tn), lambda j: (0, j))],
        out_specs=pl.BlockSpec((r, tn), lambda j: (0, j)),
        out_shape=jax.ShapeDtypeStruct((r, n), F32),
        compiler_params=_params(("arbitrary",)),
        name="ada_mod",
    )(c_all, w_ada, b_ada.reshape(1, n))


def _inproj_body(*refs, n_qk, cast_w, rider):
    refs = list(refs)
    x_ref, sc_ref, sh_ref, g_ref, w_ref, wga_ref, wa2_ref, ba2_ref, cos_ref, sin_ref = refs[:10]
    del refs[:10]
    src_ref = refs.pop(0) if rider else None
    qk_ref, rest_ref, la_ref = refs[:3]
    del refs[:3]
    wcast_ref = refs.pop(0) if cast_w else None
    dst_ref = refs.pop(0) if rider else None
    (h_ref,) = refs
    j = pl.program_id(1)

    def ride():
        if rider:
            dst_ref[...] = src_ref[...].astype(dst_ref.dtype)

    @pl.when(j == 0)
    def _():
        gain = g_ref[...] * (1.0 + sc_ref[0])
        hb = (_rms(x_ref[...]) * gain + sh_ref[0]).astype(BF16)
        h_ref[...] = hb
        ga = _dot_nt(hb, wga_ref[...].astype(BF16))
        z = _dot(ga.astype(BF16), wa2_ref[...].astype(BF16)) + ba2_ref[...]
        la_ref[...] = (jnp.minimum(z, 0.0) - jnp.log1p(jnp.exp(-jnp.abs(z)))) * (1.0 / GLA_TAU)

    if cast_w:
        w_tile = w_ref[...].T.astype(BF16)
        wcast_ref[...] = w_tile
        weight = lambda: w_tile
    else:
        weight = lambda: w_ref[...]

    @pl.when(j < n_qk)
    def _():
        ride()
        acc = _dot(h_ref[...], weight())
        cos, sin = cos_ref[...], sin_ref[...]
        half = RET_DK // 2
        scale = jnp.where(j >= n_qk // 2, RET_DK ** -0.5, 1.0)
        outs = []
        for hh in range(acc.shape[1] // RET_DK):
            x1 = acc[:, hh * RET_DK: hh * RET_DK + half]
            x2 = acc[:, hh * RET_DK + half: (hh + 1) * RET_DK]
            outs += [x1 * cos - x2 * sin, x1 * sin + x2 * cos]
        qk_ref[...] = (jnp.concatenate(outs, axis=-1) * scale).astype(qk_ref.dtype)

    @pl.when(j >= n_qk)
    def _():
        ride()
        rest_ref[...] = _dot(h_ref[...], weight())


def _inproj(x, sc, sh, g_attn, w, w_gate_t, w_a2, b_a2, cos, sin, *, tm, rows_per_mod, cast_w=False, cast_src=None,
            tn=IN_COL_TILE):
    m, d = x.shape
    nq = w_a2.shape[1]
    n_qk = QK_W // tn
    tab_tiles = cos.shape[0] // tm
    n_j = IN_MAIN // tn
    once = pl.Buffered(1)
    sc_spec, sh_spec = (_mod_spec(v, d, tm, rows_per_mod) for v in (sc, sh))
    tab_spec = pl.BlockSpec((tm, RET_DK // 2), lambda i, j: (i % tab_tiles, 0), pipeline_mode=once)
    out_specs = [pl.BlockSpec((tm, tn), lambda i, j: (i, jnp.minimum(j, n_qk - 1))),
                 pl.BlockSpec((tm, tn), lambda i, j: (i, jnp.maximum(j - n_qk, 0))),
                 pl.BlockSpec((tm, nq), lambda i, j: (i, 0), pipeline_mode=once)]
    out_shape = [jax.ShapeDtypeStruct((m, QK_W), BF16),
                 jax.ShapeDtypeStruct((m, REST_W), F32),
                 jax.ShapeDtypeStruct((m, nq), F32)]
    if cast_w:
        assert m == tm
        w_spec = pl.BlockSpec((tn, d), lambda i, j: (j, 0))
        out_specs.append(pl.BlockSpec((d, tn), lambda i, j: (0, j)))
        out_shape.append(jax.ShapeDtypeStruct((d, IN_MAIN), BF16))
    else:
        w_spec = pl.BlockSpec((d, tn), lambda i, j: (0, j))
    in_specs = [pl.BlockSpec((tm, d), lambda i, j: (i, 0)),
                sc_spec, sh_spec,
                pl.BlockSpec((1, d), lambda i, j: (0, 0)),
                w_spec,
                pl.BlockSpec((GLA_RANK, d), lambda i, j: (0, 0)),
                pl.BlockSpec((GLA_RANK, nq), lambda i, j: (0, 0)),
                pl.BlockSpec((1, nq), lambda i, j: (0, 0)),
                tab_spec, tab_spec]
    args = [x, sc[0], sh[0], g_attn.reshape(1, d), w, w_gate_t, w_a2, b_a2.reshape(1, nq), cos, sin]
    if cast_src is not None:
        ncast = cast_src.shape[1] // CAST_COLS
        assert (m // tm) * n_j >= ncast
        ride_spec = pl.BlockSpec((cast_src.shape[0], CAST_COLS), lambda i, j: (0, jnp.minimum(i * n_j + j, ncast - 1)))
        in_specs.append(ride_spec)
        args.append(cast_src)
        out_specs.append(ride_spec)
        out_shape.append(jax.ShapeDtypeStruct(cast_src.shape, BF16))
    return pl.pallas_call(
        functools.partial(_inproj_body, n_qk=n_qk, cast_w=cast_w, rider=cast_src is not None),
        grid=(m // tm, n_j),
        in_specs=in_specs,
        out_specs=out_specs,
        out_shape=out_shape,
        scratch_shapes=[pltpu.VMEM((tm, d), BF16)],
        compiler_params=_params(("arbitrary", "arbitrary")),
        name="in_proj",
    )(*args)


def _ret_log_gamma():
    return np.log1p(-np.exp2(-5.0 - np.arange(RET_HEADS, dtype=np.float64)))


def _ret_log_gamma_tile():
    return jnp.asarray(np.broadcast_to(_ret_log_gamma()[:, None, None], (RET_HEADS, SUBLANES, LANES)), F32)


def _ret_decay_matrix(chunk):
    idx = np.arange(chunk, dtype=np.float64)
    rel = idx[:, None] - idx[None, :]
    lg = _ret_log_gamma()
    return jnp.asarray(np.where(rel[None] >= 0, np.exp(np.maximum(rel, 0.0)[None] * lg[:, None, None]), 0.0), F32)


def _ret_heads(dmat_ref, lg_ref, q_ref, k_ref, v_ref, g_ref, gout_ref, s_ref, new_ref, fresh, valid):
    cl = q_ref.shape[0]
    idx = lax.broadcasted_iota(jnp.int32, (cl, 1), 0).astype(F32)
    for h in range(RET_HEADS):
        sl = slice(h * RET_DV, (h + 1) * RET_DV)
        lg = lg_ref[h][:1, :1]
        q_dec = jnp.exp((idx + 1.0) * lg)
        k_dec = jnp.exp((cl - 1.0 - idx) * lg)
        c_dec = jnp.exp(cl * lg)
        qb, kb = q_ref[:, sl], k_ref[:, sl]
        v = v_ref[:, sl]
        vb = v.astype(BF16)
        s_old = jnp.where(fresh, 0.0, s_ref[h])
        scores = _dot_nt(qb, kb) * dmat_ref[h]
        o = _dot(scores.astype(BF16), vb) + _dot(qb, s_old.astype(BF16)) * q_dec
        s_new = s_old * c_dec + _dot_tn(kb, (v * k_dec).astype(BF16))
        s_ref[h] = jnp.where(valid, s_new, s_old)
        new_ref[:, sl] = (_rms(o) * gout_ref[:, sl] * _silu(g_ref[:, sl])).astype(new_ref.dtype)


def _gla_level_map(c):
    t = np.arange(c)[:, None]
    s = np.arange(c)[None, :]
    x = np.bitwise_xor(t, s)
    lev = np.floor(np.log2(np.maximum(x, 1))).astype(np.int32)
    lev = np.where(t == s, -1, lev)
    lev = np.where(t < s, -2, lev)
    return lev.astype(np.int32)


SCORE_BLK = 128
LOG2E = 1.4426950408889634


def _gla_chunk(q, k, la2, lev, uppers, at_level=None):
    cl = q.shape[0]
    nblk = cl // SCORE_BLK
    blk = lambda x, i: x[i * SCORE_BLK:(i + 1) * SCORE_BLK]
    p = la2
    tot = la2
    diag = [jnp.zeros((SCORE_BLK, SCORE_BLK), F32) for _ in range(nblk)]
    off = {}
    level = 0
    half = 1
    while half < cl:
        if at_level is not None:
            at_level(level)
        upper = uppers[level]
        z = (jnp.where(upper, q, k) * jnp.exp2(jnp.where(upper, p, tot - p))).astype(BF16)
        if half < SCORE_BLK:
            for i in range(nblk):
                diag[i] = jnp.where(lev == level, _dot_nt(blk(z, i), blk(z, i)), diag[i])
        else:
            hb = half // SCORE_BLK
            for i in range(nblk):
                if (i // hb) % 2 == 1:
                    base = (i // (2 * hb)) * 2 * hb
                    for j in range(base, base + hb):
                        off[(i, j)] = _dot_nt(blk(z, i), blk(z, j))
        partner = jnp.where(upper, pltpu.roll(tot, half, 0), pltpu.roll(tot, cl - half, 0))
        p = p + jnp.where(upper, partner, 0.0)
        tot = tot + partner
        half *= 2
        level += 1
    dg = jnp.sum(q * k, axis=-1, keepdims=True)
    rows = []
    for i in range(nblk):
        d_i = jnp.where(lev == -1, blk(dg, i), diag[i])
        rows.append(jnp.concatenate([off[(i, j)] for j in range(i)] + [d_i], axis=-1))
    return rows, p, tot


def _gla_heads(q_ref, k_ref, v_ref, g_ref, la_ref, lev_ref, gout_ref, st_ref, new_ref, fresh, valid, at_slot=None):
    cl = q_ref.shape[0]
    lev = lev_ref[...]
    row = lax.broadcasted_iota(jnp.int32, (cl, GLA_DK), 0)
    uppers = []
    half = 1
    while half < cl:
        uppers.append((row & half) != 0)
        half *= 2
    nlev = len(uppers)
    for h in range(GLA_HEADS):
        ks = slice(h * GLA_DK, (h + 1) * GLA_DK)
        vs = slice(h * GLA_DV, (h + 1) * GLA_DV)
        q = q_ref[:, ks] * (GLA_DK ** -0.5)
        k = k_ref[:, ks]
        hook = None if at_slot is None else (lambda level, h=h: at_slot(h * nlev + level))
        rows, p, tot = _gla_chunk(q, k, la_ref[:, ks] * LOG2E, lev, uppers, hook)
        vb = v_ref[:, vs].astype(BF16)
        st_old = jnp.where(fresh, 0.0, st_ref[h])
        qt = (q * jnp.exp2(p)).astype(BF16)
        intra = jnp.concatenate([_dot(r.astype(BF16), vb[:r.shape[1]]) for r in rows], axis=0)
        o = intra + _dot_nt(qt, st_old.astype(BF16))
        kt = (k * jnp.exp2(tot - p)).astype(BF16)
        st_new = st_old * jnp.exp2(tot[0:1, :]) + _dot_tn(vb, kt)
        st_ref[h] = jnp.where(valid, st_new, st_old)
        new_ref[:, RET_W + h * GLA_DV: RET_W + (h + 1) * GLA_DV] = (
            _rms(o) * gout_ref[:, vs] * _silu(g_ref[:, vs])).astype(new_ref.dtype)


def _mixer_body(dmat_ref, lg_ref, q_ref, k_ref, rv_ref, rg_ref, gq_ref, gk_ref, gv_ref, gr_ref, la_ref, lev_ref,
                gro_ref, ggo_ref, x_ref, gt_ref, sc_ref, sh_ref, gf_ref, w_ref,
                x1_ref, h_ref, sr_out, sg_out, s_ref, st_ref, mixa_ref, mixb_ref, *, nc, n_chunks):
    s = pl.program_id(0)
    valid = s < n_chunks
    c = jnp.minimum(s, n_chunks - 1) % nc
    fresh = c == 0

    @pl.when(s == 0)
    def _():
        mixb_ref[...] = jnp.zeros_like(mixb_ref)

    def step(old_ref, new_ref):
        old = old_ref[...]
        d = x_ref.shape[1]
        nblk = d // PROJ_BLK
        ssq = [jnp.zeros((x_ref.shape[0], 1), F32)]

        def project(blocks):
            for cb in blocks:
                cs = slice(cb * PROJ_BLK, (cb + 1) * PROJ_BLK)
                x1 = x_ref[:, cs] + gt_ref[0][:, cs] * _dot(old, w_ref[:, cs])
                x1_ref[:, cs] = x1
                ssq[0] = ssq[0] + jnp.sum(x1 * x1, axis=-1, keepdims=True)

        nslots = GLA_HEADS * (x_ref.shape[0].bit_length() - 1)
        due = {(cb * nslots) // nblk: cb for cb in range(nblk)}
        _gla_heads(gq_ref, gk_ref, gv_ref, gr_ref, la_ref, lev_ref, ggo_ref, st_ref, new_ref, fresh, valid,
                   at_slot=lambda slot: project([due[slot]]) if slot in due else None)
        _ret_heads(dmat_ref, lg_ref, q_ref, k_ref, rv_ref, rg_ref, gro_ref, s_ref, new_ref, fresh, valid)
        inv = lax.rsqrt(ssq[0] * (1.0 / d) + EPS)
        gain = gf_ref[...] * (1.0 + sc_ref[0])
        for cb in range(nblk):
            cs = slice(cb * PROJ_BLK, (cb + 1) * PROJ_BLK)
            h_ref[:, cs] = (x1_ref[:, cs] * inv * gain[:, cs] + sh_ref[0][:, cs]).astype(h_ref.dtype)

    @pl.when(s % 2 == 0)
    def _():
        step(mixb_ref, mixa_ref)

    @pl.when(s % 2 == 1)
    def _():
        step(mixa_ref, mixb_ref)

    @pl.when(valid & (c == nc - 1))
    def _():
        sr_out[0] = s_ref[...]
        for h in range(GLA_HEADS):
            sg_out[0, h] = st_ref[h].T


def _mixer_prompt(qk, rest, la, x, gt, sc, sh, g_ret_out, g_gla_out, g_ffn, w_out_b, *, batch, seq, chunk):
    m, d = x.shape
    nc = seq // chunk
    n_chunks = batch * nc
    lg = _ret_log_gamma_tile()
    lev = jnp.asarray(_gla_level_map(SCORE_BLK))
    gqk_w = GLA_HEADS * GLA_DK
    cur = lambda s: jnp.minimum(s, n_chunks - 1)
    prv = lambda s: jnp.maximum(s - 1, 0)
    rowc = lambda w, blk: pl.BlockSpec((chunk, w), lambda s: (cur(s), blk))
    const = lambda shape: pl.BlockSpec(shape, lambda s: (0,) * len(shape))
    mod_spec = pl.BlockSpec((1, 1, d), lambda s: (prv(s) // nc, 0, 0))
    return pl.pallas_call(
        functools.partial(_mixer_body, nc=nc, n_chunks=n_chunks),
        grid=(n_chunks + 1,),
        in_specs=[const((RET_HEADS, chunk, chunk)), const((RET_HEADS, SUBLANES, LANES)),
                  rowc(RET_W, 0), rowc(RET_W, 1),
                  rowc(RET_W, R_RV // RET_W), rowc(RET_W, R_RG // RET_W),
                  rowc(gqk_w, R_GQ // gqk_w), rowc(gqk_w, R_GK // gqk_w),
                  rowc(GLA_W, R_GV // GLA_W), rowc(GLA_W, R_GR // GLA_W),
                  rowc(gqk_w, 0),
                  const((SCORE_BLK, SCORE_BLK)), const((1, RET_W)), const((1, GLA_W)),
                  pl.BlockSpec((chunk, d), lambda s: (prv(s), 0)),
                  mod_spec, mod_spec, mod_spec,
                  const((1, d)),
                  pl.BlockSpec((RET_W + GLA_W, d), lambda s: (0, 0), pipeline_mode=pl.Buffered(1))],
        out_specs=[pl.BlockSpec((chunk, d), lambda s: (prv(s), 0)),
                   pl.BlockSpec((chunk, d), lambda s: (prv(s), 0)),
                   pl.BlockSpec((1, RET_HEADS, RET_DK, RET_DV), lambda s: (cur(s) // nc, 0, 0, 0)),
                   pl.BlockSpec((1, GLA_HEADS, GLA_DK, GLA_DV), lambda s: (cur(s) // nc, 0, 0, 0))],
        out_shape=[jax.ShapeDtypeStruct((m, d), F32),
                   jax.ShapeDtypeStruct((m, d), BF16),
                   jax.ShapeDtypeStruct((batch, RET_HEADS, RET_DK, RET_DV), F32),
                   jax.ShapeDtypeStruct((batch, GLA_HEADS, GLA_DK, GLA_DV), F32)],
        scratch_shapes=[pltpu.VMEM((RET_HEADS, RET_DK, RET_DV), F32),
                        pltpu.VMEM((GLA_HEADS, GLA_DV, GLA_DK), F32),
                        pltpu.VMEM((chunk, RET_W + GLA_W), BF16),
                        pltpu.VMEM((chunk, RET_W + GLA_W), BF16)],
        compiler_params=_params(("arbitrary",)),
        name="mixer_prompt",
    )(_ret_decay_matrix(chunk), lg, qk, qk, rest, rest, rest, rest, rest, rest, la, lev,
      g_ret_out.reshape(1, RET_W), g_gla_out.reshape(1, GLA_W), x, gt, sc, sh, g_ffn.reshape(1, d), w_out_b)


def _columns(rows):
    n = rows[0].shape[1]
    pad = (-len(rows)) % 8
    stack = jnp.concatenate(rows + [jnp.zeros((pad, n), F32)] if pad else rows, axis=0)
    return stack.T


def _state_body(lg_ref, qk_ref, rest_ref, la_ref, sr_ref, sg_ref, gro_ref, ggo_ref, o_ref, sr_out, sg_out):
    nb = qk_ref.shape[0]
    for i in range(nb):
        qkrow = qk_ref[i]
        rrow = rest_ref[i]
        larow = la_ref[i]
        rcols = _columns([qkrow[:, j * RET_DK:(j + 1) * RET_DK] for j in range(2 * RET_HEADS)])
        gq = [rrow[:, R_GQ + h * GLA_DK: R_GQ + (h + 1) * GLA_DK] * (GLA_DK ** -0.5) for h in range(GLA_HEADS)]
        gk = [rrow[:, R_GK + h * GLA_DK: R_GK + (h + 1) * GLA_DK] for h in range(GLA_HEADS)]
        ga = [jnp.exp(larow[:, h * GLA_DK:(h + 1) * GLA_DK]) for h in range(GLA_HEADS)]
        gcols = _columns(gq + gk + ga)
        outs = []
        for h in range(RET_HEADS):
            v = rrow[:, R_RV + h * RET_DV: R_RV + (h + 1) * RET_DV]
            gate = rrow[:, R_RG + h * RET_DV: R_RG + (h + 1) * RET_DV]
            gamma = jnp.exp(lg_ref[h][:1, :1])
            qc = rcols[:, h:h + 1]
            kc = rcols[:, RET_HEADS + h:RET_HEADS + h + 1]
            s_new = sr_ref[0, i, h] * gamma + kc * v
            sr_out[0, i, h] = s_new
            o = jnp.sum(qc * s_new, axis=0, keepdims=True)
            outs.append(_rms(o) * gro_ref[:, h * RET_DV:(h + 1) * RET_DV] * _silu(gate))
        for h in range(GLA_HEADS):
            v = rrow[:, R_GV + h * GLA_DV: R_GV + (h + 1) * GLA_DV]
            gate = rrow[:, R_GR + h * GLA_DV: R_GR + (h + 1) * GLA_DV]
            qc = gcols[:, h:h + 1]
            kc = gcols[:, GLA_HEADS + h:GLA_HEADS + h + 1]
            ac = gcols[:, 2 * GLA_HEADS + h:2 * GLA_HEADS + h + 1]
            s_new = sg_ref[0, i, h] * ac + kc * v
            sg_out[0, i, h] = s_new
            o = jnp.sum(qc * s_new, axis=0, keepdims=True)
            outs.append(_rms(o) * ggo_ref[:, h * GLA_DV:(h + 1) * GLA_DV] * _silu(gate))
        o_ref[i] = jnp.concatenate(outs, axis=-1)


def _outproj_body(oa_ref, ob_ref, x_ref, gt_ref, sc_ref, sh_ref, g_ref, w_ref, x1_ref, h_ref):
    ka = oa_ref.shape[1]
    mix = _dot(oa_ref[...].astype(BF16), w_ref[:ka, :]) + _dot(ob_ref[...].astype(BF16), w_ref[ka:, :])
    x1 = x_ref[...] + gt_ref[0] * mix
    x1_ref[...] = x1
    h = _rms(x1) * g_ref[...]
    h_ref[...] = (h * (1.0 + sc_ref[0]) + sh_ref[0]).astype(BF16)


def _outproj(o_a, o_b, x, gt, sc, sh, g_ffn, w_out_bf16, *, tm, rows_per_mod):
    m, d = x.shape
    ka, kb = o_a.shape[1], o_b.shape[1]
    return pl.pallas_call(
        _outproj_body,
        grid=(m // tm,),
        in_specs=[pl.BlockSpec((tm, ka), lambda i: (i, 0)),
                  pl.BlockSpec((tm, kb), lambda i: (i, 0)),
                  pl.BlockSpec((tm, d), lambda i: (i, 0)),
                  *(_mod_spec(v, d, tm, rows_per_mod) for v in (gt, sc, sh)),
                  pl.BlockSpec((1, d), lambda i: (0, 0)),
                  pl.BlockSpec((ka + kb, d), lambda i: (0, 0), pipeline_mode=pl.Buffered(1))],
        out_specs=[pl.BlockSpec((tm, d), lambda i: (i, 0)),
                   pl.BlockSpec((tm, d), lambda i: (i, 0))],
        out_shape=[jax.ShapeDtypeStruct((m, d), F32),
                   jax.ShapeDtypeStruct((m, d), BF16)],
        compiler_params=_params(("arbitrary",)),
        name="out_proj",
    )(o_a, o_b, x, gt[0], sc[0], sh[0], g_ffn.reshape(1, d), w_out_bf16)


HALO = SUBLANES


def _ffn_up_body(h_ref, wa_ref, wb_ref, cwa_ref, cwb_ref, cba_ref, cbb_ref, g_ref, cs_ref, win_ref, tail_ref, *, tiles_per_seq):
    i, f = pl.program_id(0), pl.program_id(1)
    hb = h_ref[...]
    tm = hb.shape[0]
    first = (i % tiles_per_seq) == 0
    ucs = []
    for part, (w_ref, cw_ref, cb_ref) in enumerate(((wa_ref, cwa_ref, cba_ref), (wb_ref, cwb_ref, cbb_ref))):
        u = _dot(hb, w_ref[...])
        prev = jnp.where(first, 0.0, tail_ref[part, f])
        pieces = []
        for c in range(u.shape[1] // LANES):
            lc = slice(c * LANES, (c + 1) * LANES)
            win_ref[part, c, 0:HALO, :] = prev[:, lc]
            win_ref[part, c, HALO:HALO + tm, :] = u[:, lc]
            cw, cb = cw_ref[:, lc], cb_ref[:, lc]
            pieces.append(cb + cw[0:1] * win_ref[part, c, HALO - 2:HALO - 2 + tm, :]
                          + cw[1:2] * win_ref[part, c, HALO - 1:HALO - 1 + tm, :] + cw[2:3] * u[:, lc])
        ucs.append(jnp.concatenate(pieces, axis=-1))
        tail_ref[part, f] = u[tm - HALO:]
        for r in range(CONV_W - 1):
            cs_ref[0, 0, r, part:part + 1, :] = u[tm - (CONV_W - 1) + r: tm - (CONV_W - 1) + r + 1]
    g_ref[...] = (_silu(ucs[0]) * ucs[1]).astype(g_ref.dtype)


def _ffn_up_state_body(h_ref, wa_ref, wb_ref, cwa_ref, cwb_ref, cba_ref, cbb_ref,
                       lg_ref, qk_ref, rest_ref, la_ref, sr_ref, sg_ref, gro_ref, ggo_ref, wd_ref,
                       g_ref, cs_ref, o_ref, sr_out, sg_out, wdb_ref, win_ref, tail_ref, *, tiles_per_seq):
    _ffn_up_body(h_ref, wa_ref, wb_ref, cwa_ref, cwb_ref, cba_ref, cbb_ref, g_ref, cs_ref, win_ref, tail_ref,
                 tiles_per_seq=tiles_per_seq)
    _state_body(lg_ref, qk_ref, rest_ref, la_ref, sr_ref, sg_ref, gro_ref, ggo_ref, o_ref, sr_out, sg_out)
    wdb_ref[...] = wd_ref[...].astype(wdb_ref.dtype)


def _ffn_up_prompt(h, w_up, conv_w, conv_b, qk_s, rest_s, la_s, state_ret, state_gla, g_ret_out, g_gla_out, w_down,
                   *, batch, seq, tm, tf, nb, cast_rows):
    m, d = h.shape
    ff = w_up.shape[1] // 2
    nf = ff // tf
    cb = conv_b.reshape(1, 2 * ff)
    tps = seq // tm
    ms = qk_s.shape[0]
    nblk = ms // nb
    ncast = ff // cast_rows
    assert (m // tm) * nf >= nblk + ncast
    mix = RET_W + GLA_W
    lg = _ret_log_gamma_tile()
    sblk = lambda i, f: jnp.minimum(i * nf + f, nblk - 1)
    ret_spec = pl.BlockSpec((1, nb, RET_HEADS, RET_DK, RET_DV), lambda i, f: (0, sblk(i, f), 0, 0, 0))
    gla_spec = pl.BlockSpec((1, nb, GLA_HEADS, GLA_DK, GLA_DV), lambda i, f: (0, sblk(i, f), 0, 0, 0))
    row_spec = lambda w: pl.BlockSpec((nb, 1, w), lambda i, f: (sblk(i, f), 0, 0))
    cast_spec = pl.BlockSpec((cast_rows, d), lambda i, f: (jnp.clip(i * nf + f - nblk, 0, ncast - 1), 0))
    body = functools.partial(_ffn_up_state_body, tiles_per_seq=tps)
    gate, tails, o_s, s_ret, s_gla, w_down_b = pl.pallas_call(
        body,
        grid=(m // tm, nf),
        in_specs=[pl.BlockSpec((tm, d), lambda i, f: (i, 0)),
                  pl.BlockSpec((d, tf), lambda i, f: (0, f)),
                  pl.BlockSpec((d, tf), lambda i, f: (0, nf + f)),
                  pl.BlockSpec((CONV_W, tf), lambda i, f: (0, f)),
                  pl.BlockSpec((CONV_W, tf), lambda i, f: (0, nf + f)),
                  pl.BlockSpec((1, tf), lambda i, f: (0, f)),
                  pl.BlockSpec((1, tf), lambda i, f: (0, nf + f)),
                  pl.BlockSpec((RET_HEADS, SUBLANES, LANES), lambda i, f: (0, 0, 0)),
                  row_spec(QK_W), row_spec(REST_W), row_spec(la_s.shape[1]),
                  ret_spec, gla_spec,
                  pl.BlockSpec((1, RET_W), lambda i, f: (0, 0)),
                  pl.BlockSpec((1, GLA_W), lambda i, f: (0, 0)),
                  cast_spec],
        out_specs=[pl.BlockSpec((tm, tf), lambda i, f: (i, f)),
                   pl.BlockSpec((1, 1, CONV_W - 1, 2, tf), lambda i, f: (i // tps, i % tps, 0, 0, f)),
                   row_spec(mix), ret_spec, gla_spec, cast_spec],
        out_shape=[jax.ShapeDtypeStruct((m, ff), BF16),
                   jax.ShapeDtypeStruct((batch, tps, CONV_W - 1, 2, ff), F32),
                   jax.ShapeDtypeStruct((ms, 1, mix), F32),
                   jax.ShapeDtypeStruct(state_ret.shape, F32),
                   jax.ShapeDtypeStruct(state_gla.shape, F32),
                   jax.ShapeDtypeStruct(w_down.shape, BF16)],
        scratch_shapes=[pltpu.VMEM((2, tf // LANES, HALO + tm, LANES), F32), pltpu.VMEM((2, nf, HALO, tf), F32)],
        compiler_params=_params(("arbitrary", "arbitrary")),
        name="ffn_up_prompt",
    )(h, w_up, w_up, conv_w, conv_w, cb, cb,
      lg, qk_s.astype(F32).reshape(ms, 1, QK_W), rest_s.reshape(ms, 1, REST_W), la_s.reshape(ms, 1, la_s.shape[1]),
      state_ret, state_gla, g_ret_out.reshape(1, RET_W), g_gla_out.reshape(1, GLA_W), w_down)
    return gate, tails[:, tps - 1], o_s.reshape(ms, mix), s_ret, s_gla, w_down_b


def _ffn_up_step_body(h_ref, w_ref, cw_ref, cb_ref, st_ref, g_ref, cs_ref, uca_ref, *, nf):
    j = pl.program_id(0)
    u = _dot(h_ref[...], w_ref[...])
    s1 = st_ref[0, :, 1, :]
    cw = cw_ref[...]
    uc = cb_ref[...] + cw[0:1] * st_ref[0, :, 0, :] + cw[1:2] * s1 + cw[2:3] * u
    cs_ref[0, :, 0, :] = s1
    cs_ref[0, :, 1, :] = u

    @pl.when(j < nf)
    def _():
        uca_ref[j] = uc

    @pl.when(j >= nf)
    def _():
        g_ref[...] = (_silu(uca_ref[j - nf]) * uc).astype(g_ref.dtype)


def _ffn_up_step(h, w_up, conv_w, conv_b, state_conv, *, tf):
    m, d = h.shape
    ff = w_up.shape[1] // 2
    nf = ff // tf
    st_spec = pl.BlockSpec((1, m, CONV_W - 1, tf), lambda j: (0, 0, 0, j))
    return pl.pallas_call(
        functools.partial(_ffn_up_step_body, nf=nf),
        grid=(2 * nf,),
        in_specs=[pl.BlockSpec((m, d), lambda j: (0, 0)),
                  pl.BlockSpec((d, tf), lambda j: (0, j)),
                  pl.BlockSpec((CONV_W, tf), lambda j: (0, j)),
                  pl.BlockSpec((1, tf), lambda j: (0, j)),
                  st_spec],
        out_specs=[pl.BlockSpec((m, tf), lambda j: (0, jnp.maximum(j - nf, 0))), st_spec],
        out_shape=[jax.ShapeDtypeStruct((m, ff), BF16),
                   jax.ShapeDtypeStruct(state_conv.shape, F32)],
        scratch_shapes=[pltpu.VMEM((nf, m, tf), F32)],
        compiler_params=_params(("arbitrary",)),
        name="ffn_up_step",
    )(h, w_up, conv_w, conv_b.reshape(1, 2 * ff), state_conv)


def _ffn_down_body(g_ref, w_ref, x1_ref, gt_ref, gf_ref, y_ref, xs_ref, inv_ref):
    k = pl.program_id(1)
    last = pl.num_programs(1) - 1
    d = y_ref.shape[1]
    xb = x1_ref.shape[1]
    xs_ref[k] = x1_ref[...]

    @pl.when(k == 0)
    def _():
        for cb in range(d // PROJ_BLK):
            cs = slice(cb * PROJ_BLK, (cb + 1) * PROJ_BLK)
            y_ref[:, cs] = _dot(g_ref[...], w_ref[:, cs])

    @pl.when((k > 0) & (k < last))
    def _():
        for cb in range(d // PROJ_BLK):
            cs = slice(cb * PROJ_BLK, (cb + 1) * PROJ_BLK)
            y_ref[:, cs] += _dot(g_ref[...], w_ref[:, cs])

    @pl.when(k == last)
    def _():
        ssq = jnp.zeros((y_ref.shape[0], 1), F32)
        for cb in range(d // PROJ_BLK):
            cs = slice(cb * PROJ_BLK, (cb + 1) * PROJ_BLK)
            x1 = xs_ref[(cb * PROJ_BLK) // xb, :, (cb * PROJ_BLK) % xb: (cb * PROJ_BLK) % xb + PROJ_BLK]
            x2 = x1 + gt_ref[0][:, cs] * (y_ref[:, cs] + _dot(g_ref[...], w_ref[:, cs]))
            y_ref[:, cs] = x2
            ssq = ssq + jnp.sum(x2 * x2, axis=-1, keepdims=True)
        inv_ref[...] = lax.rsqrt(ssq * (1.0 / d) + EPS)

        rg = min(y_ref.shape[0], NORM_ROWS)

        def norm_rows(r, carry):
            rows = pl.ds(pl.multiple_of(r * rg, rg), rg)
            y_ref[rows, :] = y_ref[rows, :] * inv_ref[rows, :] * gf_ref[...]
            return carry

        lax.fori_loop(0, y_ref.shape[0] // rg, norm_rows, 0)


def _ffn_down(g, w_down, x1, gt, g_final, *, tm, tk, rows_per_mod):
    m, d = x1.shape
    ff = g.shape[1]
    nk = ff // tk
    assert nk >= 2
    xb = d // nk
    assert d % nk == 0 and xb % PROJ_BLK == 0
    return pl.pallas_call(
        _ffn_down_body,
        grid=(m // tm, nk),
        in_specs=[pl.BlockSpec((tm, tk), lambda i, k: (i, k)),
                  pl.BlockSpec((tk, d), lambda i, k: (k, 0)),
                  pl.BlockSpec((tm, xb), lambda i, k: (i, k)),
                  _mod_spec(gt, d, tm, rows_per_mod),
                  pl.BlockSpec((1, d), lambda i, k: (0, 0))],
        out_specs=pl.BlockSpec((tm, d), lambda i, k: (i, 0)),
        out_shape=jax.ShapeDtypeStruct((m, d), F32),
        scratch_shapes=[pltpu.VMEM((nk, tm, xb), F32), pltpu.VMEM((tm, 1), F32)],
        compiler_params=_params(("arbitrary", "arbitrary")),
        name="ffn_down",
    )(g, w_down, x1, gt[0], g_final.reshape(1, d))


def _rope_tables(pos, rows=None):
    half = RET_DK // 2
    inv = ROPE_THETA ** (-np.arange(half, dtype=np.float64) / half)
    ang = np.asarray(pos, np.float64)[:, None] * inv[None, :]
    tabs = (np.cos(ang), np.sin(ang))
    if rows is not None:
        tabs = tuple(np.broadcast_to(t, (rows, half)) for t in tabs)
    return tuple(jnp.asarray(t, F32) for t in tabs)


def kernel(x_prompt, x_sample, c_prompt, c_sample, state_ret, state_gla, state_conv, w_ada, b_ada, g_attn, w_in, w_a2, b_a2, g_ret_out, g_gla_out, w_out, g_ffn, w_up, conv_w, conv_b, w_down, g_final):
    bp, t_p, d = x_prompt.shape
    bs, t_s, _ = x_sample.shape
    assert t_s == 1 and w_ada.shape[0] == 1
    mp = bp * t_p
    w_ada, b_ada, g_attn, w_in, w_a2, b_a2, g_ret_out, g_gla_out, w_out, g_ffn, w_up, conv_w, conv_b, w_down = (
        a[0] for a in (w_ada, b_ada, g_attn, w_in, w_a2, b_a2, g_ret_out, g_gla_out, w_out, g_ffn, w_up, conv_w, conv_b, w_down))

    mod = _ada(jnp.concatenate([c_sample, c_prompt], axis=0), w_ada, b_ada)
    sh1p, sc1p, gt1p, sh2p, sc2p, gt2p = (mod[bs:, i * d:(i + 1) * d].reshape(bp, 1, d) for i in range(6))
    sh1s, sc1s, gt1s, sh2s, sc2s, gt2s = ((mod[None], i) for i in range(6))

    cos_p, sin_p = _rope_tables(np.arange(t_p))
    cos_s, sin_s = _rope_tables(PAST_LEN + np.arange(t_s), rows=bs)
    w_out_b = w_out.astype(BF16)
    w_in_t = w_in.T
    w_gate_t = w_in_t[IN_MAIN:]

    xp = x_prompt.reshape(mp, d)
    xs = x_sample.reshape(bs, d)
    qk_s, rest_s, la_s, w_in_b = _inproj(xs, sc1s, sh1s, g_attn, w_in_t, w_gate_t, w_a2, b_a2, cos_s, sin_s,
                                         tm=bs, rows_per_mod=bs, cast_w=True)
    qk_p, rest_p, la_p, w_up_b = _inproj(xp, (sc1p, 0), (sh1p, 0), g_attn, w_in_b, w_gate_t, w_a2, b_a2, cos_p, sin_p,
                                         tm=ROW_TILE, rows_per_mod=t_p, cast_src=w_up)
    x1_p, h2_p, s_ret_p, s_gla_p = _mixer_prompt(qk_p, rest_p, la_p, xp, gt1p, sc2p, sh2p, g_ret_out, g_gla_out, g_ffn,
                                                 w_out_b, batch=bp, seq=t_p, chunk=CHUNK)
    g_p, cs_p, o_s, s_ret_s, s_gla_s, w_down_b = _ffn_up_prompt(
        h2_p, w_up_b, conv_w, conv_b, qk_s, rest_s, la_s, state_ret, state_gla, g_ret_out, g_gla_out, w_down,
        batch=bp, seq=t_p, tm=ROW_TILE, tf=FF_COL_TILE, nb=STATE_SEQS, cast_rows=CAST_ROWS)
    y_p = _ffn_down(g_p, w_down_b, x1_p, (gt2p, 0), g_final, tm=ROW_TILE, tk=DOWN_K_TILE, rows_per_mod=t_p)

    x1_s, h2_s = _outproj(o_s[:, :RET_W], o_s[:, RET_W:], xs, gt1s, sc2s, sh2s, g_ffn, w_out_b, tm=bs, rows_per_mod=bs)
    ff = w_down.shape[0]
    g_s, cs_s = _ffn_up_step(h2_s, w_up_b, conv_w, conv_b, state_conv, tf=ff // 4)
    y_s = _ffn_down(g_s, w_down_b, x1_s, gt2s, g_final, tm=bs, tk=ff // 2, rows_per_mod=bs)

    return (y_p.reshape(bp, t_p, d), y_s.reshape(bs, t_s, d),
            s_ret_p[None], s_ret_s, s_gla_p[None], s_gla_s,
            cs_p.reshape(1, bp, CONV_W - 1, -1), cs_s)
```

```python
import functools

import numpy as np
import jax
import jax.numpy as jnp
from jax import lax
from jax.experimental import pallas as pl
from jax.experimental.pallas import tpu as pltpu

F32 = jnp.float32
BF16 = jnp.bfloat16

RET_HEADS = 4
RET_DK = 256
RET_DV = 256
GLA_HEADS = 4
GLA_DK = 128
GLA_DV = 256
GLA_RANK = 16
GLA_TAU = 16.0
ROPE_THETA = 10000.0
PAST_LEN = 16384
CONV_W = 3
EPS = 1e-6

RET_W = RET_HEADS * RET_DV
GLA_W = GLA_HEADS * GLA_DV
QK_W = 2 * RET_HEADS * RET_DK
R_RV = 0
R_RG = R_RV + RET_W
R_GQ = R_RG + RET_W
R_GK = R_GQ + GLA_HEADS * GLA_DK
R_GV = R_GK + GLA_HEADS * GLA_DK
R_GR = R_GV + GLA_W
REST_W = R_GR + GLA_W
IN_MAIN = QK_W + REST_W

V7X_VMEM_BYTES = 64 * 1024 * 1024
V7X_MXU_WIDTH = 256
LANES = 128
SUBLANES = 8
VMEM_LIMIT_BYTES = V7X_VMEM_BYTES - 8 * 1024 * 1024
PROJ_BLK = V7X_MXU_WIDTH

ROW_TILE = 1024
IN_COL_TILE = 1024
FF_COL_TILE = 512
DOWN_K_TILE = 1408
CHUNK = 256
STATE_SEQS = 2
CAST_ROWS = 256
CAST_COLS = 256
NORM_ROWS = 128


def _params(semantics):
    return pltpu.CompilerParams(dimension_semantics=semantics, vmem_limit_bytes=VMEM_LIMIT_BYTES)


def _dot(a, b):
    return jnp.dot(a, b, preferred_element_type=F32)


def _dot_nt(a, b):
    return lax.dot_general(a, b, (((1,), (1,)), ((), ())), preferred_element_type=F32)


def _dot_tn(a, b):
    return lax.dot_general(a, b, (((0,), (0,)), ((), ())), preferred_element_type=F32)


def _silu(x):
    return x * jax.nn.sigmoid(x)


def _rms(x):
    return x * lax.rsqrt(jnp.mean(x * x, axis=-1, keepdims=True) + EPS)


def _mod_spec(mod, d, tm, rows_per_mod):
    arr, col = mod
    rows = 1 if arr.shape[1] == 1 else tm
    return pl.BlockSpec((1, rows, d), lambda i, *_: ((i * tm) // rows_per_mod, 0, col))


def _ada_body(c_ref, w_ref, b_ref, o_ref):
    s = _silu(c_ref[...]).astype(BF16)
    o_ref[...] = _dot(s, w_ref[...].astype(BF16)) + b_ref[...]


def _ada(c_all, w_ada, b_ada, tn=FF_COL_TILE):
    r, d = c_all.shape
    n = w_ada.shape[1]
    return pl.pallas_call(
        _ada_body,
        grid=(n // tn,),
        in_specs=[pl.BlockSpec((r, d), lambda j: (0, 0)),
                  pl.BlockSpec((d, tn), lambda j: (0, j)),
                  pl.BlockSpec((1, tn), lambda j: (0, j))],
        out_specs=pl.BlockSpec((r, tn), lambda j: (0, j)),
        out_shape=jax.ShapeDtypeStruct((r, n), F32),
        compiler_params=_params(("arbitrary",)),
        name="ada_mod",
    )(c_all, w_ada, b_ada.reshape(1, n))


def _inproj_body(*refs, n_qk, cast_w, rider):
    refs = list(refs)
    x_ref, sc_ref, sh_ref, g_ref, w_ref, wga_ref, wa2_ref, ba2_ref, cos_ref, sin_ref = refs[:10]
    del refs[:10]
    src_ref = refs.pop(0) if rider else None
    qk_ref, rest_ref, la_ref = refs[:3]
    del refs[:3]
    wcast_ref = refs.pop(0) if cast_w else None
    dst_ref = refs.pop(0) if rider else None
    (h_ref,) = refs
    j = pl.program_id(1)

    def ride():
        if rider:
            dst_ref[...] = src_ref[...].astype(dst_ref.dtype)

    @pl.when(j == 0)
    def _():
        gain = g_ref[...] * (1.0 + sc_ref[0])
        hb = (_rms(x_ref[...]) * gain + sh_ref[0]).astype(BF16)
        h_ref[...] = hb
        ga = _dot_nt(hb, wga_ref[...].astype(BF16))
        z = _dot(ga.astype(BF16), wa2_ref[...].astype(BF16)) + ba2_ref[...]
        la_ref[...] = (jnp.minimum(z, 0.0) - jnp.log1p(jnp.exp(-jnp.abs(z)))) * (1.0 / GLA_TAU)

    if cast_w:
        w_tile = w_ref[...].T.astype(BF16)
        wcast_ref[...] = w_tile
        weight = lambda: w_tile
    else:
        weight = lambda: w_ref[...]

    @pl.when(j < n_qk)
    def _():
        ride()
        acc = _dot(h_ref[...], weight())
        cos, sin = cos_ref[...], sin_ref[...]
        half = RET_DK // 2
        scale = jnp.where(j >= n_qk // 2, RET_DK ** -0.5, 1.0)
        outs = []
        for hh in range(acc.shape[1] // RET_DK):
            x1 = acc[:, hh * RET_DK: hh * RET_DK + half]
            x2 = acc[:, hh * RET_DK + half: (hh + 1) * RET_DK]
            outs += [x1 * cos - x2 * sin, x1 * sin + x2 * cos]
        qk_ref[...] = (jnp.concatenate(outs, axis=-1) * scale).astype(qk_ref.dtype)

    @pl.when(j >= n_qk)
    def _():
        ride()
        rest_ref[...] = _dot(h_ref[...], weight())


def _inproj(x, sc, sh, g_attn, w, w_gate_t, w_a2, b_a2, cos, sin, *, tm, rows_per_mod, cast_w=False, cast_src=None,
            tn=IN_COL_TILE):
    m, d = x.shape
    nq = w_a2.shape[1]
    n_qk = QK_W // tn
    tab_tiles = cos.shape[0] // tm
    n_j = IN_MAIN // tn
    once = pl.Buffered(1)
    sc_spec, sh_spec = (_mod_spec(v, d, tm, rows_per_mod) for v in (sc, sh))
    tab_spec = pl.BlockSpec((tm, RET_DK // 2), lambda i, j: (i % tab_tiles, 0), pipeline_mode=once)
    out_specs = [pl.BlockSpec((tm, tn), lambda i, j: (i, jnp.minimum(j, n_qk - 1))),
                 pl.BlockSpec((tm, tn), lambda i, j: (i, jnp.maximum(j - n_qk, 0))),
                 pl.BlockSpec((tm, nq), lambda i, j: (i, 0), pipeline_mode=once)]
    out_shape = [jax.ShapeDtypeStruct((m, QK_W), BF16),
                 jax.ShapeDtypeStruct((m, REST_W), F32),
                 jax.ShapeDtypeStruct((m, nq), F32)]
    if cast_w:
        assert m == tm
        w_spec = pl.BlockSpec((tn, d), lambda i, j: (j, 0))
        out_specs.append(pl.BlockSpec((d, tn), lambda i, j: (0, j)))
        out_shape.append(jax.ShapeDtypeStruct((d, IN_MAIN), BF16))
    else:
        w_spec = pl.BlockSpec((d, tn), lambda i, j: (0, j))
    in_specs = [pl.BlockSpec((tm, d), lambda i, j: (i, 0)),
                sc_spec, sh_spec,
                pl.BlockSpec((1, d), lambda i, j: (0, 0)),
                w_spec,
                pl.BlockSpec((GLA_RANK, d), lambda i, j: (0, 0)),
                pl.BlockSpec((GLA_RANK, nq), lambda i, j: (0, 0)),
                pl.BlockSpec((1, nq), lambda i, j: (0, 0)),
                tab_spec, tab_spec]
    args = [x, sc[0], sh[0], g_attn.reshape(1, d), w, w_gate_t, w_a2, b_a2.reshape(1, nq), cos, sin]
    if cast_src is not None:
        ncast = cast_src.shape[1] // CAST_COLS
        assert (m // tm) * n_j >= ncast
        ride_spec = pl.BlockSpec((cast_src.shape[0], CAST_COLS), lambda i, j: (0, jnp.minimum(i * n_j + j, ncast - 1)))
        in_specs.append(ride_spec)
        args.append(cast_src)
        out_specs.append(ride_spec)
        out_shape.append(jax.ShapeDtypeStruct(cast_src.shape, BF16))
    return pl.pallas_call(
        functools.partial(_inproj_body, n_qk=n_qk, cast_w=cast_w, rider=cast_src is not None),
        grid=(m // tm, n_j),
        in_specs=in_specs,
        out_specs=out_specs,
        out_shape=out_shape,
        scratch_shapes=[pltpu.VMEM((tm, d), BF16)],
        compiler_params=_params(("arbitrary", "arbitrary")),
        name="in_proj",
    )(*args)


def _ret_log_gamma():
    return np.log1p(-np.exp2(-5.0 - np.arange(RET_HEADS, dtype=np.float64)))


def _ret_log_gamma_tile():
    return jnp.asarray(np.broadcast_to(_ret_log_gamma()[:, None, None], (RET_HEADS, SUBLANES, LANES)), F32)


def _ret_decay_matrix(chunk):
    idx = np.arange(chunk, dtype=np.float64)
    rel = idx[:, None] - idx[None, :]
    lg = _ret_log_gamma()
    return jnp.asarray(np.where(rel[None] >= 0, np.exp(np.maximum(rel, 0.0)[None] * lg[:, None, None]), 0.0), F32)


def _ret_heads(dmat_ref, lg_ref, q_ref, k_ref, v_ref, g_ref, gout_ref, s_ref, new_ref, fresh, valid):
    cl = q_ref.shape[0]
    idx = lax.broadcasted_iota(jnp.int32, (cl, 1), 0).astype(F32)
    for h in range(RET_HEADS):
        sl = slice(h * RET_DV, (h + 1) * RET_DV)
        lg = lg_ref[h][:1, :1]
        q_dec = jnp.exp((idx + 1.0) * lg)
        k_dec = jnp.exp((cl - 1.0 - idx) * lg)
        c_dec = jnp.exp(cl * lg)
        qb, kb = q_ref[:, sl], k_ref[:, sl]
        v = v_ref[:, sl]
        vb = v.astype(BF16)
        s_old = jnp.where(fresh, 0.0, s_ref[h])
        scores = _dot_nt(qb, kb) * dmat_ref[h]
        o = _dot(scores.astype(BF16), vb) + _dot(qb, s_old.astype(BF16)) * q_dec
        s_new = s_old * c_dec + _dot_tn(kb, (v * k_dec).astype(BF16))
        s_ref[h] = jnp.where(valid, s_new, s_old)
        new_ref[:, sl] = (_rms(o) * gout_ref[:, sl] * _silu(g_ref[:, sl])).astype(new_ref.dtype)


def _gla_level_map(c):
    t = np.arange(c)[:, None]
    s = np.arange(c)[None, :]
    x = np.bitwise_xor(t, s)
    lev = np.floor(np.log2(np.maximum(x, 1))).astype(np.int32)
    lev = np.where(t == s, -1, lev)
    lev = np.where(t < s, -2, lev)
    return lev.astype(np.int32)


SCORE_BLK = 128
LOG2E = 1.4426950408889634


def _gla_chunk(q, k, la2, lev, uppers, at_level=None):
    cl = q.shape[0]
    nblk = cl // SCORE_BLK
    blk = lambda x, i: x[i * SCORE_BLK:(i + 1) * SCORE_BLK]
    p = la2
    tot = la2
    diag = [jnp.zeros((SCORE_BLK, SCORE_BLK), F32) for _ in range(nblk)]
    off = {}
    level = 0
    half = 1
    while half < cl:
        if at_level is not None:
            at_level(level)
        upper = uppers[level]
        z = (jnp.where(upper, q, k) * jnp.exp2(jnp.where(upper, p, tot - p))).astype(BF16)
        if half < SCORE_BLK:
            for i in range(nblk):
                diag[i] = jnp.where(lev == level, _dot_nt(blk(z, i), blk(z, i)), diag[i])
        else:
            hb = half // SCORE_BLK
            for i in range(nblk):
                if (i // hb) % 2 == 1:
                    base = (i // (2 * hb)) * 2 * hb
                    for j in range(base, base + hb):
                        off[(i, j)] = _dot_nt(blk(z, i), blk(z, j))
        partner = jnp.where(upper, pltpu.roll(tot, half, 0), pltpu.roll(tot, cl - half, 0))
        p = p + jnp.where(upper, partner, 0.0)
        tot = tot + partner
        half *= 2
        level += 1
    dg = jnp.sum(q * k, axis=-1, keepdims=True)
    rows = []
    for i in range(nblk):
        d_i = jnp.where(lev == -1, blk(dg, i), diag[i])
        rows.append(jnp.concatenate([off[(i, j)] for j in range(i)] + [d_i], axis=-1))
    return rows, p, tot


def _gla_heads(q_ref, k_ref, v_ref, g_ref, la_ref, lev_ref, gout_ref, st_ref, new_ref, fresh, valid, at_slot=None):
    cl = q_ref.shape[0]
    lev = lev_ref[...]
    row = lax.broadcasted_iota(jnp.int32, (cl, GLA_DK), 0)
    uppers = []
    half = 1
    while half < cl:
        uppers.append((row & half) != 0)
        half *= 2
    nlev = len(uppers)
    for h in range(GLA_HEADS):
        ks = slice(h * GLA_DK, (h + 1) * GLA_DK)
        vs = slice(h * GLA_DV, (h + 1) * GLA_DV)
        q = q_ref[:, ks] * (GLA_DK ** -0.5)
        k = k_ref[:, ks]
        hook = None if at_slot is None else (lambda level, h=h: at_slot(h * nlev + level))
        rows, p, tot = _gla_chunk(q, k, la_ref[:, ks] * LOG2E, lev, uppers, hook)
        vb = v_ref[:, vs].astype(BF16)
        st_old = jnp.where(fresh, 0.0, st_ref[h])
        qt = (q * jnp.exp2(p)).astype(BF16)
        intra = jnp.concatenate([_dot(r.astype(BF16), vb[:r.shape[1]]) for r in rows], axis=0)
        o = intra + _dot_nt(qt, st_old.astype(BF16))
        kt = (k * jnp.exp2(tot - p)).astype(BF16)
        st_new = st_old * jnp.exp2(tot[0:1, :]) + _dot_tn(vb, kt)
        st_ref[h] = jnp.where(valid, st_new, st_old)
        new_ref[:, RET_W + h * GLA_DV: RET_W + (h + 1) * GLA_DV] = (
            _rms(o) * gout_ref[:, vs] * _silu(g_ref[:, vs])).astype(new_ref.dtype)


def _mixer_body(dmat_ref, lg_ref, q_ref, k_ref, rv_ref, rg_ref, gq_ref, gk_ref, gv_ref, gr_ref, la_ref, lev_ref,
                gro_ref, ggo_ref, x_ref, gt_ref, sc_ref, sh_ref, gf_ref, w_ref,
                x1_ref, h_ref, sr_out, sg_out, s_ref, st_ref, mixa_ref, mixb_ref, *, nc, n_chunks):
    s = pl.program_id(0)
    valid = s < n_chunks
    c = jnp.minimum(s, n_chunks - 1) % nc
    fresh = c == 0

    @pl.when(s == 0)
    def _():
        mixb_ref[...] = jnp.zeros_like(mixb_ref)

    def step(old_ref, new_ref):
        old = old_ref[...]
        d = x_ref.shape[1]
        nblk = d // PROJ_BLK
        ssq = [jnp.zeros((x_ref.shape[0], 1), F32)]

        def project(blocks):
            for cb in blocks:
                cs = slice(cb * PROJ_BLK, (cb + 1) * PROJ_BLK)
                x1 = x_ref[:, cs] + gt_ref[0][:, cs] * _dot(old, w_ref[:, cs])
                x1_ref[:, cs] = x1
                ssq[0] = ssq[0] + jnp.sum(x1 * x1, axis=-1, keepdims=True)

        nslots = GLA_HEADS * (x_ref.shape[0].bit_length() - 1)
        due = {(cb * nslots) // nblk: cb for cb in range(nblk)}
        _gla_heads(gq_ref, gk_ref, gv_ref, gr_ref, la_ref, lev_ref, ggo_ref, st_ref, new_ref, fresh, valid,
                   at_slot=lambda slot: project([due[slot]]) if slot in due else None)
        _ret_heads(dmat_ref, lg_ref, q_ref, k_ref, rv_ref, rg_ref, gro_ref, s_ref, new_ref, fresh, valid)
        inv = lax.rsqrt(ssq[0] * (1.0 / d) + EPS)
        gain = gf_ref[...] * (1.0 + sc_ref[0])
        for cb in range(nblk):
            cs = slice(cb * PROJ_BLK, (cb + 1) * PROJ_BLK)
            h_ref[:, cs] = (x1_ref[:, cs] * inv * gain[:, cs] + sh_ref[0][:, cs]).astype(h_ref.dtype)

    @pl.when(s % 2 == 0)
    def _():
        step(mixb_ref, mixa_ref)

    @pl.when(s % 2 == 1)
    def _():
        step(mixa_ref, mixb_ref)

    @pl.when(valid & (c == nc - 1))
    def _():
        sr_out[0] = s_ref[...]
        for h in range(GLA_HEADS):
            sg_out[0, h] = st_ref[h].T


def _mixer_prompt(qk, rest, la, x, gt, sc, sh, g_ret_out, g_gla_out, g_ffn, w_out_b, *, batch, seq, chunk):
    m, d = x.shape
    nc = seq // chunk
    n_chunks = batch * nc
    lg = _ret_log_gamma_tile()
    lev = jnp.asarray(_gla_level_map(SCORE_BLK))
    gqk_w = GLA_HEADS * GLA_DK
    cur = lambda s: jnp.minimum(s, n_chunks - 1)
    prv = lambda s: jnp.maximum(s - 1, 0)
    rowc = lambda w, blk: pl.BlockSpec((chunk, w), lambda s: (cur(s), blk))
    const = lambda shape: pl.BlockSpec(shape, lambda s: (0,) * len(shape))
    mod_spec = pl.BlockSpec((1, 1, d), lambda s: (prv(s) // nc, 0, 0))
    return pl.pallas_call(
        functools.partial(_mixer_body, nc=nc, n_chunks=n_chunks),
        grid=(n_chunks + 1,),
        in_specs=[const((RET_HEADS, chunk, chunk)), const((RET_HEADS, SUBLANES, LANES)),
                  rowc(RET_W, 0), rowc(RET_W, 1),
                  rowc(RET_W, R_RV // RET_W), rowc(RET_W, R_RG // RET_W),
                  rowc(gqk_w, R_GQ // gqk_w), rowc(gqk_w, R_GK // gqk_w),
                  rowc(GLA_W, R_GV // GLA_W), rowc(GLA_W, R_GR // GLA_W),
                  rowc(gqk_w, 0),
                  const((SCORE_BLK, SCORE_BLK)), const((1, RET_W)), const((1, GLA_W)),
                  pl.BlockSpec((chunk, d), lambda s: (prv(s), 0)),
                  mod_spec, mod_spec, mod_spec,
                  const((1, d)),
                  pl.BlockSpec((RET_W + GLA_W, d), lambda s: (0, 0), pipeline_mode=pl.Buffered(1))],
        out_specs=[pl.BlockSpec((chunk, d), lambda s: (prv(s), 0)),
                   pl.BlockSpec((chunk, d), lambda s: (prv(s), 0)),
                   pl.BlockSpec((1, RET_HEADS, RET_DK, RET_DV), lambda s: (cur(s) // nc, 0, 0, 0)),
                   pl.BlockSpec((1, GLA_HEADS, GLA_DK, GLA_DV), lambda s: (cur(s) // nc, 0, 0, 0))],
        out_shape=[jax.ShapeDtypeStruct((m, d), F32),
                   jax.ShapeDtypeStruct((m, d), BF16),
                   jax.ShapeDtypeStruct((batch, RET_HEADS, RET_DK, RET_DV), F32),
                   jax.ShapeDtypeStruct((batch, GLA_HEADS, GLA_DK, GLA_DV), F32)],
        scratch_shapes=[pltpu.VMEM((RET_HEADS, RET_DK, RET_DV), F32),
                        pltpu.VMEM((GLA_HEADS, GLA_DV, GLA_DK), F32),
                        pltpu.VMEM((chunk, RET_W + GLA_W), BF16),
                        pltpu.VMEM((chunk, RET_W + GLA_W), BF16)],
        compiler_params=_params(("arbitrary",)),
        name="mixer_prompt",
    )(_ret_decay_matrix(chunk), lg, qk, qk, rest, rest, rest, rest, rest, rest, la, lev,
      g_ret_out.reshape(1, RET_W), g_gla_out.reshape(1, GLA_W), x, gt, sc, sh, g_ffn.reshape(1, d), w_out_b)


def _columns(rows):
    n = rows[0].shape[1]
    pad = (-len(rows)) % 8
    stack = jnp.concatenate(rows + [jnp.zeros((pad, n), F32)] if pad else rows, axis=0)
    return stack.T


def _state_body(r0, lg_ref, qk_ref, rest_ref, la_ref, sr_ref, sg_ref, gro_ref, ggo_ref, o_ref, sr_out, sg_out):
    nb = sr_ref.shape[1]
    for i in range(nb):
        row = pl.ds(r0 + i, 1)
        qkrow = qk_ref[row, :]
        rrow = rest_ref[row, :]
        larow = la_ref[row, :]
        rcols = _columns([qkrow[:, j * RET_DK:(j + 1) * RET_DK] for j in range(2 * RET_HEADS)])
        gq = [rrow[:, R_GQ + h * GLA_DK: R_GQ + (h + 1) * GLA_DK] * (GLA_DK ** -0.5) for h in range(GLA_HEADS)]
        gk = [rrow[:, R_GK + h * GLA_DK: R_GK + (h + 1) * GLA_DK] for h in range(GLA_HEADS)]
        ga = [jnp.exp(larow[:, h * GLA_DK:(h + 1) * GLA_DK]) for h in range(GLA_HEADS)]
        gcols = _columns(gq + gk + ga)
        outs = []
        for h in range(RET_HEADS):
            v = rrow[:, R_RV + h * RET_DV: R_RV + (h + 1) * RET_DV]
            gate = rrow[:, R_RG + h * RET_DV: R_RG + (h + 1) * RET_DV]
            gamma = jnp.exp(lg_ref[h][:1, :1])
            qc = rcols[:, h:h + 1]
            kc = rcols[:, RET_HEADS + h:RET_HEADS + h + 1]
            s_new = sr_ref[0, i, h] * gamma + kc * v
            sr_out[0, i, h] = s_new
            o = jnp.sum(qc * s_new, axis=0, keepdims=True)
            outs.append(_rms(o) * gro_ref[:, h * RET_DV:(h + 1) * RET_DV] * _silu(gate))
        for h in range(GLA_HEADS):
            v = rrow[:, R_GV + h * GLA_DV: R_GV + (h + 1) * GLA_DV]
            gate = rrow[:, R_GR + h * GLA_DV: R_GR + (h + 1) * GLA_DV]
            qc = gcols[:, h:h + 1]
            kc = gcols[:, GLA_HEADS + h:GLA_HEADS + h + 1]
            ac = gcols[:, 2 * GLA_HEADS + h:2 * GLA_HEADS + h + 1]
            s_new = sg_ref[0, i, h] * ac + kc * v
            sg_out[0, i, h] = s_new
            o = jnp.sum(qc * s_new, axis=0, keepdims=True)
            outs.append(_rms(o) * ggo_ref[:, h * GLA_DV:(h + 1) * GLA_DV] * _silu(gate))
        o_ref[row, :] = jnp.concatenate(outs, axis=-1)


def _outproj_body(oa_ref, ob_ref, x_ref, gt_ref, sc_ref, sh_ref, g_ref, w_ref, x1_ref, h_ref):
    ka = oa_ref.shape[1]
    mix = _dot(oa_ref[...].astype(BF16), w_ref[:ka, :]) + _dot(ob_ref[...].astype(BF16), w_ref[ka:, :])
    x1 = x_ref[...] + gt_ref[0] * mix
    x1_ref[...] = x1
    h = _rms(x1) * g_ref[...]
    h_ref[...] = (h * (1.0 + sc_ref[0]) + sh_ref[0]).astype(BF16)


def _outproj(o_a, o_b, x, gt, sc, sh, g_ffn, w_out_bf16, *, tm, rows_per_mod):
    m, d = x.shape
    ka, kb = o_a.shape[1], o_b.shape[1]
    return pl.pallas_call(
        _outproj_body,
        grid=(m // tm,),
        in_specs=[pl.BlockSpec((tm, ka), lambda i: (i, 0)),
                  pl.BlockSpec((tm, kb), lambda i: (i, 0)),
                  pl.BlockSpec((tm, d), lambda i: (i, 0)),
                  *(_mod_spec(v, d, tm, rows_per_mod) for v in (gt, sc, sh)),
                  pl.BlockSpec((1, d), lambda i: (0, 0)),
                  pl.BlockSpec((ka + kb, d), lambda i: (0, 0), pipeline_mode=pl.Buffered(1))],
        out_specs=[pl.BlockSpec((tm, d), lambda i: (i, 0)),
                   pl.BlockSpec((tm, d), lambda i: (i, 0))],
        out_shape=[jax.ShapeDtypeStruct((m, d), F32),
                   jax.ShapeDtypeStruct((m, d), BF16)],
        compiler_params=_params(("arbitrary",)),
        name="out_proj",
    )(o_a, o_b, x, gt[0], sc[0], sh[0], g_ffn.reshape(1, d), w_out_bf16)


HALO = SUBLANES


def _ffn_up_body(h_ref, wa_ref, wb_ref, cwa_ref, cwb_ref, cba_ref, cbb_ref, g_ref, cs_ref, win_ref, tail_ref, *, tiles_per_seq):
    i, f = pl.program_id(0), pl.program_id(1)
    hb = h_ref[...]
    tm = hb.shape[0]
    first = (i % tiles_per_seq) == 0
    ucs = []
    for part, (w_ref, cw_ref, cb_ref) in enumerate(((wa_ref, cwa_ref, cba_ref), (wb_ref, cwb_ref, cbb_ref))):
        u = _dot(hb, w_ref[...])
        prev = jnp.where(first, 0.0, tail_ref[part, f])
        pieces = []
        for c in range(u.shape[1] // LANES):
            lc = slice(c * LANES, (c + 1) * LANES)
            win_ref[part, c, 0:HALO, :] = prev[:, lc]
            win_ref[part, c, HALO:HALO + tm, :] = u[:, lc]
            cw, cb = cw_ref[:, lc], cb_ref[:, lc]
            pieces.append(cb + cw[0:1] * win_ref[part, c, HALO - 2:HALO - 2 + tm, :]
                          + cw[1:2] * win_ref[part, c, HALO - 1:HALO - 1 + tm, :] + cw[2:3] * u[:, lc])
        ucs.append(jnp.concatenate(pieces, axis=-1))
        tail_ref[part, f] = u[tm - HALO:]
        for r in range(CONV_W - 1):
            cs_ref[0, 0, r, part:part + 1, :] = u[tm - (CONV_W - 1) + r: tm - (CONV_W - 1) + r + 1]
    g_ref[...] = (_silu(ucs[0]) * ucs[1]).astype(g_ref.dtype)


def _ffn_up_state_body(h_ref, wa_ref, wb_ref, cwa_ref, cwb_ref, cba_ref, cbb_ref,
                       lg_ref, qk_ref, rest_ref, la_ref, sr_ref, sg_ref, gro_ref, ggo_ref, wd_ref,
                       g_ref, cs_ref, o_ref, sr_out, sg_out, wdb_ref, win_ref, tail_ref, *, tiles_per_seq, nblk):
    _ffn_up_body(h_ref, wa_ref, wb_ref, cwa_ref, cwb_ref, cba_ref, cbb_ref, g_ref, cs_ref, win_ref, tail_ref,
                 tiles_per_seq=tiles_per_seq)
    nb = sr_ref.shape[1]
    blk = jnp.minimum(pl.program_id(0) * pl.num_programs(1) + pl.program_id(1), nblk - 1)
    r0 = (blk % (SUBLANES // nb)) * nb
    _state_body(r0, lg_ref, qk_ref, rest_ref, la_ref, sr_ref, sg_ref, gro_ref, ggo_ref, o_ref, sr_out, sg_out)
    wdb_ref[...] = wd_ref[...].astype(wdb_ref.dtype)


def _ffn_up_prompt(h, w_up, conv_w, conv_b, qk_s, rest_s, la_s, state_ret, state_gla, g_ret_out, g_gla_out, w_down,
                   *, batch, seq, tm, tf, nb, cast_rows):
    m, d = h.shape
    ff = w_up.shape[1] // 2
    nf = ff // tf
    cb = conv_b.reshape(1, 2 * ff)
    tps = seq // tm
    ms = qk_s.shape[0]
    nblk = ms // nb
    ncast = ff // cast_rows
    assert (m // tm) * nf >= nblk + ncast
    assert SUBLANES % nb == 0 and ms % SUBLANES == 0
    mix = RET_W + GLA_W
    lg = _ret_log_gamma_tile()
    sblk = lambda i, f: jnp.minimum(i * nf + f, nblk - 1)
    ret_spec = pl.BlockSpec((1, nb, RET_HEADS, RET_DK, RET_DV), lambda i, f: (0, sblk(i, f), 0, 0, 0))
    gla_spec = pl.BlockSpec((1, nb, GLA_HEADS, GLA_DK, GLA_DV), lambda i, f: (0, sblk(i, f), 0, 0, 0))
    row_spec = lambda w: pl.BlockSpec((SUBLANES, w), lambda i, f: (sblk(i, f) // (SUBLANES // nb), 0))
    cast_spec = pl.BlockSpec((cast_rows, d), lambda i, f: (jnp.clip(i * nf + f - nblk, 0, ncast - 1), 0))
    body = functools.partial(_ffn_up_state_body, tiles_per_seq=tps, nblk=nblk)
    gate, tails, o_s, s_ret, s_gla, w_down_b = pl.pallas_call(
        body,
        grid=(m // tm, nf),
        in_specs=[pl.BlockSpec((tm, d), lambda i, f: (i, 0)),
                  pl.BlockSpec((d, tf), lambda i, f: (0, f)),
                  pl.BlockSpec((d, tf), lambda i, f: (0, nf + f)),
                  pl.BlockSpec((CONV_W, tf), lambda i, f: (0, f)),
                  pl.BlockSpec((CONV_W, tf), lambda i, f: (0, nf + f)),
                  pl.BlockSpec((1, tf), lambda i, f: (0, f)),
                  pl.BlockSpec((1, tf), lambda i, f: (0, nf + f)),
                  pl.BlockSpec((RET_HEADS, SUBLANES, LANES), lambda i, f: (0, 0, 0)),
                  row_spec(QK_W), row_spec(REST_W), row_spec(la_s.shape[1]),
                  ret_spec, gla_spec,
                  pl.BlockSpec((1, RET_W), lambda i, f: (0, 0)),
                  pl.BlockSpec((1, GLA_W), lambda i, f: (0, 0)),
                  cast_spec],
        out_specs=[pl.BlockSpec((tm, tf), lambda i, f: (i, f)),
                   pl.BlockSpec((1, 1, CONV_W - 1, 2, tf), lambda i, f: (i // tps, i % tps, 0, 0, f)),
                   row_spec(mix), ret_spec, gla_spec, cast_spec],
        out_shape=[jax.ShapeDtypeStruct((m, ff), BF16),
                   jax.ShapeDtypeStruct((batch, tps, CONV_W - 1, 2, ff), F32),
                   jax.ShapeDtypeStruct((ms, mix), F32),
                   jax.ShapeDtypeStruct(state_ret.shape, F32),
                   jax.ShapeDtypeStruct(state_gla.shape, F32),
                   jax.ShapeDtypeStruct(w_down.shape, BF16)],
        scratch_shapes=[pltpu.VMEM((2, tf // LANES, HALO + tm, LANES), F32), pltpu.VMEM((2, nf, HALO, tf), F32)],
        compiler_params=_params(("arbitrary", "arbitrary")),
        name="ffn_up_prompt",
    )(h, w_up, w_up, conv_w, conv_w, cb, cb,
      lg, qk_s.astype(F32), rest_s, la_s,
      state_ret, state_gla, g_ret_out.reshape(1, RET_W), g_gla_out.reshape(1, GLA_W), w_down)
    return gate, tails[:, tps - 1], o_s, s_ret, s_gla, w_down_b


def _ffn_up_step_body(h_ref, w_ref, cw_ref, cb_ref, st_ref, g_ref, cs_ref, uca_ref, *, nf):
    j = pl.program_id(0)
    u = _dot(h_ref[...], w_ref[...])
    s1 = st_ref[0, :, 1, :]
    cw = cw_ref[...]
    uc = cb_ref[...] + cw[0:1] * st_ref[0, :, 0, :] + cw[1:2] * s1 + cw[2:3] * u
    cs_ref[0, :, 0, :] = s1
    cs_ref[0, :, 1, :] = u

    @pl.when(j < nf)
    def _():
        uca_ref[j] = uc

    @pl.when(j >= nf)
    def _():
        g_ref[...] = (_silu(uca_ref[j - nf]) * uc).astype(g_ref.dtype)


def _ffn_up_step(h, w_up, conv_w, conv_b, state_conv, *, tf):
    m, d = h.shape
    ff = w_up.shape[1] // 2
    nf = ff // tf
    st_spec = pl.BlockSpec((1, m, CONV_W - 1, tf), lambda j: (0, 0, 0, j))
    return pl.pallas_call(
        functools.partial(_ffn_up_step_body, nf=nf),
        grid=(2 * nf,),
        in_specs=[pl.BlockSpec((m, d), lambda j: (0, 0)),
                  pl.BlockSpec((d, tf), lambda j: (0, j)),
                  pl.BlockSpec((CONV_W, tf), lambda j: (0, j)),
                  pl.BlockSpec((1, tf), lambda j: (0, j)),
                  st_spec],
        out_specs=[pl.BlockSpec((m, tf), lambda j: (0, jnp.maximum(j - nf, 0))), st_spec],
        out_shape=[jax.ShapeDtypeStruct((m, ff), BF16),
                   jax.ShapeDtypeStruct(state_conv.shape, F32)],
        scratch_shapes=[pltpu.VMEM((nf, m, tf), F32)],
        compiler_params=_params(("arbitrary",)),
        name="ffn_up_step",
    )(h, w_up, conv_w, conv_b.reshape(1, 2 * ff), state_conv)


def _ffn_down_body(g_ref, w_ref, x1_ref, gt_ref, gf_ref, y_ref, xs_ref, inv_ref):
    k = pl.program_id(1)
    last = pl.num_programs(1) - 1
    d = y_ref.shape[1]
    xb = x1_ref.shape[1]
    xs_ref[k] = x1_ref[...]

    @pl.when(k == 0)
    def _():
        for cb in range(d // PROJ_BLK):
            cs = slice(cb * PROJ_BLK, (cb + 1) * PROJ_BLK)
            y_ref[:, cs] = _dot(g_ref[...], w_ref[:, cs])

    @pl.when((k > 0) & (k < last))
    def _():
        for cb in range(d // PROJ_BLK):
            cs = slice(cb * PROJ_BLK, (cb + 1) * PROJ_BLK)
            y_ref[:, cs] += _dot(g_ref[...], w_ref[:, cs])

    @pl.when(k == last)
    def _():
        ssq = jnp.zeros((y_ref.shape[0], 1), F32)
        for cb in range(d // PROJ_BLK):
            cs = slice(cb * PROJ_BLK, (cb + 1) * PROJ_BLK)
            x1 = xs_ref[(cb * PROJ_BLK) // xb, :, (cb * PROJ_BLK) % xb: (cb * PROJ_BLK) % xb + PROJ_BLK]
            x2 = x1 + gt_ref[0][:, cs] * (y_ref[:, cs] + _dot(g_ref[...], w_ref[:, cs]))
            y_ref[:, cs] = x2
            ssq = ssq + jnp.sum(x2 * x2, axis=-1, keepdims=True)
        inv_ref[...] = lax.rsqrt(ssq * (1.0 / d) + EPS)

        rg = min(y_ref.shape[0], NORM_ROWS)

        def norm_rows(r, carry):
            rows = pl.ds(pl.multiple_of(r * rg, rg), rg)
            y_ref[rows, :] = y_ref[rows, :] * inv_ref[rows, :] * gf_ref[...]
            return carry

        lax.fori_loop(0, y_ref.shape[0] // rg, norm_rows, 0)


def _ffn_down(g, w_down, x1, gt, g_final, *, tm, tk, rows_per_mod):
    m, d = x1.shape
    ff = g.shape[1]
    nk = ff // tk
    assert nk >= 2
    xb = d // nk
    assert d % nk == 0 and xb % PROJ_BLK == 0
    return pl.pallas_call(
        _ffn_down_body,
        grid=(m // tm, nk),
        in_specs=[pl.BlockSpec((tm, tk), lambda i, k: (i, k)),
                  pl.BlockSpec((tk, d), lambda i, k: (k, 0)),
                  pl.BlockSpec((tm, xb), lambda i, k: (i, k)),
                  _mod_spec(gt, d, tm, rows_per_mod),
                  pl.BlockSpec((1, d), lambda i, k: (0, 0))],
        out_specs=pl.BlockSpec((tm, d), lambda i, k: (i, 0)),
        out_shape=jax.ShapeDtypeStruct((m, d), F32),
        scratch_shapes=[pltpu.VMEM((nk, tm, xb), F32), pltpu.VMEM((tm, 1), F32)],
        compiler_params=_params(("arbitrary", "arbitrary")),
        name="ffn_down",
    )(g, w_down, x1, gt[0], g_final.reshape(1, d))


def _rope_tables(pos, rows=None):
    half = RET_DK // 2
    inv = ROPE_THETA ** (-np.arange(half, dtype=np.float64) / half)
    ang = np.asarray(pos, np.float64)[:, None] * inv[None, :]
    tabs = (np.cos(ang), np.sin(ang))
    if rows is not None:
        tabs = tuple(np.broadcast_to(t, (rows, half)) for t in tabs)
    return tuple(jnp.asarray(t, F32) for t in tabs)


def kernel(x_prompt, x_sample, c_prompt, c_sample, state_ret, state_gla, state_conv, w_ada, b_ada, g_attn, w_in, w_a2, b_a2, g_ret_out, g_gla_out, w_out, g_ffn, w_up, conv_w, conv_b, w_down, g_final):
    bp, t_p, d = x_prompt.shape
    bs, t_s, _ = x_sample.shape
    assert t_s == 1 and w_ada.shape[0] == 1
    mp = bp * t_p
    w_ada, b_ada, g_attn, w_in, w_a2, b_a2, g_ret_out, g_gla_out, w_out, g_ffn, w_up, conv_w, conv_b, w_down = (
        a[0] for a in (w_ada, b_ada, g_attn, w_in, w_a2, b_a2, g_ret_out, g_gla_out, w_out, g_ffn, w_up, conv_w, conv_b, w_down))

    mod = _ada(jnp.concatenate([c_sample, c_prompt], axis=0), w_ada, b_ada)
    sh1p, sc1p, gt1p, sh2p, sc2p, gt2p = (mod[bs:, i * d:(i + 1) * d].reshape(bp, 1, d) for i in range(6))
    sh1s, sc1s, gt1s, sh2s, sc2s, gt2s = ((mod[None], i) for i in range(6))

    cos_p, sin_p = _rope_tables(np.arange(t_p))
    cos_s, sin_s = _rope_tables(PAST_LEN + np.arange(t_s), rows=bs)
    w_out_b = w_out.astype(BF16)
    w_in_t = w_in.T
    w_gate_t = w_in_t[IN_MAIN:]

    xp = x_prompt.reshape(mp, d)
    xs = x_sample.reshape(bs, d)
    qk_s, rest_s, la_s, w_in_b = _inproj(xs, sc1s, sh1s, g_attn, w_in_t, w_gate_t, w_a2, b_a2, cos_s, sin_s,
                                         tm=bs, rows_per_mod=bs, cast_w=True)
    qk_p, rest_p, la_p, w_up_b = _inproj(xp, (sc1p, 0), (sh1p, 0), g_attn, w_in_b, w_gate_t, w_a2, b_a2, cos_p, sin_p,
                                         tm=ROW_TILE, rows_per_mod=t_p, cast_src=w_up)
    x1_p, h2_p, s_ret_p, s_gla_p = _mixer_prompt(qk_p, rest_p, la_p, xp, gt1p, sc2p, sh2p, g_ret_out, g_gla_out, g_ffn,
                                                 w_out_b, batch=bp, seq=t_p, chunk=CHUNK)
    g_p, cs_p, o_s, s_ret_s, s_gla_s, w_down_b = _ffn_up_prompt(
        h2_p, w_up_b, conv_w, conv_b, qk_s, rest_s, la_s, state_ret, state_gla, g_ret_out, g_gla_out, w_down,
        batch=bp, seq=t_p, tm=ROW_TILE, tf=FF_COL_TILE, nb=STATE_SEQS, cast_rows=CAST_ROWS)
    y_p = _ffn_down(g_p, w_down_b, x1_p, (gt2p, 0), g_final, tm=ROW_TILE, tk=DOWN_K_TILE, rows_per_mod=t_p)

    x1_s, h2_s = _outproj(o_s[:, :RET_W], o_s[:, RET_W:], xs, gt1s, sc2s, sh2s, g_ffn, w_out_b, tm=bs, rows_per_mod=bs)
    ff = w_down.shape[0]
    g_s, cs_s = _ffn_up_step(h2_s, w_up_b, conv_w, conv_b, state_conv, tf=ff // 4)
    y_s = _ffn_down(g_s, w_down_b, x1_s, gt2s, g_final, tm=bs, tk=ff // 2, rows_per_mod=bs)

    return (y_p.reshape(bp, t_p, d), y_s.reshape(bs, t_s, d),
            s_ret_p[None], s_ret_s, s_gla_p[None], s_gla_s,
            cs_p.reshape(1, bp, CONV_W - 1, -1), cs_s)
```

```python
import functools

import numpy as np
import jax
import jax.numpy as jnp
from jax import lax
from jax.experimental import pallas as pl
from jax.experimental.pallas import tpu as pltpu

F32 = jnp.float32
BF16 = jnp.bfloat16

RET_HEADS = 4
RET_DK = 256
RET_DV = 256
GLA_HEADS = 4
GLA_DK = 128
GLA_DV = 256
GLA_RANK = 16
GLA_TAU = 16.0
ROPE_THETA = 10000.0
PAST_LEN = 16384
CONV_W = 3
EPS = 1e-6

RET_W = RET_HEADS * RET_DV
GLA_W = GLA_HEADS * GLA_DV
QK_W = 2 * RET_HEADS * RET_DK
R_RV = 0
R_RG = R_RV + RET_W
R_GQ = R_RG + RET_W
R_GK = R_GQ + GLA_HEADS * GLA_DK
R_GV = R_GK + GLA_HEADS * GLA_DK
R_GR = R_GV + GLA_W
REST_W = R_GR + GLA_W
IN_MAIN = QK_W + REST_W

V7X_VMEM_BYTES = 64 * 1024 * 1024
V7X_MXU_WIDTH = 256
LANES = 128
SUBLANES = 8
VMEM_LIMIT_BYTES = V7X_VMEM_BYTES - 8 * 1024 * 1024
PROJ_BLK = V7X_MXU_WIDTH

ROW_TILE = 1024
IN_COL_TILE = 1024
FF_COL_TILE = 512
DOWN_K_TILE = 1408
CHUNK = 256
STATE_SEQS = 2
CAST_ROWS = 256
CAST_COLS = 256
NORM_ROWS = 128


def _params(semantics):
    return pltpu.CompilerParams(dimension_semantics=semantics, vmem_limit_bytes=VMEM_LIMIT_BYTES)


def _dot(a, b):
    return jnp.dot(a, b, preferred_element_type=F32)


def _dot_nt(a, b):
    return lax.dot_general(a, b, (((1,), (1,)), ((), ())), preferred_element_type=F32)


def _dot_tn(a, b):
    return lax.dot_general(a, b, (((0,), (0,)), ((), ())), preferred_element_type=F32)


def _silu(x):
    return x * jax.nn.sigmoid(x)


def _rms(x):
    return x * lax.rsqrt(jnp.mean(x * x, axis=-1, keepdims=True) + EPS)


def _mod_spec(mod, d, tm, rows_per_mod):
    arr, col = mod
    rows = 1 if arr.shape[1] == 1 else tm
    return pl.BlockSpec((1, rows, d), lambda i, *_: ((i * tm) // rows_per_mod, 0, col))


def _ada_body(c_ref, w_ref, b_ref, o_ref):
    s = _silu(c_ref[...]).astype(BF16)
    o_ref[...] = _dot(s, w_ref[...].astype(BF16)) + b_ref[...]


def _ada(c_all, w_ada, b_ada, tn=FF_COL_TILE):
    r, d = c_all.shape
    n = w_ada.shape[1]
    return pl.pallas_call(
        _ada_body,
        grid=(n // tn,),
        in_specs=[pl.BlockSpec((r, d), lambda j: (0, 0)),
                  pl.BlockSpec((d, tn), lambda j: (0, j)),
                  pl.BlockSpec((1, tn), lambda j: (0, j))],
        out_specs=pl.BlockSpec((r, tn), lambda j: (0, j)),
        out_shape=jax.ShapeDtypeStruct((r, n), F32),
        compiler_params=_params(("arbitrary",)),
        name="ada_mod",
    )(c_all, w_ada, b_ada.reshape(1, n))


def _inproj_body(*refs, n_qk, cast_w, rider):
    refs = list(refs)
    x_ref, sc_ref, sh_ref, g_ref, w_ref, wga_ref, wa2_ref, ba2_ref, cos_ref, sin_ref = refs[:10]
    del refs[:10]
    src_ref = refs.pop(0) if rider else None
    qk_ref, rest_ref, la_ref = refs[:3]
    del refs[:3]
    wcast_ref = refs.pop(0) if cast_w else None
    dst_ref = refs.pop(0) if rider else None
    (h_ref,) = refs
    j = pl.program_id(1)

    def ride():
        if rider:
            dst_ref[...] = src_ref[...].astype(dst_ref.dtype)

    @pl.when(j == 0)
    def _():
        gain = g_ref[...] * (1.0 + sc_ref[0])
        hb = (_rms(x_ref[...]) * gain + sh_ref[0]).astype(BF16)
        h_ref[...] = hb
        ga = _dot_nt(hb, wga_ref[...].astype(BF16))
        z = _dot(ga.astype(BF16), wa2_ref[...].astype(BF16)) + ba2_ref[...]
        la_ref[...] = (jnp.minimum(z, 0.0) - jnp.log1p(jnp.exp(-jnp.abs(z)))) * (1.0 / GLA_TAU)

    if cast_w:
        w_tile = w_ref[...].T.astype(BF16)
        wcast_ref[...] = w_tile
        weight = lambda: w_tile
    else:
        weight = lambda: w_ref[...]

    @pl.when(j < n_qk)
    def _():
        ride()
        acc = _dot(h_ref[...], weight())
        cos, sin = cos_ref[...], sin_ref[...]
        half = RET_DK // 2
        scale = jnp.where(j >= n_qk // 2, RET_DK ** -0.5, 1.0)
        outs = []
        for hh in range(acc.shape[1] // RET_DK):
            x1 = acc[:, hh * RET_DK: hh * RET_DK + half]
            x2 = acc[:, hh * RET_DK + half: (hh + 1) * RET_DK]
            outs += [x1 * cos - x2 * sin, x1 * sin + x2 * cos]
        qk_ref[...] = (jnp.concatenate(outs, axis=-1) * scale).astype(qk_ref.dtype)

    @pl.when(j >= n_qk)
    def _():
        ride()
        rest_ref[...] = _dot(h_ref[...], weight())


def _inproj(x, sc, sh, g_attn, w, w_gate_t, w_a2, b_a2, cos, sin, *, tm, rows_per_mod, cast_w=False, cast_src=None,
            qk_dtype=BF16, tn=IN_COL_TILE):
    m, d = x.shape
    nq = w_a2.shape[1]
    n_qk = QK_W // tn
    tab_tiles = cos.shape[0] // tm
    n_j = IN_MAIN // tn
    once = pl.Buffered(1)
    sc_spec, sh_spec = (_mod_spec(v, d, tm, rows_per_mod) for v in (sc, sh))
    tab_spec = pl.BlockSpec((tm, RET_DK // 2), lambda i, j: (i % tab_tiles, 0), pipeline_mode=once)
    out_specs = [pl.BlockSpec((tm, tn), lambda i, j: (i, jnp.minimum(j, n_qk - 1))),
                 pl.BlockSpec((tm, tn), lambda i, j: (i, jnp.maximum(j - n_qk, 0))),
                 pl.BlockSpec((tm, nq), lambda i, j: (i, 0), pipeline_mode=once)]
    out_shape = [jax.ShapeDtypeStruct((m, QK_W), qk_dtype),
                 jax.ShapeDtypeStruct((m, REST_W), F32),
                 jax.ShapeDtypeStruct((m, nq), F32)]
    if cast_w:
        assert m == tm
        w_spec = pl.BlockSpec((tn, d), lambda i, j: (j, 0))
        out_specs.append(pl.BlockSpec((d, tn), lambda i, j: (0, j)))
        out_shape.append(jax.ShapeDtypeStruct((d, IN_MAIN), BF16))
    else:
        w_spec = pl.BlockSpec((d, tn), lambda i, j: (0, j))
    in_specs = [pl.BlockSpec((tm, d), lambda i, j: (i, 0)),
                sc_spec, sh_spec,
                pl.BlockSpec((1, d), lambda i, j: (0, 0)),
                w_spec,
                pl.BlockSpec((GLA_RANK, d), lambda i, j: (0, 0)),
                pl.BlockSpec((GLA_RANK, nq), lambda i, j: (0, 0)),
                pl.BlockSpec((1, nq), lambda i, j: (0, 0)),
                tab_spec, tab_spec]
    args = [x, sc[0], sh[0], g_attn.reshape(1, d), w, w_gate_t, w_a2, b_a2.reshape(1, nq), cos, sin]
    if cast_src is not None:
        ncast = cast_src.shape[1] // CAST_COLS
        assert (m // tm) * n_j >= ncast
        ride_spec = pl.BlockSpec((cast_src.shape[0], CAST_COLS), lambda i, j: (0, jnp.minimum(i * n_j + j, ncast - 1)))
        in_specs.append(ride_spec)
        args.append(cast_src)
        out_specs.append(ride_spec)
        out_shape.append(jax.ShapeDtypeStruct(cast_src.shape, BF16))
    return pl.pallas_call(
        functools.partial(_inproj_body, n_qk=n_qk, cast_w=cast_w, rider=cast_src is not None),
        grid=(m // tm, n_j),
        in_specs=in_specs,
        out_specs=out_specs,
        out_shape=out_shape,
        scratch_shapes=[pltpu.VMEM((tm, d), BF16)],
        compiler_params=_params(("arbitrary", "arbitrary")),
        name="in_proj",
    )(*args)


def _ret_log_gamma():
    return np.log1p(-np.exp2(-5.0 - np.arange(RET_HEADS, dtype=np.float64)))


def _ret_log_gamma_tile():
    return jnp.asarray(np.broadcast_to(_ret_log_gamma()[:, None, None], (RET_HEADS, SUBLANES, LANES)), F32)


def _ret_decay_matrix(chunk):
    idx = np.arange(chunk, dtype=np.float64)
    rel = idx[:, None] - idx[None, :]
    lg = _ret_log_gamma()
    return jnp.asarray(np.where(rel[None] >= 0, np.exp(np.maximum(rel, 0.0)[None] * lg[:, None, None]), 0.0), F32)


def _ret_heads(dmat_ref, lg_ref, q_ref, k_ref, v_ref, g_ref, gout_ref, s_ref, new_ref, fresh, valid):
    cl = q_ref.shape[0]
    idx = lax.broadcasted_iota(jnp.int32, (cl, 1), 0).astype(F32)
    for h in range(RET_HEADS):
        sl = slice(h * RET_DV, (h + 1) * RET_DV)
        lg = lg_ref[h][:1, :1]
        q_dec = jnp.exp((idx + 1.0) * lg)
        k_dec = jnp.exp((cl - 1.0 - idx) * lg)
        c_dec = jnp.exp(cl * lg)
        qb, kb = q_ref[:, sl], k_ref[:, sl]
        v = v_ref[:, sl]
        vb = v.astype(BF16)
        s_old = jnp.where(fresh, 0.0, s_ref[h])
        scores = _dot_nt(qb, kb) * dmat_ref[h]
        o = _dot(scores.astype(BF16), vb) + _dot(qb, s_old.astype(BF16)) * q_dec
        s_new = s_old * c_dec + _dot_tn(kb, (v * k_dec).astype(BF16))
        s_ref[h] = jnp.where(valid, s_new, s_old)
        new_ref[:, sl] = (_rms(o) * gout_ref[:, sl] * _silu(g_ref[:, sl])).astype(new_ref.dtype)


def _gla_level_map(c):
    t = np.arange(c)[:, None]
    s = np.arange(c)[None, :]
    x = np.bitwise_xor(t, s)
    lev = np.floor(np.log2(np.maximum(x, 1))).astype(np.int32)
    lev = np.where(t == s, -1, lev)
    lev = np.where(t < s, -2, lev)
    return lev.astype(np.int32)


SCORE_BLK = 128
LOG2E = 1.4426950408889634


def _gla_chunk(q, k, la2, lev, uppers, at_level=None):
    cl = q.shape[0]
    nblk = cl // SCORE_BLK
    blk = lambda x, i: x[i * SCORE_BLK:(i + 1) * SCORE_BLK]
    p = la2
    tot = la2
    diag = [jnp.zeros((SCORE_BLK, SCORE_BLK), F32) for _ in range(nblk)]
    off = {}
    level = 0
    half = 1
    while half < cl:
        if at_level is not None:
            at_level(level)
        upper = uppers[level]
        z = (jnp.where(upper, q, k) * jnp.exp2(jnp.where(upper, p, tot - p))).astype(BF16)
        if half < SCORE_BLK:
            for i in range(nblk):
                diag[i] = jnp.where(lev == level, _dot_nt(blk(z, i), blk(z, i)), diag[i])
        else:
            hb = half // SCORE_BLK
            for i in range(nblk):
                if (i // hb) % 2 == 1:
                    base = (i // (2 * hb)) * 2 * hb
                    for j in range(base, base + hb):
                        off[(i, j)] = _dot_nt(blk(z, i), blk(z, j))
        partner = jnp.where(upper, pltpu.roll(tot, half, 0), pltpu.roll(tot, cl - half, 0))
        p = p + jnp.where(upper, partner, 0.0)
        tot = tot + partner
        half *= 2
        level += 1
    dg = jnp.sum(q * k, axis=-1, keepdims=True)
    rows = []
    for i in range(nblk):
        d_i = jnp.where(lev == -1, blk(dg, i), diag[i])
        rows.append(jnp.concatenate([off[(i, j)] for j in range(i)] + [d_i], axis=-1))
    return rows, p, tot


def _gla_heads(q_ref, k_ref, v_ref, g_ref, la_ref, lev_ref, gout_ref, st_ref, new_ref, fresh, valid, at_slot=None):
    cl = q_ref.shape[0]
    lev = lev_ref[...]
    row = lax.broadcasted_iota(jnp.int32, (cl, GLA_DK), 0)
    uppers = []
    half = 1
    while half < cl:
        uppers.append((row & half) != 0)
        half *= 2
    nlev = len(uppers)
    for h in range(GLA_HEADS):
        ks = slice(h * GLA_DK, (h + 1) * GLA_DK)
        vs = slice(h * GLA_DV, (h + 1) * GLA_DV)
        q = q_ref[:, ks] * (GLA_DK ** -0.5)
        k = k_ref[:, ks]
        hook = None if at_slot is None else (lambda level, h=h: at_slot(h * nlev + level))
        rows, p, tot = _gla_chunk(q, k, la_ref[:, ks] * LOG2E, lev, uppers, hook)
        vb = v_ref[:, vs].astype(BF16)
        st_old = jnp.where(fresh, 0.0, st_ref[h])
        qt = (q * jnp.exp2(p)).astype(BF16)
        intra = jnp.concatenate([_dot(r.astype(BF16), vb[:r.shape[1]]) for r in rows], axis=0)
        o = intra + _dot_nt(qt, st_old.astype(BF16))
        kt = (k * jnp.exp2(tot - p)).astype(BF16)
        st_new = st_old * jnp.exp2(tot[0:1, :]) + _dot_tn(vb, kt)
        st_ref[h] = jnp.where(valid, st_new, st_old)
        new_ref[:, RET_W + h * GLA_DV: RET_W + (h + 1) * GLA_DV] = (
            _rms(o) * gout_ref[:, vs] * _silu(g_ref[:, vs])).astype(new_ref.dtype)


def _mixer_body(dmat_ref, lg_ref, q_ref, k_ref, rv_ref, rg_ref, gq_ref, gk_ref, gv_ref, gr_ref, la_ref, lev_ref,
                gro_ref, ggo_ref, x_ref, gt_ref, sc_ref, sh_ref, gf_ref, w_ref,
                x1_ref, h_ref, sr_out, sg_out, s_ref, st_ref, mixa_ref, mixb_ref, *, nc, n_chunks):
    s = pl.program_id(0)
    valid = s < n_chunks
    c = jnp.minimum(s, n_chunks - 1) % nc
    fresh = c == 0

    @pl.when(s == 0)
    def _():
        mixb_ref[...] = jnp.zeros_like(mixb_ref)

    def step(old_ref, new_ref):
        old = old_ref[...]
        d = x_ref.shape[1]
        nblk = d // PROJ_BLK
        ssq = [jnp.zeros((x_ref.shape[0], 1), F32)]

        def project(blocks):
            for cb in blocks:
                cs = slice(cb * PROJ_BLK, (cb + 1) * PROJ_BLK)
                x1 = x_ref[:, cs] + gt_ref[0][:, cs] * _dot(old, w_ref[:, cs])
                x1_ref[:, cs] = x1
                ssq[0] = ssq[0] + jnp.sum(x1 * x1, axis=-1, keepdims=True)

        nslots = GLA_HEADS * (x_ref.shape[0].bit_length() - 1)
        due = {(cb * nslots) // nblk: cb for cb in range(nblk)}
        _gla_heads(gq_ref, gk_ref, gv_ref, gr_ref, la_ref, lev_ref, ggo_ref, st_ref, new_ref, fresh, valid,
                   at_slot=lambda slot: project([due[slot]]) if slot in due else None)
        _ret_heads(dmat_ref, lg_ref, q_ref, k_ref, rv_ref, rg_ref, gro_ref, s_ref, new_ref, fresh, valid)
        inv = lax.rsqrt(ssq[0] * (1.0 / d) + EPS)
        gain = gf_ref[...] * (1.0 + sc_ref[0])
        for cb in range(nblk):
            cs = slice(cb * PROJ_BLK, (cb + 1) * PROJ_BLK)
            h_ref[:, cs] = (x1_ref[:, cs] * inv * gain[:, cs] + sh_ref[0][:, cs]).astype(h_ref.dtype)

    @pl.when(s % 2 == 0)
    def _():
        step(mixb_ref, mixa_ref)

    @pl.when(s % 2 == 1)
    def _():
        step(mixa_ref, mixb_ref)

    @pl.when(valid & (c == nc - 1))
    def _():
        sr_out[0] = s_ref[...]
        for h in range(GLA_HEADS):
            sg_out[0, h] = st_ref[h].T


def _mixer_prompt(qk, rest, la, x, gt, sc, sh, g_ret_out, g_gla_out, g_ffn, w_out_b, *, batch, seq, chunk):
    m, d = x.shape
    nc = seq // chunk
    n_chunks = batch * nc
    lg = _ret_log_gamma_tile()
    lev = jnp.asarray(_gla_level_map(SCORE_BLK))
    gqk_w = GLA_HEADS * GLA_DK
    cur = lambda s: jnp.minimum(s, n_chunks - 1)
    prv = lambda s: jnp.maximum(s - 1, 0)
    rowc = lambda w, blk: pl.BlockSpec((chunk, w), lambda s: (cur(s), blk))
    const = lambda shape: pl.BlockSpec(shape, lambda s: (0,) * len(shape))
    mod_spec = lambda mod: pl.BlockSpec((1, 1, d), lambda s: (prv(s) // nc, 0, mod[1]))
    return pl.pallas_call(
        functools.partial(_mixer_body, nc=nc, n_chunks=n_chunks),
        grid=(n_chunks + 1,),
        in_specs=[const((RET_HEADS, chunk, chunk)), const((RET_HEADS, SUBLANES, LANES)),
                  rowc(RET_W, 0), rowc(RET_W, 1),
                  rowc(RET_W, R_RV // RET_W), rowc(RET_W, R_RG // RET_W),
                  rowc(gqk_w, R_GQ // gqk_w), rowc(gqk_w, R_GK // gqk_w),
                  rowc(GLA_W, R_GV // GLA_W), rowc(GLA_W, R_GR // GLA_W),
                  rowc(gqk_w, 0),
                  const((SCORE_BLK, SCORE_BLK)), const((1, RET_W)), const((1, GLA_W)),
                  pl.BlockSpec((chunk, d), lambda s: (prv(s), 0)),
                  mod_spec(gt), mod_spec(sc), mod_spec(sh),
                  const((1, d)),
                  pl.BlockSpec((RET_W + GLA_W, d), lambda s: (0, 0), pipeline_mode=pl.Buffered(1))],
        out_specs=[pl.BlockSpec((chunk, d), lambda s: (prv(s), 0)),
                   pl.BlockSpec((chunk, d), lambda s: (prv(s), 0)),
                   pl.BlockSpec((1, RET_HEADS, RET_DK, RET_DV), lambda s: (cur(s) // nc, 0, 0, 0)),
                   pl.BlockSpec((1, GLA_HEADS, GLA_DK, GLA_DV), lambda s: (cur(s) // nc, 0, 0, 0))],
        out_shape=[jax.ShapeDtypeStruct((m, d), F32),
                   jax.ShapeDtypeStruct((m, d), BF16),
                   jax.ShapeDtypeStruct((batch, RET_HEADS, RET_DK, RET_DV), F32),
                   jax.ShapeDtypeStruct((batch, GLA_HEADS, GLA_DK, GLA_DV), F32)],
        scratch_shapes=[pltpu.VMEM((RET_HEADS, RET_DK, RET_DV), F32),
                        pltpu.VMEM((GLA_HEADS, GLA_DV, GLA_DK), F32),
                        pltpu.VMEM((chunk, RET_W + GLA_W), BF16),
                        pltpu.VMEM((chunk, RET_W + GLA_W), BF16)],
        compiler_params=_params(("arbitrary",)),
        name="mixer_prompt",
    )(_ret_decay_matrix(chunk), lg, qk, qk, rest, rest, rest, rest, rest, rest, la, lev,
      g_ret_out.reshape(1, RET_W), g_gla_out.reshape(1, GLA_W), x, gt[0], sc[0], sh[0], g_ffn.reshape(1, d), w_out_b)


def _columns(rows):
    n = rows[0].shape[1]
    pad = (-len(rows)) % 8
    stack = jnp.concatenate(rows + [jnp.zeros((pad, n), F32)] if pad else rows, axis=0)
    return stack.T


def _state_body(r0, lg_ref, qk_ref, rest_ref, la_ref, sr_ref, sg_ref, gro_ref, ggo_ref, o_ref, sr_out, sg_out):
    nb = sr_ref.shape[1]
    for i in range(nb):
        row = pl.ds(r0 + i, 1)
        qkrow = qk_ref[row, :]
        rrow = rest_ref[row, :]
        larow = la_ref[row, :]
        rcols = _columns([qkrow[:, j * RET_DK:(j + 1) * RET_DK] for j in range(2 * RET_HEADS)])
        gq = [rrow[:, R_GQ + h * GLA_DK: R_GQ + (h + 1) * GLA_DK] * (GLA_DK ** -0.5) for h in range(GLA_HEADS)]
        gk = [rrow[:, R_GK + h * GLA_DK: R_GK + (h + 1) * GLA_DK] for h in range(GLA_HEADS)]
        ga = [jnp.exp(larow[:, h * GLA_DK:(h + 1) * GLA_DK]) for h in range(GLA_HEADS)]
        gcols = _columns(gq + gk + ga)
        outs = []
        for h in range(RET_HEADS):
            v = rrow[:, R_RV + h * RET_DV: R_RV + (h + 1) * RET_DV]
            gate = rrow[:, R_RG + h * RET_DV: R_RG + (h + 1) * RET_DV]
            gamma = jnp.exp(lg_ref[h][:1, :1])
            qc = rcols[:, h:h + 1]
            kc = rcols[:, RET_HEADS + h:RET_HEADS + h + 1]
            s_new = sr_ref[0, i, h] * gamma + kc * v
            sr_out[0, i, h] = s_new
            o = jnp.sum(qc * s_new, axis=0, keepdims=True)
            outs.append(_rms(o) * gro_ref[:, h * RET_DV:(h + 1) * RET_DV] * _silu(gate))
        for h in range(GLA_HEADS):
            v = rrow[:, R_GV + h * GLA_DV: R_GV + (h + 1) * GLA_DV]
            gate = rrow[:, R_GR + h * GLA_DV: R_GR + (h + 1) * GLA_DV]
            qc = gcols[:, h:h + 1]
            kc = gcols[:, GLA_HEADS + h:GLA_HEADS + h + 1]
            ac = gcols[:, 2 * GLA_HEADS + h:2 * GLA_HEADS + h + 1]
            s_new = sg_ref[0, i, h] * ac + kc * v
            sg_out[0, i, h] = s_new
            o = jnp.sum(qc * s_new, axis=0, keepdims=True)
            outs.append(_rms(o) * ggo_ref[:, h * GLA_DV:(h + 1) * GLA_DV] * _silu(gate))
        o_ref[row, :] = jnp.concatenate(outs, axis=-1)


def _outproj_body(o_ref, x_ref, gt_ref, sc_ref, sh_ref, g_ref, w_ref, x1_ref, h_ref):
    mix = _dot(o_ref[...].astype(BF16), w_ref[...])
    x1 = x_ref[...] + gt_ref[0] * mix
    x1_ref[...] = x1
    h = _rms(x1) * g_ref[...]
    h_ref[...] = (h * (1.0 + sc_ref[0]) + sh_ref[0]).astype(BF16)


def _outproj(o, x, gt, sc, sh, g_ffn, w_out_bf16, *, tm, rows_per_mod):
    m, d = x.shape
    ko = o.shape[1]
    return pl.pallas_call(
        _outproj_body,
        grid=(m // tm,),
        in_specs=[pl.BlockSpec((tm, ko), lambda i: (i, 0)),
                  pl.BlockSpec((tm, d), lambda i: (i, 0)),
                  *(_mod_spec(v, d, tm, rows_per_mod) for v in (gt, sc, sh)),
                  pl.BlockSpec((1, d), lambda i: (0, 0)),
                  pl.BlockSpec((ko, d), lambda i: (0, 0), pipeline_mode=pl.Buffered(1))],
        out_specs=[pl.BlockSpec((tm, d), lambda i: (i, 0)),
                   pl.BlockSpec((tm, d), lambda i: (i, 0))],
        out_shape=[jax.ShapeDtypeStruct((m, d), F32),
                   jax.ShapeDtypeStruct((m, d), BF16)],
        compiler_params=_params(("arbitrary",)),
        name="out_proj",
    )(o, x, gt[0], sc[0], sh[0], g_ffn.reshape(1, d), w_out_bf16)


HALO = SUBLANES


def _ffn_up_body(h_ref, wa_ref, wb_ref, cwa_ref, cwb_ref, cba_ref, cbb_ref, g_ref, cs_ref, win_ref, tail_ref, *, tiles_per_seq):
    i, f = pl.program_id(0), pl.program_id(1)
    hb = h_ref[...]
    tm = hb.shape[0]
    first = (i % tiles_per_seq) == 0
    ucs = []
    for part, (w_ref, cw_ref, cb_ref) in enumerate(((wa_ref, cwa_ref, cba_ref), (wb_ref, cwb_ref, cbb_ref))):
        u = _dot(hb, w_ref[...])
        prev = jnp.where(first, 0.0, tail_ref[part, f])
        pieces = []
        for c in range(u.shape[1] // LANES):
            lc = slice(c * LANES, (c + 1) * LANES)
            win_ref[part, c, 0:HALO, :] = prev[:, lc]
            win_ref[part, c, HALO:HALO + tm, :] = u[:, lc]
            cw, cb = cw_ref[:, lc], cb_ref[:, lc]
            pieces.append(cb + cw[0:1] * win_ref[part, c, HALO - 2:HALO - 2 + tm, :]
                          + cw[1:2] * win_ref[part, c, HALO - 1:HALO - 1 + tm, :] + cw[2:3] * u[:, lc])
        ucs.append(jnp.concatenate(pieces, axis=-1))
        tail_ref[part, f] = u[tm - HALO:]
        for r in range(CONV_W - 1):
            cs_ref[0, 0, r, part:part + 1, :] = u[tm - (CONV_W - 1) + r: tm - (CONV_W - 1) + r + 1]
    g_ref[...] = (_silu(ucs[0]) * ucs[1]).astype(g_ref.dtype)


def _ffn_up_state_body(h_ref, wa_ref, wb_ref, cwa_ref, cwb_ref, cba_ref, cbb_ref,
                       lg_ref, qk_ref, rest_ref, la_ref, sr_ref, sg_ref, gro_ref, ggo_ref, wd_ref,
                       g_ref, cs_ref, o_ref, sr_out, sg_out, wdb_ref, win_ref, tail_ref, *, tiles_per_seq, nblk):
    _ffn_up_body(h_ref, wa_ref, wb_ref, cwa_ref, cwb_ref, cba_ref, cbb_ref, g_ref, cs_ref, win_ref, tail_ref,
                 tiles_per_seq=tiles_per_seq)
    nb = sr_ref.shape[1]
    blk = jnp.minimum(pl.program_id(0) * pl.num_programs(1) + pl.program_id(1), nblk - 1)
    r0 = (blk % (SUBLANES // nb)) * nb
    _state_body(r0, lg_ref, qk_ref, rest_ref, la_ref, sr_ref, sg_ref, gro_ref, ggo_ref, o_ref, sr_out, sg_out)
    wdb_ref[...] = wd_ref[...].astype(wdb_ref.dtype)


def _ffn_up_prompt(h, w_up, conv_w, conv_b, qk_s, rest_s, la_s, state_ret, state_gla, g_ret_out, g_gla_out, w_down,
                   *, batch, seq, tm, tf, nb, cast_rows):
    m, d = h.shape
    ff = w_up.shape[1] // 2
    nf = ff // tf
    cb = conv_b.reshape(1, 2 * ff)
    tps = seq // tm
    ms = qk_s.shape[0]
    nblk = ms // nb
    ncast = ff // cast_rows
    assert (m // tm) * nf >= nblk + ncast
    assert SUBLANES % nb == 0 and ms % SUBLANES == 0
    mix = RET_W + GLA_W
    lg = _ret_log_gamma_tile()
    sblk = lambda i, f: jnp.minimum(i * nf + f, nblk - 1)
    ret_spec = pl.BlockSpec((1, nb, RET_HEADS, RET_DK, RET_DV), lambda i, f: (0, sblk(i, f), 0, 0, 0))
    gla_spec = pl.BlockSpec((1, nb, GLA_HEADS, GLA_DK, GLA_DV), lambda i, f: (0, sblk(i, f), 0, 0, 0))
    row_spec = lambda w: pl.BlockSpec((SUBLANES, w), lambda i, f: (sblk(i, f) // (SUBLANES // nb), 0))
    cast_spec = pl.BlockSpec((cast_rows, d), lambda i, f: (jnp.clip(i * nf + f - nblk, 0, ncast - 1), 0))
    body = functools.partial(_ffn_up_state_body, tiles_per_seq=tps, nblk=nblk)
    gate, tails, o_s, s_ret, s_gla, w_down_b = pl.pallas_call(
        body,
        grid=(m // tm, nf),
        in_specs=[pl.BlockSpec((tm, d), lambda i, f: (i, 0)),
                  pl.BlockSpec((d, tf), lambda i, f: (0, f)),
                  pl.BlockSpec((d, tf), lambda i, f: (0, nf + f)),
                  pl.BlockSpec((CONV_W, tf), lambda i, f: (0, f)),
                  pl.BlockSpec((CONV_W, tf), lambda i, f: (0, nf + f)),
                  pl.BlockSpec((1, tf), lambda i, f: (0, f)),
                  pl.BlockSpec((1, tf), lambda i, f: (0, nf + f)),
                  pl.BlockSpec((RET_HEADS, SUBLANES, LANES), lambda i, f: (0, 0, 0)),
                  row_spec(QK_W), row_spec(REST_W), row_spec(la_s.shape[1]),
                  ret_spec, gla_spec,
                  pl.BlockSpec((1, RET_W), lambda i, f: (0, 0)),
                  pl.BlockSpec((1, GLA_W), lambda i, f: (0, 0)),
                  cast_spec],
        out_specs=[pl.BlockSpec((tm, tf), lambda i, f: (i, f)),
                   pl.BlockSpec((1, 1, CONV_W - 1, 2, tf), lambda i, f: (i // tps, i % tps, 0, 0, f)),
                   row_spec(mix), ret_spec, gla_spec, cast_spec],
        out_shape=[jax.ShapeDtypeStruct((m, ff), BF16),
                   jax.ShapeDtypeStruct((batch, tps, CONV_W - 1, 2, ff), F32),
                   jax.ShapeDtypeStruct((ms, mix), F32),
                   jax.ShapeDtypeStruct(state_ret.shape, F32),
                   jax.ShapeDtypeStruct(state_gla.shape, F32),
                   jax.ShapeDtypeStruct(w_down.shape, BF16)],
        scratch_shapes=[pltpu.VMEM((2, tf // LANES, HALO + tm, LANES), F32), pltpu.VMEM((2, nf, HALO, tf), F32)],
        compiler_params=_params(("arbitrary", "arbitrary")),
        name="ffn_up_prompt",
    )(h, w_up, w_up, conv_w, conv_w, cb, cb,
      lg, qk_s, rest_s, la_s,
      state_ret, state_gla, g_ret_out.reshape(1, RET_W), g_gla_out.reshape(1, GLA_W), w_down)
    return gate, tails[:, tps - 1], o_s, s_ret, s_gla, w_down_b


def _ffn_up_step_body(h_ref, w_ref, cw_ref, cb_ref, st_ref, g_ref, cs_ref, uca_ref, *, nf):
    j = pl.program_id(0)
    u = _dot(h_ref[...], w_ref[...])
    s1 = st_ref[0, :, 1, :]
    cw = cw_ref[...]
    uc = cb_ref[...] + cw[0:1] * st_ref[0, :, 0, :] + cw[1:2] * s1 + cw[2:3] * u
    cs_ref[0, :, 0, :] = s1
    cs_ref[0, :, 1, :] = u

    @pl.when(j < nf)
    def _():
        uca_ref[j] = uc

    @pl.when(j >= nf)
    def _():
        g_ref[...] = (_silu(uca_ref[j - nf]) * uc).astype(g_ref.dtype)


def _ffn_up_step(h, w_up, conv_w, conv_b, state_conv, *, tf):
    m, d = h.shape
    ff = w_up.shape[1] // 2
    nf = ff // tf
    st_spec = pl.BlockSpec((1, m, CONV_W - 1, tf), lambda j: (0, 0, 0, j))
    return pl.pallas_call(
        functools.partial(_ffn_up_step_body, nf=nf),
        grid=(2 * nf,),
        in_specs=[pl.BlockSpec((m, d), lambda j: (0, 0)),
                  pl.BlockSpec((d, tf), lambda j: (0, j)),
                  pl.BlockSpec((CONV_W, tf), lambda j: (0, j)),
                  pl.BlockSpec((1, tf), lambda j: (0, j)),
                  st_spec],
        out_specs=[pl.BlockSpec((m, tf), lambda j: (0, jnp.maximum(j - nf, 0))), st_spec],
        out_shape=[jax.ShapeDtypeStruct((m, ff), BF16),
                   jax.ShapeDtypeStruct(state_conv.shape, F32)],
        scratch_shapes=[pltpu.VMEM((nf, m, tf), F32)],
        compiler_params=_params(("arbitrary",)),
        name="ffn_up_step",
    )(h, w_up, conv_w, conv_b.reshape(1, 2 * ff), state_conv)


def _ffn_down_body(g_ref, w_ref, x1_ref, gt_ref, gf_ref, y_ref, xs_ref, inv_ref):
    k = pl.program_id(1)
    last = pl.num_programs(1) - 1
    d = y_ref.shape[1]
    xb = x1_ref.shape[1]
    xs_ref[k] = x1_ref[...]

    @pl.when(k == 0)
    def _():
        for cb in range(d // PROJ_BLK):
            cs = slice(cb * PROJ_BLK, (cb + 1) * PROJ_BLK)
            y_ref[:, cs] = _dot(g_ref[...], w_ref[:, cs])

    @pl.when((k > 0) & (k < last))
    def _():
        for cb in range(d // PROJ_BLK):
            cs = slice(cb * PROJ_BLK, (cb + 1) * PROJ_BLK)
            y_ref[:, cs] += _dot(g_ref[...], w_ref[:, cs])

    @pl.when(k == last)
    def _():
        ssq = jnp.zeros((y_ref.shape[0], 1), F32)
        for cb in range(d // PROJ_BLK):
            cs = slice(cb * PROJ_BLK, (cb + 1) * PROJ_BLK)
            x1 = xs_ref[(cb * PROJ_BLK) // xb, :, (cb * PROJ_BLK) % xb: (cb * PROJ_BLK) % xb + PROJ_BLK]
            x2 = x1 + gt_ref[0][:, cs] * (y_ref[:, cs] + _dot(g_ref[...], w_ref[:, cs]))
            y_ref[:, cs] = x2
            ssq = ssq + jnp.sum(x2 * x2, axis=-1, keepdims=True)
        inv_ref[...] = lax.rsqrt(ssq * (1.0 / d) + EPS)

        rg = min(y_ref.shape[0], NORM_ROWS)

        def norm_rows(r, carry):
            rows = pl.ds(pl.multiple_of(r * rg, rg), rg)
            y_ref[rows, :] = y_ref[rows, :] * inv_ref[rows, :] * gf_ref[...]
            return carry

        lax.fori_loop(0, y_ref.shape[0] // rg, norm_rows, 0)


def _ffn_down(g, w_down, x1, gt, g_final, *, tm, tk, rows_per_mod):
    m, d = x1.shape
    ff = g.shape[1]
    nk = ff // tk
    assert nk >= 2
    xb = d // nk
    assert d % nk == 0 and xb % PROJ_BLK == 0
    return pl.pallas_call(
        _ffn_down_body,
        grid=(m // tm, nk),
        in_specs=[pl.BlockSpec((tm, tk), lambda i, k: (i, k)),
                  pl.BlockSpec((tk, d), lambda i, k: (k, 0)),
                  pl.BlockSpec((tm, xb), lambda i, k: (i, k)),
                  _mod_spec(gt, d, tm, rows_per_mod),
                  pl.BlockSpec((1, d), lambda i, k: (0, 0))],
        out_specs=pl.BlockSpec((tm, d), lambda i, k: (i, 0)),
        out_shape=jax.ShapeDtypeStruct((m, d), F32),
        scratch_shapes=[pltpu.VMEM((nk, tm, xb), F32), pltpu.VMEM((tm, 1), F32)],
        compiler_params=_params(("arbitrary", "arbitrary")),
        name="ffn_down",
    )(g, w_down, x1, gt[0], g_final.reshape(1, d))


def _rope_tables(pos, rows=None):
    half = RET_DK // 2
    inv = ROPE_THETA ** (-np.arange(half, dtype=np.float64) / half)
    ang = np.asarray(pos, np.float64)[:, None] * inv[None, :]
    tabs = (np.cos(ang), np.sin(ang))
    if rows is not None:
        tabs = tuple(np.broadcast_to(t, (rows, half)) for t in tabs)
    return tuple(jnp.asarray(t, F32) for t in tabs)


def kernel(x_prompt, x_sample, c_prompt, c_sample, state_ret, state_gla, state_conv, w_ada, b_ada, g_attn, w_in, w_a2, b_a2, g_ret_out, g_gla_out, w_out, g_ffn, w_up, conv_w, conv_b, w_down, g_final):
    bp, t_p, d = x_prompt.shape
    bs, t_s, _ = x_sample.shape
    assert t_s == 1 and w_ada.shape[0] == 1
    mp = bp * t_p
    w_ada, b_ada, g_attn, w_in, w_a2, b_a2, g_ret_out, g_gla_out, w_out, g_ffn, w_up, conv_w, conv_b, w_down = (
        a[0] for a in (w_ada, b_ada, g_attn, w_in, w_a2, b_a2, g_ret_out, g_gla_out, w_out, g_ffn, w_up, conv_w, conv_b, w_down))

    mod = _ada(jnp.concatenate([c_sample, c_prompt], axis=0), w_ada, b_ada)
    mod_p = mod[bs:].reshape(bp, 1, -1)
    sh1p, sc1p, gt1p, sh2p, sc2p, gt2p = ((mod_p, i) for i in range(6))
    sh1s, sc1s, gt1s, sh2s, sc2s, gt2s = ((mod[None], i) for i in range(6))

    cos_p, sin_p = _rope_tables(np.arange(t_p))
    cos_s, sin_s = _rope_tables(PAST_LEN + np.arange(t_s), rows=bs)
    w_out_b = w_out.astype(BF16)
    w_in_t = w_in.T
    w_gate_t = w_in_t[IN_MAIN:]

    xp = x_prompt.reshape(mp, d)
    xs = x_sample.reshape(bs, d)
    qk_s, rest_s, la_s, w_in_b = _inproj(xs, sc1s, sh1s, g_attn, w_in_t, w_gate_t, w_a2, b_a2, cos_s, sin_s,
                                         tm=bs, rows_per_mod=bs, cast_w=True, qk_dtype=F32)
    qk_p, rest_p, la_p, w_up_b = _inproj(xp, sc1p, sh1p, g_attn, w_in_b, w_gate_t, w_a2, b_a2, cos_p, sin_p,
                                         tm=ROW_TILE, rows_per_mod=t_p, cast_src=w_up)
    x1_p, h2_p, s_ret_p, s_gla_p = _mixer_prompt(qk_p, rest_p, la_p, xp, gt1p, sc2p, sh2p, g_ret_out, g_gla_out, g_ffn,
                                                 w_out_b, batch=bp, seq=t_p, chunk=CHUNK)
    g_p, cs_p, o_s, s_ret_s, s_gla_s, w_down_b = _ffn_up_prompt(
        h2_p, w_up_b, conv_w, conv_b, qk_s, rest_s, la_s, state_ret, state_gla, g_ret_out, g_gla_out, w_down,
        batch=bp, seq=t_p, tm=ROW_TILE, tf=FF_COL_TILE, nb=STATE_SEQS, cast_rows=CAST_ROWS)
    y_p = _ffn_down(g_p, w_down_b, x1_p, gt2p, g_final, tm=ROW_TILE, tk=DOWN_K_TILE, rows_per_mod=t_p)

    x1_s, h2_s = _outproj(o_s, xs, gt1s, sc2s, sh2s, g_ffn, w_out_b, tm=bs, rows_per_mod=bs)
    ff = w_down.shape[0]
    g_s, cs_s = _ffn_up_step(h2_s, w_up_b, conv_w, conv_b, state_conv, tf=ff // 4)
    y_s = _ffn_down(g_s, w_down_b, x1_s, gt2s, g_final, tm=bs, tk=ff // 2, rows_per_mod=bs)

    return (y_p.reshape(bp, t_p, d), y_s.reshape(bs, t_s, d),
            s_ret_p[None], s_ret_s, s_gla_p[None], s_gla_s,
            cs_p.reshape(1, bp, CONV_W - 1, -1), cs_s)
```

```python
import functools

import numpy as np
import jax
import jax.numpy as jnp
from jax import lax
from jax.experimental import pallas as pl
from jax.experimental.pallas import tpu as pltpu

F32 = jnp.float32
BF16 = jnp.bfloat16

RET_HEADS = 4
RET_DK = 256
RET_DV = 256
GLA_HEADS = 4
GLA_DK = 128
GLA_DV = 256
GLA_RANK = 16
GLA_TAU = 16.0
ROPE_THETA = 10000.0
PAST_LEN = 16384
CONV_W = 3
EPS = 1e-6

RET_W = RET_HEADS * RET_DV
GLA_W = GLA_HEADS * GLA_DV
QK_W = 2 * RET_HEADS * RET_DK
R_RV = 0
R_RG = R_RV + RET_W
R_GQ = R_RG + RET_W
R_GK = R_GQ + GLA_HEADS * GLA_DK
R_GV = R_GK + GLA_HEADS * GLA_DK
R_GR = R_GV + GLA_W
REST_W = R_GR + GLA_W
IN_MAIN = QK_W + REST_W

V7X_VMEM_BYTES = 64 * 1024 * 1024
V7X_MXU_WIDTH = 256
LANES = 128
SUBLANES = 8
VMEM_LIMIT_BYTES = V7X_VMEM_BYTES - 8 * 1024 * 1024
PROJ_BLK = V7X_MXU_WIDTH

ROW_TILE = 1024
IN_COL_TILE = 1024
FF_COL_TILE = 512
DOWN_K_TILE = 1408
CHUNK = 256
STATE_SEQS = 2
CAST_ROWS = 256
CAST_COLS = 256
NORM_ROWS = 128


def _params(semantics):
    return pltpu.CompilerParams(dimension_semantics=semantics, vmem_limit_bytes=VMEM_LIMIT_BYTES)


def _dot(a, b):
    return jnp.dot(a, b, preferred_element_type=F32)


def _dot_nt(a, b):
    return lax.dot_general(a, b, (((1,), (1,)), ((), ())), preferred_element_type=F32)


def _dot_tn(a, b):
    return lax.dot_general(a, b, (((0,), (0,)), ((), ())), preferred_element_type=F32)


def _silu(x):
    return x * jax.nn.sigmoid(x)


def _rms(x):
    return x * lax.rsqrt(jnp.mean(x * x, axis=-1, keepdims=True) + EPS)


def _mod_spec(mod, d, tm, rows_per_mod):
    arr, col = mod
    rows = 1 if arr.shape[1] == 1 else tm
    return pl.BlockSpec((1, rows, d), lambda i, *_: ((i * tm) // rows_per_mod, 0, col))


def _ada_body(c_ref, w_ref, b_ref, wo_ref, o_ref, wob_ref):
    s = _silu(c_ref[...]).astype(BF16)
    o_ref[...] = _dot(s, w_ref[...].astype(BF16)) + b_ref[...]
    wob_ref[...] = wo_ref[...].astype(wob_ref.dtype)


def _ada(c_all, w_ada, b_ada, w_out, tn=FF_COL_TILE):
    r, d = c_all.shape
    n = w_ada.shape[1]
    ncast = w_out.shape[1] // LANES
    assert n // tn >= ncast
    cast_spec = pl.BlockSpec((w_out.shape[0], LANES), lambda j: (0, jnp.minimum(j, ncast - 1)))
    return pl.pallas_call(
        _ada_body,
        grid=(n // tn,),
        in_specs=[pl.BlockSpec((r, d), lambda j: (0, 0)),
                  pl.BlockSpec((d, tn), lambda j: (0, j)),
                  pl.BlockSpec((1, tn), lambda j: (0, j)),
                  cast_spec],
        out_specs=[pl.BlockSpec((r, tn), lambda j: (0, j)), cast_spec],
        out_shape=[jax.ShapeDtypeStruct((r, n), F32), jax.ShapeDtypeStruct(w_out.shape, BF16)],
        compiler_params=_params(("arbitrary",)),
        name="ada_mod",
    )(c_all, w_ada, b_ada.reshape(1, n), w_out)


def _inproj_body(*refs, n_qk, cast_w, rider):
    refs = list(refs)
    x_ref, sc_ref, sh_ref, g_ref, w_ref, wga_ref, wa2_ref, ba2_ref, cos_ref, sin_ref = refs[:10]
    del refs[:10]
    src_ref = refs.pop(0) if rider else None
    qk_ref, rest_ref, la_ref = refs[:3]
    del refs[:3]
    wcast_ref = refs.pop(0) if cast_w else None
    dst_ref = refs.pop(0) if rider else None
    (h_ref,) = refs
    j = pl.program_id(1)

    def ride():
        if rider:
            dst_ref[...] = src_ref[...].astype(dst_ref.dtype)

    @pl.when(j == 0)
    def _():
        gain = g_ref[...] * (1.0 + sc_ref[0])
        hb = (_rms(x_ref[...]) * gain + sh_ref[0]).astype(BF16)
        h_ref[...] = hb
        ga = _dot_nt(hb, wga_ref[...].astype(BF16))
        z = _dot(ga.astype(BF16), wa2_ref[...].astype(BF16)) + ba2_ref[...]
        la_ref[...] = (jnp.minimum(z, 0.0) - jnp.log1p(jnp.exp(-jnp.abs(z)))) * (1.0 / GLA_TAU)

    if cast_w:
        w_tile = w_ref[...].T.astype(BF16)
        wcast_ref[...] = w_tile
        weight = lambda: w_tile
    else:
        weight = lambda: w_ref[...]

    @pl.when(j < n_qk)
    def _():
        ride()
        acc = _dot(h_ref[...], weight())
        cos, sin = cos_ref[...], sin_ref[...]
        half = RET_DK // 2
        scale = jnp.where(j >= n_qk // 2, RET_DK ** -0.5, 1.0)
        outs = []
        for hh in range(acc.shape[1] // RET_DK):
            x1 = acc[:, hh * RET_DK: hh * RET_DK + half]
            x2 = acc[:, hh * RET_DK + half: (hh + 1) * RET_DK]
            outs += [x1 * cos - x2 * sin, x1 * sin + x2 * cos]
        qk_ref[...] = (jnp.concatenate(outs, axis=-1) * scale).astype(qk_ref.dtype)

    @pl.when(j >= n_qk)
    def _():
        ride()
        rest_ref[...] = _dot(h_ref[...], weight())


def _inproj(x, sc, sh, g_attn, w, w_gate_t, w_a2, b_a2, cos, sin, *, tm, rows_per_mod, cast_w=False, cast_src=None,
            qk_dtype=BF16, tn=IN_COL_TILE):
    m, d = x.shape
    nq = w_a2.shape[1]
    n_qk = QK_W // tn
    tab_tiles = cos.shape[0] // tm
    n_j = IN_MAIN // tn
    once = pl.Buffered(1)
    sc_spec, sh_spec = (_mod_spec(v, d, tm, rows_per_mod) for v in (sc, sh))
    tab_spec = pl.BlockSpec((tm, RET_DK // 2), lambda i, j: (i % tab_tiles, 0), pipeline_mode=once)
    out_specs = [pl.BlockSpec((tm, tn), lambda i, j: (i, jnp.minimum(j, n_qk - 1))),
                 pl.BlockSpec((tm, tn), lambda i, j: (i, jnp.maximum(j - n_qk, 0))),
                 pl.BlockSpec((tm, nq), lambda i, j: (i, 0), pipeline_mode=once)]
    out_shape = [jax.ShapeDtypeStruct((m, QK_W), qk_dtype),
                 jax.ShapeDtypeStruct((m, REST_W), F32),
                 jax.ShapeDtypeStruct((m, nq), F32)]
    if cast_w:
        assert m == tm
        w_spec = pl.BlockSpec((tn, d), lambda i, j: (j, 0))
        out_specs.append(pl.BlockSpec((d, tn), lambda i, j: (0, j)))
        out_shape.append(jax.ShapeDtypeStruct((d, IN_MAIN), BF16))
    else:
        w_spec = pl.BlockSpec((d, tn), lambda i, j: (0, j))
    in_specs = [pl.BlockSpec((tm, d), lambda i, j: (i, 0)),
                sc_spec, sh_spec,
                pl.BlockSpec((1, d), lambda i, j: (0, 0)),
                w_spec,
                pl.BlockSpec((GLA_RANK, d), lambda i, j: (0, 0)),
                pl.BlockSpec((GLA_RANK, nq), lambda i, j: (0, 0)),
                pl.BlockSpec((1, nq), lambda i, j: (0, 0)),
                tab_spec, tab_spec]
    args = [x, sc[0], sh[0], g_attn.reshape(1, d), w, w_gate_t, w_a2, b_a2.reshape(1, nq), cos, sin]
    if cast_src is not None:
        ncast = cast_src.shape[1] // CAST_COLS
        assert (m // tm) * n_j >= ncast
        ride_spec = pl.BlockSpec((cast_src.shape[0], CAST_COLS), lambda i, j: (0, jnp.minimum(i * n_j + j, ncast - 1)))
        in_specs.append(ride_spec)
        args.append(cast_src)
        out_specs.append(ride_spec)
        out_shape.append(jax.ShapeDtypeStruct(cast_src.shape, BF16))
    return pl.pallas_call(
        functools.partial(_inproj_body, n_qk=n_qk, cast_w=cast_w, rider=cast_src is not None),
        grid=(m // tm, n_j),
        in_specs=in_specs,
        out_specs=out_specs,
        out_shape=out_shape,
        scratch_shapes=[pltpu.VMEM((tm, d), BF16)],
        compiler_params=_params(("arbitrary", "arbitrary")),
        name="in_proj",
    )(*args)


def _ret_log_gamma():
    return np.log1p(-np.exp2(-5.0 - np.arange(RET_HEADS, dtype=np.float64)))


def _ret_log_gamma_tile():
    return jnp.asarray(np.broadcast_to(_ret_log_gamma()[:, None, None], (RET_HEADS, SUBLANES, LANES)), F32)


def _ret_decay_matrix(chunk):
    idx = np.arange(chunk, dtype=np.float64)
    rel = idx[:, None] - idx[None, :]
    lg = _ret_log_gamma()
    return jnp.asarray(np.where(rel[None] >= 0, np.exp(np.maximum(rel, 0.0)[None] * lg[:, None, None]), 0.0), F32)


def _ret_heads(dmat_ref, lg_ref, q_ref, k_ref, v_ref, g_ref, gout_ref, s_ref, new_ref, fresh, valid):
    cl = q_ref.shape[0]
    idx = lax.broadcasted_iota(jnp.int32, (cl, 1), 0).astype(F32)
    for h in range(RET_HEADS):
        sl = slice(h * RET_DV, (h + 1) * RET_DV)
        lg = lg_ref[h][:1, :1]
        q_dec = jnp.exp((idx + 1.0) * lg)
        k_dec = jnp.exp((cl - 1.0 - idx) * lg)
        c_dec = jnp.exp(cl * lg)
        qb, kb = q_ref[:, sl], k_ref[:, sl]
        v = v_ref[:, sl]
        vb = v.astype(BF16)
        s_old = jnp.where(fresh, 0.0, s_ref[h])
        scores = _dot_nt(qb, kb) * dmat_ref[h]
        o = _dot(scores.astype(BF16), vb) + _dot(qb, s_old.astype(BF16)) * q_dec
        s_new = s_old * c_dec + _dot_tn(kb, (v * k_dec).astype(BF16))
        s_ref[h] = jnp.where(valid, s_new, s_old)
        new_ref[:, sl] = (_rms(o) * gout_ref[:, sl] * _silu(g_ref[:, sl])).astype(new_ref.dtype)


def _gla_level_map(c):
    t = np.arange(c)[:, None]
    s = np.arange(c)[None, :]
    x = np.bitwise_xor(t, s)
    lev = np.floor(np.log2(np.maximum(x, 1))).astype(np.int32)
    lev = np.where(t == s, -1, lev)
    lev = np.where(t < s, -2, lev)
    return lev.astype(np.int32)


SCORE_BLK = 128
LOG2E = 1.4426950408889634


def _gla_chunk(q, k, la2, lev, uppers, at_level=None):
    cl = q.shape[0]
    nblk = cl // SCORE_BLK
    blk = lambda x, i: x[i * SCORE_BLK:(i + 1) * SCORE_BLK]
    p = la2
    tot = la2
    diag = [jnp.zeros((SCORE_BLK, SCORE_BLK), F32) for _ in range(nblk)]
    off = {}
    level = 0
    half = 1
    while half < cl:
        if at_level is not None:
            at_level(level)
        upper = uppers[level]
        z = (jnp.where(upper, q, k) * jnp.exp2(jnp.where(upper, p, tot - p))).astype(BF16)
        if half < SCORE_BLK:
            for i in range(nblk):
                diag[i] = jnp.where(lev == level, _dot_nt(blk(z, i), blk(z, i)), diag[i])
        else:
            hb = half // SCORE_BLK
            for i in range(nblk):
                if (i // hb) % 2 == 1:
                    base = (i // (2 * hb)) * 2 * hb
                    for j in range(base, base + hb):
                        off[(i, j)] = _dot_nt(blk(z, i), blk(z, j))
        partner = jnp.where(upper, pltpu.roll(tot, half, 0), pltpu.roll(tot, cl - half, 0))
        p = p + jnp.where(upper, partner, 0.0)
        tot = tot + partner
        half *= 2
        level += 1
    dg = jnp.sum(q * k, axis=-1, keepdims=True)
    rows = []
    for i in range(nblk):
        d_i = jnp.where(lev == -1, blk(dg, i), diag[i])
        rows.append(jnp.concatenate([off[(i, j)] for j in range(i)] + [d_i], axis=-1))
    return rows, p, tot


def _gla_heads(q_ref, k_ref, v_ref, g_ref, la_ref, lev_ref, gout_ref, st_ref, new_ref, fresh, valid, at_slot=None):
    cl = q_ref.shape[0]
    lev = lev_ref[...]
    row = lax.broadcasted_iota(jnp.int32, (cl, GLA_DK), 0)
    uppers = []
    half = 1
    while half < cl:
        uppers.append((row & half) != 0)
        half *= 2
    nlev = len(uppers)
    for h in range(GLA_HEADS):
        ks = slice(h * GLA_DK, (h + 1) * GLA_DK)
        vs = slice(h * GLA_DV, (h + 1) * GLA_DV)
        q = q_ref[:, ks] * (GLA_DK ** -0.5)
        k = k_ref[:, ks]
        hook = None if at_slot is None else (lambda level, h=h: at_slot(h * nlev + level))
        rows, p, tot = _gla_chunk(q, k, la_ref[:, ks] * LOG2E, lev, uppers, hook)
        vb = v_ref[:, vs].astype(BF16)
        st_old = jnp.where(fresh, 0.0, st_ref[h])
        qt = (q * jnp.exp2(p)).astype(BF16)
        intra = jnp.concatenate([_dot(r.astype(BF16), vb[:r.shape[1]]) for r in rows], axis=0)
        o = intra + _dot_nt(qt, st_old.astype(BF16))
        kt = (k * jnp.exp2(tot - p)).astype(BF16)
        st_new = st_old * jnp.exp2(tot[0:1, :]) + _dot_tn(vb, kt)
        st_ref[h] = jnp.where(valid, st_new, st_old)
        new_ref[:, RET_W + h * GLA_DV: RET_W + (h + 1) * GLA_DV] = (
            _rms(o) * gout_ref[:, vs] * _silu(g_ref[:, vs])).astype(new_ref.dtype)


def _mixer_body(dmat_ref, lg_ref, q_ref, k_ref, rv_ref, rg_ref, gq_ref, gk_ref, gv_ref, gr_ref, la_ref, lev_ref,
                gro_ref, ggo_ref, x_ref, gt_ref, sc_ref, sh_ref, gf_ref, w_ref,
                x1_ref, h_ref, sr_out, sg_out, s_ref, st_ref, mixa_ref, mixb_ref, *, nc, n_chunks):
    s = pl.program_id(0)
    valid = s < n_chunks
    c = jnp.minimum(s, n_chunks - 1) % nc
    fresh = c == 0

    @pl.when(s == 0)
    def _():
        mixb_ref[...] = jnp.zeros_like(mixb_ref)

    def step(old_ref, new_ref):
        old = old_ref[...]
        d = x_ref.shape[1]
        nblk = d // PROJ_BLK
        ssq = [jnp.zeros((x_ref.shape[0], 1), F32)]

        def project(blocks):
            for cb in blocks:
                cs = slice(cb * PROJ_BLK, (cb + 1) * PROJ_BLK)
                x1 = x_ref[:, cs] + gt_ref[0][:, cs] * _dot(old, w_ref[:, cs])
                x1_ref[:, cs] = x1
                ssq[0] = ssq[0] + jnp.sum(x1 * x1, axis=-1, keepdims=True)

        nslots = GLA_HEADS * (x_ref.shape[0].bit_length() - 1)
        due = {(cb * nslots) // nblk: cb for cb in range(nblk)}
        _gla_heads(gq_ref, gk_ref, gv_ref, gr_ref, la_ref, lev_ref, ggo_ref, st_ref, new_ref, fresh, valid,
                   at_slot=lambda slot: project([due[slot]]) if slot in due else None)
        _ret_heads(dmat_ref, lg_ref, q_ref, k_ref, rv_ref, rg_ref, gro_ref, s_ref, new_ref, fresh, valid)
        inv = lax.rsqrt(ssq[0] * (1.0 / d) + EPS)
        gain = gf_ref[...] * (1.0 + sc_ref[0])
        for cb in range(nblk):
            cs = slice(cb * PROJ_BLK, (cb + 1) * PROJ_BLK)
            h_ref[:, cs] = (x1_ref[:, cs] * inv * gain[:, cs] + sh_ref[0][:, cs]).astype(h_ref.dtype)

    @pl.when(s % 2 == 0)
    def _():
        step(mixb_ref, mixa_ref)

    @pl.when(s % 2 == 1)
    def _():
        step(mixa_ref, mixb_ref)

    @pl.when(valid & (c == nc - 1))
    def _():
        sr_out[0] = s_ref[...]
        for h in range(GLA_HEADS):
            sg_out[0, h] = st_ref[h].T


def _mixer_prompt(qk, rest, la, x, gt, sc, sh, g_ret_out, g_gla_out, g_ffn, w_out_b, *, batch, seq, chunk):
    m, d = x.shape
    nc = seq // chunk
    n_chunks = batch * nc
    lg = _ret_log_gamma_tile()
    lev = jnp.asarray(_gla_level_map(SCORE_BLK))
    gqk_w = GLA_HEADS * GLA_DK
    cur = lambda s: jnp.minimum(s, n_chunks - 1)
    prv = lambda s: jnp.maximum(s - 1, 0)
    rowc = lambda w, blk: pl.BlockSpec((chunk, w), lambda s: (cur(s), blk))
    const = lambda shape: pl.BlockSpec(shape, lambda s: (0,) * len(shape))
    mod_spec = lambda mod: pl.BlockSpec((1, 1, d), lambda s: (prv(s) // nc, 0, mod[1]))
    return pl.pallas_call(
        functools.partial(_mixer_body, nc=nc, n_chunks=n_chunks),
        grid=(n_chunks + 1,),
        in_specs=[const((RET_HEADS, chunk, chunk)), const((RET_HEADS, SUBLANES, LANES)),
                  rowc(RET_W, 0), rowc(RET_W, 1),
                  rowc(RET_W, R_RV // RET_W), rowc(RET_W, R_RG // RET_W),
                  rowc(gqk_w, R_GQ // gqk_w), rowc(gqk_w, R_GK // gqk_w),
                  rowc(GLA_W, R_GV // GLA_W), rowc(GLA_W, R_GR // GLA_W),
                  rowc(gqk_w, 0),
                  const((SCORE_BLK, SCORE_BLK)), const((1, RET_W)), const((1, GLA_W)),
                  pl.BlockSpec((chunk, d), lambda s: (prv(s), 0)),
                  mod_spec(gt), mod_spec(sc), mod_spec(sh),
                  const((1, d)),
                  pl.BlockSpec((RET_W + GLA_W, d), lambda s: (0, 0), pipeline_mode=pl.Buffered(1))],
        out_specs=[pl.BlockSpec((chunk, d), lambda s: (prv(s), 0)),
                   pl.BlockSpec((chunk, d), lambda s: (prv(s), 0)),
                   pl.BlockSpec((1, RET_HEADS, RET_DK, RET_DV), lambda s: (cur(s) // nc, 0, 0, 0)),
                   pl.BlockSpec((1, GLA_HEADS, GLA_DK, GLA_DV), lambda s: (cur(s) // nc, 0, 0, 0))],
        out_shape=[jax.ShapeDtypeStruct((m, d), F32),
                   jax.ShapeDtypeStruct((m, d), BF16),
                   jax.ShapeDtypeStruct((batch, RET_HEADS, RET_DK, RET_DV), F32),
                   jax.ShapeDtypeStruct((batch, GLA_HEADS, GLA_DK, GLA_DV), F32)],
        scratch_shapes=[pltpu.VMEM((RET_HEADS, RET_DK, RET_DV), F32),
                        pltpu.VMEM((GLA_HEADS, GLA_DV, GLA_DK), F32),
                        pltpu.VMEM((chunk, RET_W + GLA_W), BF16),
                        pltpu.VMEM((chunk, RET_W + GLA_W), BF16)],
        compiler_params=_params(("arbitrary",)),
        name="mixer_prompt",
    )(_ret_decay_matrix(chunk), lg, qk, qk, rest, rest, rest, rest, rest, rest, la, lev,
      g_ret_out.reshape(1, RET_W), g_gla_out.reshape(1, GLA_W), x, gt[0], sc[0], sh[0], g_ffn.reshape(1, d), w_out_b)


def _columns(rows):
    n = rows[0].shape[1]
    pad = (-len(rows)) % 8
    stack = jnp.concatenate(rows + [jnp.zeros((pad, n), F32)] if pad else rows, axis=0)
    return stack.T


def _state_body(r0, lg_ref, qk_ref, rest_ref, la_ref, sr_ref, sg_ref, gro_ref, ggo_ref, o_ref, sr_out, sg_out):
    nb = sr_ref.shape[1]
    for i in range(nb):
        row = pl.ds(r0 + i, 1)
        qkrow = qk_ref[row, :]
        rrow = rest_ref[row, :]
        larow = la_ref[row, :]
        rcols = _columns([qkrow[:, j * RET_DK:(j + 1) * RET_DK] for j in range(2 * RET_HEADS)])
        gq = [rrow[:, R_GQ + h * GLA_DK: R_GQ + (h + 1) * GLA_DK] * (GLA_DK ** -0.5) for h in range(GLA_HEADS)]
        gk = [rrow[:, R_GK + h * GLA_DK: R_GK + (h + 1) * GLA_DK] for h in range(GLA_HEADS)]
        ga = [jnp.exp(larow[:, h * GLA_DK:(h + 1) * GLA_DK]) for h in range(GLA_HEADS)]
        gcols = _columns(gq + gk + ga)
        outs = []
        for h in range(RET_HEADS):
            v = rrow[:, R_RV + h * RET_DV: R_RV + (h + 1) * RET_DV]
            gate = rrow[:, R_RG + h * RET_DV: R_RG + (h + 1) * RET_DV]
            gamma = jnp.exp(lg_ref[h][:1, :1])
            qc = rcols[:, h:h + 1]
            kc = rcols[:, RET_HEADS + h:RET_HEADS + h + 1]
            s_new = sr_ref[0, i, h] * gamma + kc * v
            sr_out[0, i, h] = s_new
            o = jnp.sum(qc * s_new, axis=0, keepdims=True)
            outs.append(_rms(o) * gro_ref[:, h * RET_DV:(h + 1) * RET_DV] * _silu(gate))
        for h in range(GLA_HEADS):
            v = rrow[:, R_GV + h * GLA_DV: R_GV + (h + 1) * GLA_DV]
            gate = rrow[:, R_GR + h * GLA_DV: R_GR + (h + 1) * GLA_DV]
            qc = gcols[:, h:h + 1]
            kc = gcols[:, GLA_HEADS + h:GLA_HEADS + h + 1]
            ac = gcols[:, 2 * GLA_HEADS + h:2 * GLA_HEADS + h + 1]
            s_new = sg_ref[0, i, h] * ac + kc * v
            sg_out[0, i, h] = s_new
            o = jnp.sum(qc * s_new, axis=0, keepdims=True)
            outs.append(_rms(o) * ggo_ref[:, h * GLA_DV:(h + 1) * GLA_DV] * _silu(gate))
        o_ref[row, :] = jnp.concatenate(outs, axis=-1)


def _outproj_body(o_ref, x_ref, gt_ref, sc_ref, sh_ref, g_ref, w_ref, x1_ref, h_ref):
    mix = _dot(o_ref[...].astype(BF16), w_ref[...])
    x1 = x_ref[...] + gt_ref[0] * mix
    x1_ref[...] = x1
    h = _rms(x1) * g_ref[...]
    h_ref[...] = (h * (1.0 + sc_ref[0]) + sh_ref[0]).astype(BF16)


def _outproj(o, x, gt, sc, sh, g_ffn, w_out_bf16, *, tm, rows_per_mod):
    m, d = x.shape
    ko = o.shape[1]
    return pl.pallas_call(
        _outproj_body,
        grid=(m // tm,),
        in_specs=[pl.BlockSpec((tm, ko), lambda i: (i, 0)),
                  pl.BlockSpec((tm, d), lambda i: (i, 0)),
                  *(_mod_spec(v, d, tm, rows_per_mod) for v in (gt, sc, sh)),
                  pl.BlockSpec((1, d), lambda i: (0, 0)),
                  pl.BlockSpec((ko, d), lambda i: (0, 0), pipeline_mode=pl.Buffered(1))],
        out_specs=[pl.BlockSpec((tm, d), lambda i: (i, 0)),
                   pl.BlockSpec((tm, d), lambda i: (i, 0))],
        out_shape=[jax.ShapeDtypeStruct((m, d), F32),
                   jax.ShapeDtypeStruct((m, d), BF16)],
        compiler_params=_params(("arbitrary",)),
        name="out_proj",
    )(o, x, gt[0], sc[0], sh[0], g_ffn.reshape(1, d), w_out_bf16)


HALO = SUBLANES


def _ffn_up_body(h_ref, wa_ref, wb_ref, cwa_ref, cwb_ref, cba_ref, cbb_ref, g_ref, cs_ref, win_ref, tail_ref, *, tiles_per_seq):
    i, f = pl.program_id(0), pl.program_id(1)
    hb = h_ref[...]
    tm = hb.shape[0]
    first = (i % tiles_per_seq) == 0
    ucs = []
    for part, (w_ref, cw_ref, cb_ref) in enumerate(((wa_ref, cwa_ref, cba_ref), (wb_ref, cwb_ref, cbb_ref))):
        u = _dot(hb, w_ref[...])
        prev = jnp.where(first, 0.0, tail_ref[part, f])
        pieces = []
        for c in range(u.shape[1] // LANES):
            lc = slice(c * LANES, (c + 1) * LANES)
            win_ref[part, c, 0:HALO, :] = prev[:, lc]
            win_ref[part, c, HALO:HALO + tm, :] = u[:, lc]
            cw, cb = cw_ref[:, lc], cb_ref[:, lc]
            pieces.append(cb + cw[0:1] * win_ref[part, c, HALO - 2:HALO - 2 + tm, :]
                          + cw[1:2] * win_ref[part, c, HALO - 1:HALO - 1 + tm, :] + cw[2:3] * u[:, lc])
        ucs.append(jnp.concatenate(pieces, axis=-1))
        tail_ref[part, f] = u[tm - HALO:]
        for r in range(CONV_W - 1):
            cs_ref[0, 0, r, part:part + 1, :] = u[tm - (CONV_W - 1) + r: tm - (CONV_W - 1) + r + 1]
    g_ref[...] = (_silu(ucs[0]) * ucs[1]).astype(g_ref.dtype)


def _ffn_up_state_body(h_ref, wa_ref, wb_ref, cwa_ref, cwb_ref, cba_ref, cbb_ref,
                       lg_ref, qk_ref, rest_ref, la_ref, sr_ref, sg_ref, gro_ref, ggo_ref, wd_ref,
                       g_ref, cs_ref, o_ref, sr_out, sg_out, wdb_ref, win_ref, tail_ref, *, tiles_per_seq, nblk):
    _ffn_up_body(h_ref, wa_ref, wb_ref, cwa_ref, cwb_ref, cba_ref, cbb_ref, g_ref, cs_ref, win_ref, tail_ref,
                 tiles_per_seq=tiles_per_seq)
    nb = sr_ref.shape[1]
    blk = jnp.minimum(pl.program_id(0) * pl.num_programs(1) + pl.program_id(1), nblk - 1)
    r0 = (blk % (SUBLANES // nb)) * nb
    _state_body(r0, lg_ref, qk_ref, rest_ref, la_ref, sr_ref, sg_ref, gro_ref, ggo_ref, o_ref, sr_out, sg_out)
    wdb_ref[...] = wd_ref[...].astype(wdb_ref.dtype)


def _ffn_up_prompt(h, w_up, conv_w, conv_b, qk_s, rest_s, la_s, state_ret, state_gla, g_ret_out, g_gla_out, w_down,
                   *, batch, seq, tm, tf, nb, cast_rows):
    m, d = h.shape
    ff = w_up.shape[1] // 2
    nf = ff // tf
    cb = conv_b.reshape(1, 2 * ff)
    tps = seq // tm
    ms = qk_s.shape[0]
    nblk = ms // nb
    ncast = ff // cast_rows
    assert (m // tm) * nf >= nblk + ncast
    assert SUBLANES % nb == 0 and ms % SUBLANES == 0
    mix = RET_W + GLA_W
    lg = _ret_log_gamma_tile()
    sblk = lambda i, f: jnp.minimum(i * nf + f, nblk - 1)
    ret_spec = pl.BlockSpec((1, nb, RET_HEADS, RET_DK, RET_DV), lambda i, f: (0, sblk(i, f), 0, 0, 0))
    gla_spec = pl.BlockSpec((1, nb, GLA_HEADS, GLA_DK, GLA_DV), lambda i, f: (0, sblk(i, f), 0, 0, 0))
    row_spec = lambda w: pl.BlockSpec((SUBLANES, w), lambda i, f: (sblk(i, f) // (SUBLANES // nb), 0))
    cast_spec = pl.BlockSpec((cast_rows, d), lambda i, f: (jnp.clip(i * nf + f - nblk, 0, ncast - 1), 0))
    body = functools.partial(_ffn_up_state_body, tiles_per_seq=tps, nblk=nblk)
    gate, tails, o_s, s_ret, s_gla, w_down_b = pl.pallas_call(
        body,
        grid=(m // tm, nf),
        in_specs=[pl.BlockSpec((tm, d), lambda i, f: (i, 0)),
                  pl.BlockSpec((d, tf), lambda i, f: (0, f)),
                  pl.BlockSpec((d, tf), lambda i, f: (0, nf + f)),
                  pl.BlockSpec((CONV_W, tf), lambda i, f: (0, f)),
                  pl.BlockSpec((CONV_W, tf), lambda i, f: (0, nf + f)),
                  pl.BlockSpec((1, tf), lambda i, f: (0, f)),
                  pl.BlockSpec((1, tf), lambda i, f: (0, nf + f)),
                  pl.BlockSpec((RET_HEADS, SUBLANES, LANES), lambda i, f: (0, 0, 0)),
                  row_spec(QK_W), row_spec(REST_W), row_spec(la_s.shape[1]),
                  ret_spec, gla_spec,
                  pl.BlockSpec((1, RET_W), lambda i, f: (0, 0)),
                  pl.BlockSpec((1, GLA_W), lambda i, f: (0, 0)),
                  cast_spec],
        out_specs=[pl.BlockSpec((tm, tf), lambda i, f: (i, f)),
                   pl.BlockSpec((1, 1, CONV_W - 1, 2, tf), lambda i, f: (i // tps, i % tps, 0, 0, f)),
                   row_spec(mix), ret_spec, gla_spec, cast_spec],
        out_shape=[jax.ShapeDtypeStruct((m, ff), BF16),
                   jax.ShapeDtypeStruct((batch, tps, CONV_W - 1, 2, ff), F32),
                   jax.ShapeDtypeStruct((ms, mix), F32),
                   jax.ShapeDtypeStruct(state_ret.shape, F32),
                   jax.ShapeDtypeStruct(state_gla.shape, F32),
                   jax.ShapeDtypeStruct(w_down.shape, BF16)],
        scratch_shapes=[pltpu.VMEM((2, tf // LANES, HALO + tm, LANES), F32), pltpu.VMEM((2, nf, HALO, tf), F32)],
        compiler_params=_params(("arbitrary", "arbitrary")),
        name="ffn_up_prompt",
    )(h, w_up, w_up, conv_w, conv_w, cb, cb,
      lg, qk_s, rest_s, la_s,
      state_ret, state_gla, g_ret_out.reshape(1, RET_W), g_gla_out.reshape(1, GLA_W), w_down)
    return gate, tails[:, tps - 1], o_s, s_ret, s_gla, w_down_b


def _ffn_up_step_body(h_ref, w_ref, cw_ref, cb_ref, st_ref, g_ref, cs_ref, uca_ref, *, nf):
    j = pl.program_id(0)
    u = _dot(h_ref[...], w_ref[...])
    s1 = st_ref[0, :, 1, :]
    cw = cw_ref[...]
    uc = cb_ref[...] + cw[0:1] * st_ref[0, :, 0, :] + cw[1:2] * s1 + cw[2:3] * u
    cs_ref[0, :, 0, :] = s1
    cs_ref[0, :, 1, :] = u

    @pl.when(j < nf)
    def _():
        uca_ref[j] = uc

    @pl.when(j >= nf)
    def _():
        g_ref[...] = (_silu(uca_ref[j - nf]) * uc).astype(g_ref.dtype)


def _ffn_up_step(h, w_up, conv_w, conv_b, state_conv, *, tf):
    m, d = h.shape
    ff = w_up.shape[1] // 2
    nf = ff // tf
    st_spec = pl.BlockSpec((1, m, CONV_W - 1, tf), lambda j: (0, 0, 0, j))
    return pl.pallas_call(
        functools.partial(_ffn_up_step_body, nf=nf),
        grid=(2 * nf,),
        in_specs=[pl.BlockSpec((m, d), lambda j: (0, 0)),
                  pl.BlockSpec((d, tf), lambda j: (0, j)),
                  pl.BlockSpec((CONV_W, tf), lambda j: (0, j)),
                  pl.BlockSpec((1, tf), lambda j: (0, j)),
                  st_spec],
        out_specs=[pl.BlockSpec((m, tf), lambda j: (0, jnp.maximum(j - nf, 0))), st_spec],
        out_shape=[jax.ShapeDtypeStruct((m, ff), BF16),
                   jax.ShapeDtypeStruct(state_conv.shape, F32)],
        scratch_shapes=[pltpu.VMEM((nf, m, tf), F32)],
        compiler_params=_params(("arbitrary",)),
        name="ffn_up_step",
    )(h, w_up, conv_w, conv_b.reshape(1, 2 * ff), state_conv)


def _ffn_down_body(g_ref, w_ref, x1_ref, gt_ref, gf_ref, y_ref, xs_ref, inv_ref):
    k = pl.program_id(1)
    last = pl.num_programs(1) - 1
    d = y_ref.shape[1]
    xb = x1_ref.shape[1]
    xs_ref[k] = x1_ref[...]

    @pl.when(k == 0)
    def _():
        for cb in range(d // PROJ_BLK):
            cs = slice(cb * PROJ_BLK, (cb + 1) * PROJ_BLK)
            y_ref[:, cs] = _dot(g_ref[...], w_ref[:, cs])

    @pl.when((k > 0) & (k < last))
    def _():
        for cb in range(d // PROJ_BLK):
            cs = slice(cb * PROJ_BLK, (cb + 1) * PROJ_BLK)
            y_ref[:, cs] += _dot(g_ref[...], w_ref[:, cs])

    @pl.when(k == last)
    def _():
        ssq = jnp.zeros((y_ref.shape[0], 1), F32)
        for cb in range(d // PROJ_BLK):
            cs = slice(cb * PROJ_BLK, (cb + 1) * PROJ_BLK)
            x1 = xs_ref[(cb * PROJ_BLK) // xb, :, (cb * PROJ_BLK) % xb: (cb * PROJ_BLK) % xb + PROJ_BLK]
            x2 = x1 + gt_ref[0][:, cs] * (y_ref[:, cs] + _dot(g_ref[...], w_ref[:, cs]))
            y_ref[:, cs] = x2
            ssq = ssq + jnp.sum(x2 * x2, axis=-1, keepdims=True)
        inv_ref[...] = lax.rsqrt(ssq * (1.0 / d) + EPS)

        rg = min(y_ref.shape[0], NORM_ROWS)

        def norm_rows(r, carry):
            rows = pl.ds(pl.multiple_of(r * rg, rg), rg)
            y_ref[rows, :] = y_ref[rows, :] * inv_ref[rows, :] * gf_ref[...]
            return carry

        lax.fori_loop(0, y_ref.shape[0] // rg, norm_rows, 0)


def _ffn_down(g, w_down, x1, gt, g_final, *, tm, tk, rows_per_mod):
    m, d = x1.shape
    ff = g.shape[1]
    nk = ff // tk
    assert nk >= 2
    xb = d // nk
    assert d % nk == 0 and xb % PROJ_BLK == 0
    return pl.pallas_call(
        _ffn_down_body,
        grid=(m // tm, nk),
        in_specs=[pl.BlockSpec((tm, tk), lambda i, k: (i, k)),
                  pl.BlockSpec((tk, d), lambda i, k: (k, 0)),
                  pl.BlockSpec((tm, xb), lambda i, k: (i, k)),
                  _mod_spec(gt, d, tm, rows_per_mod),
                  pl.BlockSpec((1, d), lambda i, k: (0, 0))],
        out_specs=pl.BlockSpec((tm, d), lambda i, k: (i, 0)),
        out_shape=jax.ShapeDtypeStruct((m, d), F32),
        scratch_shapes=[pltpu.VMEM((nk, tm, xb), F32), pltpu.VMEM((tm, 1), F32)],
        compiler_params=_params(("arbitrary", "arbitrary")),
        name="ffn_down",
    )(g, w_down, x1, gt[0], g_final.reshape(1, d))


def _rope_tables(pos, rows=None):
    half = RET_DK // 2
    inv = ROPE_THETA ** (-np.arange(half, dtype=np.float64) / half)
    ang = np.asarray(pos, np.float64)[:, None] * inv[None, :]
    tabs = (np.cos(ang), np.sin(ang))
    if rows is not None:
        tabs = tuple(np.broadcast_to(t, (rows, half)) for t in tabs)
    return tuple(jnp.asarray(t, F32) for t in tabs)


def kernel(x_prompt, x_sample, c_prompt, c_sample, state_ret, state_gla, state_conv, w_ada, b_ada, g_attn, w_in, w_a2, b_a2, g_ret_out, g_gla_out, w_out, g_ffn, w_up, conv_w, conv_b, w_down, g_final):
    bp, t_p, d = x_prompt.shape
    bs, t_s, _ = x_sample.shape
    assert t_s == 1 and w_ada.shape[0] == 1
    mp = bp * t_p
    w_ada, b_ada, g_attn, w_in, w_a2, b_a2, g_ret_out, g_gla_out, w_out, g_ffn, w_up, conv_w, conv_b, w_down = (
        a[0] for a in (w_ada, b_ada, g_attn, w_in, w_a2, b_a2, g_ret_out, g_gla_out, w_out, g_ffn, w_up, conv_w, conv_b, w_down))

    mod, w_out_b = _ada(jnp.concatenate([c_sample, c_prompt], axis=0), w_ada, b_ada, w_out)
    mod_p = mod[bs:].reshape(bp, 1, -1)
    sh1p, sc1p, gt1p, sh2p, sc2p, gt2p = ((mod_p, i) for i in range(6))
    sh1s, sc1s, gt1s, sh2s, sc2s, gt2s = ((mod[None], i) for i in range(6))

    cos_p, sin_p = _rope_tables(np.arange(t_p))
    cos_s, sin_s = _rope_tables(PAST_LEN + np.arange(t_s), rows=bs)
    w_in_t = w_in.T
    w_gate_t = w_in_t[IN_MAIN:]

    xp = x_prompt.reshape(mp, d)
    xs = x_sample.reshape(bs, d)
    qk_s, rest_s, la_s, w_in_b = _inproj(xs, sc1s, sh1s, g_attn, w_in_t, w_gate_t, w_a2, b_a2, cos_s, sin_s,
                                         tm=bs, rows_per_mod=bs, cast_w=True, qk_dtype=F32)
    qk_p, rest_p, la_p, w_up_b = _inproj(xp, sc1p, sh1p, g_attn, w_in_b, w_gate_t, w_a2, b_a2, cos_p, sin_p,
                                         tm=ROW_TILE, rows_per_mod=t_p, cast_src=w_up)
    x1_p, h2_p, s_ret_p, s_gla_p = _mixer_prompt(qk_p, rest_p, la_p, xp, gt1p, sc2p, sh2p, g_ret_out, g_gla_out, g_ffn,
                                                 w_out_b, batch=bp, seq=t_p, chunk=CHUNK)
    g_p, cs_p, o_s, s_ret_s, s_gla_s, w_down_b = _ffn_up_prompt(
        h2_p, w_up_b, conv_w, conv_b, qk_s, rest_s, la_s, state_ret, state_gla, g_ret_out, g_gla_out, w_down,
        batch=bp, seq=t_p, tm=ROW_TILE, tf=FF_COL_TILE, nb=STATE_SEQS, cast_rows=CAST_ROWS)
    y_p = _ffn_down(g_p, w_down_b, x1_p, gt2p, g_final, tm=ROW_TILE, tk=DOWN_K_TILE, rows_per_mod=t_p)

    x1_s, h2_s = _outproj(o_s, xs, gt1s, sc2s, sh2s, g_ffn, w_out_b, tm=bs, rows_per_mod=bs)
    ff = w_down.shape[0]
    g_s, cs_s = _ffn_up_step(h2_s, w_up_b, conv_w, conv_b, state_conv, tf=ff // 4)
    y_s = _ffn_down(g_s, w_down_b, x1_s, gt2s, g_final, tm=bs, tk=ff // 2, rows_per_mod=bs)

    return (y_p.reshape(bp, t_p, d), y_s.reshape(bs, t_s, d),
            s_ret_p[None], s_ret_s, s_gla_p[None], s_gla_s,
            cs_p.reshape(1, bp, CONV_W - 1, -1), cs_s)
```
